```python
import jax, jax.numpy as jnp
from jax import lax
import numpy as np

D_MODEL = 2048
BATCH = 8
SEQ = 2048
DEPTH = 1

HEAD_DIM = 128
ATTN_HEADS = D_MODEL // HEAD_DIM
ATTN_WIDTH = ATTN_HEADS * HEAD_DIM
DILATION_PATTERNS = ((128, 1), (512, 4), (2048, 16))
ATTN_BLOCK = 128
ROPE_THETA = 500000.0
ROPE_DIM = HEAD_DIM // 4
POS_OFFSET_MAX = 4096

REC_EXPAND = 128
REC_HEADS = D_MODEL // REC_EXPAND
REC_KEY_DIM = REC_EXPAND
REC_VAL_DIM = D_MODEL // REC_HEADS
REC_WIDTH = REC_HEADS * REC_KEY_DIM
REC_VWIDTH = REC_HEADS * REC_VAL_DIM
REC_CHUNK = 64

IN_SECTIONS = (ATTN_WIDTH, ATTN_WIDTH, ATTN_WIDTH,
               REC_WIDTH, REC_WIDTH, REC_VWIDTH, REC_VWIDTH,
               D_MODEL, D_MODEL)
IN_WIDTH = sum(IN_SECTIONS)

N_GROUPS = 4
EXPERTS_PER_GROUP = 8
N_EXPERTS = N_GROUPS * EXPERTS_PER_GROUP
TOP_K = 2
EXPERT_HIDDEN = D_MODEL // 2
MOE_BLOCK = 128

NORM_EPS = 1e-6

kernel_name = "hybrid_dilated_attn_hgrn2_hiermoe_block"


def rms_norm(x, g):
    xf = x.astype(jnp.float32)
    y = xf * lax.rsqrt(jnp.mean(xf * xf, axis=-1, keepdims=True) + NORM_EPS)
    return (y * g.astype(jnp.float32)).astype(x.dtype)


def modulate(y, shift, scale):
    return y * (1 + scale[:, None, :]) + shift[:, None, :]


def partial_rotary(x, positions):
    half = ROPE_DIM // 2
    inv_freq = ROPE_THETA ** (-jnp.arange(0, ROPE_DIM, 2, dtype=jnp.float32) / ROPE_DIM)
    ang = positions.astype(jnp.float32)[..., None] * inv_freq
    cos, sin = jnp.cos(ang)[:, :, None, :], jnp.sin(ang)[:, :, None, :]
    xf = x.astype(jnp.float32)
    x1, x2, xp = xf[..., :half], xf[..., half:ROPE_DIM], xf[..., ROPE_DIM:]
    out = jnp.concatenate([x1 * cos - x2 * sin, x2 * cos + x1 * sin, xp], axis=-1)
    return out.astype(x.dtype)


def dilated_window_attention(q, k, v, window, dilation):
    B, S, H, Dh = q.shape
    span = window // dilation
    n_sub = S // dilation
    L = min(ATTN_BLOCK, n_sub)
    nb = -(-n_sub // L)
    tail = nb * L - n_sub
    BR = B * dilation

    def to_sub(a):
        a = a.reshape(B, n_sub, dilation, H, Dh)
        return jnp.swapaxes(a, 1, 2).reshape(BR, n_sub, H, Dh)

    def key_blocks(a):
        a = jnp.pad(to_sub(a), ((0, 0), (L, tail), (0, 0), (0, 0)))
        prev = a[:, :nb * L].reshape(BR, nb, L, H, Dh)
        cur = a[:, L:].reshape(BR, nb, L, H, Dh)
        return jnp.concatenate([prev, cur], axis=2)

    qs = jnp.pad(to_sub(q), ((0, 0), (0, tail), (0, 0), (0, 0))).reshape(BR, nb, L, H, Dh)
    kb, vb = key_blocks(k), key_blocks(v)

    qi = jnp.arange(L)[:, None]
    kj = jnp.arange(2 * L)[None, :]
    dist = qi - kj + L
    kpos = jnp.arange(nb)[:, None, None] * L + kj[None] - L
    valid = (dist >= 0)[None] & (dist <= span)[None] & (kpos >= 0)

    s = jnp.einsum('bnqhd,bnkhd->bnhqk', qs, kb).astype(jnp.float32) * (Dh ** -0.5)
    s = jnp.where(valid[None, :, None], s, -jnp.inf)
    m = jnp.max(s, axis=-1, keepdims=True)
    p = jnp.exp(s - m)
    den = jnp.sum(p, axis=-1, keepdims=True)
    o = jnp.einsum('bnhqk,bnkhd->bnqhd', p, vb.astype(jnp.float32)) / jnp.swapaxes(den, 2, 3)
    lse = jnp.swapaxes((m + jnp.log(den))[..., 0], 2, 3)

    def from_sub(a):
        trail = a.shape[3:]
        a = a.reshape((B, dilation, nb * L) + trail)[:, :, :n_sub]
        return jnp.swapaxes(a, 1, 2).reshape((B, S) + trail)

    return from_sub(o), from_sub(lse)


def dilated_mixture_attention(q, k, v):
    outs, lses = [], []
    for window, dilation in DILATION_PATTERNS:
        o, lse = dilated_window_attention(q, k, v, window, dilation)
        outs.append(o)
        lses.append(lse)
    alpha = jax.nn.softmax(jnp.stack(lses), axis=0)
    return jnp.einsum('pbsh,pbshd->bshd', alpha, jnp.stack(outs))


def chunked_gated_recurrence(q, k, v, log_f):
    B, S, H, K = q.shape
    V = v.shape[-1]
    C = min(REC_CHUNK, S)
    nc = S // C

    def to_chunks(a):
        return a.reshape(B, nc, C, H, a.shape[-1]).transpose(1, 0, 3, 2, 4)

    causal = jnp.tril(jnp.ones((C, C), dtype=bool))

    def step(state, blk):
        qb, kb, vb, gb = blk
        b = jnp.cumsum(gb, axis=2)
        rel = jnp.where(causal[:, :, None], b[:, :, :, None, :] - b[:, :, None, :, :], -jnp.inf)
        scores = jnp.einsum('bhtk,bhtsk,bhsk->bhts', qb, jnp.exp(rel), kb)
        intra = jnp.einsum('bhts,bhsv->bhtv', scores, vb)
        inter = jnp.einsum('bhtk,bhkv->bhtv', qb * jnp.exp(b), state)
        b_last = b[:, :, -1, :]
        new_state = jnp.exp(b_last)[..., None] * state + jnp.einsum(
            'bhsk,bhsv->bhkv', kb * jnp.exp(b_last[:, :, None, :] - b), vb)
        return new_state, intra + inter

    state0 = jnp.zeros((B, H, K, V), jnp.float32)
    _, out = lax.scan(step, state0, (to_chunks(q), to_chunks(k), to_chunks(v), to_chunks(log_f)))
    return out.transpose(1, 0, 3, 2, 4).reshape(B, S, H, V)


def hgrn2_branch(q_r, f_r, i_r, g_r, lower_bound, norm_g):
    B, S, _ = q_r.shape
    lb = lower_bound.reshape(REC_HEADS, REC_KEY_DIM)
    fpre = f_r.astype(jnp.float32).reshape(B, S, REC_HEADS, REC_KEY_DIM)
    log_f = jnp.log(lb + (1 - lb) * jax.nn.sigmoid(fpre))
    k = (1 - lb) * jax.nn.sigmoid(-fpre)
    q = jax.nn.silu(q_r.astype(jnp.float32)).reshape(B, S, REC_HEADS, REC_KEY_DIM)
    v = i_r.astype(jnp.float32).reshape(B, S, REC_HEADS, REC_VAL_DIM)
    o = chunked_gated_recurrence(q, k, v, log_f)
    gate = jax.nn.silu(g_r.astype(jnp.float32).reshape(B, S, REC_HEADS, REC_VAL_DIM))
    o = rms_norm(o, norm_g) * gate
    return o.reshape(B, S, REC_VWIDTH).astype(q_r.dtype)


def hybrid_mixer(u, positions, w_in, w_attn_branch, w_rec_branch, w_mix_out, rec_norm_g, lower_bound):
    B, S, _ = u.shape
    proj = u @ w_in
    splits = np.cumsum(IN_SECTIONS)[:-1].tolist()
    q_a, k_a, v_a, q_r, f_r, i_r, g_r, gate_a, gate_r = jnp.split(proj, splits, axis=-1)
    q_a = partial_rotary(q_a.reshape(B, S, ATTN_HEADS, HEAD_DIM), positions)
    k_a = partial_rotary(k_a.reshape(B, S, ATTN_HEADS, HEAD_DIM), positions)
    v_a = v_a.reshape(B, S, ATTN_HEADS, HEAD_DIM)
    y_attn = dilated_mixture_attention(q_a, k_a, v_a).reshape(B, S, ATTN_WIDTH).astype(u.dtype)
    y_rec = hgrn2_branch(q_r, f_r, i_r, g_r, lower_bound, rec_norm_g)
    merged = (jax.nn.sigmoid(gate_a) * (y_attn @ w_attn_branch)
              + jax.nn.sigmoid(gate_r) * (y_rec @ w_rec_branch))
    return merged @ w_mix_out


def hierarchical_route(xt, wg, bg, we, be):
    T = xt.shape[0]
    lg = (xt @ wg).astype(jnp.float32) + bg.astype(jnp.float32)
    pg = jax.nn.softmax(lg, axis=-1)
    g_sel = jnp.argmax(lg, axis=-1)
    p_group = jnp.take_along_axis(pg, g_sel[:, None], axis=-1)
    le = ((xt @ we).astype(jnp.float32) + be.astype(jnp.float32)).reshape(T, N_GROUPS, EXPERTS_PER_GROUP)
    le_sel = jnp.take_along_axis(le, g_sel[:, None, None], axis=1)[:, 0]
    top_logits, top_local = lax.top_k(le_sel, TOP_K)
    weights = p_group * jax.nn.softmax(top_logits, axis=-1)
    expert_idx = (g_sel[:, None] * EXPERTS_PER_GROUP + top_local).astype(jnp.int32)
    return expert_idx, weights


def grouped_expert_ffn(xt, expert_idx, weights, w_gate, w_up, w_down):
    T, D = xt.shape
    n_assign = T * TOP_K
    n_blocks = -(-n_assign // MOE_BLOCK) + N_EXPERTS
    n_slots = n_blocks * MOE_BLOCK
    flat_e = expert_idx.reshape(-1)
    order = jnp.argsort(flat_e)
    sorted_e = flat_e[order]
    counts = jnp.bincount(flat_e, length=N_EXPERTS)
    padded = ((counts + MOE_BLOCK - 1) // MOE_BLOCK) * MOE_BLOCK
    pad_end = jnp.cumsum(padded)
    pad_start = pad_end - padded
    start = jnp.cumsum(counts) - counts
    rank = jnp.arange(n_assign) - start[sorted_e]
    dest = pad_start[sorted_e] + rank
    token_of_slot = jnp.full((n_slots,), T, jnp.int32).at[dest].set((order // TOP_K).astype(jnp.int32))
    weight_of_slot = jnp.zeros((n_slots,), jnp.float32).at[dest].set(weights.reshape(-1)[order])
    block_expert = jnp.minimum(
        jnp.searchsorted(pad_end, jnp.arange(n_blocks) * MOE_BLOCK, side='right'), N_EXPERTS - 1)
    xpad = jnp.concatenate([xt, jnp.zeros((1, D), xt.dtype)], axis=0)

    def expert_block(args):
        tok, e = args
        xb = xpad[tok]
        hdn = jax.nn.silu(xb @ w_gate[e]) * (xb @ w_up[e])
        return hdn @ w_down[e]

    yb = lax.map(expert_block, (token_of_slot.reshape(n_blocks, MOE_BLOCK), block_expert))
    ys = yb.reshape(n_slots, D).astype(jnp.float32) * weight_of_slot[:, None]
    out = jax.ops.segment_sum(ys, token_of_slot, num_segments=T + 1)[:T]
    return out.astype(xt.dtype)


def hierarchical_moe(u, wg, bg, we, be, w_gate, w_up, w_down):
    B, S, D = u.shape
    xt = u.reshape(B * S, D)
    expert_idx, weights = hierarchical_route(xt, wg, bg, we, be)
    return grouped_expert_ffn(xt, expert_idx, weights, w_gate, w_up, w_down).reshape(B, S, D)


def setup_inputs(seed: int = 0) -> dict:
    key = jax.random.key(seed)
    ks = jax.random.split(key, 24)
    D = D_MODEL

    def nrm(k, shape, scale):
        return jax.random.normal(k, shape, jnp.float32) * scale

    x = nrm(ks[0], (BATCH, SEQ, D), 1.0)
    c = nrm(ks[1], (BATCH, D), 1.0)
    positions = (jax.random.randint(ks[2], (BATCH, 1), 0, POS_OFFSET_MAX, dtype=jnp.int32)
                 + jnp.arange(SEQ, dtype=jnp.int32)[None, :])
    return {
        "x": x,
        "c": c,
        "positions": positions,
        "ada_w": nrm(ks[3], (DEPTH, D, 6 * D), 0.5 * D ** -0.5),
        "ada_b": nrm(ks[4], (DEPTH, 6 * D), 0.02),
        "mix_norm_g": 1.0 + nrm(ks[5], (DEPTH, D), 0.02),
        "w_in": nrm(ks[6], (DEPTH, D, IN_WIDTH), D ** -0.5),
        "w_attn_branch": nrm(ks[7], (DEPTH, ATTN_WIDTH, D), ATTN_WIDTH ** -0.5),
        "w_rec_branch": nrm(ks[8], (DEPTH, REC_VWIDTH, D), REC_VWIDTH ** -0.5),
        "w_mix_out": nrm(ks[9], (DEPTH, D, D), D ** -0.5),
        "rec_norm_g": 1.0 + nrm(ks[10], (DEPTH, REC_VAL_DIM), 0.02),
        "rec_lb_logits": nrm(ks[11], (DEPTH + 1, REC_WIDTH), 0.1),
        "ffn_norm_g": 1.0 + nrm(ks[12], (DEPTH, D), 0.02),
        "router_group_w": nrm(ks[13], (DEPTH, D, N_GROUPS), D ** -0.5),
        "router_group_b": nrm(ks[14], (DEPTH, N_GROUPS), 0.01),
        "router_expert_w": nrm(ks[15], (DEPTH, D, N_EXPERTS), D ** -0.5),
        "router_expert_b": nrm(ks[16], (DEPTH, N_EXPERTS), 0.01),
        "expert_w_gate": nrm(ks[17], (DEPTH, N_EXPERTS, D, EXPERT_HIDDEN), D ** -0.5),
        "expert_w_up": nrm(ks[18], (DEPTH, N_EXPERTS, D, EXPERT_HIDDEN), D ** -0.5),
        "expert_w_down": nrm(ks[19], (DEPTH, N_EXPERTS, EXPERT_HIDDEN, D), EXPERT_HIDDEN ** -0.5),
        "final_norm_g": 1.0 + nrm(ks[20], (D,), 0.02),
    }


def reference(x, c, positions, ada_w, ada_b, mix_norm_g, w_in, w_attn_branch, w_rec_branch,
              w_mix_out, rec_norm_g, rec_lb_logits, ffn_norm_g, router_group_w, router_group_b,
              router_expert_w, router_expert_b, expert_w_gate, expert_w_up, expert_w_down,
              final_norm_g):
    cond = jax.nn.silu(c)
    lower_bounds = jnp.cumsum(jax.nn.softmax(rec_lb_logits.astype(jnp.float32), axis=0), axis=0)
    h = x
    for layer in range(DEPTH):
        mod = cond @ ada_w[layer] + ada_b[layer]
        sh_m, sc_m, gt_m, sh_f, sc_f, gt_f = jnp.split(mod, 6, axis=-1)
        u = modulate(rms_norm(h, mix_norm_g[layer]), sh_m, sc_m)
        mix = hybrid_mixer(u, positions, w_in[layer], w_attn_branch[layer], w_rec_branch[layer],
                           w_mix_out[layer], rec_norm_g[layer], lower_bounds[layer])
        h = h + gt_m[:, None, :] * mix
        u = modulate(rms_norm(h, ffn_norm_g[layer]), sh_f, sc_f)
        ffn = hierarchical_moe(u, router_group_w[layer], router_group_b[layer],
                               router_expert_w[layer], router_expert_b[layer],
                               expert_w_gate[layer], expert_w_up[layer], expert_w_down[layer])
        h = h + gt_f[:, None, :] * ffn
    return rms_norm(h, final_norm_g)
```

```python
import functools

import jax
import jax.numpy as jnp
from jax import lax
from jax.experimental import pallas as pl
from jax.experimental.pallas import tpu as pltpu

F32 = jnp.float32
BF16 = jnp.bfloat16

D_MODEL = 2048
HEAD_DIM = 128
N_HEADS = D_MODEL // HEAD_DIM
ROPE_DIM = HEAD_DIM // 4
ROPE_HALF = ROPE_DIM // 2
ROPE_THETA = 500000.0
ATTN_SPAN = 128
ATTN_BLOCK = 128
REC_CHUNK = 64
N_GROUPS = 4
EXPERTS_PER_GROUP = 8
N_EXPERTS = N_GROUPS * EXPERTS_PER_GROUP
EXPERT_HIDDEN = D_MODEL // 2
TOP_K = 2
NORM_EPS = 1e-6
IN_WIDTH = 9 * D_MODEL

LANES = 128
VMEM_LIMIT = 56 * 1024 * 1024

MOE_ROWS = 256
HEADS_PER_STEP = 2


def _cparams(sem):
    return pltpu.CompilerParams(dimension_semantics=sem, vmem_limit_bytes=VMEM_LIMIT)


def _sigmoid(x):
    return 1.0 / (1.0 + jnp.exp(-x))


def _silu(x):
    return x * _sigmoid(x)


def _mod_kernel(c_ref, w_ref, b_ref, o_ref):
    cond = _silu(c_ref[...])
    o_ref[...] = jnp.dot(cond, w_ref[...], precision=lax.Precision.HIGHEST,
                         preferred_element_type=F32) + b_ref[...]


def _modulation(c, w, b):
    bsz, d = c.shape
    n = w.shape[1]
    tn = 1536
    return pl.pallas_call(
        _mod_kernel,
        out_shape=jax.ShapeDtypeStruct((bsz, n), F32),
        grid=(n // tn,),
        in_specs=[pl.BlockSpec((bsz, d), lambda j: (0, 0)),
                  pl.BlockSpec((d, tn), lambda j: (0, j)),
                  pl.BlockSpec((1, tn), lambda j: (0, j))],
        out_specs=pl.BlockSpec((bsz, tn), lambda j: (0, j)),
        compiler_params=_cparams(("arbitrary",)),
        name="adaln_mod",
    )(c, w, b.reshape(1, n))


def _rope_kernel(pos_ref, freq_ref, c_ref, sa_ref, sb_ref):
    ang = pos_ref[...] * freq_ref[...]
    lane = lax.broadcasted_iota(jnp.int32, ang.shape, 1)
    cos, sin = jnp.cos(ang), jnp.sin(ang)
    c_ref[...] = jnp.where(lane < ROPE_DIM, cos, 1.0)
    sa_ref[...] = jnp.where(lane < ROPE_HALF, -sin, 0.0)
    sb_ref[...] = jnp.where((lane >= ROPE_HALF) & (lane < ROPE_DIM), sin, 0.0)


def _rope_tables(positions):
    t = positions.size
    tm = 2048
    inv_freq = ROPE_THETA ** (-jnp.arange(0, ROPE_DIM, 2, dtype=F32) / ROPE_DIM)
    freq = jnp.concatenate([inv_freq, inv_freq, jnp.zeros((LANES - ROPE_DIM,), F32)]).reshape(1, LANES)
    pos = positions.astype(F32).reshape(t, 1)
    out = jax.ShapeDtypeStruct((t, LANES), F32)
    return pl.pallas_call(
        _rope_kernel,
        out_shape=(out, out, out),
        grid=(t // tm,),
        in_specs=[pl.BlockSpec((tm, 1), lambda i: (i, 0)),
                  pl.BlockSpec((1, LANES), lambda i: (0, 0))],
        out_specs=tuple(pl.BlockSpec((tm, LANES), lambda i: (i, 0)) for _ in range(3)),
        compiler_params=_cparams(("arbitrary",)),
        name="rope_tables",
    )(pos, freq)


def _norm_mod_kernel(x_ref, g_ref, sh_ref, sc_ref, o_ref):
    x = x_ref[...]
    y = x * lax.rsqrt(jnp.mean(x * x, axis=-1, keepdims=True) + NORM_EPS) * g_ref[...]
    o_ref[...] = (y * (1.0 + sc_ref[0]) + sh_ref[0]).astype(o_ref.dtype)


def _norm_modulate(x2d, g, shift, scale, seq):
    t, d = x2d.shape
    tm = 512
    per_b = seq // tm
    bsz = shift.shape[0]
    return pl.pallas_call(
        _norm_mod_kernel,
        out_shape=jax.ShapeDtypeStruct((t, d), BF16),
        grid=(t // tm,),
        in_specs=[pl.BlockSpec((tm, d), lambda i: (i, 0)),
                  pl.BlockSpec((1, d), lambda i: (0, 0)),
                  pl.BlockSpec((1, 1, d), lambda i: (i // per_b, 0, 0)),
                  pl.BlockSpec((1, 1, d), lambda i: (i // per_b, 0, 0))],
        out_specs=pl.BlockSpec((tm, d), lambda i: (i, 0)),
        compiler_params=_cparams(("arbitrary",)),
        name="norm_modulate",
    )(x2d, g.reshape(1, d), shift.reshape(bsz, 1, d), scale.reshape(bsz, 1, d))


def _proj_kernel(a_ref, w_ref, o_ref):
    o_ref[...] = jnp.dot(a_ref[...], w_ref[...], preferred_element_type=F32).astype(o_ref.dtype)


def _proj_rope_kernel(a_ref, w_ref, c_ref, sa_ref, sb_ref, o_ref):
    acc = jnp.dot(a_ref[...], w_ref[...], preferred_element_type=F32)
    c, sa, sb = c_ref[...], sa_ref[...], sb_ref[...]
    for h in range(acc.shape[1] // HEAD_DIM):
        x = acc[:, h * HEAD_DIM:(h + 1) * HEAD_DIM]
        up = pltpu.roll(x, HEAD_DIM - ROPE_HALF, 1)
        dn = pltpu.roll(x, ROPE_HALF, 1)
        o_ref[:, h * HEAD_DIM:(h + 1) * HEAD_DIM] = (x * c + up * sa + dn * sb).astype(o_ref.dtype)


def _project(u, w, col0, ncols, out_dtype, rope=None):
    t, d = u.shape
    tm, tn = 1024, 512
    j0 = col0 // tn
    in_specs = [pl.BlockSpec((tm, d), lambda i, j: (i, 0)),
                pl.BlockSpec((d, tn), lambda i, j: (0, j0 + j))]
    args = [u, w]
    kern = _proj_kernel
    if rope is not None:
        in_specs += [pl.BlockSpec((tm, LANES), lambda i, j: (i, 0)) for _ in range(3)]
        args += list(rope)
        kern = _proj_rope_kernel
    return pl.pallas_call(
        kern,
        out_shape=jax.ShapeDtypeStruct((t, ncols), out_dtype),
        grid=(t // tm, ncols // tn),
        in_specs=in_specs,
        out_specs=pl.BlockSpec((tm, tn), lambda i, j: (i, j)),
        compiler_params=_cparams(("arbitrary", "arbitrary")),
        name="in_proj_rope" if rope is not None else "in_proj",
    )(*args)


def _window_attention(q, k_win, v_win, q_pos0, k_pos0):
    s = lax.dot_general(q, k_win, (((1,), (1,)), ((), ())), preferred_element_type=F32)
    s = s * (HEAD_DIM ** -0.5)
    qi = lax.broadcasted_iota(jnp.int32, s.shape, 0)
    kj = lax.broadcasted_iota(jnp.int32, s.shape, 1)
    dist = (qi - kj) + (q_pos0 - k_pos0)
    s = jnp.where((dist >= 0) & (dist <= ATTN_SPAN), s, -jnp.inf)
    m = jnp.max(s, axis=-1, keepdims=True)
    p = jnp.exp(s - m)
    den = jnp.sum(p, axis=-1, keepdims=True)
    o = jnp.dot(p.astype(BF16), v_win, preferred_element_type=F32)
    return o / den, m + jnp.log(den)


def _sliding_block(q_ref, k_ref, v_ref, qb, lanes):
    L = ATTN_BLOCK
    if isinstance(qb, int):
        q0, k0 = qb * L, max(qb - 1, 0) * L
    else:
        q0 = pl.multiple_of(qb * L, L)
        k0 = pl.multiple_of(jnp.maximum(qb - 1, 0) * L, L)
    q = q_ref[0, pl.ds(q0, L), lanes]
    kw = k_ref[0, pl.ds(k0, 2 * L), lanes]
    vw = v_ref[0, pl.ds(k0, 2 * L), lanes]
    return _window_attention(q, kw, vw, q0, k0)


def _attn_kernel(qn_ref, kn_ref, vn_ref, q4_ref, k4_ref, v4_ref, *rest):
    n16 = 4
    q16 = rest[0:n16]
    k16 = rest[n16:2 * n16]
    v16 = rest[2 * n16:3 * n16]
    o_ref = rest[3 * n16]
    o1_s, l1_s, o3_s, l3_s = rest[3 * n16 + 1:]
    j4 = pl.program_id(2)
    L = ATTN_BLOCK
    seq = qn_ref.shape[1]
    sub4 = q4_ref.shape[1]

    @pl.when(j4 == 0)
    def _():
        def body(qb, carry):
            for g in range(HEADS_PER_STEP):
                lanes = slice(g * HEAD_DIM, (g + 1) * HEAD_DIM)
                o, lse = _sliding_block(qn_ref, kn_ref, vn_ref, qb, lanes)
                rows = pl.ds(pl.multiple_of(qb * L, L), L)
                o1_s[g, rows, :] = o
                l1_s[g, rows, :] = jnp.broadcast_to(lse, (L, HEAD_DIM))
            return carry
        lax.fori_loop(0, seq // L, body, 0)

    for g in range(HEADS_PER_STEP):
        lanes = slice(g * HEAD_DIM, (g + 1) * HEAD_DIM)
        for r in range(n16):
            q = q16[r][0, :, lanes]
            o, lse = _window_attention(q, k16[r][0, :, lanes], v16[r][0, :, lanes], 0, 0)
            o3_s[g, pl.ds(r, L, stride=4), :] = o
            l3_s[g, pl.ds(r, L, stride=4), :] = jnp.broadcast_to(lse, (L, HEAD_DIM))
        for qb in range(sub4 // L):
            o2, lse2 = _sliding_block(q4_ref, k4_ref, v4_ref, qb, lanes)
            l2 = jnp.broadcast_to(lse2, (L, HEAD_DIM))
            rows = slice(qb * L, (qb + 1) * L)
            o3, l3 = o3_s[g, rows, :], l3_s[g, rows, :]
            nat = pl.ds(qb * L * 4 + j4, L, stride=4)
            o1, l1 = o1_s[g, nat, :], l1_s[g, nat, :]
            mx = jnp.maximum(jnp.maximum(l1, l2), l3)
            w1, w2, w3 = jnp.exp(l1 - mx), jnp.exp(l2 - mx), jnp.exp(l3 - mx)
            y = (w1 * o1 + w2 * o2 + w3 * o3) / (w1 + w2 + w3)
            o_ref[0, rows, lanes] = y.astype(o_ref.dtype)


def _attention(q, k, v, bsz, seq):
    t, width = q.shape
    gw = HEADS_PER_STEP * HEAD_DIM
    ng = width // gw
    nat = lambda a: a.reshape(bsz, seq, width)
    s4 = lambda a: a.reshape(bsz, seq // 4, 4 * width)
    s16 = lambda a: a.reshape(bsz, seq // 16, 16 * width)
    nat_spec = pl.BlockSpec((1, seq, gw), lambda b, h, j: (b, 0, h))
    s4_spec = pl.BlockSpec((1, seq // 4, gw), lambda b, h, j: (b, 0, j * ng + h))

    def s16_spec(r):
        return pl.BlockSpec((1, seq // 16, gw), lambda b, h, j: (b, 0, (j + 4 * r) * ng + h))

    in_specs = [nat_spec] * 3 + [s4_spec] * 3
    args = [nat(q), nat(k), nat(v), s4(q), s4(k), s4(v)]
    for a in (q, k, v):
        for r in range(4):
            in_specs.append(s16_spec(r))
            args.append(s16(a))
    out = pl.pallas_call(
        _attn_kernel,
        out_shape=jax.ShapeDtypeStruct((bsz, seq // 4, 4 * width), BF16),
        grid=(bsz, ng, 4),
        in_specs=in_specs,
        out_specs=s4_spec,
        scratch_shapes=[pltpu.VMEM((HEADS_PER_STEP, seq, HEAD_DIM), F32),
                        pltpu.VMEM((HEADS_PER_STEP, seq, HEAD_DIM), F32),
                        pltpu.VMEM((HEADS_PER_STEP, seq // 4, HEAD_DIM), F32),
                        pltpu.VMEM((HEADS_PER_STEP, seq // 4, HEAD_DIM), F32)],
        compiler_params=_cparams(("arbitrary", "arbitrary", "arbitrary")),
        name="dilated_attention",
    )(*args)
    return out.reshape(t, width)


def _split3(x):
    hi = x.astype(BF16)
    r1 = x - hi.astype(F32)
    mid = r1.astype(BF16)
    lo = (r1 - mid.astype(F32)).astype(BF16)
    return hi, mid, lo


def _rec_kernel(q_ref, f_ref, i_ref, g_ref, lb_ref, ng_ref, o_ref, st_ref):
    C = REC_CHUNK
    seq = q_ref.shape[1]
    st_ref[...] = jnp.zeros_like(st_ref)
    ti = lax.broadcasted_iota(jnp.int32, (C, C), 0)
    si = lax.broadcasted_iota(jnp.int32, (C, C), 1)
    causal = si <= ti
    tri = jnp.where(causal, 1.0, 0.0).astype(BF16)
    anchor = C // 2 - 1

    def body(c, carry):
        rows = pl.ds(pl.multiple_of(c * C, C), C)
        for g in range(HEADS_PER_STEP):
            lanes = slice(g * HEAD_DIM, (g + 1) * HEAD_DIM)
            lb = lb_ref[:, lanes]
            x = f_ref[0, rows, lanes]
            z = jnp.exp(-jnp.abs(x))
            r = 1.0 / (1.0 + z)
            pos = x >= 0
            sig_p = jnp.where(pos, r, z * r)
            sig_n = jnp.where(pos, z * r, r)
            logf = jnp.log(lb + (1.0 - lb) * sig_p)
            kk = (1.0 - lb) * sig_n
            hi, mid, lo = _split3(logf)
            b = (jnp.dot(tri, hi, preferred_element_type=F32)
                 + jnp.dot(tri, mid, preferred_element_type=F32)
                 + jnp.dot(tri, lo, preferred_element_type=F32))
            b_last = b[C - 1:C, :]
            b_mid = b[anchor:anchor + 1, :]
            qs = _silu(q_ref[0, rows, lanes].astype(F32))
            v = i_ref[0, rows, lanes]
            q_in = (qs * jnp.exp(b)).astype(BF16)
            q_a = (qs * jnp.exp(b - b_mid)).astype(BF16)
            k_a = (kk * jnp.exp(b_mid - b)).astype(BF16)
            k_e = (kk * jnp.exp(b_last - b)).astype(BF16)
            sc = lax.dot_general(q_a, k_a, (((1,), (1,)), ((), ())), preferred_element_type=F32)
            sc = jnp.where(causal, sc, 0.0).astype(BF16)
            st = st_ref[g]
            o = (jnp.dot(sc, v, preferred_element_type=F32)
                 + lax.dot_general(q_in, st.astype(BF16), (((1,), (1,)), ((), ())),
                                   preferred_element_type=F32))
            upd = lax.dot_general(v, k_e, (((0,), (0,)), ((), ())), preferred_element_type=F32)
            st_ref[g] = jnp.exp(b_last) * st + upd
            y = o * lax.rsqrt(jnp.mean(o * o, axis=-1, keepdims=True) + NORM_EPS) * ng_ref[...]
            y = y * _silu(g_ref[0, rows, lanes].astype(F32))
            o_ref[0, rows, lanes] = y.astype(o_ref.dtype)
        return carry

    lax.fori_loop(0, seq // C, body, 0)


def _recurrence(q_r, f_r, i_r, g_r, lower_bound, norm_g, bsz, seq):
    t, width = q_r.shape
    gw = HEADS_PER_STEP * HEAD_DIM
    v3 = lambda a: a.reshape(bsz, seq, width)
    spec = pl.BlockSpec((1, seq, gw), lambda b, h: (b, 0, h))
    out = pl.pallas_call(
        _rec_kernel,
        out_shape=jax.ShapeDtypeStruct((bsz, seq, width), BF16),
        grid=(bsz, width // gw),
        in_specs=[spec, spec, spec, spec,
                  pl.BlockSpec((1, gw), lambda b, h: (0, h)),
                  pl.BlockSpec((1, HEAD_DIM), lambda b, h: (0, 0))],
        out_specs=spec,
        scratch_shapes=[pltpu.VMEM((HEADS_PER_STEP, HEAD_DIM, HEAD_DIM), F32)],
        compiler_params=_cparams(("arbitrary", "arbitrary")),
        name="hgrn2_recurrence",
    )(v3(q_r), v3(f_r), v3(i_r), v3(g_r), lower_bound.reshape(1, width), norm_g.reshape(1, HEAD_DIM))
    return out.reshape(t, width)


def _merge_kernel(ya_ref, yr_ref, wa_ref, wr_ref, ga_ref, gr_ref, o_ref):
    a = jnp.dot(ya_ref[...], wa_ref[...], preferred_element_type=F32)
    r = jnp.dot(yr_ref[...], wr_ref[...], preferred_element_type=F32)
    m = _sigmoid(ga_ref[...].astype(F32)) * a + _sigmoid(gr_ref[...].astype(F32)) * r
    o_ref[...] = m.astype(o_ref.dtype)


def _merge(ya, yr, wa, wr, ga, gr):
    t, d = ya.shape
    n = wa.shape[1]
    tm, tn = 1024, 512
    row = pl.BlockSpec((tm, d), lambda i, j: (i, 0))
    col = pl.BlockSpec((d, tn), lambda i, j: (0, j))
    tile = pl.BlockSpec((tm, tn), lambda i, j: (i, j))
    return pl.pallas_call(
        _merge_kernel,
        out_shape=jax.ShapeDtypeStruct((t, n), BF16),
        grid=(t // tm, n // tn),
        in_specs=[row, row, col, col, tile, tile],
        out_specs=tile,
        compiler_params=_cparams(("arbitrary", "arbitrary")),
        name="branch_merge",
    )(ya, yr, wa, wr, ga, gr)


def _mixout_kernel(m_ref, w_ref, x_ref, gt_ref, g_ref, sh_ref, sc_ref, rw_ref, rb_ref,
                   h_ref, u_ref, route_ref):
    mix = jnp.dot(m_ref[...], w_ref[...], preferred_element_type=F32)
    h = x_ref[...] + gt_ref[0] * mix
    h_ref[...] = h
    u = h * lax.rsqrt(jnp.mean(h * h, axis=-1, keepdims=True) + NORM_EPS) * g_ref[...]
    u = u * (1.0 + sc_ref[0]) + sh_ref[0]
    u_ref[...] = u.astype(u_ref.dtype)
    u_hi = u.astype(BF16)
    u_lo = (u - u_hi.astype(F32)).astype(BF16)
    rw = rw_ref[...]
    w_hi = rw.astype(BF16)
    w_lo = (rw - w_hi.astype(F32)).astype(BF16)
    logits = (jnp.dot(u_hi, w_hi, preferred_element_type=F32)
              + jnp.dot(u_lo, w_hi, preferred_element_type=F32)
              + jnp.dot(u_hi, w_lo, preferred_element_type=F32)) + rb_ref[...]
    lane = lax.broadcasted_iota(jnp.int32, logits.shape, 1).astype(F32)
    big = float(LANES)
    neg = -jnp.inf
    lg = jnp.where(lane < N_GROUPS, logits, neg)
    mg = jnp.max(lg, axis=-1, keepdims=True)
    g_sel = jnp.min(jnp.where(lg == mg, lane, big), axis=-1, keepdims=True)
    p_group = 1.0 / jnp.sum(jnp.exp(lg - mg), axis=-1, keepdims=True)
    lo = N_GROUPS + EXPERTS_PER_GROUP * g_sel
    le = jnp.where((lane >= lo) & (lane < lo + EXPERTS_PER_GROUP), logits, neg)
    t1 = jnp.max(le, axis=-1, keepdims=True)
    i1 = jnp.min(jnp.where(le == t1, lane, big), axis=-1, keepdims=True)
    le2 = jnp.where(lane == i1, neg, le)
    t2 = jnp.max(le2, axis=-1, keepdims=True)
    i2 = jnp.min(jnp.where(le2 == t2, lane, big), axis=-1, keepdims=True)
    e21 = jnp.exp(t2 - t1)
    w1 = p_group / (1.0 + e21)
    w2 = p_group * e21 / (1.0 + e21)
    route = jnp.where(lane == 0, i1 - N_GROUPS,
                      jnp.where(lane == 1, i2 - N_GROUPS,
                                jnp.where(lane == 2, w1, jnp.where(lane == 3, w2, 0.0))))
    route_ref[...] = route


def _mix_out(merged, w_out, x2d, gate, g, shift, scale, rw, rb, seq):
    t, d = x2d.shape
    tm = 256
    per_b = seq // tm
    bsz = gate.shape[0]
    row = lambda dt: pl.BlockSpec((tm, d), lambda i: (i, 0))
    per_batch = pl.BlockSpec((1, 1, d), lambda i: (i // per_b, 0, 0))
    const = lambda shape: pl.BlockSpec(shape, lambda i: (0,) * len(shape))
    return pl.pallas_call(
        _mixout_kernel,
        out_shape=(jax.ShapeDtypeStruct((t, d), F32),
                   jax.ShapeDtypeStruct((t, d), BF16),
                   jax.ShapeDtypeStruct((t, LANES), F32)),
        grid=(t // tm,),
        in_specs=[row(BF16), const((d, d)), row(F32), per_batch, const((1, d)), per_batch, per_batch,
                  const((d, LANES)), const((1, LANES))],
        out_specs=(row(F32), row(BF16), pl.BlockSpec((tm, LANES), lambda i: (i, 0))),
        compiler_params=_cparams(("arbitrary",)),
        name="mix_out_router",
    )(merged, w_out, x2d, gate.reshape(bsz, 1, d), g.reshape(1, d),
      shift.reshape(bsz, 1, d), scale.reshape(bsz, 1, d), rw, rb)


def _expert_kernel(be_ref, nu_ref, x_ref, wg_ref, wu_ref, wd_ref, ws_ref, o_ref):
    i = pl.program_id(0)

    @pl.when(i < nu_ref[0])
    def _():
        x = x_ref[...]
        hg = jnp.dot(x, wg_ref[0], preferred_element_type=F32)
        hu = jnp.dot(x, wu_ref[0], preferred_element_type=F32)
        hdn = (_silu(hg) * hu).astype(BF16)
        y = jnp.dot(hdn, wd_ref[0], preferred_element_type=F32)
        o_ref[...] = y * ws_ref[...]

    @pl.when(i >= nu_ref[0])
    def _():
        o_ref[...] = jnp.zeros_like(o_ref)


def _expert_ffn(xs, w_gate, w_up, w_down, w_slot, block_expert, n_used):
    n_slots, d = xs.shape
    hid = w_gate.shape[2]
    bm = MOE_ROWS
    grid_spec = pltpu.PrefetchScalarGridSpec(
        num_scalar_prefetch=2,
        grid=(n_slots // bm,),
        in_specs=[pl.BlockSpec((bm, d), lambda i, be, nu: (i, 0)),
                  pl.BlockSpec((1, d, hid), lambda i, be, nu: (be[i], 0, 0)),
                  pl.BlockSpec((1, d, hid), lambda i, be, nu: (be[i], 0, 0)),
                  pl.BlockSpec((1, hid, d), lambda i, be, nu: (be[i], 0, 0)),
                  pl.BlockSpec((bm, 1), lambda i, be, nu: (i, 0))],
        out_specs=pl.BlockSpec((bm, d), lambda i, be, nu: (i, 0)),
    )
    return pl.pallas_call(
        _expert_kernel,
        out_shape=jax.ShapeDtypeStruct((n_slots, d), F32),
        grid_spec=grid_spec,
        compiler_params=_cparams(("arbitrary",)),
        name="expert_ffn",
    )(block_expert, n_used, xs, w_gate, w_up, w_down, w_slot)


def _final_kernel(h_ref, f_ref, gt_ref, g_ref, o_ref):
    h = h_ref[...] + gt_ref[0] * f_ref[...]
    o_ref[...] = h * lax.rsqrt(jnp.mean(h * h, axis=-1, keepdims=True) + NORM_EPS) * g_ref[...]


def _final(h, ffn, gate, g, seq):
    t, d = h.shape
    tm = 512
    per_b = seq // tm
    bsz = gate.shape[0]
    row = pl.BlockSpec((tm, d), lambda i: (i, 0))
    return pl.pallas_call(
        _final_kernel,
        out_shape=jax.ShapeDtypeStruct((t, d), F32),
        grid=(t // tm,),
        in_specs=[row, row, pl.BlockSpec((1, 1, d), lambda i: (i // per_b, 0, 0)),
                  pl.BlockSpec((1, d), lambda i: (0, 0))],
        out_specs=row,
        compiler_params=_cparams(("arbitrary",)),
        name="final_norm",
    )(h, ffn, gate.reshape(bsz, 1, d), g.reshape(1, d))


def _dispatch_plan(expert_idx, weights):
    n_assign = expert_idx.size
    n_blocks = n_assign // MOE_ROWS + N_EXPERTS
    n_slots = n_blocks * MOE_ROWS
    flat_e = expert_idx.reshape(-1)
    order = jnp.argsort(flat_e, stable=True)
    sorted_e = flat_e[order]
    counts = jnp.sum(flat_e[:, None] == jnp.arange(N_EXPERTS)[None, :], axis=0).astype(jnp.int32)
    padded = ((counts + MOE_ROWS - 1) // MOE_ROWS) * MOE_ROWS
    pad_end = jnp.cumsum(padded)
    pad_start = pad_end - padded
    start = jnp.cumsum(counts) - counts
    rank = jnp.arange(n_assign, dtype=jnp.int32) - start[sorted_e]
    dest_sorted = pad_start[sorted_e] + rank
    token_of_slot = jnp.zeros((n_slots,), jnp.int32).at[dest_sorted].set((order // TOP_K).astype(jnp.int32))
    weight_of_slot = jnp.zeros((n_slots,), F32).at[dest_sorted].set(weights.reshape(-1)[order])
    dest = jnp.zeros((n_assign,), jnp.int32).at[order].set(dest_sorted).reshape(-1, TOP_K)
    block_expert = jnp.minimum(
        jnp.searchsorted(pad_end, jnp.arange(n_blocks, dtype=jnp.int32) * MOE_ROWS, side='right'),
        N_EXPERTS - 1).astype(jnp.int32)
    n_used = (pad_end[-1:] // MOE_ROWS).astype(jnp.int32)
    return token_of_slot, weight_of_slot, dest, block_expert, n_used


def kernel(x, c, positions, ada_w, ada_b, mix_norm_g, w_in, w_attn_branch, w_rec_branch, w_mix_out,
           rec_norm_g, rec_lb_logits, ffn_norm_g, router_group_w, router_group_b, router_expert_w,
           router_expert_b, expert_w_gate, expert_w_up, expert_w_down, final_norm_g):
    bsz, seq, d = x.shape
    t = bsz * seq
    depth = ada_w.shape[0]
    assert depth == 1, "final norm is fused after the single layer"
    lower_bounds = jnp.cumsum(jax.nn.softmax(rec_lb_logits.astype(F32), axis=0), axis=0)
    rope = _rope_tables(positions)
    h = x.reshape(t, d)
    for layer in range(depth):
        mod = _modulation(c, ada_w[layer], ada_b[layer])
        sh_m, sc_m, gt_m, sh_f, sc_f, gt_f = jnp.split(mod, 6, axis=-1)
        u = _norm_modulate(h, mix_norm_g[layer], sh_m, sc_m, seq)
        w = w_in[layer].astype(BF16)
        qk = _project(u, w, 0, 2 * d, BF16, rope=rope)
        rest = _project(u, w, 2 * d, 2 * d, BF16)
        f_r = _project(u, w, 4 * d, d, F32)
        tail = _project(u, w, 5 * d, 4 * d, BF16)
        q_a, k_a = qk[:, :d], qk[:, d:]
        v_a, q_r = rest[:, :d], rest[:, d:]
        i_r, g_r, gate_a, gate_r = (tail[:, k * d:(k + 1) * d] for k in range(4))
        y_attn = _attention(q_a, k_a, v_a, bsz, seq)
        y_rec = _recurrence(q_r, f_r, i_r, g_r, lower_bounds[layer], rec_norm_g[layer], bsz, seq)
        merged = _merge(y_attn, y_rec, w_attn_branch[layer].astype(BF16),
                        w_rec_branch[layer].astype(BF16), gate_a, gate_r)
        rw = jnp.concatenate([router_group_w[layer], router_expert_w[layer],
                              jnp.zeros((d, LANES - N_GROUPS - N_EXPERTS), F32)], axis=1)
        rb = jnp.concatenate([router_group_b[layer], router_expert_b[layer],
                              jnp.zeros((LANES - N_GROUPS - N_EXPERTS,), F32)]).reshape(1, LANES)
        h, u2, route = _mix_out(merged, w_mix_out[layer].astype(BF16), h, gt_m, ffn_norm_g[layer],
                                sh_f, sc_f, rw, rb, seq)
        expert_idx = route[:, :TOP_K].astype(jnp.int32)
        weights = route[:, TOP_K:2 * TOP_K]
        tok, w_slot, dest, block_expert, n_used = _dispatch_plan(expert_idx, weights)
        xs = u2[tok]
        ys = _expert_ffn(xs, expert_w_gate[layer].astype(BF16), expert_w_up[layer].astype(BF16),
                         expert_w_down[layer].astype(BF16), w_slot.reshape(-1, 1), block_expert, n_used)
        ffn = ys[dest[:, 0]] + ys[dest[:, 1]]
        h = _final(h, ffn, gt_f, final_norm_g, seq)
    return h.reshape(bsz, seq, d)
```

```python
import functools

import jax
import jax.numpy as jnp
from jax import lax
from jax.experimental import pallas as pl
from jax.experimental.pallas import tpu as pltpu

F32 = jnp.float32
BF16 = jnp.bfloat16

D_MODEL = 2048
HEAD_DIM = 128
N_HEADS = D_MODEL // HEAD_DIM
ROPE_DIM = HEAD_DIM // 4
ROPE_HALF = ROPE_DIM // 2
ROPE_THETA = 500000.0
ATTN_SPAN = 128
ATTN_BLOCK = 128
REC_CHUNK = 64
N_GROUPS = 4
EXPERTS_PER_GROUP = 8
N_EXPERTS = N_GROUPS * EXPERTS_PER_GROUP
EXPERT_HIDDEN = D_MODEL // 2
TOP_K = 2
NORM_EPS = 1e-6
IN_WIDTH = 9 * D_MODEL

LANES = 128
VMEM_LIMIT = 56 * 1024 * 1024

MOE_ROWS = 256
HEADS_PER_STEP = 2
STREAMS = 4
DILATIONS = ((128, 1), (512, 4), (2048, 16))


def _cparams(sem):
    return pltpu.CompilerParams(dimension_semantics=sem, vmem_limit_bytes=VMEM_LIMIT)


def _sigmoid(x):
    return 1.0 / (1.0 + jnp.exp(-x))


def _silu(x):
    return x * _sigmoid(x)


def _mod_kernel(c_ref, w_ref, b_ref, o_ref):
    cond = _silu(c_ref[...])
    o_ref[...] = jnp.dot(cond, w_ref[...], precision=lax.Precision.HIGHEST,
                         preferred_element_type=F32) + b_ref[...]


def _modulation(c, w, b):
    bsz, d = c.shape
    n = w.shape[1]
    tn = 1536
    return pl.pallas_call(
        _mod_kernel,
        out_shape=jax.ShapeDtypeStruct((bsz, n), F32),
        grid=(n // tn,),
        in_specs=[pl.BlockSpec((bsz, d), lambda j: (0, 0)),
                  pl.BlockSpec((d, tn), lambda j: (0, j)),
                  pl.BlockSpec((1, tn), lambda j: (0, j))],
        out_specs=pl.BlockSpec((bsz, tn), lambda j: (0, j)),
        compiler_params=_cparams(("arbitrary",)),
        name="adaln_mod",
    )(c, w, b.reshape(1, n))


def _rope_kernel(pos_ref, freq_ref, c_ref, sa_ref, sb_ref):
    ang = pos_ref[...] * freq_ref[...]
    lane = lax.broadcasted_iota(jnp.int32, ang.shape, 1)
    cos, sin = jnp.cos(ang), jnp.sin(ang)
    c_ref[...] = jnp.where(lane < ROPE_DIM, cos, 1.0)
    sa_ref[...] = jnp.where(lane < ROPE_HALF, -sin, 0.0)
    sb_ref[...] = jnp.where((lane >= ROPE_HALF) & (lane < ROPE_DIM), sin, 0.0)


def _rope_tables(positions):
    t = positions.size
    tm = 2048
    inv_freq = ROPE_THETA ** (-jnp.arange(0, ROPE_DIM, 2, dtype=F32) / ROPE_DIM)
    freq = jnp.concatenate([inv_freq, inv_freq, jnp.zeros((LANES - ROPE_DIM,), F32)]).reshape(1, LANES)
    pos = positions.astype(F32).reshape(t, 1)
    out = jax.ShapeDtypeStruct((t, LANES), F32)
    return pl.pallas_call(
        _rope_kernel,
        out_shape=(out, out, out),
        grid=(t // tm,),
        in_specs=[pl.BlockSpec((tm, 1), lambda i: (i, 0)),
                  pl.BlockSpec((1, LANES), lambda i: (0, 0))],
        out_specs=tuple(pl.BlockSpec((tm, LANES), lambda i: (i, 0)) for _ in range(3)),
        compiler_params=_cparams(("arbitrary",)),
        name="rope_tables",
    )(pos, freq)


def _norm_mod_kernel(x_ref, g_ref, sh_ref, sc_ref, xp_ref, u_ref, slab_ref):
    rows = x_ref.shape[0]
    sub = rows // STREAMS
    nslab = x_ref.shape[1] // LANES
    for c in range(nslab):
        slab_ref[c] = x_ref[:, c * LANES:(c + 1) * LANES]
    for j in range(STREAMS):
        x = jnp.concatenate([slab_ref[c, pl.ds(j, sub, stride=STREAMS), :] for c in range(nslab)], axis=1)
        xp_ref[0, j] = x
        y = x * lax.rsqrt(jnp.mean(x * x, axis=-1, keepdims=True) + NORM_EPS) * g_ref[...]
        u_ref[0, j] = (y * (1.0 + sc_ref[0]) + sh_ref[0]).astype(u_ref.dtype)


def _norm_modulate(x2d, g, shift, scale, seq):
    t, d = x2d.shape
    tm = 512
    per_b = seq // tm
    bsz = shift.shape[0]
    sub = tm // STREAMS
    out_spec = pl.BlockSpec((1, STREAMS, sub, d), lambda i: (i // per_b, 0, i % per_b, 0))
    xp, u = pl.pallas_call(
        _norm_mod_kernel,
        out_shape=(jax.ShapeDtypeStruct((bsz, STREAMS, seq // STREAMS, d), F32),
                   jax.ShapeDtypeStruct((bsz, STREAMS, seq // STREAMS, d), BF16)),
        grid=(t // tm,),
        in_specs=[pl.BlockSpec((tm, d), lambda i: (i, 0)),
                  pl.BlockSpec((1, d), lambda i: (0, 0)),
                  pl.BlockSpec((1, 1, d), lambda i: (i // per_b, 0, 0)),
                  pl.BlockSpec((1, 1, d), lambda i: (i // per_b, 0, 0))],
        out_specs=(out_spec, out_spec),
        scratch_shapes=[pltpu.VMEM((d // LANES, tm, LANES), F32)],
        compiler_params=_cparams(("arbitrary",)),
        name="norm_modulate",
    )(x2d, g.reshape(1, d), shift.reshape(bsz, 1, d), scale.reshape(bsz, 1, d))
    return xp.reshape(t, d), u.reshape(t, d)


def _proj_kernel(a_ref, w_ref, o_ref):
    o_ref[...] = jnp.dot(a_ref[...], w_ref[...], preferred_element_type=F32).astype(o_ref.dtype)


def _proj_rope_kernel(a_ref, w_ref, c_ref, sa_ref, sb_ref, o_ref):
    acc = jnp.dot(a_ref[...], w_ref[...], preferred_element_type=F32)
    c, sa, sb = c_ref[...], sa_ref[...], sb_ref[...]
    for h in range(acc.shape[1] // HEAD_DIM):
        x = acc[:, h * HEAD_DIM:(h + 1) * HEAD_DIM]
        up = pltpu.roll(x, HEAD_DIM - ROPE_HALF, 1)
        dn = pltpu.roll(x, ROPE_HALF, 1)
        o_ref[:, h * HEAD_DIM:(h + 1) * HEAD_DIM] = (x * c + up * sa + dn * sb).astype(o_ref.dtype)


def _project(u, w, col0, ncols, out_dtype, rope=None):
    t, d = u.shape
    tm, tn = 1024, 512
    j0 = col0 // tn
    in_specs = [pl.BlockSpec((tm, d), lambda i, j: (i, 0)),
                pl.BlockSpec((d, tn), lambda i, j: (0, j0 + j))]
    args = [u, w]
    kern = _proj_kernel
    if rope is not None:
        in_specs += [pl.BlockSpec((tm, LANES), lambda i, j: (i, 0)) for _ in range(3)]
        args += list(rope)
        kern = _proj_rope_kernel
    return pl.pallas_call(
        kern,
        out_shape=jax.ShapeDtypeStruct((t, ncols), out_dtype),
        grid=(t // tm, ncols // tn),
        in_specs=in_specs,
        out_specs=pl.BlockSpec((tm, tn), lambda i, j: (i, j)),
        compiler_params=_cparams(("arbitrary", "arbitrary")),
        name="in_proj_rope" if rope is not None else "in_proj",
    )(*args)


def _key_pieces(qb, sub):
    L = ATTN_BLOCK
    pieces = [(0, 0, L * (qb + 1))]
    for r in range(1, STREAMS):
        back = 0 if qb == 0 else (L // 2 if r == 1 else L // 4)
        pieces.append((r, L * qb - back, L + back))
    return pieces


def _attention_bias(sub):
    L = ATTN_BLOCK
    nqb = sub // L
    kmax = max(sum(p[2] for p in _key_pieces(qb, sub)) for qb in range(nqb))
    qi = jnp.arange(L, dtype=jnp.int32)[:, None]
    out = []
    for j in range(STREAMS):
        row = []
        for qb in range(nqb):
            cols = []
            for r, start, size in _key_pieces(qb, sub):
                jp = (j + r) % STREAMS
                kn = start + jnp.arange(size, dtype=jnp.int32)[None, :]
                dt = STREAMS * (L * qb + qi - kn) + (j - jp)
                cnt = jnp.zeros(dt.shape, F32)
                for window, dil in DILATIONS:
                    cnt += ((dt >= 0) & (dt <= window) & (dt % dil == 0)).astype(F32)
                cols.append(jnp.log2(cnt))
            tile = jnp.concatenate(cols, axis=1)
            row.append(jnp.pad(tile, ((0, 0), (0, kmax - tile.shape[1])), constant_values=-jnp.inf))
        out.append(jnp.stack(row))
    return jnp.stack(out)


def _attn_kernel(q_ref, k_ref, v_ref, bias_ref, o_ref):
    j = pl.program_id(2)
    L = ATTN_BLOCK
    sub = o_ref.shape[1]
    scale = HEAD_DIM ** -0.5 * 1.4426950408889634
    for g in range(HEADS_PER_STEP):
        lanes = slice(g * HEAD_DIM, (g + 1) * HEAD_DIM)
        for qb in range(sub // L):
            rows = []
            for r, start, size in _key_pieces(qb, sub):
                base = pl.multiple_of(((j + r) % STREAMS) * sub, sub)
                rows.append(pl.ds(base + start, size))
            ktot = sum(p[2] for p in _key_pieces(qb, sub))
            q = q_ref[0, pl.ds(pl.multiple_of(j * sub, sub) + L * qb, L), lanes]
            k_all = jnp.concatenate([k_ref[0, rw, lanes] for rw in rows], axis=0)
            v_all = jnp.concatenate([v_ref[0, rw, lanes] for rw in rows], axis=0)
            s = lax.dot_general(q, k_all, (((1,), (1,)), ((), ())), preferred_element_type=F32)
            s = s * scale + bias_ref[j, qb, :, :ktot]
            m = jnp.max(s, axis=-1, keepdims=True)
            p = jnp.exp2(s - m).astype(BF16)
            v_ext = jnp.concatenate([v_all, jnp.ones_like(v_all)], axis=1)
            oe = jnp.dot(p, v_ext, preferred_element_type=F32)
            y = oe[:, :HEAD_DIM] / oe[:, HEAD_DIM:]
            o_ref[0, L * qb:L * (qb + 1), lanes] = y.astype(o_ref.dtype)


def _attention(q, k, v, bsz, seq):
    t, width = q.shape
    gw = HEADS_PER_STEP * HEAD_DIM
    sub = seq // STREAMS
    v3 = lambda a: a.reshape(bsz, seq, width)
    bias = _attention_bias(sub)
    full = pl.BlockSpec((1, seq, gw), lambda b, h, j: (b, 0, h))
    out = pl.pallas_call(
        _attn_kernel,
        out_shape=jax.ShapeDtypeStruct((bsz, seq, width), BF16),
        grid=(bsz, width // gw, STREAMS),
        in_specs=[full, full, full,
                  pl.BlockSpec(bias.shape, lambda b, h, j: (0, 0, 0, 0))],
        out_specs=pl.BlockSpec((1, sub, gw), lambda b, h, j: (b, j, h)),
        compiler_params=_cparams(("arbitrary", "arbitrary", "arbitrary")),
        name="dilated_attention",
    )(v3(q), v3(k), v3(v), bias)
    return out.reshape(t, width)


def _split3(x):
    hi = x.astype(BF16)
    r1 = x - hi.astype(F32)
    mid = r1.astype(BF16)
    lo = (r1 - mid.astype(F32)).astype(BF16)
    return hi, mid, lo


def _rec_kernel(q_ref, f_ref, i_ref, g_ref, lb_ref, ng_ref, o_ref, *st_refs):
    C = REC_CHUNK
    seq = q_ref.shape[1]
    sub = seq // STREAMS
    piece = C // STREAMS
    for st_ref in st_refs:
        st_ref[...] = jnp.zeros_like(st_ref)
    pi = lax.broadcasted_iota(jnp.int32, (C, C), 0)
    si = lax.broadcasted_iota(jnp.int32, (C, C), 1)
    time_of = lambda p: STREAMS * (p % piece) + p // piece
    causal = time_of(si) <= time_of(pi)
    tri = jnp.where(causal, 1.0, 0.0).astype(BF16)
    row_of = lambda tau: (tau % STREAMS) * piece + tau // STREAMS
    last = row_of(C - 1)
    anchor = row_of(C // 2 - 1)

    def load(ref, c, lanes):
        return jnp.concatenate(
            [ref[0, pl.ds(pl.multiple_of(j * sub + c * piece, piece), piece), lanes] for j in range(STREAMS)],
            axis=0)

    def body(c, carry):
        for g in range(HEADS_PER_STEP):
            st_ref = st_refs[g]
            lanes = slice(g * HEAD_DIM, (g + 1) * HEAD_DIM)
            lb = lb_ref[:, lanes]
            x = load(f_ref, c, lanes)
            z = jnp.exp(-jnp.abs(x))
            r = 1.0 / (1.0 + z)
            pos = x >= 0
            sig_p = jnp.where(pos, r, z * r)
            sig_n = jnp.where(pos, z * r, r)
            logf = jnp.log(lb + (1.0 - lb) * sig_p)
            kk = (1.0 - lb) * sig_n
            hi, mid, lo = _split3(logf)
            b = (jnp.dot(tri, hi, preferred_element_type=F32)
                 + jnp.dot(tri, mid, preferred_element_type=F32)
                 + jnp.dot(tri, lo, preferred_element_type=F32))
            b_last = b[last:last + 1, :]
            b_mid = b[anchor:anchor + 1, :]
            qs = _silu(load(q_ref, c, lanes).astype(F32))
            v = load(i_ref, c, lanes)
            q_in = (qs * jnp.exp(b)).astype(BF16)
            q_a = (qs * jnp.exp(b - b_mid)).astype(BF16)
            k_a = (kk * jnp.exp(b_mid - b)).astype(BF16)
            k_e = (kk * jnp.exp(b_last - b)).astype(BF16)
            sc = lax.dot_general(q_a, k_a, (((1,), (1,)), ((), ())), preferred_element_type=F32)
            sc = jnp.where(causal, sc, 0.0).astype(BF16)
            st = st_ref[...]
            o = (jnp.dot(sc, v, preferred_element_type=F32)
                 + lax.dot_general(q_in, st.astype(BF16), (((1,), (1,)), ((), ())),
                                   preferred_element_type=F32))
            upd = lax.dot_general(v, k_e, (((0,), (0,)), ((), ())), preferred_element_type=F32)
            st_ref[...] = jnp.exp(b_last) * st + upd
            y = o * lax.rsqrt(jnp.mean(o * o, axis=-1, keepdims=True) + NORM_EPS) * ng_ref[...]
            y = (y * _silu(load(g_ref, c, lanes).astype(F32))).astype(o_ref.dtype)
            for j in range(STREAMS):
                o_ref[0, pl.ds(pl.multiple_of(j * sub + c * piece, piece), piece), lanes] = (
                    y[j * piece:(j + 1) * piece])
        return carry

    lax.fori_loop(0, seq // C, body, 0, unroll=2)


def _recurrence(q_r, f_r, i_r, g_r, lower_bound, norm_g, bsz, seq):
    t, width = q_r.shape
    gw = HEADS_PER_STEP * HEAD_DIM
    v3 = lambda a: a.reshape(bsz, seq, width)
    spec = pl.BlockSpec((1, seq, gw), lambda b, h: (b, 0, h))
    out = pl.pallas_call(
        _rec_kernel,
        out_shape=jax.ShapeDtypeStruct((bsz, seq, width), BF16),
        grid=(bsz, width // gw),
        in_specs=[spec, spec, spec, spec,
                  pl.BlockSpec((1, gw), lambda b, h: (0, h)),
                  pl.BlockSpec((1, HEAD_DIM), lambda b, h: (0, 0))],
        out_specs=spec,
        scratch_shapes=[pltpu.VMEM((HEAD_DIM, HEAD_DIM), F32) for _ in range(HEADS_PER_STEP)],
        compiler_params=_cparams(("arbitrary", "arbitrary")),
        name="hgrn2_recurrence",
    )(v3(q_r), v3(f_r), v3(i_r), v3(g_r), lower_bound.reshape(1, width), norm_g.reshape(1, HEAD_DIM))
    return out.reshape(t, width)


def _merge_kernel(ya_ref, yr_ref, wa_ref, wr_ref, ga_ref, gr_ref, o_ref):
    a = jnp.dot(ya_ref[...], wa_ref[...], preferred_element_type=F32)
    r = jnp.dot(yr_ref[...], wr_ref[...], preferred_element_type=F32)
    m = _sigmoid(ga_ref[...].astype(F32)) * a + _sigmoid(gr_ref[...].astype(F32)) * r
    o_ref[...] = m.astype(o_ref.dtype)


def _merge(ya, yr, wa, wr, ga, gr):
    t, d = ya.shape
    n = wa.shape[1]
    tm, tn = 1024, 512
    row = pl.BlockSpec((tm, d), lambda i, j: (i, 0))
    col = pl.BlockSpec((d, tn), lambda i, j: (0, j))
    tile = pl.BlockSpec((tm, tn), lambda i, j: (i, j))
    return pl.pallas_call(
        _merge_kernel,
        out_shape=jax.ShapeDtypeStruct((t, n), BF16),
        grid=(t // tm, n // tn),
        in_specs=[row, row, col, col, tile, tile],
        out_specs=tile,
        compiler_params=_cparams(("arbitrary", "arbitrary")),
        name="branch_merge",
    )(ya, yr, wa, wr, ga, gr)


def _mixout_kernel(m_ref, w_ref, x_ref, gt_ref, g_ref, sh_ref, sc_ref, rw_ref, rb_ref,
                   h_ref, u_ref, route_ref):
    mix = jnp.dot(m_ref[...], w_ref[...], preferred_element_type=F32)
    h = x_ref[...] + gt_ref[0] * mix
    h_ref[...] = h
    u = h * lax.rsqrt(jnp.mean(h * h, axis=-1, keepdims=True) + NORM_EPS) * g_ref[...]
    u = u * (1.0 + sc_ref[0]) + sh_ref[0]
    u_ref[...] = u.astype(u_ref.dtype)
    u_hi = u.astype(BF16)
    u_lo = (u - u_hi.astype(F32)).astype(BF16)
    rw = rw_ref[...]
    w_hi = rw.astype(BF16)
    w_lo = (rw - w_hi.astype(F32)).astype(BF16)
    logits = (jnp.dot(u_hi, w_hi, preferred_element_type=F32)
              + jnp.dot(u_lo, w_hi, preferred_element_type=F32)
              + jnp.dot(u_hi, w_lo, preferred_element_type=F32)) + rb_ref[...]
    lane = lax.broadcasted_iota(jnp.int32, logits.shape, 1).astype(F32)
    big = float(LANES)
    neg = -jnp.inf
    lg = jnp.where(lane < N_GROUPS, logits, neg)
    mg = jnp.max(lg, axis=-1, keepdims=True)
    g_sel = jnp.min(jnp.where(lg == mg, lane, big), axis=-1, keepdims=True)
    p_group = 1.0 / jnp.sum(jnp.exp(lg - mg), axis=-1, keepdims=True)
    lo = N_GROUPS + EXPERTS_PER_GROUP * g_sel
    le = jnp.where((lane >= lo) & (lane < lo + EXPERTS_PER_GROUP), logits, neg)
    t1 = jnp.max(le, axis=-1, keepdims=True)
    i1 = jnp.min(jnp.where(le == t1, lane, big), axis=-1, keepdims=True)
    le2 = jnp.where(lane == i1, neg, le)
    t2 = jnp.max(le2, axis=-1, keepdims=True)
    i2 = jnp.min(jnp.where(le2 == t2, lane, big), axis=-1, keepdims=True)
    e21 = jnp.exp(t2 - t1)
    w1 = p_group / (1.0 + e21)
    w2 = p_group * e21 / (1.0 + e21)
    route = jnp.where(lane == 0, i1 - N_GROUPS,
                      jnp.where(lane == 1, i2 - N_GROUPS,
                                jnp.where(lane == 2, w1, jnp.where(lane == 3, w2, 0.0))))
    route_ref[...] = route


def _mix_out(merged, w_out, x2d, gate, g, shift, scale, rw, rb, seq):
    t, d = x2d.shape
    tm = 256
    per_b = seq // tm
    bsz = gate.shape[0]
    row = lambda dt: pl.BlockSpec((tm, d), lambda i: (i, 0))
    per_batch = pl.BlockSpec((1, 1, d), lambda i: (i // per_b, 0, 0))
    const = lambda shape: pl.BlockSpec(shape, lambda i: (0,) * len(shape))
    return pl.pallas_call(
        _mixout_kernel,
        out_shape=(jax.ShapeDtypeStruct((t, d), F32),
                   jax.ShapeDtypeStruct((t, d), BF16),
                   jax.ShapeDtypeStruct((t, LANES), F32)),
        grid=(t // tm,),
        in_specs=[row(BF16), const((d, d)), row(F32), per_batch, const((1, d)), per_batch, per_batch,
                  const((d, LANES)), const((1, LANES))],
        out_specs=(row(F32), row(BF16), pl.BlockSpec((tm, LANES), lambda i: (i, 0))),
        compiler_params=_cparams(("arbitrary",)),
        name="mix_out_router",
    )(merged, w_out, x2d, gate.reshape(bsz, 1, d), g.reshape(1, d),
      shift.reshape(bsz, 1, d), scale.reshape(bsz, 1, d), rw, rb)


def _expert_kernel(be_ref, nu_ref, x_ref, wg_ref, wu_ref, wd_ref, ws_ref, o_ref):
    i = pl.program_id(0)

    @pl.when(i < nu_ref[0])
    def _():
        x = x_ref[...]
        hg = jnp.dot(x, wg_ref[0], preferred_element_type=F32)
        hu = jnp.dot(x, wu_ref[0], preferred_element_type=F32)
        hdn = (_silu(hg) * hu).astype(BF16)
        y = jnp.dot(hdn, wd_ref[0], preferred_element_type=F32)
        o_ref[...] = y * ws_ref[...]

    @pl.when(i >= nu_ref[0])
    def _():
        o_ref[...] = jnp.zeros_like(o_ref)


def _expert_ffn(xs, w_gate, w_up, w_down, w_slot, block_expert, n_used):
    n_slots, d = xs.shape
    hid = w_gate.shape[2]
    bm = MOE_ROWS
    grid_spec = pltpu.PrefetchScalarGridSpec(
        num_scalar_prefetch=2,
        grid=(n_slots // bm,),
        in_specs=[pl.BlockSpec((bm, d), lambda i, be, nu: (i, 0)),
                  pl.BlockSpec((1, d, hid), lambda i, be, nu: (be[i], 0, 0)),
                  pl.BlockSpec((1, d, hid), lambda i, be, nu: (be[i], 0, 0)),
                  pl.BlockSpec((1, hid, d), lambda i, be, nu: (be[i], 0, 0)),
                  pl.BlockSpec((bm, 1), lambda i, be, nu: (i, 0))],
        out_specs=pl.BlockSpec((bm, d), lambda i, be, nu: (i, 0)),
    )
    return pl.pallas_call(
        _expert_kernel,
        out_shape=jax.ShapeDtypeStruct((n_slots, d), F32),
        grid_spec=grid_spec,
        compiler_params=_cparams(("arbitrary",)),
        name="expert_ffn",
    )(block_expert, n_used, xs, w_gate, w_up, w_down, w_slot)


def _final_kernel(h_ref, f_ref, gt_ref, g_ref, o_ref, slab_ref):
    sub = h_ref.shape[2]
    nslab = h_ref.shape[3] // LANES
    for j in range(STREAMS):
        h = h_ref[0, j] + gt_ref[0] * f_ref[0, j]
        y = h * lax.rsqrt(jnp.mean(h * h, axis=-1, keepdims=True) + NORM_EPS) * g_ref[...]
        for c in range(nslab):
            slab_ref[c, pl.ds(j, sub, stride=STREAMS), :] = y[:, c * LANES:(c + 1) * LANES]
    for c in range(nslab):
        o_ref[:, c * LANES:(c + 1) * LANES] = slab_ref[c]


def _final(h, ffn, gate, g, seq):
    t, d = h.shape
    tm = 512
    per_b = seq // tm
    bsz = gate.shape[0]
    sub = tm // STREAMS
    v4 = lambda a: a.reshape(bsz, STREAMS, seq // STREAMS, d)
    spec = pl.BlockSpec((1, STREAMS, sub, d), lambda i: (i // per_b, 0, i % per_b, 0))
    return pl.pallas_call(
        _final_kernel,
        out_shape=jax.ShapeDtypeStruct((t, d), F32),
        grid=(t // tm,),
        in_specs=[spec, spec, pl.BlockSpec((1, 1, d), lambda i: (i // per_b, 0, 0)),
                  pl.BlockSpec((1, d), lambda i: (0, 0))],
        out_specs=pl.BlockSpec((tm, d), lambda i: (i, 0)),
        scratch_shapes=[pltpu.VMEM((d // LANES, tm, LANES), F32)],
        compiler_params=_cparams(("arbitrary",)),
        name="final_norm",
    )(v4(h), v4(ffn), gate.reshape(bsz, 1, d), g.reshape(1, d))


def _dispatch_plan(expert_idx, weights):
    n_assign = expert_idx.size
    n_blocks = n_assign // MOE_ROWS + N_EXPERTS
    n_slots = n_blocks * MOE_ROWS
    flat_e = expert_idx.reshape(-1)
    order = jnp.argsort(flat_e, stable=True)
    sorted_e = flat_e[order]
    counts = jnp.sum(flat_e[:, None] == jnp.arange(N_EXPERTS)[None, :], axis=0).astype(jnp.int32)
    padded = ((counts + MOE_ROWS - 1) // MOE_ROWS) * MOE_ROWS
    pad_end = jnp.cumsum(padded)
    pad_start = pad_end - padded
    start = jnp.cumsum(counts) - counts
    rank = jnp.arange(n_assign, dtype=jnp.int32) - start[sorted_e]
    dest_sorted = pad_start[sorted_e] + rank
    token_of_slot = jnp.zeros((n_slots,), jnp.int32).at[dest_sorted].set((order // TOP_K).astype(jnp.int32))
    weight_of_slot = jnp.zeros((n_slots,), F32).at[dest_sorted].set(weights.reshape(-1)[order])
    dest = jnp.zeros((n_assign,), jnp.int32).at[order].set(dest_sorted).reshape(-1, TOP_K)
    block_expert = jnp.minimum(
        jnp.searchsorted(pad_end, jnp.arange(n_blocks, dtype=jnp.int32) * MOE_ROWS, side='right'),
        N_EXPERTS - 1).astype(jnp.int32)
    n_used = (pad_end[-1:] // MOE_ROWS).astype(jnp.int32)
    return token_of_slot, weight_of_slot, dest, block_expert, n_used


def kernel(x, c, positions, ada_w, ada_b, mix_norm_g, w_in, w_attn_branch, w_rec_branch, w_mix_out,
           rec_norm_g, rec_lb_logits, ffn_norm_g, router_group_w, router_group_b, router_expert_w,
           router_expert_b, expert_w_gate, expert_w_up, expert_w_down, final_norm_g):
    bsz, seq, d = x.shape
    t = bsz * seq
    depth = ada_w.shape[0]
    assert depth == 1, "final norm is fused after the single layer"
    lower_bounds = jnp.cumsum(jax.nn.softmax(rec_lb_logits.astype(F32), axis=0), axis=0)
    pos_streams = positions.reshape(bsz, seq // STREAMS, STREAMS).transpose(0, 2, 1)
    rope = _rope_tables(pos_streams)
    h = x.reshape(t, d)
    for layer in range(depth):
        mod = _modulation(c, ada_w[layer], ada_b[layer])
        sh_m, sc_m, gt_m, sh_f, sc_f, gt_f = jnp.split(mod, 6, axis=-1)
        h, u = _norm_modulate(h, mix_norm_g[layer], sh_m, sc_m, seq)
        w = w_in[layer].astype(BF16)
        qk = _project(u, w, 0, 2 * d, BF16, rope=rope)
        rest = _project(u, w, 2 * d, 2 * d, BF16)
        f_r = _project(u, w, 4 * d, d, F32)
        tail = _project(u, w, 5 * d, 4 * d, BF16)
        q_a, k_a = qk[:, :d], qk[:, d:]
        v_a, q_r = rest[:, :d], rest[:, d:]
        i_r, g_r, gate_a, gate_r = (tail[:, k * d:(k + 1) * d] for k in range(4))
        y_attn = _attention(q_a, k_a, v_a, bsz, seq)
        y_rec = _recurrence(q_r, f_r, i_r, g_r, lower_bounds[layer], rec_norm_g[layer], bsz, seq)
        merged = _merge(y_attn, y_rec, w_attn_branch[layer].astype(BF16),
                        w_rec_branch[layer].astype(BF16), gate_a, gate_r)
        rw = jnp.concatenate([router_group_w[layer], router_expert_w[layer],
                              jnp.zeros((d, LANES - N_GROUPS - N_EXPERTS), F32)], axis=1)
        rb = jnp.concatenate([router_group_b[layer], router_expert_b[layer],
                              jnp.zeros((LANES - N_GROUPS - N_EXPERTS,), F32)]).reshape(1, LANES)
        h, u2, route = _mix_out(merged, w_mix_out[layer].astype(BF16), h, gt_m, ffn_norm_g[layer],
                                sh_f, sc_f, rw, rb, seq)
        expert_idx = route[:, :TOP_K].astype(jnp.int32)
        weights = route[:, TOP_K:2 * TOP_K]
        tok, w_slot, dest, block_expert, n_used = _dispatch_plan(expert_idx, weights)
        xs = u2[tok]
        ys = _expert_ffn(xs, expert_w_gate[layer].astype(BF16), expert_w_up[layer].astype(BF16),
                         expert_w_down[layer].astype(BF16), w_slot.reshape(-1, 1), block_expert, n_used)
        ffn = ys[dest[:, 0]] + ys[dest[:, 1]]
        h = _final(h, ffn, gt_f, final_norm_g, seq)
    return h.reshape(bsz, seq, d)
```

```python
import functools

import jax
import jax.numpy as jnp
from jax import lax
from jax.experimental import pallas as pl
from jax.experimental.pallas import tpu as pltpu
from jax.experimental.pallas import tpu_sc as plsc

F32 = jnp.float32
BF16 = jnp.bfloat16

D_MODEL = 2048
HEAD_DIM = 128
N_HEADS = D_MODEL // HEAD_DIM
ROPE_DIM = HEAD_DIM // 4
ROPE_HALF = ROPE_DIM // 2
ROPE_THETA = 500000.0
ATTN_SPAN = 128
ATTN_BLOCK = 128
REC_CHUNK = 64
N_GROUPS = 4
EXPERTS_PER_GROUP = 8
N_EXPERTS = N_GROUPS * EXPERTS_PER_GROUP
EXPERT_HIDDEN = D_MODEL // 2
TOP_K = 2
NORM_EPS = 1e-6
IN_WIDTH = 9 * D_MODEL

LANES = 128
VMEM_LIMIT = 56 * 1024 * 1024

MOE_ROWS = 256
HEADS_PER_STEP = 2
GATHER_CHUNKS = 4
STREAMS = 4
DILATIONS = ((128, 1), (512, 4), (2048, 16))


def _cparams(sem):
    return pltpu.CompilerParams(dimension_semantics=sem, vmem_limit_bytes=VMEM_LIMIT)


def _sigmoid(x):
    return 1.0 / (1.0 + jnp.exp(-x))


def _silu(x):
    return x * _sigmoid(x)


def _pack_bf16_pairs(x):
    w = x.shape[1] // 2
    lo = lax.bitcast_convert_type(x[:, :w].astype(BF16).astype(F32), jnp.uint32)
    hi = lax.bitcast_convert_type(x[:, w:].astype(BF16).astype(F32), jnp.uint32)
    return (lo >> 16) | (hi & jnp.uint32(0xFFFF0000))


def _unpack_bf16_pairs(words):
    lo = lax.bitcast_convert_type(words << 16, F32)
    hi = lax.bitcast_convert_type(words & jnp.uint32(0xFFFF0000), F32)
    return jnp.concatenate([lo, hi], axis=1)


def _mod_kernel(c_ref, w_ref, b_ref, o_ref):
    cond = _silu(c_ref[...])
    o_ref[...] = jnp.dot(cond, w_ref[...], precision=lax.Precision.HIGHEST,
                         preferred_element_type=F32) + b_ref[...]


def _modulation(c, w, b):
    bsz, d = c.shape
    n = w.shape[1]
    tn = 1536
    return pl.pallas_call(
        _mod_kernel,
        out_shape=jax.ShapeDtypeStruct((bsz, n), F32),
        grid=(n // tn,),
        in_specs=[pl.BlockSpec((bsz, d), lambda j: (0, 0)),
                  pl.BlockSpec((d, tn), lambda j: (0, j)),
                  pl.BlockSpec((1, tn), lambda j: (0, j))],
        out_specs=pl.BlockSpec((bsz, tn), lambda j: (0, j)),
        compiler_params=_cparams(("arbitrary",)),
        name="adaln_mod",
    )(c, w, b.reshape(1, n))


def _rope_kernel(pos_ref, freq_ref, c_ref, sa_ref, sb_ref):
    ang = pos_ref[...] * freq_ref[...]
    lane = lax.broadcasted_iota(jnp.int32, ang.shape, 1)
    cos, sin = jnp.cos(ang), jnp.sin(ang)
    c_ref[...] = jnp.where(lane < ROPE_DIM, cos, 1.0)
    sa_ref[...] = jnp.where(lane < ROPE_HALF, -sin, 0.0)
    sb_ref[...] = jnp.where((lane >= ROPE_HALF) & (lane < ROPE_DIM), sin, 0.0)


def _rope_tables(positions):
    t = positions.size
    tm = 2048
    inv_freq = ROPE_THETA ** (-jnp.arange(0, ROPE_DIM, 2, dtype=F32) / ROPE_DIM)
    freq = jnp.concatenate([inv_freq, inv_freq, jnp.zeros((LANES - ROPE_DIM,), F32)]).reshape(1, LANES)
    pos = positions.astype(F32).reshape(t, 1)
    out = jax.ShapeDtypeStruct((t, LANES), F32)
    return pl.pallas_call(
        _rope_kernel,
        out_shape=(out, out, out),
        grid=(t // tm,),
        in_specs=[pl.BlockSpec((tm, 1), lambda i: (i, 0)),
                  pl.BlockSpec((1, LANES), lambda i: (0, 0))],
        out_specs=tuple(pl.BlockSpec((tm, LANES), lambda i: (i, 0)) for _ in range(3)),
        compiler_params=_cparams(("arbitrary",)),
        name="rope_tables",
    )(pos, freq)


def _norm_mod_kernel(x_ref, g_ref, sh_ref, sc_ref, xp_ref, u_ref, slab_ref):
    rows = x_ref.shape[0]
    sub = rows // STREAMS
    nslab = x_ref.shape[1] // LANES
    for c in range(nslab):
        slab_ref[c] = x_ref[:, c * LANES:(c + 1) * LANES]
    for j in range(STREAMS):
        x = jnp.concatenate([slab_ref[c, pl.ds(j, sub, stride=STREAMS), :] for c in range(nslab)], axis=1)
        xp_ref[0, j] = x
        y = x * lax.rsqrt(jnp.mean(x * x, axis=-1, keepdims=True) + NORM_EPS) * g_ref[...]
        u_ref[0, j] = (y * (1.0 + sc_ref[0]) + sh_ref[0]).astype(u_ref.dtype)


def _norm_modulate(x2d, g, shift, scale, seq):
    t, d = x2d.shape
    tm = 512
    per_b = seq // tm
    bsz = shift.shape[0]
    sub = tm // STREAMS
    out_spec = pl.BlockSpec((1, STREAMS, sub, d), lambda i: (i // per_b, 0, i % per_b, 0))
    xp, u = pl.pallas_call(
        _norm_mod_kernel,
        out_shape=(jax.ShapeDtypeStruct((bsz, STREAMS, seq // STREAMS, d), F32),
                   jax.ShapeDtypeStruct((bsz, STREAMS, seq // STREAMS, d), BF16)),
        grid=(t // tm,),
        in_specs=[pl.BlockSpec((tm, d), lambda i: (i, 0)),
                  pl.BlockSpec((1, d), lambda i: (0, 0)),
                  pl.BlockSpec((1, 1, d), lambda i: (i // per_b, 0, 0)),
                  pl.BlockSpec((1, 1, d), lambda i: (i // per_b, 0, 0))],
        out_specs=(out_spec, out_spec),
        scratch_shapes=[pltpu.VMEM((d // LANES, tm, LANES), F32)],
        compiler_params=_cparams(("arbitrary",)),
        name="norm_modulate",
    )(x2d, g.reshape(1, d), shift.reshape(bsz, 1, d), scale.reshape(bsz, 1, d))
    return xp.reshape(t, d), u.reshape(t, d)


def _proj_kernel(a_ref, w_ref, o_ref):
    o_ref[...] = jnp.dot(a_ref[...], w_ref[...], preferred_element_type=F32).astype(o_ref.dtype)


def _proj_rope_kernel(a_ref, w_ref, c_ref, sa_ref, sb_ref, o_ref):
    acc = jnp.dot(a_ref[...], w_ref[...], preferred_element_type=F32)
    c, sa, sb = c_ref[...], sa_ref[...], sb_ref[...]
    for h in range(acc.shape[1] // HEAD_DIM):
        x = acc[:, h * HEAD_DIM:(h + 1) * HEAD_DIM]
        up = pltpu.roll(x, HEAD_DIM - ROPE_HALF, 1)
        dn = pltpu.roll(x, ROPE_HALF, 1)
        o_ref[:, h * HEAD_DIM:(h + 1) * HEAD_DIM] = (x * c + up * sa + dn * sb).astype(o_ref.dtype)


def _project(u, w, col0, ncols, out_dtype, rope=None):
    t, d = u.shape
    tm, tn = 1024, 512
    j0 = col0 // tn
    in_specs = [pl.BlockSpec((tm, d), lambda i, j: (i, 0)),
                pl.BlockSpec((d, tn), lambda i, j: (0, j0 + j))]
    args = [u, w]
    kern = _proj_kernel
    if rope is not None:
        in_specs += [pl.BlockSpec((tm, LANES), lambda i, j: (i, 0)) for _ in range(3)]
        args += list(rope)
        kern = _proj_rope_kernel
    return pl.pallas_call(
        kern,
        out_shape=jax.ShapeDtypeStruct((t, ncols), out_dtype),
        grid=(t // tm, ncols // tn),
        in_specs=in_specs,
        out_specs=pl.BlockSpec((tm, tn), lambda i, j: (i, j)),
        compiler_params=_cparams(("arbitrary", "arbitrary")),
        name="in_proj_rope" if rope is not None else "in_proj",
    )(*args)


def _key_pieces(qb, sub):
    L = ATTN_BLOCK
    pieces = [(0, 0, L * (qb + 1))]
    for r in range(1, STREAMS):
        back = 0 if qb == 0 else (L // 2 if r == 1 else L // 4)
        pieces.append((r, L * qb - back, L + back))
    return pieces


def _attention_bias(sub):
    L = ATTN_BLOCK
    nqb = sub // L
    kmax = max(sum(p[2] for p in _key_pieces(qb, sub)) for qb in range(nqb))
    qi = jnp.arange(L, dtype=jnp.int32)[:, None]
    out = []
    for j in range(STREAMS):
        row = []
        for qb in range(nqb):
            cols = []
            for r, start, size in _key_pieces(qb, sub):
                jp = (j + r) % STREAMS
                kn = start + jnp.arange(size, dtype=jnp.int32)[None, :]
                dt = STREAMS * (L * qb + qi - kn) + (j - jp)
                cnt = jnp.zeros(dt.shape, F32)
                for window, dil in DILATIONS:
                    cnt += ((dt >= 0) & (dt <= window) & (dt % dil == 0)).astype(F32)
                cols.append(jnp.log2(cnt))
            tile = jnp.concatenate(cols, axis=1)
            row.append(jnp.pad(tile, ((0, 0), (0, kmax - tile.shape[1])), constant_values=-jnp.inf))
        out.append(jnp.stack(row))
    return jnp.stack(out)


def _attn_kernel(q_ref, k_ref, v_ref, bias_ref, o_ref):
    j = pl.program_id(2)
    L = ATTN_BLOCK
    sub = o_ref.shape[1]
    scale = HEAD_DIM ** -0.5 * 1.4426950408889634
    for g in range(HEADS_PER_STEP):
        lanes = slice(g * HEAD_DIM, (g + 1) * HEAD_DIM)
        for qb in range(sub // L):
            rows = []
            for r, start, size in _key_pieces(qb, sub):
                base = pl.multiple_of(((j + r) % STREAMS) * sub, sub)
                rows.append(pl.ds(base + start, size))
            ktot = sum(p[2] for p in _key_pieces(qb, sub))
            q = q_ref[0, pl.ds(pl.multiple_of(j * sub, sub) + L * qb, L), lanes]
            k_all = jnp.concatenate([k_ref[0, rw, lanes] for rw in rows], axis=0)
            v_all = jnp.concatenate([v_ref[0, rw, lanes] for rw in rows], axis=0)
            s = lax.dot_general(q, k_all, (((1,), (1,)), ((), ())), preferred_element_type=F32)
            s = s * scale + bias_ref[j, qb, :, :ktot]
            m = jnp.max(s, axis=-1, keepdims=True)
            p = jnp.exp2(s - m).astype(BF16)
            v_ext = jnp.concatenate([v_all, jnp.ones_like(v_all)], axis=1)
            oe = jnp.dot(p, v_ext, preferred_element_type=F32)
            y = oe[:, :HEAD_DIM] / oe[:, HEAD_DIM:]
            o_ref[0, L * qb:L * (qb + 1), lanes] = y.astype(o_ref.dtype)


def _cols(src, bsz, seq, gw, grid_rank):
    a, col0 = src
    off = col0 // gw
    if grid_rank == 3:
        spec = pl.BlockSpec((1, seq, gw), lambda b, h, j: (b, 0, off + h))
    else:
        spec = pl.BlockSpec((1, seq, gw), lambda b, h: (b, 0, off + h))
    return a.reshape(bsz, seq, a.shape[1]), spec


def _attention(q, k, v, width, bsz, seq):
    t = bsz * seq
    gw = HEADS_PER_STEP * HEAD_DIM
    sub = seq // STREAMS
    bias = _attention_bias(sub)
    (qa, qs), (ka, ks), (va, vs) = (_cols(s, bsz, seq, gw, 3) for s in (q, k, v))
    out = pl.pallas_call(
        _attn_kernel,
        out_shape=jax.ShapeDtypeStruct((bsz, seq, width), BF16),
        grid=(bsz, width // gw, STREAMS),
        in_specs=[qs, ks, vs,
                  pl.BlockSpec(bias.shape, lambda b, h, j: (0, 0, 0, 0))],
        out_specs=pl.BlockSpec((1, sub, gw), lambda b, h, j: (b, j, h)),
        compiler_params=_cparams(("arbitrary", "arbitrary", "arbitrary")),
        name="dilated_attention",
    )(qa, ka, va, bias)
    return out.reshape(t, width)


def _split3(x):
    hi = x.astype(BF16)
    r1 = x - hi.astype(F32)
    mid = r1.astype(BF16)
    lo = (r1 - mid.astype(F32)).astype(BF16)
    return hi, mid, lo


def _rec_kernel(q_ref, f_ref, i_ref, g_ref, lb_ref, ng_ref, o_ref, *st_refs):
    C = REC_CHUNK
    seq = q_ref.shape[1]
    sub = seq // STREAMS
    piece = C // STREAMS
    for st_ref in st_refs:
        st_ref[...] = jnp.zeros_like(st_ref)
    pi = lax.broadcasted_iota(jnp.int32, (C, C), 0)
    si = lax.broadcasted_iota(jnp.int32, (C, C), 1)
    time_of = lambda p: STREAMS * (p % piece) + p // piece
    causal = time_of(si) <= time_of(pi)
    tri = jnp.where(causal, 1.0, 0.0).astype(BF16)
    row_of = lambda tau: (tau % STREAMS) * piece + tau // STREAMS
    last = row_of(C - 1)
    anchor = row_of(C // 2 - 1)

    def load(ref, c, lanes):
        return jnp.concatenate(
            [ref[0, pl.ds(pl.multiple_of(j * sub + c * piece, piece), piece), lanes] for j in range(STREAMS)],
            axis=0)

    def body(c, carry):
        for g in range(HEADS_PER_STEP):
            st_ref = st_refs[g]
            lanes = slice(g * HEAD_DIM, (g + 1) * HEAD_DIM)
            lb = lb_ref[:, lanes]
            x = load(f_ref, c, lanes)
            z = jnp.exp(-jnp.abs(x))
            r = 1.0 / (1.0 + z)
            pos = x >= 0
            sig_p = jnp.where(pos, r, z * r)
            sig_n = jnp.where(pos, z * r, r)
            logf = jnp.log(lb + (1.0 - lb) * sig_p)
            kk = (1.0 - lb) * sig_n
            hi, mid, lo = _split3(logf)
            b = (jnp.dot(tri, hi, preferred_element_type=F32)
                 + jnp.dot(tri, mid, preferred_element_type=F32)
                 + jnp.dot(tri, lo, preferred_element_type=F32))
            b_last = b[last:last + 1, :]
            b_mid = b[anchor:anchor + 1, :]
            qs = _silu(load(q_ref, c, lanes).astype(F32))
            v = load(i_ref, c, lanes)
            q_in = (qs * jnp.exp(b)).astype(BF16)
            q_a = (qs * jnp.exp(b - b_mid)).astype(BF16)
            k_a = (kk * jnp.exp(b_mid - b)).astype(BF16)
            k_e = (kk * jnp.exp(b_last - b)).astype(BF16)
            sc = lax.dot_general(q_a, k_a, (((1,), (1,)), ((), ())), preferred_element_type=F32)
            sc = jnp.where(causal, sc, 0.0).astype(BF16)
            st = st_ref[...]
            o = (jnp.dot(sc, v, preferred_element_type=F32)
                 + lax.dot_general(q_in, st.astype(BF16), (((1,), (1,)), ((), ())),
                                   preferred_element_type=F32))
            upd = lax.dot_general(v, k_e, (((0,), (0,)), ((), ())), preferred_element_type=F32)
            st_ref[...] = jnp.exp(b_last) * st + upd
            y = o * lax.rsqrt(jnp.mean(o * o, axis=-1, keepdims=True) + NORM_EPS) * ng_ref[...]
            y = (y * _silu(load(g_ref, c, lanes).astype(F32))).astype(o_ref.dtype)
            for j in range(STREAMS):
                o_ref[0, pl.ds(pl.multiple_of(j * sub + c * piece, piece), piece), lanes] = (
                    y[j * piece:(j + 1) * piece])
        return carry

    lax.fori_loop(0, seq // C, body, 0, unroll=2)


def _recurrence(q_r, f_r, i_r, g_r, lower_bound, norm_g, bsz, seq):
    t = bsz * seq
    width = lower_bound.shape[0]
    gw = HEADS_PER_STEP * HEAD_DIM
    spec = pl.BlockSpec((1, seq, gw), lambda b, h: (b, 0, h))
    (qa, qs), (fa, fs), (ia, isp), (ga, gs) = (_cols(s, bsz, seq, gw, 2) for s in (q_r, f_r, i_r, g_r))
    out = pl.pallas_call(
        _rec_kernel,
        out_shape=jax.ShapeDtypeStruct((bsz, seq, width), BF16),
        grid=(bsz, width // gw),
        in_specs=[qs, fs, isp, gs,
                  pl.BlockSpec((1, gw), lambda b, h: (0, h)),
                  pl.BlockSpec((1, HEAD_DIM), lambda b, h: (0, 0))],
        out_specs=spec,
        scratch_shapes=[pltpu.VMEM((HEAD_DIM, HEAD_DIM), F32) for _ in range(HEADS_PER_STEP)],
        compiler_params=_cparams(("arbitrary", "arbitrary")),
        name="hgrn2_recurrence",
    )(qa, fa, ia, ga, lower_bound.reshape(1, width), norm_g.reshape(1, HEAD_DIM))
    return out.reshape(t, width)


def _merge_kernel(ya_ref, yr_ref, wa_ref, wr_ref, ga_ref, gr_ref, o_ref):
    a = jnp.dot(ya_ref[...], wa_ref[...], preferred_element_type=F32)
    r = jnp.dot(yr_ref[...], wr_ref[...], preferred_element_type=F32)
    m = _sigmoid(ga_ref[...].astype(F32)) * a + _sigmoid(gr_ref[...].astype(F32)) * r
    o_ref[...] = m.astype(o_ref.dtype)


def _merge(ya, yr, wa, wr, ga, gr):
    t, d = ya.shape
    n = wa.shape[1]
    tm, tn = 1024, 512
    row = pl.BlockSpec((tm, d), lambda i, j: (i, 0))
    col = pl.BlockSpec((d, tn), lambda i, j: (0, j))
    tile = pl.BlockSpec((tm, tn), lambda i, j: (i, j))
    ga_off, gr_off = ga[1] // tn, gr[1] // tn
    return pl.pallas_call(
        _merge_kernel,
        out_shape=jax.ShapeDtypeStruct((t, n), BF16),
        grid=(t // tm, n // tn),
        in_specs=[row, row, col, col,
                  pl.BlockSpec((tm, tn), lambda i, j: (i, ga_off + j)),
                  pl.BlockSpec((tm, tn), lambda i, j: (i, gr_off + j))],
        out_specs=tile,
        compiler_params=_cparams(("arbitrary", "arbitrary")),
        name="branch_merge",
    )(ya, yr, wa, wr, ga[0], gr[0])


def _mixout_kernel(m_ref, w_ref, x_ref, gt_ref, g_ref, sh_ref, sc_ref, rw_ref, rb_ref,
                   h_ref, *rest):
    u_refs, route_ref = rest[:-1], rest[-1]
    mix = jnp.dot(m_ref[...], w_ref[...], preferred_element_type=F32)
    h = x_ref[...] + gt_ref[0] * mix
    h_ref[...] = h
    u = h * lax.rsqrt(jnp.mean(h * h, axis=-1, keepdims=True) + NORM_EPS) * g_ref[...]
    u = u * (1.0 + sc_ref[0]) + sh_ref[0]
    wc = 2 * u_refs[0].shape[1]
    for c, u_ref in enumerate(u_refs):
        u_ref[...] = _pack_bf16_pairs(u[:, c * wc:(c + 1) * wc])
    u_hi = u.astype(BF16)
    u_lo = (u - u_hi.astype(F32)).astype(BF16)
    rw = rw_ref[...]
    w_hi = rw.astype(BF16)
    w_lo = (rw - w_hi.astype(F32)).astype(BF16)
    logits = (jnp.dot(u_hi, w_hi, preferred_element_type=F32)
              + jnp.dot(u_lo, w_hi, preferred_element_type=F32)
              + jnp.dot(u_hi, w_lo, preferred_element_type=F32)) + rb_ref[...]
    lane = lax.broadcasted_iota(jnp.int32, logits.shape, 1).astype(F32)
    big = float(LANES)
    neg = -jnp.inf
    lg = jnp.where(lane < N_GROUPS, logits, neg)
    mg = jnp.max(lg, axis=-1, keepdims=True)
    g_sel = jnp.min(jnp.where(lg == mg, lane, big), axis=-1, keepdims=True)
    p_group = 1.0 / jnp.sum(jnp.exp(lg - mg), axis=-1, keepdims=True)
    lo = N_GROUPS + EXPERTS_PER_GROUP * g_sel
    le = jnp.where((lane >= lo) & (lane < lo + EXPERTS_PER_GROUP), logits, neg)
    t1 = jnp.max(le, axis=-1, keepdims=True)
    i1 = jnp.min(jnp.where(le == t1, lane, big), axis=-1, keepdims=True)
    le2 = jnp.where(lane == i1, neg, le)
    t2 = jnp.max(le2, axis=-1, keepdims=True)
    i2 = jnp.min(jnp.where(le2 == t2, lane, big), axis=-1, keepdims=True)
    e21 = jnp.exp(t2 - t1)
    w1 = p_group / (1.0 + e21)
    w2 = p_group * e21 / (1.0 + e21)
    route = jnp.where(lane == 0, i1 - N_GROUPS,
                      jnp.where(lane == 1, i2 - N_GROUPS,
                                jnp.where(lane == 2, w1, jnp.where(lane == 3, w2, 0.0))))
    route_ref[...] = route


def _mix_out(merged, w_out, x2d, gate, g, shift, scale, rw, rb, seq):
    t, d = x2d.shape
    tm = 256
    per_b = seq // tm
    bsz = gate.shape[0]
    row = lambda dt: pl.BlockSpec((tm, d), lambda i: (i, 0))
    per_batch = pl.BlockSpec((1, 1, d), lambda i: (i // per_b, 0, 0))
    const = lambda shape: pl.BlockSpec(shape, lambda i: (0,) * len(shape))
    wc = d // GATHER_CHUNKS // 2
    outs = pl.pallas_call(
        _mixout_kernel,
        out_shape=(jax.ShapeDtypeStruct((t, d), F32),
                   *[jax.ShapeDtypeStruct((t, wc), jnp.uint32) for _ in range(GATHER_CHUNKS)],
                   jax.ShapeDtypeStruct((t, LANES), F32)),
        grid=(t // tm,),
        in_specs=[row(BF16), const((d, d)), row(F32), per_batch, const((1, d)), per_batch, per_batch,
                  const((d, LANES)), const((1, LANES))],
        out_specs=(row(F32), *[pl.BlockSpec((tm, wc), lambda i: (i, 0)) for _ in range(GATHER_CHUNKS)],
                   pl.BlockSpec((tm, LANES), lambda i: (i, 0))),
        compiler_params=_cparams(("arbitrary",)),
        name="mix_out_router",
    )(merged, w_out, x2d, gate.reshape(bsz, 1, d), g.reshape(1, d),
      shift.reshape(bsz, 1, d), scale.reshape(bsz, 1, d), rw, rb)
    return outs[0], list(outs[1:-1]), outs[-1]


def _expert_kernel(be_ref, nu_ref, *refs):
    x_refs = refs[:GATHER_CHUNKS]
    wg_ref, wu_ref, wd_ref, ws_ref = refs[GATHER_CHUNKS:GATHER_CHUNKS + 4]
    o_refs = refs[GATHER_CHUNKS + 4:]
    wc = 2 * o_refs[0].shape[1]
    i = pl.program_id(0)

    @pl.when(i < nu_ref[0])
    def _():
        x = jnp.concatenate([_unpack_bf16_pairs(r[...]) for r in x_refs], axis=1).astype(BF16)
        hg = jnp.dot(x, wg_ref[0], preferred_element_type=F32)
        hu = jnp.dot(x, wu_ref[0], preferred_element_type=F32)
        hdn = (_silu(hg) * hu).astype(BF16)
        y = jnp.dot(hdn, wd_ref[0], preferred_element_type=F32)
        y = y * ws_ref[...]
        for c, o_ref in enumerate(o_refs):
            o_ref[...] = _pack_bf16_pairs(y[:, c * wc:(c + 1) * wc])

    @pl.when(i >= nu_ref[0])
    def _():
        for o_ref in o_refs:
            o_ref[...] = jnp.zeros_like(o_ref)


def _expert_ffn(xs, w_gate, w_up, w_down, w_slot, block_expert, n_used):
    n_slots, wc = xs[0].shape
    d = 2 * wc * GATHER_CHUNKS
    hid = w_gate.shape[2]
    bm = MOE_ROWS
    chunk = pl.BlockSpec((bm, wc), lambda i, be, nu: (i, 0))
    grid_spec = pltpu.PrefetchScalarGridSpec(
        num_scalar_prefetch=2,
        grid=(n_slots // bm,),
        in_specs=[*[chunk] * GATHER_CHUNKS,
                  pl.BlockSpec((1, d, hid), lambda i, be, nu: (be[i], 0, 0)),
                  pl.BlockSpec((1, d, hid), lambda i, be, nu: (be[i], 0, 0)),
                  pl.BlockSpec((1, hid, d), lambda i, be, nu: (be[i], 0, 0)),
                  pl.BlockSpec((bm, 1), lambda i, be, nu: (i, 0))],
        out_specs=[chunk] * GATHER_CHUNKS,
    )
    return pl.pallas_call(
        _expert_kernel,
        out_shape=[jax.ShapeDtypeStruct((n_slots, wc), jnp.uint32) for _ in range(GATHER_CHUNKS)],
        grid_spec=grid_spec,
        compiler_params=_cparams(("arbitrary",)),
        name="expert_ffn",
    )(block_expert, n_used, *xs, w_gate, w_up, w_down, w_slot)


def _gather_rows(table, idx):
    m = idx.shape[0]
    d = table.shape[1]
    window = LANES
    mesh = plsc.VectorSubcoreMesh(core_axis_name="core", subcore_axis_name="subcore")

    @pl.kernel(out_type=jax.ShapeDtypeStruct((m, d), table.dtype), mesh=mesh, scratch_types=[])
    def gather(x_hbm, i_hbm, o_hbm):
        def body(i_vmem, o_vmem):
            pltpu.sync_copy(x_hbm.at[i_vmem.at[0]], o_vmem)

        pltpu.emit_pipeline(
            body,
            grid=(m // window,),
            in_specs=[pl.BlockSpec((1, window), lambda i: (0, i))],
            out_specs=[pl.BlockSpec((window, d), lambda i: (i, 0))],
            core_axis_name=("core", "subcore"),
            dimension_semantics=(pltpu.PARALLEL,),
        )(i_hbm, o_hbm)

    return gather(table, idx.reshape(1, m))


def _final_kernel(h_ref, *rest):
    y_refs = rest[:TOP_K * GATHER_CHUNKS]
    gt_ref, g_ref, o_ref, slab_ref = rest[TOP_K * GATHER_CHUNKS:]
    sub = h_ref.shape[2]
    nslab = h_ref.shape[3] // LANES
    for j in range(STREAMS):
        ffn = sum(jnp.concatenate([_unpack_bf16_pairs(y_refs[k * GATHER_CHUNKS + c][0, 0, j])
                                   for c in range(GATHER_CHUNKS)], axis=1) for k in range(TOP_K))
        h = h_ref[0, j] + gt_ref[0] * ffn
        y = h * lax.rsqrt(jnp.mean(h * h, axis=-1, keepdims=True) + NORM_EPS) * g_ref[...]
        for c in range(nslab):
            slab_ref[c, pl.ds(j, sub, stride=STREAMS), :] = y[:, c * LANES:(c + 1) * LANES]
    for c in range(nslab):
        o_ref[:, c * LANES:(c + 1) * LANES] = slab_ref[c]


def _final(h, y2, gate, g, seq):
    t, d = h.shape
    tm = 512
    per_b = seq // tm
    bsz = gate.shape[0]
    sub = tm // STREAMS
    spec = pl.BlockSpec((1, STREAMS, sub, d), lambda i: (i // per_b, 0, i % per_b, 0))
    wc = d // GATHER_CHUNKS // 2
    y5 = [a.reshape(TOP_K, bsz, STREAMS, seq // STREAMS, wc) for a in y2]

    def yspec(k):
        return pl.BlockSpec((1, 1, STREAMS, sub, wc), lambda i: (k, i // per_b, 0, i % per_b, 0))

    return pl.pallas_call(
        _final_kernel,
        out_shape=jax.ShapeDtypeStruct((t, d), F32),
        grid=(t // tm,),
        in_specs=[spec, *[yspec(k) for k in range(TOP_K) for _ in range(GATHER_CHUNKS)],
                  pl.BlockSpec((1, 1, d), lambda i: (i // per_b, 0, 0)),
                  pl.BlockSpec((1, d), lambda i: (0, 0))],
        out_specs=pl.BlockSpec((tm, d), lambda i: (i, 0)),
        scratch_shapes=[pltpu.VMEM((d // LANES, tm, LANES), F32)],
        compiler_params=_cparams(("arbitrary",)),
        name="final_norm",
    )(h.reshape(bsz, STREAMS, seq // STREAMS, d), *(y5 * TOP_K), gate.reshape(bsz, 1, d), g.reshape(1, d))


def _dispatch_plan(expert_idx, weights):
    n_assign = expert_idx.size
    n_blocks = n_assign // MOE_ROWS + N_EXPERTS
    n_slots = n_blocks * MOE_ROWS
    flat_e = expert_idx.reshape(-1)
    flat_w = weights.reshape(-1)
    ids = jnp.arange(n_assign, dtype=jnp.int32)
    eids = jnp.arange(N_EXPERTS, dtype=jnp.int32)[None, :]
    _, order = lax.sort((flat_e, ids), num_keys=1, is_stable=True)
    _, rank_sorted = lax.sort((order, ids), num_keys=1)
    hot_a = (flat_e[:, None] == eids).astype(jnp.int32)
    counts = jnp.sum(hot_a, axis=0)
    padded = ((counts + MOE_ROWS - 1) // MOE_ROWS) * MOE_ROWS
    pad_end = jnp.cumsum(padded)
    pad_start = pad_end - padded
    start = jnp.cumsum(counts) - counts
    dest = (rank_sorted + jnp.sum(hot_a * (pad_start - start)[None, :], axis=1)).reshape(-1, TOP_K)
    blk0 = jnp.arange(n_blocks, dtype=jnp.int32) * MOE_ROWS
    block_expert = jnp.minimum(jnp.sum((pad_end[None, :] <= blk0[:, None]).astype(jnp.int32), axis=1),
                               N_EXPERTS - 1)
    hot_b = (block_expert[:, None] == eids).astype(jnp.int32)
    blk_shift = jnp.sum(hot_b * (start - pad_start)[None, :], axis=1)
    blk_count = jnp.sum(hot_b * (pad_start + counts)[None, :], axis=1)
    slot = jnp.arange(n_slots, dtype=jnp.int32).reshape(n_blocks, MOE_ROWS)
    valid = (slot < blk_count[:, None]).reshape(-1)
    src = jnp.clip(slot + blk_shift[:, None], 0, n_assign - 1).reshape(-1)
    assign = order[src]
    token_of_slot = jnp.where(valid, assign // TOP_K, 0)
    weight_of_slot = jnp.where(valid, flat_w[assign], 0.0)
    n_used = (pad_end[-1:] // MOE_ROWS).astype(jnp.int32)
    return token_of_slot, weight_of_slot, dest, block_expert, n_used


def kernel(x, c, positions, ada_w, ada_b, mix_norm_g, w_in, w_attn_branch, w_rec_branch, w_mix_out,
           rec_norm_g, rec_lb_logits, ffn_norm_g, router_group_w, router_group_b, router_expert_w,
           router_expert_b, expert_w_gate, expert_w_up, expert_w_down, final_norm_g):
    bsz, seq, d = x.shape
    t = bsz * seq
    depth = ada_w.shape[0]
    assert depth == 1, "final norm is fused after the single layer"
    lower_bounds = jnp.cumsum(jax.nn.softmax(rec_lb_logits.astype(F32), axis=0), axis=0)
    pos_streams = positions.reshape(bsz, seq // STREAMS, STREAMS).transpose(0, 2, 1)
    rope = _rope_tables(pos_streams)
    h = x.reshape(t, d)
    for layer in range(depth):
        mod = _modulation(c, ada_w[layer], ada_b[layer])
        sh_m, sc_m, gt_m, sh_f, sc_f, gt_f = jnp.split(mod, 6, axis=-1)
        h, u = _norm_modulate(h, mix_norm_g[layer], sh_m, sc_m, seq)
        w = w_in[layer].astype(BF16)
        qk = _project(u, w, 0, 2 * d, BF16, rope=rope)
        rest = _project(u, w, 2 * d, 2 * d, BF16)
        f_r = _project(u, w, 4 * d, d, F32)
        tail = _project(u, w, 5 * d, 4 * d, BF16)
        y_attn = _attention((qk, 0), (qk, d), (rest, 0), d, bsz, seq)
        y_rec = _recurrence((rest, d), (f_r, 0), (tail, 0), (tail, d), lower_bounds[layer],
                            rec_norm_g[layer], bsz, seq)
        merged = _merge(y_attn, y_rec, w_attn_branch[layer].astype(BF16),
                        w_rec_branch[layer].astype(BF16), (tail, 2 * d), (tail, 3 * d))
        rw = jnp.concatenate([router_group_w[layer], router_expert_w[layer],
                              jnp.zeros((d, LANES - N_GROUPS - N_EXPERTS), F32)], axis=1)
        rb = jnp.concatenate([router_group_b[layer], router_expert_b[layer],
                              jnp.zeros((LANES - N_GROUPS - N_EXPERTS,), F32)]).reshape(1, LANES)
        h, u2, route = _mix_out(merged, w_mix_out[layer].astype(BF16), h, gt_m, ffn_norm_g[layer],
                                sh_f, sc_f, rw, rb, seq)
        expert_idx = route[:, :TOP_K].astype(jnp.int32)
        weights = route[:, TOP_K:2 * TOP_K]
        tok, w_slot, dest, block_expert, n_used = _dispatch_plan(expert_idx, weights)
        xs = [_gather_rows(a, tok) for a in u2]
        ys = _expert_ffn(xs, expert_w_gate[layer].astype(BF16), expert_w_up[layer].astype(BF16),
                         expert_w_down[layer].astype(BF16), w_slot.reshape(-1, 1), block_expert, n_used)
        dest_kt = dest.T.reshape(-1)
        y2 = [_gather_rows(a, dest_kt) for a in ys]
        h = _final(h, y2, gt_f, final_norm_g, seq)
    return h.reshape(bsz, seq, d)
```

```python
import functools

import jax
import jax.numpy as jnp
from jax import lax
from jax.experimental import pallas as pl
from jax.experimental.pallas import tpu as pltpu
from jax.experimental.pallas import tpu_sc as plsc

F32 = jnp.float32
BF16 = jnp.bfloat16

D_MODEL = 2048
HEAD_DIM = 128
N_HEADS = D_MODEL // HEAD_DIM
ROPE_DIM = HEAD_DIM // 4
ROPE_HALF = ROPE_DIM // 2
ROPE_THETA = 500000.0
ATTN_SPAN = 128
ATTN_BLOCK = 128
REC_CHUNK = 64
N_GROUPS = 4
EXPERTS_PER_GROUP = 8
N_EXPERTS = N_GROUPS * EXPERTS_PER_GROUP
EXPERT_HIDDEN = D_MODEL // 2
TOP_K = 2
NORM_EPS = 1e-6
IN_WIDTH = 9 * D_MODEL

LANES = 128
VMEM_LIMIT = 56 * 1024 * 1024

MOE_ROWS = 256
HEADS_PER_STEP = 2
GATHER_CHUNKS = 4
STREAMS = 4
DILATIONS = ((128, 1), (512, 4), (2048, 16))


def _cparams(sem):
    return pltpu.CompilerParams(dimension_semantics=sem, vmem_limit_bytes=VMEM_LIMIT)


def _sigmoid(x):
    return 1.0 / (1.0 + jnp.exp(-x))


def _silu(x):
    return x * _sigmoid(x)


def _pack_bf16_pairs(x):
    w = x.shape[1] // 2
    lo = lax.bitcast_convert_type(x[:, :w].astype(BF16).astype(F32), jnp.uint32)
    hi = lax.bitcast_convert_type(x[:, w:].astype(BF16).astype(F32), jnp.uint32)
    return (lo >> 16) | (hi & jnp.uint32(0xFFFF0000))


def _unpack_bf16_pairs(words):
    lo = lax.bitcast_convert_type(words << 16, F32)
    hi = lax.bitcast_convert_type(words & jnp.uint32(0xFFFF0000), F32)
    return jnp.concatenate([lo, hi], axis=1)


def _mod_kernel(c_ref, w_ref, b_ref, o_ref):
    cond = _silu(c_ref[...])
    o_ref[...] = jnp.dot(cond, w_ref[...], precision=lax.Precision.HIGHEST,
                         preferred_element_type=F32) + b_ref[...]


def _modulation(c, w, b):
    bsz, d = c.shape
    n = w.shape[1]
    tn = 1536
    return pl.pallas_call(
        _mod_kernel,
        out_shape=jax.ShapeDtypeStruct((bsz, n), F32),
        grid=(n // tn,),
        in_specs=[pl.BlockSpec((bsz, d), lambda j: (0, 0)),
                  pl.BlockSpec((d, tn), lambda j: (0, j)),
                  pl.BlockSpec((1, tn), lambda j: (0, j))],
        out_specs=pl.BlockSpec((bsz, tn), lambda j: (0, j)),
        compiler_params=_cparams(("arbitrary",)),
        name="adaln_mod",
    )(c, w, b.reshape(1, n))


def _rope_kernel(pos_ref, freq_ref, c_ref, sa_ref, sb_ref):
    ang = pos_ref[...] * freq_ref[...]
    lane = lax.broadcasted_iota(jnp.int32, ang.shape, 1)
    cos, sin = jnp.cos(ang), jnp.sin(ang)
    c_ref[...] = jnp.where(lane < ROPE_DIM, cos, 1.0)
    sa_ref[...] = jnp.where(lane < ROPE_HALF, -sin, 0.0)
    sb_ref[...] = jnp.where((lane >= ROPE_HALF) & (lane < ROPE_DIM), sin, 0.0)


def _rope_tables(positions):
    t = positions.size
    tm = 2048
    inv_freq = ROPE_THETA ** (-jnp.arange(0, ROPE_DIM, 2, dtype=F32) / ROPE_DIM)
    freq = jnp.concatenate([inv_freq, inv_freq, jnp.zeros((LANES - ROPE_DIM,), F32)]).reshape(1, LANES)
    pos = positions.astype(F32).reshape(t, 1)
    out = jax.ShapeDtypeStruct((t, LANES), F32)
    return pl.pallas_call(
        _rope_kernel,
        out_shape=(out, out, out),
        grid=(t // tm,),
        in_specs=[pl.BlockSpec((tm, 1), lambda i: (i, 0)),
                  pl.BlockSpec((1, LANES), lambda i: (0, 0))],
        out_specs=tuple(pl.BlockSpec((tm, LANES), lambda i: (i, 0)) for _ in range(3)),
        compiler_params=_cparams(("arbitrary",)),
        name="rope_tables",
    )(pos, freq)


def _norm_mod_kernel(x_ref, g_ref, sh_ref, sc_ref, xp_ref, u_ref, slab_ref):
    rows = x_ref.shape[0]
    sub = rows // STREAMS
    nslab = x_ref.shape[1] // LANES
    for c in range(nslab):
        slab_ref[c] = x_ref[:, c * LANES:(c + 1) * LANES]
    for j in range(STREAMS):
        x = jnp.concatenate([slab_ref[c, pl.ds(j, sub, stride=STREAMS), :] for c in range(nslab)], axis=1)
        xp_ref[0, j] = x
        y = x * lax.rsqrt(jnp.mean(x * x, axis=-1, keepdims=True) + NORM_EPS) * g_ref[...]
        u_ref[0, j] = (y * (1.0 + sc_ref[0]) + sh_ref[0]).astype(u_ref.dtype)


def _norm_modulate(x2d, g, shift, scale, seq):
    t, d = x2d.shape
    tm = 512
    per_b = seq // tm
    bsz = shift.shape[0]
    sub = tm // STREAMS
    out_spec = pl.BlockSpec((1, STREAMS, sub, d), lambda i: (i // per_b, 0, i % per_b, 0))
    xp, u = pl.pallas_call(
        _norm_mod_kernel,
        out_shape=(jax.ShapeDtypeStruct((bsz, STREAMS, seq // STREAMS, d), F32),
                   jax.ShapeDtypeStruct((bsz, STREAMS, seq // STREAMS, d), BF16)),
        grid=(t // tm,),
        in_specs=[pl.BlockSpec((tm, d), lambda i: (i, 0)),
                  pl.BlockSpec((1, d), lambda i: (0, 0)),
                  pl.BlockSpec((1, 1, d), lambda i: (i // per_b, 0, 0)),
                  pl.BlockSpec((1, 1, d), lambda i: (i // per_b, 0, 0))],
        out_specs=(out_spec, out_spec),
        scratch_shapes=[pltpu.VMEM((d // LANES, tm, LANES), F32)],
        compiler_params=_cparams(("arbitrary",)),
        name="norm_modulate",
    )(x2d, g.reshape(1, d), shift.reshape(bsz, 1, d), scale.reshape(bsz, 1, d))
    return xp.reshape(t, d), u.reshape(t, d)


def _cast_weight_tile(w_ref, wb_ref):
    rows = 256

    def body(r, carry):
        sl = pl.ds(pl.multiple_of(r * rows, rows), rows)
        wb_ref[sl, :] = w_ref[sl, :].astype(BF16)
        return carry

    lax.fori_loop(0, w_ref.shape[0] // rows, body, 0)


def _proj_kernel(a_ref, w_ref, o_ref, wb_ref):
    @pl.when(pl.program_id(1) == 0)
    def _():
        _cast_weight_tile(w_ref, wb_ref)

    o_ref[...] = jnp.dot(a_ref[...], wb_ref[...], preferred_element_type=F32).astype(o_ref.dtype)


def _proj_rope_kernel(a_ref, w_ref, c_ref, sa_ref, sb_ref, o_ref, wb_ref):
    @pl.when(pl.program_id(1) == 0)
    def _():
        _cast_weight_tile(w_ref, wb_ref)

    acc = jnp.dot(a_ref[...], wb_ref[...], preferred_element_type=F32)
    c, sa, sb = c_ref[...], sa_ref[...], sb_ref[...]
    for h in range(acc.shape[1] // HEAD_DIM):
        x = acc[:, h * HEAD_DIM:(h + 1) * HEAD_DIM]
        up = pltpu.roll(x, HEAD_DIM - ROPE_HALF, 1)
        dn = pltpu.roll(x, ROPE_HALF, 1)
        o_ref[:, h * HEAD_DIM:(h + 1) * HEAD_DIM] = (x * c + up * sa + dn * sb).astype(o_ref.dtype)


def _project(u, w, col0, ncols, out_dtype, rope=None):
    t, d = u.shape
    tm, tn = 1024, 1024
    j0 = col0 // tn
    in_specs = [pl.BlockSpec((tm, d), lambda j, i: (i, 0)),
                pl.BlockSpec((d, tn), lambda j, i: (0, j0 + j))]
    args = [u, w]
    kern = _proj_kernel
    if rope is not None:
        in_specs += [pl.BlockSpec((tm, LANES), lambda j, i: (i, 0)) for _ in range(3)]
        args += list(rope)
        kern = _proj_rope_kernel
    return pl.pallas_call(
        kern,
        out_shape=jax.ShapeDtypeStruct((t, ncols), out_dtype),
        grid=(ncols // tn, t // tm),
        in_specs=in_specs,
        out_specs=pl.BlockSpec((tm, tn), lambda j, i: (i, j)),
        scratch_shapes=[pltpu.VMEM((d, tn), BF16)],
        compiler_params=_cparams(("arbitrary", "arbitrary")),
        name="in_proj_rope" if rope is not None else "in_proj",
    )(*args)


def _key_pieces(qb, sub):
    L = ATTN_BLOCK
    pieces = [(0, 0, L * (qb + 1))]
    for r in range(1, STREAMS):
        back = 0 if qb == 0 else (L // 2 if r == 1 else L // 4)
        pieces.append((r, L * qb - back, L + back))
    return pieces


def _attention_bias(sub):
    L = ATTN_BLOCK
    nqb = sub // L
    kmax = max(sum(p[2] for p in _key_pieces(qb, sub)) for qb in range(nqb))
    qi = jnp.arange(L, dtype=jnp.int32)[:, None]
    out = []
    for j in range(STREAMS):
        row = []
        for qb in range(nqb):
            cols = []
            for r, start, size in _key_pieces(qb, sub):
                jp = (j + r) % STREAMS
                kn = start + jnp.arange(size, dtype=jnp.int32)[None, :]
                dt = STREAMS * (L * qb + qi - kn) + (j - jp)
                cnt = jnp.zeros(dt.shape, F32)
                for window, dil in DILATIONS:
                    cnt += ((dt >= 0) & (dt <= window) & (dt % dil == 0)).astype(F32)
                cols.append(jnp.log2(cnt))
            tile = jnp.concatenate(cols, axis=1)
            row.append(jnp.pad(tile, ((0, 0), (0, kmax - tile.shape[1])), constant_values=-jnp.inf))
        out.append(jnp.stack(row))
    return jnp.stack(out)


def _attn_kernel(q_ref, k_ref, v_ref, bias_ref, o_ref):
    j = pl.program_id(2)
    L = ATTN_BLOCK
    sub = o_ref.shape[1]
    scale = HEAD_DIM ** -0.5 * 1.4426950408889634
    for g in range(HEADS_PER_STEP):
        lanes = slice(g * HEAD_DIM, (g + 1) * HEAD_DIM)
        for qb in range(sub // L):
            rows = []
            for r, start, size in _key_pieces(qb, sub):
                base = pl.multiple_of(((j + r) % STREAMS) * sub, sub)
                rows.append(pl.ds(base + start, size))
            ktot = sum(p[2] for p in _key_pieces(qb, sub))
            q = q_ref[0, pl.ds(pl.multiple_of(j * sub, sub) + L * qb, L), lanes]
            k_all = jnp.concatenate([k_ref[0, rw, lanes] for rw in rows], axis=0)
            v_all = jnp.concatenate([v_ref[0, rw, lanes] for rw in rows], axis=0)
            s = lax.dot_general(q, k_all, (((1,), (1,)), ((), ())), preferred_element_type=F32)
            s = s * scale + bias_ref[j, qb, :, :ktot]
            m = jnp.max(s, axis=-1, keepdims=True)
            p = jnp.exp2(s - m).astype(BF16)
            v_ext = jnp.concatenate([v_all, jnp.ones_like(v_all)], axis=1)
            oe = jnp.dot(p, v_ext, preferred_element_type=F32)
            y = oe[:, :HEAD_DIM] / oe[:, HEAD_DIM:]
            o_ref[0, L * qb:L * (qb + 1), lanes] = y.astype(o_ref.dtype)


def _cols(src, bsz, seq, gw, grid_rank):
    a, col0 = src
    off = col0 // gw
    if grid_rank == 3:
        spec = pl.BlockSpec((1, seq, gw), lambda b, h, j: (b, 0, off + h))
    else:
        spec = pl.BlockSpec((1, seq, gw), lambda b, h: (b, 0, off + h))
    return a.reshape(bsz, seq, a.shape[1]), spec


def _attention(q, k, v, width, bsz, seq):
    t = bsz * seq
    gw = HEADS_PER_STEP * HEAD_DIM
    sub = seq // STREAMS
    bias = _attention_bias(sub)
    (qa, qs), (ka, ks), (va, vs) = (_cols(s, bsz, seq, gw, 3) for s in (q, k, v))
    out = pl.pallas_call(
        _attn_kernel,
        out_shape=jax.ShapeDtypeStruct((bsz, seq, width), BF16),
        grid=(bsz, width // gw, STREAMS),
        in_specs=[qs, ks, vs,
                  pl.BlockSpec(bias.shape, lambda b, h, j: (0, 0, 0, 0))],
        out_specs=pl.BlockSpec((1, sub, gw), lambda b, h, j: (b, j, h)),
        compiler_params=_cparams(("arbitrary", "arbitrary", "arbitrary")),
        name="dilated_attention",
    )(qa, ka, va, bias)
    return out.reshape(t, width)


def _split3(x):
    hi = x.astype(BF16)
    r1 = x - hi.astype(F32)
    mid = r1.astype(BF16)
    lo = (r1 - mid.astype(F32)).astype(BF16)
    return hi, mid, lo


def _rec_kernel(q_ref, f_ref, i_ref, g_ref, lb_ref, ng_ref, o_ref, *st_refs):
    C = REC_CHUNK
    seq = q_ref.shape[1]
    sub = seq // STREAMS
    piece = C // STREAMS
    for st_ref in st_refs:
        st_ref[...] = jnp.zeros_like(st_ref)
    pi = lax.broadcasted_iota(jnp.int32, (C, C), 0)
    si = lax.broadcasted_iota(jnp.int32, (C, C), 1)
    time_of = lambda p: STREAMS * (p % piece) + p // piece
    causal = time_of(si) <= time_of(pi)
    tri = jnp.where(causal, 1.0, 0.0).astype(BF16)
    row_of = lambda tau: (tau % STREAMS) * piece + tau // STREAMS
    last = row_of(C - 1)
    anchor = row_of(C // 2 - 1)

    def load(ref, c, lanes):
        return jnp.concatenate(
            [ref[0, pl.ds(pl.multiple_of(j * sub + c * piece, piece), piece), lanes] for j in range(STREAMS)],
            axis=0)

    def body(c, carry):
        for g in range(HEADS_PER_STEP):
            st_ref = st_refs[g]
            lanes = slice(g * HEAD_DIM, (g + 1) * HEAD_DIM)
            lb = lb_ref[:, lanes]
            x = load(f_ref, c, lanes)
            z = jnp.exp(-jnp.abs(x))
            r = 1.0 / (1.0 + z)
            pos = x >= 0
            sig_p = jnp.where(pos, r, z * r)
            sig_n = jnp.where(pos, z * r, r)
            logf = jnp.log(lb + (1.0 - lb) * sig_p)
            kk = (1.0 - lb) * sig_n
            hi, mid, lo = _split3(logf)
            b = (jnp.dot(tri, hi, preferred_element_type=F32)
                 + jnp.dot(tri, mid, preferred_element_type=F32)
                 + jnp.dot(tri, lo, preferred_element_type=F32))
            b_last = b[last:last + 1, :]
            b_mid = b[anchor:anchor + 1, :]
            qs = _silu(load(q_ref, c, lanes).astype(F32))
            v = load(i_ref, c, lanes)
            q_in = (qs * jnp.exp(b)).astype(BF16)
            q_a = (qs * jnp.exp(b - b_mid)).astype(BF16)
            k_a = (kk * jnp.exp(b_mid - b)).astype(BF16)
            k_e = (kk * jnp.exp(b_last - b)).astype(BF16)
            sc = lax.dot_general(q_a, k_a, (((1,), (1,)), ((), ())), preferred_element_type=F32)
            sc = jnp.where(causal, sc, 0.0).astype(BF16)
            st = st_ref[...]
            o = (jnp.dot(sc, v, preferred_element_type=F32)
                 + lax.dot_general(q_in, st.astype(BF16), (((1,), (1,)), ((), ())),
                                   preferred_element_type=F32))
            upd = lax.dot_general(v, k_e, (((0,), (0,)), ((), ())), preferred_element_type=F32)
            st_ref[...] = jnp.exp(b_last) * st + upd
            y = o * lax.rsqrt(jnp.mean(o * o, axis=-1, keepdims=True) + NORM_EPS) * ng_ref[...]
            y = (y * _silu(load(g_ref, c, lanes).astype(F32))).astype(o_ref.dtype)
            for j in range(STREAMS):
                o_ref[0, pl.ds(pl.multiple_of(j * sub + c * piece, piece), piece), lanes] = (
                    y[j * piece:(j + 1) * piece])
        return carry

    lax.fori_loop(0, seq // C, body, 0, unroll=2)


def _recurrence(q_r, f_r, i_r, g_r, lower_bound, norm_g, bsz, seq):
    t = bsz * seq
    width = lower_bound.shape[0]
    gw = HEADS_PER_STEP * HEAD_DIM
    spec = pl.BlockSpec((1, seq, gw), lambda b, h: (b, 0, h))
    (qa, qs), (fa, fs), (ia, isp), (ga, gs) = (_cols(s, bsz, seq, gw, 2) for s in (q_r, f_r, i_r, g_r))
    out = pl.pallas_call(
        _rec_kernel,
        out_shape=jax.ShapeDtypeStruct((bsz, seq, width), BF16),
        grid=(bsz, width // gw),
        in_specs=[qs, fs, isp, gs,
                  pl.BlockSpec((1, gw), lambda b, h: (0, h)),
                  pl.BlockSpec((1, HEAD_DIM), lambda b, h: (0, 0))],
        out_specs=spec,
        scratch_shapes=[pltpu.VMEM((HEAD_DIM, HEAD_DIM), F32) for _ in range(HEADS_PER_STEP)],
        compiler_params=_cparams(("arbitrary", "arbitrary")),
        name="hgrn2_recurrence",
    )(qa, fa, ia, ga, lower_bound.reshape(1, width), norm_g.reshape(1, HEAD_DIM))
    return out.reshape(t, width)


def _merge_kernel(ya_ref, yr_ref, wa_ref, wr_ref, ga_ref, gr_ref, o_ref):
    a = jnp.dot(ya_ref[...], wa_ref[...], preferred_element_type=F32)
    r = jnp.dot(yr_ref[...], wr_ref[...], preferred_element_type=F32)
    m = _sigmoid(ga_ref[...].astype(F32)) * a + _sigmoid(gr_ref[...].astype(F32)) * r
    o_ref[...] = m.astype(o_ref.dtype)


def _merge(ya, yr, wa, wr, ga, gr):
    t, d = ya.shape
    n = wa.shape[1]
    tm, tn = 1024, 512
    row = pl.BlockSpec((tm, d), lambda i, j: (i, 0))
    col = pl.BlockSpec((d, tn), lambda i, j: (0, j))
    tile = pl.BlockSpec((tm, tn), lambda i, j: (i, j))
    ga_off, gr_off = ga[1] // tn, gr[1] // tn
    return pl.pallas_call(
        _merge_kernel,
        out_shape=jax.ShapeDtypeStruct((t, n), BF16),
        grid=(t // tm, n // tn),
        in_specs=[row, row, col, col,
                  pl.BlockSpec((tm, tn), lambda i, j: (i, ga_off + j)),
                  pl.BlockSpec((tm, tn), lambda i, j: (i, gr_off + j))],
        out_specs=tile,
        compiler_params=_cparams(("arbitrary", "arbitrary")),
        name="branch_merge",
    )(ya, yr, wa, wr, ga[0], gr[0])


def _mixout_kernel(m_ref, w_ref, x_ref, gt_ref, g_ref, sh_ref, sc_ref, rw_ref, rb_ref,
                   h_ref, *rest):
    u_refs, route_ref = rest[:-1], rest[-1]
    mix = jnp.dot(m_ref[...], w_ref[...], preferred_element_type=F32)
    h = x_ref[...] + gt_ref[0] * mix
    h_ref[...] = h
    u = h * lax.rsqrt(jnp.mean(h * h, axis=-1, keepdims=True) + NORM_EPS) * g_ref[...]
    u = u * (1.0 + sc_ref[0]) + sh_ref[0]
    wc = 2 * u_refs[0].shape[1]
    for c, u_ref in enumerate(u_refs):
        u_ref[...] = _pack_bf16_pairs(u[:, c * wc:(c + 1) * wc])
    u_hi = u.astype(BF16)
    u_lo = (u - u_hi.astype(F32)).astype(BF16)
    rw = rw_ref[...]
    w_hi = rw.astype(BF16)
    w_lo = (rw - w_hi.astype(F32)).astype(BF16)
    logits = (jnp.dot(u_hi, w_hi, preferred_element_type=F32)
              + jnp.dot(u_lo, w_hi, preferred_element_type=F32)
              + jnp.dot(u_hi, w_lo, preferred_element_type=F32)) + rb_ref[...]
    lane = lax.broadcasted_iota(jnp.int32, logits.shape, 1).astype(F32)
    big = float(LANES)
    neg = -jnp.inf
    lg = jnp.where(lane < N_GROUPS, logits, neg)
    mg = jnp.max(lg, axis=-1, keepdims=True)
    g_sel = jnp.min(jnp.where(lg == mg, lane, big), axis=-1, keepdims=True)
    p_group = 1.0 / jnp.sum(jnp.exp(lg - mg), axis=-1, keepdims=True)
    lo = N_GROUPS + EXPERTS_PER_GROUP * g_sel
    le = jnp.where((lane >= lo) & (lane < lo + EXPERTS_PER_GROUP), logits, neg)
    t1 = jnp.max(le, axis=-1, keepdims=True)
    i1 = jnp.min(jnp.where(le == t1, lane, big), axis=-1, keepdims=True)
    le2 = jnp.where(lane == i1, neg, le)
    t2 = jnp.max(le2, axis=-1, keepdims=True)
    i2 = jnp.min(jnp.where(le2 == t2, lane, big), axis=-1, keepdims=True)
    e21 = jnp.exp(t2 - t1)
    w1 = p_group / (1.0 + e21)
    w2 = p_group * e21 / (1.0 + e21)
    route = jnp.where(lane == 0, i1 - N_GROUPS,
                      jnp.where(lane == 1, i2 - N_GROUPS,
                                jnp.where(lane == 2, w1, jnp.where(lane == 3, w2, 0.0))))
    route_ref[...] = route


def _mix_out(merged, w_out, x2d, gate, g, shift, scale, rw, rb, seq):
    t, d = x2d.shape
    tm = 256
    per_b = seq // tm
    bsz = gate.shape[0]
    row = lambda dt: pl.BlockSpec((tm, d), lambda i: (i, 0))
    per_batch = pl.BlockSpec((1, 1, d), lambda i: (i // per_b, 0, 0))
    const = lambda shape: pl.BlockSpec(shape, lambda i: (0,) * len(shape))
    wc = d // GATHER_CHUNKS // 2
    outs = pl.pallas_call(
        _mixout_kernel,
        out_shape=(jax.ShapeDtypeStruct((t, d), F32),
                   *[jax.ShapeDtypeStruct((t, wc), jnp.uint32) for _ in range(GATHER_CHUNKS)],
                   jax.ShapeDtypeStruct((t, LANES), F32)),
        grid=(t // tm,),
        in_specs=[row(BF16), const((d, d)), row(F32), per_batch, const((1, d)), per_batch, per_batch,
                  const((d, LANES)), const((1, LANES))],
        out_specs=(row(F32), *[pl.BlockSpec((tm, wc), lambda i: (i, 0)) for _ in range(GATHER_CHUNKS)],
                   pl.BlockSpec((tm, LANES), lambda i: (i, 0))),
        compiler_params=_cparams(("arbitrary",)),
        name="mix_out_router",
    )(merged, w_out, x2d, gate.reshape(bsz, 1, d), g.reshape(1, d),
      shift.reshape(bsz, 1, d), scale.reshape(bsz, 1, d), rw, rb)
    return outs[0], list(outs[1:-1]), outs[-1]


def _expert_kernel(be_ref, nx_ref, nu_ref, *refs):
    nch = GATHER_CHUNKS
    x_refs = refs[:nch]
    w_hbm = refs[nch:nch + 3]
    ws_ref = refs[nch + 3]
    o_refs = refs[nch + 4:2 * nch + 4]
    stage = refs[2 * nch + 4:2 * nch + 7]
    wb = refs[2 * nch + 7:2 * nch + 10]
    sem = refs[2 * nch + 10]
    wc = 2 * o_refs[0].shape[1]
    i = pl.program_id(0)
    e = be_ref[i]
    nxt = nx_ref[i]
    active = i < nu_ref[0]
    first = i == 0
    run_start = jnp.logical_or(first, e != be_ref[jnp.maximum(i - 1, 0)])

    def weight_copies(expert):
        return [pltpu.make_async_copy(w_hbm[k].at[expert], stage[k], sem.at[k]) for k in range(3)]

    @pl.when(jnp.logical_and(active, first))
    def _():
        for cp in weight_copies(e):
            cp.start()

    @pl.when(jnp.logical_and(active, run_start))
    def _():
        for cp in weight_copies(e):
            cp.wait()
        for k in range(3):
            _cast_weight_tile(stage[k], wb[k])

        @pl.when(nxt >= 0)
        def _():
            for cp in weight_copies(nxt):
                cp.start()

    @pl.when(active)
    def _():
        x = jnp.concatenate([_unpack_bf16_pairs(r[...]) for r in x_refs], axis=1).astype(BF16)
        hg = jnp.dot(x, wb[0][...], preferred_element_type=F32)
        hu = jnp.dot(x, wb[1][...], preferred_element_type=F32)
        hdn = (_silu(hg) * hu).astype(BF16)
        y = jnp.dot(hdn, wb[2][...], preferred_element_type=F32)
        y = y * ws_ref[...]
        for c, o_ref in enumerate(o_refs):
            o_ref[...] = _pack_bf16_pairs(y[:, c * wc:(c + 1) * wc])

    @pl.when(jnp.logical_not(active))
    def _():
        for o_ref in o_refs:
            o_ref[...] = jnp.zeros_like(o_ref)


def _expert_ffn(xs, w_gate, w_up, w_down, w_slot, block_expert, next_expert, n_used):
    n_slots, wc = xs[0].shape
    d = 2 * wc * GATHER_CHUNKS
    hid = w_gate.shape[2]
    bm = MOE_ROWS
    chunk = pl.BlockSpec((bm, wc), lambda i, be, nx, nu: (i, 0))
    hbm = pl.BlockSpec(memory_space=pl.ANY)
    grid_spec = pltpu.PrefetchScalarGridSpec(
        num_scalar_prefetch=3,
        grid=(n_slots // bm,),
        in_specs=[*[chunk] * GATHER_CHUNKS, hbm, hbm, hbm,
                  pl.BlockSpec((bm, 1), lambda i, be, nx, nu: (i, 0))],
        out_specs=[chunk] * GATHER_CHUNKS,
        scratch_shapes=[pltpu.VMEM((d, hid), F32), pltpu.VMEM((d, hid), F32), pltpu.VMEM((hid, d), F32),
                        pltpu.VMEM((d, hid), BF16), pltpu.VMEM((d, hid), BF16), pltpu.VMEM((hid, d), BF16),
                        pltpu.SemaphoreType.DMA((3,))],
    )
    return pl.pallas_call(
        _expert_kernel,
        out_shape=[jax.ShapeDtypeStruct((n_slots, wc), jnp.uint32) for _ in range(GATHER_CHUNKS)],
        grid_spec=grid_spec,
        compiler_params=_cparams(("arbitrary",)),
        name="expert_ffn",
    )(block_expert, next_expert, n_used, *xs, w_gate, w_up, w_down, w_slot)


def _gather_rows(table, idx):
    m = idx.shape[0]
    d = table.shape[1]
    window = LANES
    mesh = plsc.VectorSubcoreMesh(core_axis_name="core", subcore_axis_name="subcore")

    @pl.kernel(out_type=jax.ShapeDtypeStruct((m, d), table.dtype), mesh=mesh, scratch_types=[])
    def gather(x_hbm, i_hbm, o_hbm):
        def body(i_vmem, o_vmem):
            pltpu.sync_copy(x_hbm.at[i_vmem.at[0]], o_vmem)

        pltpu.emit_pipeline(
            body,
            grid=(m // window,),
            in_specs=[pl.BlockSpec((1, window), lambda i: (0, i))],
            out_specs=[pl.BlockSpec((window, d), lambda i: (i, 0))],
            core_axis_name=("core", "subcore"),
            dimension_semantics=(pltpu.PARALLEL,),
        )(i_hbm, o_hbm)

    return gather(table, idx.reshape(1, m))


def _final_kernel(h_ref, *rest):
    y_refs = rest[:TOP_K * GATHER_CHUNKS]
    gt_ref, g_ref, o_ref, slab_ref = rest[TOP_K * GATHER_CHUNKS:]
    sub = h_ref.shape[2]
    nslab = h_ref.shape[3] // LANES
    for j in range(STREAMS):
        ffn = sum(jnp.concatenate([_unpack_bf16_pairs(y_refs[k * GATHER_CHUNKS + c][0, 0, j])
                                   for c in range(GATHER_CHUNKS)], axis=1) for k in range(TOP_K))
        h = h_ref[0, j] + gt_ref[0] * ffn
        y = h * lax.rsqrt(jnp.mean(h * h, axis=-1, keepdims=True) + NORM_EPS) * g_ref[...]
        for c in range(nslab):
            slab_ref[c, pl.ds(j, sub, stride=STREAMS), :] = y[:, c * LANES:(c + 1) * LANES]
    for c in range(nslab):
        o_ref[:, c * LANES:(c + 1) * LANES] = slab_ref[c]


def _final(h, y2, gate, g, seq):
    t, d = h.shape
    tm = 512
    per_b = seq // tm
    bsz = gate.shape[0]
    sub = tm // STREAMS
    spec = pl.BlockSpec((1, STREAMS, sub, d), lambda i: (i // per_b, 0, i % per_b, 0))
    wc = d // GATHER_CHUNKS // 2
    y5 = [a.reshape(TOP_K, bsz, STREAMS, seq // STREAMS, wc) for a in y2]

    def yspec(k):
        return pl.BlockSpec((1, 1, STREAMS, sub, wc), lambda i: (k, i // per_b, 0, i % per_b, 0))

    return pl.pallas_call(
        _final_kernel,
        out_shape=jax.ShapeDtypeStruct((t, d), F32),
        grid=(t // tm,),
        in_specs=[spec, *[yspec(k) for k in range(TOP_K) for _ in range(GATHER_CHUNKS)],
                  pl.BlockSpec((1, 1, d), lambda i: (i // per_b, 0, 0)),
                  pl.BlockSpec((1, d), lambda i: (0, 0))],
        out_specs=pl.BlockSpec((tm, d), lambda i: (i, 0)),
        scratch_shapes=[pltpu.VMEM((d // LANES, tm, LANES), F32)],
        compiler_params=_cparams(("arbitrary",)),
        name="final_norm",
    )(h.reshape(bsz, STREAMS, seq // STREAMS, d), *(y5 * TOP_K), gate.reshape(bsz, 1, d), g.reshape(1, d))


def _dispatch_plan(expert_idx, weights):
    n_assign = expert_idx.size
    n_blocks = n_assign // MOE_ROWS + N_EXPERTS
    n_slots = n_blocks * MOE_ROWS
    flat_e = expert_idx.reshape(-1)
    flat_w = weights.reshape(-1)
    ids = jnp.arange(n_assign, dtype=jnp.int32)
    eids = jnp.arange(N_EXPERTS, dtype=jnp.int32)[None, :]
    _, order = lax.sort((flat_e, ids), num_keys=1, is_stable=True)
    _, rank_sorted = lax.sort((order, ids), num_keys=1)
    hot_a = (flat_e[:, None] == eids).astype(jnp.int32)
    counts = jnp.sum(hot_a, axis=0)
    padded = ((counts + MOE_ROWS - 1) // MOE_ROWS) * MOE_ROWS
    pad_end = jnp.cumsum(padded)
    pad_start = pad_end - padded
    start = jnp.cumsum(counts) - counts
    dest = (rank_sorted + jnp.sum(hot_a * (pad_start - start)[None, :], axis=1)).reshape(-1, TOP_K)
    blk0 = jnp.arange(n_blocks, dtype=jnp.int32) * MOE_ROWS
    block_expert = jnp.minimum(jnp.sum((pad_end[None, :] <= blk0[:, None]).astype(jnp.int32), axis=1),
                               N_EXPERTS - 1)
    hot_b = (block_expert[:, None] == eids).astype(jnp.int32)
    blk_shift = jnp.sum(hot_b * (start - pad_start)[None, :], axis=1)
    blk_count = jnp.sum(hot_b * (pad_start + counts)[None, :], axis=1)
    slot = jnp.arange(n_slots, dtype=jnp.int32).reshape(n_blocks, MOE_ROWS)
    valid = (slot < blk_count[:, None]).reshape(-1)
    src = jnp.clip(slot + blk_shift[:, None], 0, n_assign - 1).reshape(-1)
    assign = order[src]
    token_of_slot = jnp.where(valid, assign // TOP_K, 0)
    weight_of_slot = jnp.where(valid, flat_w[assign], 0.0)
    n_used = (pad_end[-1:] // MOE_ROWS).astype(jnp.int32)
    bi = jnp.arange(n_blocks, dtype=jnp.int32)
    later = ((bi[None, :] > bi[:, None]) & (block_expert[None, :] != block_expert[:, None])
             & (bi[None, :] < n_used[0]))
    next_expert = jnp.where(jnp.any(later, axis=1), block_expert[jnp.argmax(later, axis=1)], -1)
    return token_of_slot, weight_of_slot, dest, block_expert, next_expert.astype(jnp.int32), n_used


def kernel(x, c, positions, ada_w, ada_b, mix_norm_g, w_in, w_attn_branch, w_rec_branch, w_mix_out,
           rec_norm_g, rec_lb_logits, ffn_norm_g, router_group_w, router_group_b, router_expert_w,
           router_expert_b, expert_w_gate, expert_w_up, expert_w_down, final_norm_g):
    bsz, seq, d = x.shape
    t = bsz * seq
    depth = ada_w.shape[0]
    assert depth == 1, "final norm is fused after the single layer"
    lower_bounds = jnp.cumsum(jax.nn.softmax(rec_lb_logits.astype(F32), axis=0), axis=0)
    pos_streams = positions.reshape(bsz, seq // STREAMS, STREAMS).transpose(0, 2, 1)
    rope = _rope_tables(pos_streams)
    h = x.reshape(t, d)
    for layer in range(depth):
        mod = _modulation(c, ada_w[layer], ada_b[layer])
        sh_m, sc_m, gt_m, sh_f, sc_f, gt_f = jnp.split(mod, 6, axis=-1)
        h, u = _norm_modulate(h, mix_norm_g[layer], sh_m, sc_m, seq)
        w = w_in[layer]
        qk = _project(u, w, 0, 2 * d, BF16, rope=rope)
        rest = _project(u, w, 2 * d, 2 * d, BF16)
        f_r = _project(u, w, 4 * d, d, F32)
        tail = _project(u, w, 5 * d, 4 * d, BF16)
        y_attn = _attention((qk, 0), (qk, d), (rest, 0), d, bsz, seq)
        y_rec = _recurrence((rest, d), (f_r, 0), (tail, 0), (tail, d), lower_bounds[layer],
                            rec_norm_g[layer], bsz, seq)
        merged = _merge(y_attn, y_rec, w_attn_branch[layer].astype(BF16),
                        w_rec_branch[layer].astype(BF16), (tail, 2 * d), (tail, 3 * d))
        rw = jnp.concatenate([router_group_w[layer], router_expert_w[layer],
                              jnp.zeros((d, LANES - N_GROUPS - N_EXPERTS), F32)], axis=1)
        rb = jnp.concatenate([router_group_b[layer], router_expert_b[layer],
                              jnp.zeros((LANES - N_GROUPS - N_EXPERTS,), F32)]).reshape(1, LANES)
        h, u2, route = _mix_out(merged, w_mix_out[layer].astype(BF16), h, gt_m, ffn_norm_g[layer],
                                sh_f, sc_f, rw, rb, seq)
        expert_idx = route[:, :TOP_K].astype(jnp.int32)
        weights = route[:, TOP_K:2 * TOP_K]
        tok, w_slot, dest, block_expert, next_expert, n_used = _dispatch_plan(expert_idx, weights)
        xs = [_gather_rows(a, tok) for a in u2]
        ys = _expert_ffn(xs, expert_w_gate[layer], expert_w_up[layer], expert_w_down[layer],
                         w_slot.reshape(-1, 1), block_expert, next_expert, n_used)
        dest_kt = dest.T.reshape(-1)
        y2 = [_gather_rows(a, dest_kt) for a in ys]
        h = _final(h, y2, gt_f, final_norm_g, seq)
    return h.reshape(bsz, seq, d)
```

```python
import functools

import jax
import jax.numpy as jnp
from jax import lax
from jax.experimental import pallas as pl
from jax.experimental.pallas import tpu as pltpu
from jax.experimental.pallas import tpu_sc as plsc

F32 = jnp.float32
BF16 = jnp.bfloat16

D_MODEL = 2048
HEAD_DIM = 128
N_HEADS = D_MODEL // HEAD_DIM
ROPE_DIM = HEAD_DIM // 4
ROPE_HALF = ROPE_DIM // 2
ROPE_THETA = 500000.0
ATTN_SPAN = 128
ATTN_BLOCK = 128
REC_CHUNK = 64
N_GROUPS = 4
EXPERTS_PER_GROUP = 8
N_EXPERTS = N_GROUPS * EXPERTS_PER_GROUP
EXPERT_HIDDEN = D_MODEL // 2
TOP_K = 2
NORM_EPS = 1e-6
IN_WIDTH = 9 * D_MODEL

LANES = 128
VMEM_LIMIT = 56 * 1024 * 1024

MOE_ROWS = 256
HEADS_PER_STEP = 2
GATHER_CHUNKS = 4
STREAMS = 4
DILATIONS = ((128, 1), (512, 4), (2048, 16))


def _cparams(sem):
    return pltpu.CompilerParams(dimension_semantics=sem, vmem_limit_bytes=VMEM_LIMIT)


def _sigmoid(x):
    return 1.0 / (1.0 + jnp.exp(-x))


def _silu(x):
    return x * _sigmoid(x)


def _pack_bf16_pairs(x):
    w = x.shape[1] // 2
    lo = lax.bitcast_convert_type(x[:, :w].astype(BF16).astype(F32), jnp.uint32)
    hi = lax.bitcast_convert_type(x[:, w:].astype(BF16).astype(F32), jnp.uint32)
    return (lo >> 16) | (hi & jnp.uint32(0xFFFF0000))


def _unpack_bf16_pairs(words):
    lo = lax.bitcast_convert_type(words << 16, F32)
    hi = lax.bitcast_convert_type(words & jnp.uint32(0xFFFF0000), F32)
    return jnp.concatenate([lo, hi], axis=1)


def _mod_kernel(c_ref, w_ref, b_ref, o_ref):
    cond = _silu(c_ref[...])
    o_ref[...] = jnp.dot(cond, w_ref[...], precision=lax.Precision.HIGHEST,
                         preferred_element_type=F32) + b_ref[...]


def _modulation(c, w, b):
    bsz, d = c.shape
    n = w.shape[1]
    tn = 1536
    return pl.pallas_call(
        _mod_kernel,
        out_shape=jax.ShapeDtypeStruct((bsz, n), F32),
        grid=(n // tn,),
        in_specs=[pl.BlockSpec((bsz, d), lambda j: (0, 0)),
                  pl.BlockSpec((d, tn), lambda j: (0, j)),
                  pl.BlockSpec((1, tn), lambda j: (0, j))],
        out_specs=pl.BlockSpec((bsz, tn), lambda j: (0, j)),
        compiler_params=_cparams(("arbitrary",)),
        name="adaln_mod",
    )(c, w, b.reshape(1, n))


def _rope_kernel(pos_ref, freq_ref, c_ref, sa_ref, sb_ref):
    ang = pos_ref[...] * freq_ref[...]
    lane = lax.broadcasted_iota(jnp.int32, ang.shape, 1)
    cos, sin = jnp.cos(ang), jnp.sin(ang)
    c_ref[...] = jnp.where(lane < ROPE_DIM, cos, 1.0)
    sa_ref[...] = jnp.where(lane < ROPE_HALF, -sin, 0.0)
    sb_ref[...] = jnp.where((lane >= ROPE_HALF) & (lane < ROPE_DIM), sin, 0.0)


def _rope_tables(positions):
    t = positions.size
    tm = 2048
    inv_freq = ROPE_THETA ** (-jnp.arange(0, ROPE_DIM, 2, dtype=F32) / ROPE_DIM)
    freq = jnp.concatenate([inv_freq, inv_freq, jnp.zeros((LANES - ROPE_DIM,), F32)]).reshape(1, LANES)
    pos = positions.astype(F32).reshape(t, 1)
    out = jax.ShapeDtypeStruct((t, LANES), F32)
    return pl.pallas_call(
        _rope_kernel,
        out_shape=(out, out, out),
        grid=(t // tm,),
        in_specs=[pl.BlockSpec((tm, 1), lambda i: (i, 0)),
                  pl.BlockSpec((1, LANES), lambda i: (0, 0))],
        out_specs=tuple(pl.BlockSpec((tm, LANES), lambda i: (i, 0)) for _ in range(3)),
        compiler_params=_cparams(("arbitrary",)),
        name="rope_tables",
    )(pos, freq)


def _norm_mod_kernel(x_ref, g_ref, sh_ref, sc_ref, xp_ref, u_ref, slab_ref):
    rows = x_ref.shape[0]
    sub = rows // STREAMS
    nslab = x_ref.shape[1] // LANES
    for c in range(nslab):
        slab_ref[c] = x_ref[:, c * LANES:(c + 1) * LANES]
    for j in range(STREAMS):
        x = jnp.concatenate([slab_ref[c, pl.ds(j, sub, stride=STREAMS), :] for c in range(nslab)], axis=1)
        xp_ref[0, j] = x
        y = x * lax.rsqrt(jnp.mean(x * x, axis=-1, keepdims=True) + NORM_EPS) * g_ref[...]
        u_ref[0, j] = (y * (1.0 + sc_ref[0]) + sh_ref[0]).astype(u_ref.dtype)


def _norm_modulate(x2d, g, shift, scale, seq):
    t, d = x2d.shape
    tm = 512
    per_b = seq // tm
    bsz = shift.shape[0]
    sub = tm // STREAMS
    out_spec = pl.BlockSpec((1, STREAMS, sub, d), lambda i: (i // per_b, 0, i % per_b, 0))
    xp, u = pl.pallas_call(
        _norm_mod_kernel,
        out_shape=(jax.ShapeDtypeStruct((bsz, STREAMS, seq // STREAMS, d), F32),
                   jax.ShapeDtypeStruct((bsz, STREAMS, seq // STREAMS, d), BF16)),
        grid=(t // tm,),
        in_specs=[pl.BlockSpec((tm, d), lambda i: (i, 0)),
                  pl.BlockSpec((1, d), lambda i: (0, 0)),
                  pl.BlockSpec((1, 1, d), lambda i: (i // per_b, 0, 0)),
                  pl.BlockSpec((1, 1, d), lambda i: (i // per_b, 0, 0))],
        out_specs=(out_spec, out_spec),
        scratch_shapes=[pltpu.VMEM((d // LANES, tm, LANES), F32)],
        compiler_params=_cparams(("arbitrary",)),
        name="norm_modulate",
    )(x2d, g.reshape(1, d), shift.reshape(bsz, 1, d), scale.reshape(bsz, 1, d))
    return xp.reshape(t, d), u.reshape(t, d)


def _cast_weight_tile(w_ref, wb_ref):
    rows = 256

    def body(r, carry):
        sl = pl.ds(pl.multiple_of(r * rows, rows), rows)
        wb_ref[sl, :] = w_ref[sl, :].astype(BF16)
        return carry

    lax.fori_loop(0, w_ref.shape[0] // rows, body, 0)


def _proj_kernel(a_ref, w_ref, o_ref, wb_ref):
    @pl.when(pl.program_id(1) == 0)
    def _():
        _cast_weight_tile(w_ref, wb_ref)

    o_ref[...] = jnp.dot(a_ref[...], wb_ref[...], preferred_element_type=F32).astype(o_ref.dtype)


def _proj_rope_kernel(a_ref, w_ref, c_ref, sa_ref, sb_ref, o_ref, wb_ref):
    @pl.when(pl.program_id(1) == 0)
    def _():
        _cast_weight_tile(w_ref, wb_ref)

    acc = jnp.dot(a_ref[...], wb_ref[...], preferred_element_type=F32)
    c, sa, sb = c_ref[...], sa_ref[...], sb_ref[...]
    for h in range(acc.shape[1] // HEAD_DIM):
        x = acc[:, h * HEAD_DIM:(h + 1) * HEAD_DIM]
        up = pltpu.roll(x, HEAD_DIM - ROPE_HALF, 1)
        dn = pltpu.roll(x, ROPE_HALF, 1)
        o_ref[:, h * HEAD_DIM:(h + 1) * HEAD_DIM] = (x * c + up * sa + dn * sb).astype(o_ref.dtype)


def _project(u, w, col0, ncols, out_dtype, rope=None):
    t, d = u.shape
    tm, tn = 1024, 1024
    j0 = col0 // tn
    in_specs = [pl.BlockSpec((tm, d), lambda j, i: (i, 0)),
                pl.BlockSpec((d, tn), lambda j, i: (0, j0 + j))]
    args = [u, w]
    kern = _proj_kernel
    if rope is not None:
        in_specs += [pl.BlockSpec((tm, LANES), lambda j, i: (i, 0)) for _ in range(3)]
        args += list(rope)
        kern = _proj_rope_kernel
    return pl.pallas_call(
        kern,
        out_shape=jax.ShapeDtypeStruct((t, ncols), out_dtype),
        grid=(ncols // tn, t // tm),
        in_specs=in_specs,
        out_specs=pl.BlockSpec((tm, tn), lambda j, i: (i, j)),
        scratch_shapes=[pltpu.VMEM((d, tn), BF16)],
        compiler_params=_cparams(("arbitrary", "arbitrary")),
        name="in_proj_rope" if rope is not None else "in_proj",
    )(*args)


def _key_pieces(qb, sub):
    L = ATTN_BLOCK
    pieces = [(0, 0, L * (qb + 1))]
    for r in range(1, STREAMS):
        back = 0 if qb == 0 else (L // 2 if r == 1 else L // 4)
        pieces.append((r, L * qb - back, L + back))
    return pieces


def _attention_bias(sub):
    L = ATTN_BLOCK
    nqb = sub // L
    kmax = max(sum(p[2] for p in _key_pieces(qb, sub)) for qb in range(nqb))
    qi = jnp.arange(L, dtype=jnp.int32)[:, None]
    out = []
    for j in range(STREAMS):
        row = []
        for qb in range(nqb):
            cols = []
            for r, start, size in _key_pieces(qb, sub):
                jp = (j + r) % STREAMS
                kn = start + jnp.arange(size, dtype=jnp.int32)[None, :]
                dt = STREAMS * (L * qb + qi - kn) + (j - jp)
                cnt = jnp.zeros(dt.shape, F32)
                for window, dil in DILATIONS:
                    cnt += ((dt >= 0) & (dt <= window) & (dt % dil == 0)).astype(F32)
                cols.append(jnp.log2(cnt))
            tile = jnp.concatenate(cols, axis=1)
            row.append(jnp.pad(tile, ((0, 0), (0, kmax - tile.shape[1])), constant_values=-jnp.inf))
        out.append(jnp.stack(row))
    return jnp.stack(out)


def _attn_kernel(q_ref, k_ref, v_ref, bias_ref, o_ref):
    j = pl.program_id(2)
    L = ATTN_BLOCK
    sub = o_ref.shape[1]
    scale = HEAD_DIM ** -0.5 * 1.4426950408889634
    for g in range(HEADS_PER_STEP):
        lanes = slice(g * HEAD_DIM, (g + 1) * HEAD_DIM)
        for qb in range(sub // L):
            rows = []
            for r, start, size in _key_pieces(qb, sub):
                base = pl.multiple_of(((j + r) % STREAMS) * sub, sub)
                rows.append(pl.ds(base + start, size))
            ktot = sum(p[2] for p in _key_pieces(qb, sub))
            q = q_ref[0, pl.ds(pl.multiple_of(j * sub, sub) + L * qb, L), lanes]
            k_all = jnp.concatenate([k_ref[0, rw, lanes] for rw in rows], axis=0)
            v_all = jnp.concatenate([v_ref[0, rw, lanes] for rw in rows], axis=0)
            s = lax.dot_general(q, k_all, (((1,), (1,)), ((), ())), preferred_element_type=F32)
            s = s * scale + bias_ref[j, qb, :, :ktot]
            m = jnp.max(s, axis=-1, keepdims=True)
            p = jnp.exp2(s - m).astype(BF16)
            v_ext = jnp.concatenate([v_all, jnp.ones_like(v_all)], axis=1)
            oe = jnp.dot(p, v_ext, preferred_element_type=F32)
            y = oe[:, :HEAD_DIM] / oe[:, HEAD_DIM:]
            o_ref[0, L * qb:L * (qb + 1), lanes] = y.astype(o_ref.dtype)


def _cols(src, bsz, seq, gw, grid_rank):
    a, col0 = src
    off = col0 // gw
    if grid_rank == 3:
        spec = pl.BlockSpec((1, seq, gw), lambda b, h, j: (b, 0, off + h))
    else:
        spec = pl.BlockSpec((1, seq, gw), lambda b, h: (b, 0, off + h))
    return a.reshape(bsz, seq, a.shape[1]), spec


def _attention(q, k, v, width, bsz, seq):
    t = bsz * seq
    gw = HEADS_PER_STEP * HEAD_DIM
    sub = seq // STREAMS
    bias = _attention_bias(sub)
    (qa, qs), (ka, ks), (va, vs) = (_cols(s, bsz, seq, gw, 3) for s in (q, k, v))
    out = pl.pallas_call(
        _attn_kernel,
        out_shape=jax.ShapeDtypeStruct((bsz, seq, width), BF16),
        grid=(bsz, width // gw, STREAMS),
        in_specs=[qs, ks, vs,
                  pl.BlockSpec(bias.shape, lambda b, h, j: (0, 0, 0, 0))],
        out_specs=pl.BlockSpec((1, sub, gw), lambda b, h, j: (b, j, h)),
        compiler_params=_cparams(("arbitrary", "arbitrary", "arbitrary")),
        name="dilated_attention",
    )(qa, ka, va, bias)
    return out.reshape(t, width)


def _split3(x):
    hi = x.astype(BF16)
    r1 = x - hi.astype(F32)
    mid = r1.astype(BF16)
    lo = (r1 - mid.astype(F32)).astype(BF16)
    return hi, mid, lo


def _rec_scratch():
    C, K = REC_CHUNK, HEAD_DIM
    return [pltpu.VMEM((K, K), F32), pltpu.VMEM((C, K), F32), pltpu.VMEM((C, K), F32),
            pltpu.VMEM((C, C), BF16), pltpu.VMEM((C, K), BF16), pltpu.VMEM((K, K), F32),
            pltpu.VMEM((1, K), F32)]


def _rec_kernel(q_ref, f_ref, i_ref, g_ref, lb_ref, ng_ref, o_ref, *scratch):
    C = REC_CHUNK
    seq = q_ref.shape[1]
    sub = seq // STREAMS
    piece = C // STREAMS
    for g in range(HEADS_PER_STEP):
        st_ref = scratch[g * (len(scratch) // HEADS_PER_STEP)]
        st_ref[...] = jnp.zeros_like(st_ref)
    pi = lax.broadcasted_iota(jnp.int32, (C, C), 0)
    si = lax.broadcasted_iota(jnp.int32, (C, C), 1)
    time_of = lambda p: STREAMS * (p % piece) + p // piece
    causal = time_of(si) <= time_of(pi)
    tri = jnp.where(causal, 1.0, 0.0).astype(BF16)
    row_of = lambda tau: (tau % STREAMS) * piece + tau // STREAMS
    last = row_of(C - 1)
    anchor = row_of(C // 2 - 1)

    def rows(c, j):
        start = j * sub + c * piece
        return pl.ds(start if isinstance(start, int) else pl.multiple_of(start, piece), piece)

    def load(ref, c, lanes):
        return jnp.concatenate([ref[0, rows(c, j), lanes] for j in range(STREAMS)], axis=0)

    def gates(c, g):
        lanes, (_, b_s, kk_s, _, _, _, _) = head(g)
        lb = lb_ref[:, lanes]
        x = load(f_ref, c, lanes)
        z = jnp.exp(-jnp.abs(x))
        r = 1.0 / (1.0 + z)
        pos = x >= 0
        sig_p = jnp.where(pos, r, z * r)
        sig_n = jnp.where(pos, z * r, r)
        logf = jnp.log(lb + (1.0 - lb) * sig_p)
        hi, mid, lo = _split3(logf)
        b_s[...] = (jnp.dot(tri, hi, preferred_element_type=F32)
                    + jnp.dot(tri, mid, preferred_element_type=F32)
                    + jnp.dot(tri, lo, preferred_element_type=F32))
        kk_s[...] = (1.0 - lb) * sig_n

    def scores(c, g):
        lanes, (_, b_s, kk_s, sc_s, qin_s, upd_s, dec_s) = head(g)
        b, kk = b_s[...], kk_s[...]
        b_last = b[last:last + 1, :]
        b_mid = b[anchor:anchor + 1, :]
        qs = _silu(load(q_ref, c, lanes).astype(F32))
        v = load(i_ref, c, lanes)
        q_a = (qs * jnp.exp(b - b_mid)).astype(BF16)
        k_a = (kk * jnp.exp(b_mid - b)).astype(BF16)
        k_e = (kk * jnp.exp(b_last - b)).astype(BF16)
        sc = lax.dot_general(q_a, k_a, (((1,), (1,)), ((), ())), preferred_element_type=F32)
        sc_s[...] = jnp.where(causal, sc, 0.0).astype(BF16)
        qin_s[...] = (qs * jnp.exp(b)).astype(BF16)
        upd_s[...] = lax.dot_general(v, k_e, (((0,), (0,)), ((), ())), preferred_element_type=F32)
        dec_s[...] = jnp.exp(b_last)

    def output(c, g):
        lanes, (st_ref, _, _, sc_s, qin_s, upd_s, dec_s) = head(g)
        v = load(i_ref, c, lanes)
        st = st_ref[...]
        o = (jnp.dot(sc_s[...], v, preferred_element_type=F32)
             + lax.dot_general(qin_s[...], st.astype(BF16), (((1,), (1,)), ((), ())),
                               preferred_element_type=F32))
        st_ref[...] = dec_s[...] * st + upd_s[...]
        y = o * lax.rsqrt(jnp.mean(o * o, axis=-1, keepdims=True) + NORM_EPS) * ng_ref[...]
        y = (y * _silu(load(g_ref, c, lanes).astype(F32))).astype(o_ref.dtype)
        for j in range(STREAMS):
            o_ref[0, rows(c, j), lanes] = y[j * piece:(j + 1) * piece]

    per_head = len(scratch) // HEADS_PER_STEP

    def head(g):
        return slice(g * HEAD_DIM, (g + 1) * HEAD_DIM), scratch[g * per_head:(g + 1) * per_head]

    def stages(c, run):
        for g in range(HEADS_PER_STEP):
            if run[2]:
                output(c - 2, g)
            if run[1]:
                scores(c - 1, g)
            if run[0]:
                gates(c, g)

    nc = seq // C
    stages(0, (True, False, False))
    stages(1, (True, True, False))

    def body(c, carry):
        stages(c, (True, True, True))
        return carry

    lax.fori_loop(2, nc, body, 0)
    stages(nc, (False, True, True))
    stages(nc + 1, (False, False, True))


def _recurrence(q_r, f_r, i_r, g_r, lower_bound, norm_g, bsz, seq):
    t = bsz * seq
    width = lower_bound.shape[0]
    gw = HEADS_PER_STEP * HEAD_DIM
    spec = pl.BlockSpec((1, seq, gw), lambda b, h: (b, 0, h))
    (qa, qs), (fa, fs), (ia, isp), (ga, gs) = (_cols(s, bsz, seq, gw, 2) for s in (q_r, f_r, i_r, g_r))
    out = pl.pallas_call(
        _rec_kernel,
        out_shape=jax.ShapeDtypeStruct((bsz, seq, width), BF16),
        grid=(bsz, width // gw),
        in_specs=[qs, fs, isp, gs,
                  pl.BlockSpec((1, gw), lambda b, h: (0, h)),
                  pl.BlockSpec((1, HEAD_DIM), lambda b, h: (0, 0))],
        out_specs=spec,
        scratch_shapes=[s for _ in range(HEADS_PER_STEP) for s in _rec_scratch()],
        compiler_params=_cparams(("arbitrary", "arbitrary")),
        name="hgrn2_recurrence",
    )(qa, fa, ia, ga, lower_bound.reshape(1, width), norm_g.reshape(1, HEAD_DIM))
    return out.reshape(t, width)


def _merge_kernel(ya_ref, yr_ref, wa_ref, wr_ref, ga_ref, gr_ref, o_ref):
    a = jnp.dot(ya_ref[...], wa_ref[...], preferred_element_type=F32)
    r = jnp.dot(yr_ref[...], wr_ref[...], preferred_element_type=F32)
    m = _sigmoid(ga_ref[...].astype(F32)) * a + _sigmoid(gr_ref[...].astype(F32)) * r
    o_ref[...] = m.astype(o_ref.dtype)


def _merge(ya, yr, wa, wr, ga, gr):
    t, d = ya.shape
    n = wa.shape[1]
    tm, tn = 1024, 512
    row = pl.BlockSpec((tm, d), lambda i, j: (i, 0))
    col = pl.BlockSpec((d, tn), lambda i, j: (0, j))
    tile = pl.BlockSpec((tm, tn), lambda i, j: (i, j))
    ga_off, gr_off = ga[1] // tn, gr[1] // tn
    return pl.pallas_call(
        _merge_kernel,
        out_shape=jax.ShapeDtypeStruct((t, n), BF16),
        grid=(t // tm, n // tn),
        in_specs=[row, row, col, col,
                  pl.BlockSpec((tm, tn), lambda i, j: (i, ga_off + j)),
                  pl.BlockSpec((tm, tn), lambda i, j: (i, gr_off + j))],
        out_specs=tile,
        compiler_params=_cparams(("arbitrary", "arbitrary")),
        name="branch_merge",
    )(ya, yr, wa, wr, ga[0], gr[0])


def _mixout_kernel(m_ref, w_ref, x_ref, gt_ref, g_ref, sh_ref, sc_ref, rw_ref, rb_ref,
                   h_ref, *rest):
    u_refs, route_ref = rest[:-1], rest[-1]
    mix = jnp.dot(m_ref[...], w_ref[...], preferred_element_type=F32)
    h = x_ref[...] + gt_ref[0] * mix
    h_ref[...] = h
    u = h * lax.rsqrt(jnp.mean(h * h, axis=-1, keepdims=True) + NORM_EPS) * g_ref[...]
    u = u * (1.0 + sc_ref[0]) + sh_ref[0]
    wc = 2 * u_refs[0].shape[1]
    for c, u_ref in enumerate(u_refs):
        u_ref[...] = _pack_bf16_pairs(u[:, c * wc:(c + 1) * wc])
    u_hi = u.astype(BF16)
    u_lo = (u - u_hi.astype(F32)).astype(BF16)
    rw = rw_ref[...]
    w_hi = rw.astype(BF16)
    w_lo = (rw - w_hi.astype(F32)).astype(BF16)
    logits = (jnp.dot(u_hi, w_hi, preferred_element_type=F32)
              + jnp.dot(u_lo, w_hi, preferred_element_type=F32)
              + jnp.dot(u_hi, w_lo, preferred_element_type=F32)) + rb_ref[...]
    lane = lax.broadcasted_iota(jnp.int32, logits.shape, 1).astype(F32)
    big = float(LANES)
    neg = -jnp.inf
    lg = jnp.where(lane < N_GROUPS, logits, neg)
    mg = jnp.max(lg, axis=-1, keepdims=True)
    g_sel = jnp.min(jnp.where(lg == mg, lane, big), axis=-1, keepdims=True)
    p_group = 1.0 / jnp.sum(jnp.exp(lg - mg), axis=-1, keepdims=True)
    lo = N_GROUPS + EXPERTS_PER_GROUP * g_sel
    le = jnp.where((lane >= lo) & (lane < lo + EXPERTS_PER_GROUP), logits, neg)
    t1 = jnp.max(le, axis=-1, keepdims=True)
    i1 = jnp.min(jnp.where(le == t1, lane, big), axis=-1, keepdims=True)
    le2 = jnp.where(lane == i1, neg, le)
    t2 = jnp.max(le2, axis=-1, keepdims=True)
    i2 = jnp.min(jnp.where(le2 == t2, lane, big), axis=-1, keepdims=True)
    e21 = jnp.exp(t2 - t1)
    w1 = p_group / (1.0 + e21)
    w2 = p_group * e21 / (1.0 + e21)
    route = jnp.where(lane == 0, i1 - N_GROUPS,
                      jnp.where(lane == 1, i2 - N_GROUPS,
                                jnp.where(lane == 2, w1, jnp.where(lane == 3, w2, 0.0))))
    route_ref[...] = route


def _mix_out(merged, w_out, x2d, gate, g, shift, scale, rw, rb, seq):
    t, d = x2d.shape
    tm = 256
    per_b = seq // tm
    bsz = gate.shape[0]
    row = lambda dt: pl.BlockSpec((tm, d), lambda i: (i, 0))
    per_batch = pl.BlockSpec((1, 1, d), lambda i: (i // per_b, 0, 0))
    const = lambda shape: pl.BlockSpec(shape, lambda i: (0,) * len(shape))
    wc = d // GATHER_CHUNKS // 2
    outs = pl.pallas_call(
        _mixout_kernel,
        out_shape=(jax.ShapeDtypeStruct((t, d), F32),
                   *[jax.ShapeDtypeStruct((t, wc), jnp.uint32) for _ in range(GATHER_CHUNKS)],
                   jax.ShapeDtypeStruct((t, LANES), F32)),
        grid=(t // tm,),
        in_specs=[row(BF16), const((d, d)), row(F32), per_batch, const((1, d)), per_batch, per_batch,
                  const((d, LANES)), const((1, LANES))],
        out_specs=(row(F32), *[pl.BlockSpec((tm, wc), lambda i: (i, 0)) for _ in range(GATHER_CHUNKS)],
                   pl.BlockSpec((tm, LANES), lambda i: (i, 0))),
        compiler_params=_cparams(("arbitrary",)),
        name="mix_out_router",
    )(merged, w_out, x2d, gate.reshape(bsz, 1, d), g.reshape(1, d),
      shift.reshape(bsz, 1, d), scale.reshape(bsz, 1, d), rw, rb)
    return outs[0], list(outs[1:-1]), outs[-1]


def _expert_kernel(be_ref, nx_ref, nu_ref, *refs):
    nch = GATHER_CHUNKS
    x_refs = refs[:nch]
    w_hbm = refs[nch:nch + 3]
    ws_ref = refs[nch + 3]
    o_refs = refs[nch + 4:2 * nch + 4]
    stage = refs[2 * nch + 4:2 * nch + 7]
    wb = refs[2 * nch + 7:2 * nch + 10]
    sem = refs[2 * nch + 10]
    wc = 2 * o_refs[0].shape[1]
    i = pl.program_id(0)
    e = be_ref[i]
    nxt = nx_ref[i]
    active = i < nu_ref[0]
    first = i == 0
    run_start = jnp.logical_or(first, e != be_ref[jnp.maximum(i - 1, 0)])

    def weight_copies(expert):
        return [pltpu.make_async_copy(w_hbm[k].at[expert], stage[k], sem.at[k]) for k in range(3)]

    @pl.when(jnp.logical_and(active, first))
    def _():
        for cp in weight_copies(e):
            cp.start()

    @pl.when(jnp.logical_and(active, run_start))
    def _():
        for cp in weight_copies(e):
            cp.wait()
        for k in range(3):
            _cast_weight_tile(stage[k], wb[k])

        @pl.when(nxt >= 0)
        def _():
            for cp in weight_copies(nxt):
                cp.start()

    @pl.when(active)
    def _():
        x = jnp.concatenate([_unpack_bf16_pairs(r[...]) for r in x_refs], axis=1).astype(BF16)
        hg = jnp.dot(x, wb[0][...], preferred_element_type=F32)
        hu = jnp.dot(x, wb[1][...], preferred_element_type=F32)
        hdn = (_silu(hg) * hu).astype(BF16)
        y = jnp.dot(hdn, wb[2][...], preferred_element_type=F32)
        y = y * ws_ref[...]
        for c, o_ref in enumerate(o_refs):
            o_ref[...] = _pack_bf16_pairs(y[:, c * wc:(c + 1) * wc])

    @pl.when(jnp.logical_not(active))
    def _():
        for o_ref in o_refs:
            o_ref[...] = jnp.zeros_like(o_ref)


def _expert_ffn(xs, w_gate, w_up, w_down, w_slot, block_expert, next_expert, n_used):
    n_slots, wc = xs[0].shape
    d = 2 * wc * GATHER_CHUNKS
    hid = w_gate.shape[2]
    bm = MOE_ROWS
    chunk = pl.BlockSpec((bm, wc), lambda i, be, nx, nu: (i, 0))
    hbm = pl.BlockSpec(memory_space=pl.ANY)
    grid_spec = pltpu.PrefetchScalarGridSpec(
        num_scalar_prefetch=3,
        grid=(n_slots // bm,),
        in_specs=[*[chunk] * GATHER_CHUNKS, hbm, hbm, hbm,
                  pl.BlockSpec((bm, 1), lambda i, be, nx, nu: (i, 0))],
        out_specs=[chunk] * GATHER_CHUNKS,
        scratch_shapes=[pltpu.VMEM((d, hid), F32), pltpu.VMEM((d, hid), F32), pltpu.VMEM((hid, d), F32),
                        pltpu.VMEM((d, hid), BF16), pltpu.VMEM((d, hid), BF16), pltpu.VMEM((hid, d), BF16),
                        pltpu.SemaphoreType.DMA((3,))],
    )
    return pl.pallas_call(
        _expert_kernel,
        out_shape=[jax.ShapeDtypeStruct((n_slots, wc), jnp.uint32) for _ in range(GATHER_CHUNKS)],
        grid_spec=grid_spec,
        compiler_params=_cparams(("arbitrary",)),
        name="expert_ffn",
    )(block_expert, next_expert, n_used, *xs, w_gate, w_up, w_down, w_slot)


def _gather_rows(table, idx):
    m = idx.shape[0]
    d = table.shape[1]
    window = LANES
    mesh = plsc.VectorSubcoreMesh(core_axis_name="core", subcore_axis_name="subcore")

    @pl.kernel(out_type=jax.ShapeDtypeStruct((m, d), table.dtype), mesh=mesh, scratch_types=[])
    def gather(x_hbm, i_hbm, o_hbm):
        def body(i_vmem, o_vmem):
            pltpu.sync_copy(x_hbm.at[i_vmem.at[0]], o_vmem)

        pltpu.emit_pipeline(
            body,
            grid=(m // window,),
            in_specs=[pl.BlockSpec((1, window), lambda i: (0, i))],
            out_specs=[pl.BlockSpec((window, d), lambda i: (i, 0))],
            core_axis_name=("core", "subcore"),
            dimension_semantics=(pltpu.PARALLEL,),
        )(i_hbm, o_hbm)

    return gather(table, idx.reshape(1, m))


def _final_kernel(h_ref, *rest):
    y_refs = rest[:TOP_K * GATHER_CHUNKS]
    gt_ref, g_ref, o_ref, slab_ref = rest[TOP_K * GATHER_CHUNKS:]
    sub = h_ref.shape[2]
    nslab = h_ref.shape[3] // LANES
    for j in range(STREAMS):
        ffn = sum(jnp.concatenate([_unpack_bf16_pairs(y_refs[k * GATHER_CHUNKS + c][0, 0, j])
                                   for c in range(GATHER_CHUNKS)], axis=1) for k in range(TOP_K))
        h = h_ref[0, j] + gt_ref[0] * ffn
        y = h * lax.rsqrt(jnp.mean(h * h, axis=-1, keepdims=True) + NORM_EPS) * g_ref[...]
        for c in range(nslab):
            slab_ref[c, pl.ds(j, sub, stride=STREAMS), :] = y[:, c * LANES:(c + 1) * LANES]
    for c in range(nslab):
        o_ref[:, c * LANES:(c + 1) * LANES] = slab_ref[c]


def _final(h, y2, gate, g, seq):
    t, d = h.shape
    tm = 512
    per_b = seq // tm
    bsz = gate.shape[0]
    sub = tm // STREAMS
    spec = pl.BlockSpec((1, STREAMS, sub, d), lambda i: (i // per_b, 0, i % per_b, 0))
    wc = d // GATHER_CHUNKS // 2
    y5 = [a.reshape(TOP_K, bsz, STREAMS, seq // STREAMS, wc) for a in y2]

    def yspec(k):
        return pl.BlockSpec((1, 1, STREAMS, sub, wc), lambda i: (k, i // per_b, 0, i % per_b, 0))

    return pl.pallas_call(
        _final_kernel,
        out_shape=jax.ShapeDtypeStruct((t, d), F32),
        grid=(t // tm,),
        in_specs=[spec, *[yspec(k) for k in range(TOP_K) for _ in range(GATHER_CHUNKS)],
                  pl.BlockSpec((1, 1, d), lambda i: (i // per_b, 0, 0)),
                  pl.BlockSpec((1, d), lambda i: (0, 0))],
        out_specs=pl.BlockSpec((tm, d), lambda i: (i, 0)),
        scratch_shapes=[pltpu.VMEM((d // LANES, tm, LANES), F32)],
        compiler_params=_cparams(("arbitrary",)),
        name="final_norm",
    )(h.reshape(bsz, STREAMS, seq // STREAMS, d), *(y5 * TOP_K), gate.reshape(bsz, 1, d), g.reshape(1, d))


def _dispatch_plan(expert_idx, weights):
    n_assign = expert_idx.size
    n_blocks = n_assign // MOE_ROWS + N_EXPERTS
    n_slots = n_blocks * MOE_ROWS
    flat_e = expert_idx.reshape(-1)
    flat_w = weights.reshape(-1)
    ids = jnp.arange(n_assign, dtype=jnp.int32)
    eids = jnp.arange(N_EXPERTS, dtype=jnp.int32)[None, :]
    _, order = lax.sort((flat_e, ids), num_keys=1, is_stable=True)
    _, rank_sorted = lax.sort((order, ids), num_keys=1)
    hot_a = (flat_e[:, None] == eids).astype(jnp.int32)
    counts = jnp.sum(hot_a, axis=0)
    padded = ((counts + MOE_ROWS - 1) // MOE_ROWS) * MOE_ROWS
    pad_end = jnp.cumsum(padded)
    pad_start = pad_end - padded
    start = jnp.cumsum(counts) - counts
    dest = (rank_sorted + jnp.sum(hot_a * (pad_start - start)[None, :], axis=1)).reshape(-1, TOP_K)
    blk0 = jnp.arange(n_blocks, dtype=jnp.int32) * MOE_ROWS
    block_expert = jnp.minimum(jnp.sum((pad_end[None, :] <= blk0[:, None]).astype(jnp.int32), axis=1),
                               N_EXPERTS - 1)
    hot_b = (block_expert[:, None] == eids).astype(jnp.int32)
    blk_shift = jnp.sum(hot_b * (start - pad_start)[None, :], axis=1)
    blk_count = jnp.sum(hot_b * (pad_start + counts)[None, :], axis=1)
    slot = jnp.arange(n_slots, dtype=jnp.int32).reshape(n_blocks, MOE_ROWS)
    valid = (slot < blk_count[:, None]).reshape(-1)
    src = jnp.clip(slot + blk_shift[:, None], 0, n_assign - 1).reshape(-1)
    assign = order[src]
    token_of_slot = jnp.where(valid, assign // TOP_K, slot.reshape(-1) % (n_assign // TOP_K))
    weight_of_slot = jnp.where(valid, flat_w[assign], 0.0)
    n_used = (pad_end[-1:] // MOE_ROWS).astype(jnp.int32)
    bi = jnp.arange(n_blocks, dtype=jnp.int32)
    later = ((bi[None, :] > bi[:, None]) & (block_expert[None, :] != block_expert[:, None])
             & (bi[None, :] < n_used[0]))
    next_expert = jnp.where(jnp.any(later, axis=1), block_expert[jnp.argmax(later, axis=1)], -1)
    return token_of_slot, weight_of_slot, dest, block_expert, next_expert.astype(jnp.int32), n_used


def kernel(x, c, positions, ada_w, ada_b, mix_norm_g, w_in, w_attn_branch, w_rec_branch, w_mix_out,
           rec_norm_g, rec_lb_logits, ffn_norm_g, router_group_w, router_group_b, router_expert_w,
           router_expert_b, expert_w_gate, expert_w_up, expert_w_down, final_norm_g):
    bsz, seq, d = x.shape
    t = bsz * seq
    depth = ada_w.shape[0]
    assert depth == 1, "final norm is fused after the single layer"
    lower_bounds = jnp.cumsum(jax.nn.softmax(rec_lb_logits.astype(F32), axis=0), axis=0)
    pos_streams = positions.reshape(bsz, seq // STREAMS, STREAMS).transpose(0, 2, 1)
    rope = _rope_tables(pos_streams)
    h = x.reshape(t, d)
    for layer in range(depth):
        mod = _modulation(c, ada_w[layer], ada_b[layer])
        sh_m, sc_m, gt_m, sh_f, sc_f, gt_f = jnp.split(mod, 6, axis=-1)
        h, u = _norm_modulate(h, mix_norm_g[layer], sh_m, sc_m, seq)
        w = w_in[layer]
        qk = _project(u, w, 0, 2 * d, BF16, rope=rope)
        rest = _project(u, w, 2 * d, 2 * d, BF16)
        f_r = _project(u, w, 4 * d, d, F32)
        tail = _project(u, w, 5 * d, 4 * d, BF16)
        y_attn = _attention((qk, 0), (qk, d), (rest, 0), d, bsz, seq)
        y_rec = _recurrence((rest, d), (f_r, 0), (tail, 0), (tail, d), lower_bounds[layer],
                            rec_norm_g[layer], bsz, seq)
        merged = _merge(y_attn, y_rec, w_attn_branch[layer].astype(BF16),
                        w_rec_branch[layer].astype(BF16), (tail, 2 * d), (tail, 3 * d))
        rw = jnp.concatenate([router_group_w[layer], router_expert_w[layer],
                              jnp.zeros((d, LANES - N_GROUPS - N_EXPERTS), F32)], axis=1)
        rb = jnp.concatenate([router_group_b[layer], router_expert_b[layer],
                              jnp.zeros((LANES - N_GROUPS - N_EXPERTS,), F32)]).reshape(1, LANES)
        h, u2, route = _mix_out(merged, w_mix_out[layer].astype(BF16), h, gt_m, ffn_norm_g[layer],
                                sh_f, sc_f, rw, rb, seq)
        expert_idx = route[:, :TOP_K].astype(jnp.int32)
        weights = route[:, TOP_K:2 * TOP_K]
        tok, w_slot, dest, block_expert, next_expert, n_used = _dispatch_plan(expert_idx, weights)
        xs = [_gather_rows(a, tok) for a in u2]
        ys = _expert_ffn(xs, expert_w_gate[layer], expert_w_up[layer], expert_w_down[layer],
                         w_slot.reshape(-1, 1), block_expert, next_expert, n_used)
        dest_kt = dest.T.reshape(-1)
        y2 = [_gather_rows(a, dest_kt) for a in ys]
        h = _final(h, y2, gt_f, final_norm_g, seq)
    return h.reshape(bsz, seq, d)
```

```python
import functools

import jax
import jax.numpy as jnp
from jax import lax
from jax.experimental import pallas as pl
from jax.experimental.pallas import tpu as pltpu
from jax.experimental.pallas import tpu_sc as plsc

F32 = jnp.float32
BF16 = jnp.bfloat16

D_MODEL = 2048
HEAD_DIM = 128
N_HEADS = D_MODEL // HEAD_DIM
ROPE_DIM = HEAD_DIM // 4
ROPE_HALF = ROPE_DIM // 2
ROPE_THETA = 500000.0
ATTN_SPAN = 128
ATTN_BLOCK = 128
REC_CHUNK = 64
N_GROUPS = 4
EXPERTS_PER_GROUP = 8
N_EXPERTS = N_GROUPS * EXPERTS_PER_GROUP
EXPERT_HIDDEN = D_MODEL // 2
TOP_K = 2
NORM_EPS = 1e-6
IN_WIDTH = 9 * D_MODEL

LANES = 128
VMEM_LIMIT = 56 * 1024 * 1024

MOE_ROWS = 256
HEADS_PER_STEP = 2
REC_HEADS_PER_STEP = 4
GATHER_CHUNKS = 4
STREAMS = 4
DILATIONS = ((128, 1), (512, 4), (2048, 16))


def _cparams(sem):
    return pltpu.CompilerParams(dimension_semantics=sem, vmem_limit_bytes=VMEM_LIMIT)


def _sigmoid(x):
    return 1.0 / (1.0 + jnp.exp(-x))


def _silu(x):
    return x * _sigmoid(x)


def _pack_bf16_pairs(x):
    w = x.shape[1] // 2
    lo = lax.bitcast_convert_type(x[:, :w].astype(BF16).astype(F32), jnp.uint32)
    hi = lax.bitcast_convert_type(x[:, w:].astype(BF16).astype(F32), jnp.uint32)
    return (lo >> 16) | (hi & jnp.uint32(0xFFFF0000))


def _unpack_bf16_pairs(words):
    lo = lax.bitcast_convert_type(words << 16, F32)
    hi = lax.bitcast_convert_type(words & jnp.uint32(0xFFFF0000), F32)
    return jnp.concatenate([lo, hi], axis=1)


def _mod_kernel(c_ref, w_ref, b_ref, o_ref):
    cond = _silu(c_ref[...])
    o_ref[...] = jnp.dot(cond, w_ref[...], precision=lax.Precision.HIGHEST,
                         preferred_element_type=F32) + b_ref[...]


def _modulation(c, w, b):
    bsz, d = c.shape
    n = w.shape[1]
    tn = 1536
    return pl.pallas_call(
        _mod_kernel,
        out_shape=jax.ShapeDtypeStruct((bsz, n), F32),
        grid=(n // tn,),
        in_specs=[pl.BlockSpec((bsz, d), lambda j: (0, 0)),
                  pl.BlockSpec((d, tn), lambda j: (0, j)),
                  pl.BlockSpec((1, tn), lambda j: (0, j))],
        out_specs=pl.BlockSpec((bsz, tn), lambda j: (0, j)),
        compiler_params=_cparams(("arbitrary",)),
        name="adaln_mod",
    )(c, w, b.reshape(1, n))


def _rope_kernel(pos_ref, freq_ref, c_ref, sa_ref, sb_ref):
    ang = pos_ref[...] * freq_ref[...]
    lane = lax.broadcasted_iota(jnp.int32, ang.shape, 1)
    cos, sin = jnp.cos(ang), jnp.sin(ang)
    c_ref[...] = jnp.where(lane < ROPE_DIM, cos, 1.0)
    sa_ref[...] = jnp.where(lane < ROPE_HALF, -sin, 0.0)
    sb_ref[...] = jnp.where((lane >= ROPE_HALF) & (lane < ROPE_DIM), sin, 0.0)


def _rope_tables(positions):
    t = positions.size
    tm = 2048
    inv_freq = ROPE_THETA ** (-jnp.arange(0, ROPE_DIM, 2, dtype=F32) / ROPE_DIM)
    freq = jnp.concatenate([inv_freq, inv_freq, jnp.zeros((LANES - ROPE_DIM,), F32)]).reshape(1, LANES)
    pos = positions.astype(F32).reshape(t, 1)
    out = jax.ShapeDtypeStruct((t, LANES), F32)
    return pl.pallas_call(
        _rope_kernel,
        out_shape=(out, out, out),
        grid=(t // tm,),
        in_specs=[pl.BlockSpec((tm, 1), lambda i: (i, 0)),
                  pl.BlockSpec((1, LANES), lambda i: (0, 0))],
        out_specs=tuple(pl.BlockSpec((tm, LANES), lambda i: (i, 0)) for _ in range(3)),
        compiler_params=_cparams(("arbitrary",)),
        name="rope_tables",
    )(pos, freq)


def _norm_mod_kernel(x_ref, g_ref, sh_ref, sc_ref, xp_ref, u_ref, slab_ref):
    rows = x_ref.shape[0]
    sub = rows // STREAMS
    nslab = x_ref.shape[1] // LANES
    for c in range(nslab):
        slab_ref[c] = x_ref[:, c * LANES:(c + 1) * LANES]
    for j in range(STREAMS):
        x = jnp.concatenate([slab_ref[c, pl.ds(j, sub, stride=STREAMS), :] for c in range(nslab)], axis=1)
        xp_ref[0, j] = x
        y = x * lax.rsqrt(jnp.mean(x * x, axis=-1, keepdims=True) + NORM_EPS) * g_ref[...]
        u_ref[0, j] = (y * (1.0 + sc_ref[0]) + sh_ref[0]).astype(u_ref.dtype)


def _norm_modulate(x2d, g, shift, scale, seq):
    t, d = x2d.shape
    tm = 512
    per_b = seq // tm
    bsz = shift.shape[0]
    sub = tm // STREAMS
    out_spec = pl.BlockSpec((1, STREAMS, sub, d), lambda i: (i // per_b, 0, i % per_b, 0))
    xp, u = pl.pallas_call(
        _norm_mod_kernel,
        out_shape=(jax.ShapeDtypeStruct((bsz, STREAMS, seq // STREAMS, d), F32),
                   jax.ShapeDtypeStruct((bsz, STREAMS, seq // STREAMS, d), BF16)),
        grid=(t // tm,),
        in_specs=[pl.BlockSpec((tm, d), lambda i: (i, 0)),
                  pl.BlockSpec((1, d), lambda i: (0, 0)),
                  pl.BlockSpec((1, 1, d), lambda i: (i // per_b, 0, 0)),
                  pl.BlockSpec((1, 1, d), lambda i: (i // per_b, 0, 0))],
        out_specs=(out_spec, out_spec),
        scratch_shapes=[pltpu.VMEM((d // LANES, tm, LANES), F32)],
        compiler_params=_cparams(("arbitrary",)),
        name="norm_modulate",
    )(x2d, g.reshape(1, d), shift.reshape(bsz, 1, d), scale.reshape(bsz, 1, d))
    return xp.reshape(t, d), u.reshape(t, d)


def _cast_weight_tile(w_ref, wb_ref):
    rows = 256

    def body(r, carry):
        sl = pl.ds(pl.multiple_of(r * rows, rows), rows)
        wb_ref[sl, :] = w_ref[sl, :].astype(BF16)
        return carry

    lax.fori_loop(0, w_ref.shape[0] // rows, body, 0)


def _proj_kernel(a_ref, w_ref, o_ref, wb_ref):
    @pl.when(pl.program_id(1) == 0)
    def _():
        _cast_weight_tile(w_ref, wb_ref)

    o_ref[...] = jnp.dot(a_ref[...], wb_ref[...], preferred_element_type=F32).astype(o_ref.dtype)


def _proj_rope_kernel(a_ref, w_ref, c_ref, sa_ref, sb_ref, o_ref, wb_ref):
    @pl.when(pl.program_id(1) == 0)
    def _():
        _cast_weight_tile(w_ref, wb_ref)

    acc = jnp.dot(a_ref[...], wb_ref[...], preferred_element_type=F32)
    c, sa, sb = c_ref[...], sa_ref[...], sb_ref[...]
    for h in range(acc.shape[1] // HEAD_DIM):
        x = acc[:, h * HEAD_DIM:(h + 1) * HEAD_DIM]
        up = pltpu.roll(x, HEAD_DIM - ROPE_HALF, 1)
        dn = pltpu.roll(x, ROPE_HALF, 1)
        o_ref[:, h * HEAD_DIM:(h + 1) * HEAD_DIM] = (x * c + up * sa + dn * sb).astype(o_ref.dtype)


def _project(u, w, col0, ncols, out_dtype, rope=None):
    t, d = u.shape
    tm, tn = 1024, 1024
    j0 = col0 // tn
    in_specs = [pl.BlockSpec((tm, d), lambda j, i: (i, 0)),
                pl.BlockSpec((d, tn), lambda j, i: (0, j0 + j))]
    args = [u, w]
    kern = _proj_kernel
    if rope is not None:
        in_specs += [pl.BlockSpec((tm, LANES), lambda j, i: (i, 0)) for _ in range(3)]
        args += list(rope)
        kern = _proj_rope_kernel
    return pl.pallas_call(
        kern,
        out_shape=jax.ShapeDtypeStruct((t, ncols), out_dtype),
        grid=(ncols // tn, t // tm),
        in_specs=in_specs,
        out_specs=pl.BlockSpec((tm, tn), lambda j, i: (i, j)),
        scratch_shapes=[pltpu.VMEM((d, tn), BF16)],
        compiler_params=_cparams(("arbitrary", "arbitrary")),
        name="in_proj_rope" if rope is not None else "in_proj",
    )(*args)


def _key_pieces(qb, sub):
    L = ATTN_BLOCK
    pieces = [(0, 0, L * (qb + 1))]
    for r in range(1, STREAMS):
        back = 0 if qb == 0 else (L // 2 if r == 1 else L // 4)
        pieces.append((r, L * qb - back, L + back))
    return pieces


def _attention_bias(sub):
    L = ATTN_BLOCK
    nqb = sub // L
    kmax = max(sum(p[2] for p in _key_pieces(qb, sub)) for qb in range(nqb))
    qi = jnp.arange(L, dtype=jnp.int32)[:, None]
    out = []
    for j in range(STREAMS):
        row = []
        for qb in range(nqb):
            cols = []
            for r, start, size in _key_pieces(qb, sub):
                jp = (j + r) % STREAMS
                kn = start + jnp.arange(size, dtype=jnp.int32)[None, :]
                dt = STREAMS * (L * qb + qi - kn) + (j - jp)
                cnt = jnp.zeros(dt.shape, F32)
                for window, dil in DILATIONS:
                    cnt += ((dt >= 0) & (dt <= window) & (dt % dil == 0)).astype(F32)
                cols.append(jnp.log2(cnt))
            tile = jnp.concatenate(cols, axis=1)
            row.append(jnp.pad(tile, ((0, 0), (0, kmax - tile.shape[1])), constant_values=-jnp.inf))
        out.append(jnp.stack(row))
    return jnp.stack(out)


def _attn_kernel(q_ref, k_ref, v_ref, bias_ref, o_ref, s0, s1, m0, m1, p0, p1):
    j = pl.program_id(2)
    L = ATTN_BLOCK
    sub = o_ref.shape[1]
    scale = HEAD_DIM ** -0.5 * 1.4426950408889634
    s_buf, m_buf, p_buf = (s0, s1), (m0, m1), (p0, p1)
    units = [(g, qb) for g in range(HEADS_PER_STEP) for qb in range(sub // L)]

    def key_rows(qb):
        rows = []
        for r, start, size in _key_pieces(qb, sub):
            base = pl.multiple_of(((j + r) % STREAMS) * sub, sub)
            rows.append(pl.ds(base + start, size))
        return rows, sum(p[2] for p in _key_pieces(qb, sub))

    def score_stage(u):
        g, qb = units[u]
        lanes = slice(g * HEAD_DIM, (g + 1) * HEAD_DIM)
        rows, ktot = key_rows(qb)
        q = q_ref[0, pl.ds(pl.multiple_of(j * sub, sub) + L * qb, L), lanes]
        k_all = jnp.concatenate([k_ref[0, rw, lanes] for rw in rows], axis=0)
        mx = None
        for c in range(0, ktot, 2 * LANES):
            w = min(2 * LANES, ktot - c)
            s = lax.dot_general(q, k_all[c:c + w], (((1,), (1,)), ((), ())), preferred_element_type=F32)
            s = s * scale + bias_ref[j, qb, :, c:c + w]
            s_buf[u % 2][:, c:c + w] = s
            for cc in range(0, w, LANES):
                part = s[:, cc:cc + LANES]
                mx = part if mx is None else jnp.maximum(mx, part)
        m_buf[u % 2][...] = jnp.broadcast_to(jnp.max(mx, axis=-1, keepdims=True), mx.shape)

    def prob_stage(u):
        _, qb = units[u]
        _, ktot = key_rows(qb)
        m = m_buf[u % 2][...]
        for c in range(0, ktot, LANES):
            p_buf[u % 2][:, c:c + LANES] = jnp.exp2(s_buf[u % 2][:, c:c + LANES] - m).astype(BF16)

    def out_stage(u):
        g, qb = units[u]
        lanes = slice(g * HEAD_DIM, (g + 1) * HEAD_DIM)
        rows, ktot = key_rows(qb)
        v_all = jnp.concatenate([v_ref[0, rw, lanes] for rw in rows], axis=0)
        v_ext = jnp.concatenate([v_all, jnp.ones_like(v_all)], axis=1)
        oe = jnp.dot(p_buf[u % 2][:, :ktot], v_ext, preferred_element_type=F32)
        y = oe[:, :HEAD_DIM] / oe[:, HEAD_DIM:]
        o_ref[0, L * qb:L * (qb + 1), lanes] = y.astype(o_ref.dtype)

    for t in range(len(units) + 2):
        if 0 <= t - 2:
            out_stage(t - 2)
        if 0 <= t - 1 < len(units):
            prob_stage(t - 1)
        if t < len(units):
            score_stage(t)


def _cols(src, bsz, seq, gw, grid_rank):
    a, col0 = src
    off = col0 // gw
    if grid_rank == 3:
        spec = pl.BlockSpec((1, seq, gw), lambda b, h, j: (b, 0, off + h))
    else:
        spec = pl.BlockSpec((1, seq, gw), lambda b, h: (b, 0, off + h))
    return a.reshape(bsz, seq, a.shape[1]), spec


def _attention(q, k, v, width, bsz, seq):
    t = bsz * seq
    gw = HEADS_PER_STEP * HEAD_DIM
    sub = seq // STREAMS
    bias = _attention_bias(sub)
    (qa, qs), (ka, ks), (va, vs) = (_cols(s, bsz, seq, gw, 3) for s in (q, k, v))
    out = pl.pallas_call(
        _attn_kernel,
        out_shape=jax.ShapeDtypeStruct((bsz, seq, width), BF16),
        grid=(bsz, width // gw, STREAMS),
        in_specs=[qs, ks, vs,
                  pl.BlockSpec(bias.shape, lambda b, h, j: (0, 0, 0, 0))],
        out_specs=pl.BlockSpec((1, sub, gw), lambda b, h, j: (b, j, h)),
        scratch_shapes=[pltpu.VMEM((ATTN_BLOCK, bias.shape[-1]), F32)] * 2
        + [pltpu.VMEM((ATTN_BLOCK, LANES), F32)] * 2
        + [pltpu.VMEM((ATTN_BLOCK, bias.shape[-1]), BF16)] * 2,
        compiler_params=_cparams(("arbitrary", "arbitrary", "arbitrary")),
        name="dilated_attention",
    )(qa, ka, va, bias)
    return out.reshape(t, width)


def _split3(x):
    hi = x.astype(BF16)
    r1 = x - hi.astype(F32)
    mid = r1.astype(BF16)
    lo = (r1 - mid.astype(F32)).astype(BF16)
    return hi, mid, lo


def _rec_scratch():
    C, K = REC_CHUNK, HEAD_DIM
    return [pltpu.VMEM((K, K), F32), pltpu.VMEM((C, K), F32), pltpu.VMEM((C, K), F32),
            pltpu.VMEM((C, C), BF16), pltpu.VMEM((C, K), BF16), pltpu.VMEM((K, K), F32),
            pltpu.VMEM((1, K), F32)]


def _rec_kernel(q_ref, f_ref, i_ref, g_ref, lb_ref, ng_ref, o_ref, *scratch):
    C = REC_CHUNK
    seq = q_ref.shape[1]
    sub = seq // STREAMS
    piece = C // STREAMS
    for g in range(REC_HEADS_PER_STEP):
        st_ref = scratch[g * (len(scratch) // REC_HEADS_PER_STEP)]
        st_ref[...] = jnp.zeros_like(st_ref)
    pi = lax.broadcasted_iota(jnp.int32, (C, C), 0)
    si = lax.broadcasted_iota(jnp.int32, (C, C), 1)
    time_of = lambda p: STREAMS * (p % piece) + p // piece
    causal = time_of(si) <= time_of(pi)
    tri = jnp.where(causal, 1.0, 0.0).astype(BF16)
    row_of = lambda tau: (tau % STREAMS) * piece + tau // STREAMS
    last = row_of(C - 1)
    anchor = row_of(C // 2 - 1)

    def rows(c, j):
        start = j * sub + c * piece
        return pl.ds(start if isinstance(start, int) else pl.multiple_of(start, piece), piece)

    def load(ref, c, lanes):
        return jnp.concatenate([ref[0, rows(c, j), lanes] for j in range(STREAMS)], axis=0)

    def gates(c, g):
        lanes, (_, b_s, kk_s, _, _, _, _) = head(g)
        lb = lb_ref[:, lanes]
        x = load(f_ref, c, lanes)
        z = jnp.exp(-jnp.abs(x))
        r = 1.0 / (1.0 + z)
        pos = x >= 0
        sig_p = jnp.where(pos, r, z * r)
        sig_n = jnp.where(pos, z * r, r)
        logf = jnp.log(lb + (1.0 - lb) * sig_p)
        hi, mid, lo = _split3(logf)
        b_s[...] = (jnp.dot(tri, hi, preferred_element_type=F32)
                    + jnp.dot(tri, mid, preferred_element_type=F32)
                    + jnp.dot(tri, lo, preferred_element_type=F32))
        kk_s[...] = (1.0 - lb) * sig_n

    def scores(c, g):
        lanes, (_, b_s, kk_s, sc_s, qin_s, upd_s, dec_s) = head(g)
        b, kk = b_s[...], kk_s[...]
        b_last = b[last:last + 1, :]
        b_mid = b[anchor:anchor + 1, :]
        qs = _silu(load(q_ref, c, lanes).astype(F32))
        v = load(i_ref, c, lanes)
        q_a = (qs * jnp.exp(b - b_mid)).astype(BF16)
        k_a = (kk * jnp.exp(b_mid - b)).astype(BF16)
        k_e = (kk * jnp.exp(b_last - b)).astype(BF16)
        sc = lax.dot_general(q_a, k_a, (((1,), (1,)), ((), ())), preferred_element_type=F32)
        sc_s[...] = jnp.where(causal, sc, 0.0).astype(BF16)
        qin_s[...] = (qs * jnp.exp(b)).astype(BF16)
        upd_s[...] = lax.dot_general(v, k_e, (((0,), (0,)), ((), ())), preferred_element_type=F32)
        dec_s[...] = jnp.exp(b_last)

    def output(c, g):
        lanes, (st_ref, _, _, sc_s, qin_s, upd_s, dec_s) = head(g)
        v = load(i_ref, c, lanes)
        st = st_ref[...]
        o = (jnp.dot(sc_s[...], v, preferred_element_type=F32)
             + lax.dot_general(qin_s[...], st.astype(BF16), (((1,), (1,)), ((), ())),
                               preferred_element_type=F32))
        st_ref[...] = dec_s[...] * st + upd_s[...]
        y = o * lax.rsqrt(jnp.mean(o * o, axis=-1, keepdims=True) + NORM_EPS) * ng_ref[...]
        y = (y * _silu(load(g_ref, c, lanes).astype(F32))).astype(o_ref.dtype)
        for j in range(STREAMS):
            o_ref[0, rows(c, j), lanes] = y[j * piece:(j + 1) * piece]

    per_head = len(scratch) // REC_HEADS_PER_STEP

    def head(g):
        return slice(g * HEAD_DIM, (g + 1) * HEAD_DIM), scratch[g * per_head:(g + 1) * per_head]

    def stages(c, run):
        for g in range(REC_HEADS_PER_STEP):
            if run[2]:
                output(c - 2, g)
            if run[1]:
                scores(c - 1, g)
            if run[0]:
                gates(c, g)

    nc = seq // C
    stages(0, (True, False, False))
    stages(1, (True, True, False))

    def body(c, carry):
        stages(c, (True, True, True))
        return carry

    lax.fori_loop(2, nc, body, 0)
    stages(nc, (False, True, True))
    stages(nc + 1, (False, False, True))


def _recurrence(q_r, f_r, i_r, g_r, lower_bound, norm_g, bsz, seq):
    t = bsz * seq
    width = lower_bound.shape[0]
    gw = REC_HEADS_PER_STEP * HEAD_DIM
    spec = pl.BlockSpec((1, seq, gw), lambda b, h: (b, 0, h))
    (qa, qs), (fa, fs), (ia, isp), (ga, gs) = (_cols(s, bsz, seq, gw, 2) for s in (q_r, f_r, i_r, g_r))
    out = pl.pallas_call(
        _rec_kernel,
        out_shape=jax.ShapeDtypeStruct((bsz, seq, width), BF16),
        grid=(bsz, width // gw),
        in_specs=[qs, fs, isp, gs,
                  pl.BlockSpec((1, gw), lambda b, h: (0, h)),
                  pl.BlockSpec((1, HEAD_DIM), lambda b, h: (0, 0))],
        out_specs=spec,
        scratch_shapes=[s for _ in range(REC_HEADS_PER_STEP) for s in _rec_scratch()],
        compiler_params=_cparams(("arbitrary", "arbitrary")),
        name="hgrn2_recurrence",
    )(qa, fa, ia, ga, lower_bound.reshape(1, width), norm_g.reshape(1, HEAD_DIM))
    return out.reshape(t, width)


def _merge_kernel(ya_ref, yr_ref, wa_ref, wr_ref, ga_ref, gr_ref, o_ref):
    a = jnp.dot(ya_ref[...], wa_ref[...], preferred_element_type=F32)
    r = jnp.dot(yr_ref[...], wr_ref[...], preferred_element_type=F32)
    m = _sigmoid(ga_ref[...].astype(F32)) * a + _sigmoid(gr_ref[...].astype(F32)) * r
    o_ref[...] = m.astype(o_ref.dtype)


def _merge(ya, yr, wa, wr, ga, gr):
    t, d = ya.shape
    n = wa.shape[1]
    tm, tn = 1024, 512
    row = pl.BlockSpec((tm, d), lambda i, j: (i, 0))
    col = pl.BlockSpec((d, tn), lambda i, j: (0, j))
    tile = pl.BlockSpec((tm, tn), lambda i, j: (i, j))
    ga_off, gr_off = ga[1] // tn, gr[1] // tn
    return pl.pallas_call(
        _merge_kernel,
        out_shape=jax.ShapeDtypeStruct((t, n), BF16),
        grid=(t // tm, n // tn),
        in_specs=[row, row, col, col,
                  pl.BlockSpec((tm, tn), lambda i, j: (i, ga_off + j)),
                  pl.BlockSpec((tm, tn), lambda i, j: (i, gr_off + j))],
        out_specs=tile,
        compiler_params=_cparams(("arbitrary", "arbitrary")),
        name="branch_merge",
    )(ya, yr, wa, wr, ga[0], gr[0])


def _mixout_kernel(m_ref, w_ref, x_ref, gt_ref, g_ref, sh_ref, sc_ref, rw_ref, rb_ref,
                   h_ref, *rest):
    u_refs, route_ref = rest[:-1], rest[-1]
    mix = jnp.dot(m_ref[...], w_ref[...], preferred_element_type=F32)
    h = x_ref[...] + gt_ref[0] * mix
    h_ref[...] = h
    u = h * lax.rsqrt(jnp.mean(h * h, axis=-1, keepdims=True) + NORM_EPS) * g_ref[...]
    u = u * (1.0 + sc_ref[0]) + sh_ref[0]
    wc = 2 * u_refs[0].shape[1]
    for c, u_ref in enumerate(u_refs):
        u_ref[...] = _pack_bf16_pairs(u[:, c * wc:(c + 1) * wc])
    u_hi = u.astype(BF16)
    u_lo = (u - u_hi.astype(F32)).astype(BF16)
    rw = rw_ref[...]
    w_hi = rw.astype(BF16)
    w_lo = (rw - w_hi.astype(F32)).astype(BF16)
    logits = (jnp.dot(u_hi, w_hi, preferred_element_type=F32)
              + jnp.dot(u_lo, w_hi, preferred_element_type=F32)
              + jnp.dot(u_hi, w_lo, preferred_element_type=F32)) + rb_ref[...]
    lane = lax.broadcasted_iota(jnp.int32, logits.shape, 1).astype(F32)
    big = float(LANES)
    neg = -jnp.inf
    lg = jnp.where(lane < N_GROUPS, logits, neg)
    mg = jnp.max(lg, axis=-1, keepdims=True)
    g_sel = jnp.min(jnp.where(lg == mg, lane, big), axis=-1, keepdims=True)
    p_group = 1.0 / jnp.sum(jnp.exp(lg - mg), axis=-1, keepdims=True)
    lo = N_GROUPS + EXPERTS_PER_GROUP * g_sel
    le = jnp.where((lane >= lo) & (lane < lo + EXPERTS_PER_GROUP), logits, neg)
    t1 = jnp.max(le, axis=-1, keepdims=True)
    i1 = jnp.min(jnp.where(le == t1, lane, big), axis=-1, keepdims=True)
    le2 = jnp.where(lane == i1, neg, le)
    t2 = jnp.max(le2, axis=-1, keepdims=True)
    i2 = jnp.min(jnp.where(le2 == t2, lane, big), axis=-1, keepdims=True)
    e21 = jnp.exp(t2 - t1)
    w1 = p_group / (1.0 + e21)
    w2 = p_group * e21 / (1.0 + e21)
    route = jnp.where(lane == 0, i1 - N_GROUPS,
                      jnp.where(lane == 1, i2 - N_GROUPS,
                                jnp.where(lane == 2, w1, jnp.where(lane == 3, w2, 0.0))))
    route_ref[...] = route


def _mix_out(merged, w_out, x2d, gate, g, shift, scale, rw, rb, seq):
    t, d = x2d.shape
    tm = 512
    per_b = seq // tm
    bsz = gate.shape[0]
    row = lambda dt: pl.BlockSpec((tm, d), lambda i: (i, 0))
    per_batch = pl.BlockSpec((1, 1, d), lambda i: (i // per_b, 0, 0))
    const = lambda shape: pl.BlockSpec(shape, lambda i: (0,) * len(shape))
    wc = d // GATHER_CHUNKS // 2
    outs = pl.pallas_call(
        _mixout_kernel,
        out_shape=(jax.ShapeDtypeStruct((t, d), F32),
                   *[jax.ShapeDtypeStruct((t, wc), jnp.uint32) for _ in range(GATHER_CHUNKS)],
                   jax.ShapeDtypeStruct((t, LANES), F32)),
        grid=(t // tm,),
        in_specs=[row(BF16), const((d, d)), row(F32), per_batch, const((1, d)), per_batch, per_batch,
                  const((d, LANES)), const((1, LANES))],
        out_specs=(row(F32), *[pl.BlockSpec((tm, wc), lambda i: (i, 0)) for _ in range(GATHER_CHUNKS)],
                   pl.BlockSpec((tm, LANES), lambda i: (i, 0))),
        compiler_params=_cparams(("arbitrary",)),
        name="mix_out_router",
    )(merged, w_out, x2d, gate.reshape(bsz, 1, d), g.reshape(1, d),
      shift.reshape(bsz, 1, d), scale.reshape(bsz, 1, d), rw, rb)
    return outs[0], list(outs[1:-1]), outs[-1]


def _expert_kernel(be_ref, nx_ref, nu_ref, *refs):
    nch = GATHER_CHUNKS
    x_refs = refs[:nch]
    w_hbm = refs[nch:nch + 3]
    ws_ref = refs[nch + 3]
    o_refs = refs[nch + 4:2 * nch + 4]
    stage = refs[2 * nch + 4:2 * nch + 7]
    wb = refs[2 * nch + 7:2 * nch + 10]
    sem = refs[2 * nch + 10]
    wc = 2 * o_refs[0].shape[1]
    i = pl.program_id(0)
    e = be_ref[i]
    nxt = nx_ref[i]
    active = i < nu_ref[0]
    first = i == 0
    run_start = jnp.logical_or(first, e != be_ref[jnp.maximum(i - 1, 0)])

    def weight_copies(expert):
        return [pltpu.make_async_copy(w_hbm[k].at[expert], stage[k], sem.at[k]) for k in range(3)]

    @pl.when(jnp.logical_and(active, first))
    def _():
        for cp in weight_copies(e):
            cp.start()

    @pl.when(jnp.logical_and(active, run_start))
    def _():
        for cp in weight_copies(e):
            cp.wait()
        for k in range(3):
            _cast_weight_tile(stage[k], wb[k])

        @pl.when(nxt >= 0)
        def _():
            for cp in weight_copies(nxt):
                cp.start()

    @pl.when(active)
    def _():
        x = jnp.concatenate([_unpack_bf16_pairs(r[...]) for r in x_refs], axis=1).astype(BF16)
        hg = jnp.dot(x, wb[0][...], preferred_element_type=F32)
        hu = jnp.dot(x, wb[1][...], preferred_element_type=F32)
        hdn = (_silu(hg) * hu).astype(BF16)
        y = jnp.dot(hdn, wb[2][...], preferred_element_type=F32)
        y = y * ws_ref[...]
        for c, o_ref in enumerate(o_refs):
            o_ref[...] = _pack_bf16_pairs(y[:, c * wc:(c + 1) * wc])

    @pl.when(jnp.logical_not(active))
    def _():
        for o_ref in o_refs:
            o_ref[...] = jnp.zeros_like(o_ref)


def _expert_ffn(xs, w_gate, w_up, w_down, w_slot, block_expert, next_expert, n_used):
    n_slots, wc = xs[0].shape
    d = 2 * wc * GATHER_CHUNKS
    hid = w_gate.shape[2]
    bm = MOE_ROWS
    chunk = pl.BlockSpec((bm, wc), lambda i, be, nx, nu: (i, 0))
    hbm = pl.BlockSpec(memory_space=pl.ANY)
    grid_spec = pltpu.PrefetchScalarGridSpec(
        num_scalar_prefetch=3,
        grid=(n_slots // bm,),
        in_specs=[*[chunk] * GATHER_CHUNKS, hbm, hbm, hbm,
                  pl.BlockSpec((bm, 1), lambda i, be, nx, nu: (i, 0))],
        out_specs=[chunk] * GATHER_CHUNKS,
        scratch_shapes=[pltpu.VMEM((d, hid), F32), pltpu.VMEM((d, hid), F32), pltpu.VMEM((hid, d), F32),
                        pltpu.VMEM((d, hid), BF16), pltpu.VMEM((d, hid), BF16), pltpu.VMEM((hid, d), BF16),
                        pltpu.SemaphoreType.DMA((3,))],
    )
    return pl.pallas_call(
        _expert_kernel,
        out_shape=[jax.ShapeDtypeStruct((n_slots, wc), jnp.uint32) for _ in range(GATHER_CHUNKS)],
        grid_spec=grid_spec,
        compiler_params=_cparams(("arbitrary",)),
        name="expert_ffn",
    )(block_expert, next_expert, n_used, *xs, w_gate, w_up, w_down, w_slot)


def _gather_rows(table, idx):
    m = idx.shape[0]
    d = table.shape[1]
    window = LANES
    mesh = plsc.VectorSubcoreMesh(core_axis_name="core", subcore_axis_name="subcore")

    @pl.kernel(out_type=jax.ShapeDtypeStruct((m, d), table.dtype), mesh=mesh, scratch_types=[])
    def gather(x_hbm, i_hbm, o_hbm):
        def body(i_vmem, o_vmem):
            pltpu.sync_copy(x_hbm.at[i_vmem.at[0]], o_vmem)

        pltpu.emit_pipeline(
            body,
            grid=(m // window,),
            in_specs=[pl.BlockSpec((1, window), lambda i: (0, i))],
            out_specs=[pl.BlockSpec((window, d), lambda i: (i, 0))],
            core_axis_name=("core", "subcore"),
            dimension_semantics=(pltpu.PARALLEL,),
        )(i_hbm, o_hbm)

    return gather(table, idx.reshape(1, m))


def _final_kernel(h_ref, *rest):
    y_refs = rest[:TOP_K * GATHER_CHUNKS]
    gt_ref, g_ref, o_ref, slab_ref = rest[TOP_K * GATHER_CHUNKS:]
    sub = h_ref.shape[2]
    nslab = h_ref.shape[3] // LANES
    for j in range(STREAMS):
        ffn = sum(jnp.concatenate([_unpack_bf16_pairs(y_refs[k * GATHER_CHUNKS + c][0, 0, j])
                                   for c in range(GATHER_CHUNKS)], axis=1) for k in range(TOP_K))
        h = h_ref[0, j] + gt_ref[0] * ffn
        y = h * lax.rsqrt(jnp.mean(h * h, axis=-1, keepdims=True) + NORM_EPS) * g_ref[...]
        for c in range(nslab):
            slab_ref[c, pl.ds(j, sub, stride=STREAMS), :] = y[:, c * LANES:(c + 1) * LANES]
    for c in range(nslab):
        o_ref[:, c * LANES:(c + 1) * LANES] = slab_ref[c]


def _final(h, y2, gate, g, seq):
    t, d = h.shape
    tm = 512
    per_b = seq // tm
    bsz = gate.shape[0]
    sub = tm // STREAMS
    spec = pl.BlockSpec((1, STREAMS, sub, d), lambda i: (i // per_b, 0, i % per_b, 0))
    wc = d // GATHER_CHUNKS // 2
    y5 = [a.reshape(TOP_K, bsz, STREAMS, seq // STREAMS, wc) for a in y2]

    def yspec(k):
        return pl.BlockSpec((1, 1, STREAMS, sub, wc), lambda i: (k, i // per_b, 0, i % per_b, 0))

    return pl.pallas_call(
        _final_kernel,
        out_shape=jax.ShapeDtypeStruct((t, d), F32),
        grid=(t // tm,),
        in_specs=[spec, *[yspec(k) for k in range(TOP_K) for _ in range(GATHER_CHUNKS)],
                  pl.BlockSpec((1, 1, d), lambda i: (i // per_b, 0, 0)),
                  pl.BlockSpec((1, d), lambda i: (0, 0))],
        out_specs=pl.BlockSpec((tm, d), lambda i: (i, 0)),
        scratch_shapes=[pltpu.VMEM((d // LANES, tm, LANES), F32)],
        compiler_params=_cparams(("arbitrary",)),
        name="final_norm",
    )(h.reshape(bsz, STREAMS, seq // STREAMS, d), *(y5 * TOP_K), gate.reshape(bsz, 1, d), g.reshape(1, d))


def _dispatch_plan(expert_idx, weights):
    n_assign = expert_idx.size
    n_blocks = n_assign // MOE_ROWS + N_EXPERTS
    n_slots = n_blocks * MOE_ROWS
    flat_e = expert_idx.reshape(-1)
    flat_w = weights.reshape(-1)
    ids = jnp.arange(n_assign, dtype=jnp.int32)
    eids = jnp.arange(N_EXPERTS, dtype=jnp.int32)[None, :]
    _, order = lax.sort((flat_e, ids), num_keys=1, is_stable=True)
    _, rank_sorted = lax.sort((order, ids), num_keys=1)
    hot_a = (flat_e[:, None] == eids).astype(jnp.int32)
    counts = jnp.sum(hot_a, axis=0)
    padded = ((counts + MOE_ROWS - 1) // MOE_ROWS) * MOE_ROWS
    pad_end = jnp.cumsum(padded)
    pad_start = pad_end - padded
    start = jnp.cumsum(counts) - counts
    dest = (rank_sorted + jnp.sum(hot_a * (pad_start - start)[None, :], axis=1)).reshape(-1, TOP_K)
    blk0 = jnp.arange(n_blocks, dtype=jnp.int32) * MOE_ROWS
    block_expert = jnp.minimum(jnp.sum((pad_end[None, :] <= blk0[:, None]).astype(jnp.int32), axis=1),
                               N_EXPERTS - 1)
    hot_b = (block_expert[:, None] == eids).astype(jnp.int32)
    blk_shift = jnp.sum(hot_b * (start - pad_start)[None, :], axis=1)
    blk_count = jnp.sum(hot_b * (pad_start + counts)[None, :], axis=1)
    slot = jnp.arange(n_slots, dtype=jnp.int32).reshape(n_blocks, MOE_ROWS)
    valid = (slot < blk_count[:, None]).reshape(-1)
    src = jnp.clip(slot + blk_shift[:, None], 0, n_assign - 1).reshape(-1)
    assign = order[src]
    token_of_slot = jnp.where(valid, assign // TOP_K, slot.reshape(-1) % (n_assign // TOP_K))
    weight_of_slot = jnp.where(valid, flat_w[assign], 0.0)
    n_used = (pad_end[-1:] // MOE_ROWS).astype(jnp.int32)
    bi = jnp.arange(n_blocks, dtype=jnp.int32)
    later = ((bi[None, :] > bi[:, None]) & (block_expert[None, :] != block_expert[:, None])
             & (bi[None, :] < n_used[0]))
    next_expert = jnp.where(jnp.any(later, axis=1), block_expert[jnp.argmax(later, axis=1)], -1)
    return token_of_slot, weight_of_slot, dest, block_expert, next_expert.astype(jnp.int32), n_used


def kernel(x, c, positions, ada_w, ada_b, mix_norm_g, w_in, w_attn_branch, w_rec_branch, w_mix_out,
           rec_norm_g, rec_lb_logits, ffn_norm_g, router_group_w, router_group_b, router_expert_w,
           router_expert_b, expert_w_gate, expert_w_up, expert_w_down, final_norm_g):
    bsz, seq, d = x.shape
    t = bsz * seq
    depth = ada_w.shape[0]
    assert depth == 1, "final norm is fused after the single layer"
    lower_bounds = jnp.cumsum(jax.nn.softmax(rec_lb_logits.astype(F32), axis=0), axis=0)
    pos_streams = positions.reshape(bsz, seq // STREAMS, STREAMS).transpose(0, 2, 1)
    rope = _rope_tables(pos_streams)
    h = x.reshape(t, d)
    for layer in range(depth):
        mod = _modulation(c, ada_w[layer], ada_b[layer])
        sh_m, sc_m, gt_m, sh_f, sc_f, gt_f = jnp.split(mod, 6, axis=-1)
        h, u = _norm_modulate(h, mix_norm_g[layer], sh_m, sc_m, seq)
        w = w_in[layer]
        qk = _project(u, w, 0, 2 * d, BF16, rope=rope)
        rest = _project(u, w, 2 * d, 2 * d, BF16)
        f_r = _project(u, w, 4 * d, d, F32)
        tail = _project(u, w, 5 * d, 4 * d, BF16)
        y_attn = _attention((qk, 0), (qk, d), (rest, 0), d, bsz, seq)
        y_rec = _recurrence((rest, d), (f_r, 0), (tail, 0), (tail, d), lower_bounds[layer],
                            rec_norm_g[layer], bsz, seq)
        merged = _merge(y_attn, y_rec, w_attn_branch[layer].astype(BF16),
                        w_rec_branch[layer].astype(BF16), (tail, 2 * d), (tail, 3 * d))
        rw = jnp.concatenate([router_group_w[layer], router_expert_w[layer],
                              jnp.zeros((d, LANES - N_GROUPS - N_EXPERTS), F32)], axis=1)
        rb = jnp.concatenate([router_group_b[layer], router_expert_b[layer],
                              jnp.zeros((LANES - N_GROUPS - N_EXPERTS,), F32)]).reshape(1, LANES)
        h, u2, route = _mix_out(merged, w_mix_out[layer].astype(BF16), h, gt_m, ffn_norm_g[layer],
                                sh_f, sc_f, rw, rb, seq)
        expert_idx = route[:, :TOP_K].astype(jnp.int32)
        weights = route[:, TOP_K:2 * TOP_K]
        tok, w_slot, dest, block_expert, next_expert, n_used = _dispatch_plan(expert_idx, weights)
        xs = [_gather_rows(a, tok) for a in u2]
        ys = _expert_ffn(xs, expert_w_gate[layer], expert_w_up[layer], expert_w_down[layer],
                         w_slot.reshape(-1, 1), block_expert, next_expert, n_used)
        dest_kt = dest.T.reshape(-1)
        y2 = [_gather_rows(a, dest_kt) for a in ys]
        h = _final(h, y2, gt_f, final_norm_g, seq)
    return h.reshape(bsz, seq, d)
```

```python
import functools

import jax
import jax.numpy as jnp
from jax import lax
from jax.experimental import pallas as pl
from jax.experimental.pallas import tpu as pltpu
from jax.experimental.pallas import tpu_sc as plsc

F32 = jnp.float32
BF16 = jnp.bfloat16

D_MODEL = 2048
HEAD_DIM = 128
N_HEADS = D_MODEL // HEAD_DIM
ROPE_DIM = HEAD_DIM // 4
ROPE_HALF = ROPE_DIM // 2
ROPE_THETA = 500000.0
ATTN_SPAN = 128
ATTN_BLOCK = 128
REC_CHUNK = 64
N_GROUPS = 4
EXPERTS_PER_GROUP = 8
N_EXPERTS = N_GROUPS * EXPERTS_PER_GROUP
EXPERT_HIDDEN = D_MODEL // 2
TOP_K = 2
NORM_EPS = 1e-6
IN_WIDTH = 9 * D_MODEL

LANES = 128
VMEM_LIMIT = 56 * 1024 * 1024

MOE_ROWS = 256
HEADS_PER_STEP = 2
REC_HEADS_PER_STEP = 4
GATHER_CHUNKS = 4
STREAMS = 4
DILATIONS = ((128, 1), (512, 4), (2048, 16))


def _cparams(sem):
    return pltpu.CompilerParams(dimension_semantics=sem, vmem_limit_bytes=VMEM_LIMIT)


def _sigmoid(x):
    return 1.0 / (1.0 + jnp.exp(-x))


def _silu(x):
    return x * _sigmoid(x)


def _pack_bf16_pairs(x):
    w = x.shape[1] // 2
    lo = lax.bitcast_convert_type(x[:, :w].astype(BF16).astype(F32), jnp.uint32)
    hi = lax.bitcast_convert_type(x[:, w:].astype(BF16).astype(F32), jnp.uint32)
    return (lo >> 16) | (hi & jnp.uint32(0xFFFF0000))


def _unpack_bf16_pairs(words):
    lo = lax.bitcast_convert_type(words << 16, F32)
    hi = lax.bitcast_convert_type(words & jnp.uint32(0xFFFF0000), F32)
    return jnp.concatenate([lo, hi], axis=1)


def _mod_kernel(c_ref, w_ref, b_ref, o_ref):
    cond = _silu(c_ref[...])
    o_ref[...] = jnp.dot(cond, w_ref[...], precision=lax.Precision.HIGHEST,
                         preferred_element_type=F32) + b_ref[...]


def _modulation(c, w, b):
    bsz, d = c.shape
    n = w.shape[1]
    tn = 1536
    return pl.pallas_call(
        _mod_kernel,
        out_shape=jax.ShapeDtypeStruct((bsz, n), F32),
        grid=(n // tn,),
        in_specs=[pl.BlockSpec((bsz, d), lambda j: (0, 0)),
                  pl.BlockSpec((d, tn), lambda j: (0, j)),
                  pl.BlockSpec((1, tn), lambda j: (0, j))],
        out_specs=pl.BlockSpec((bsz, tn), lambda j: (0, j)),
        compiler_params=_cparams(("arbitrary",)),
        name="adaln_mod",
    )(c, w, b.reshape(1, n))


def _rope_kernel(pos_ref, freq_ref, c_ref, sa_ref, sb_ref):
    ang = pos_ref[...] * freq_ref[...]
    lane = lax.broadcasted_iota(jnp.int32, ang.shape, 1)
    cos, sin = jnp.cos(ang), jnp.sin(ang)
    c_ref[...] = jnp.where(lane < ROPE_DIM, cos, 1.0)
    sa_ref[...] = jnp.where(lane < ROPE_HALF, -sin, 0.0)
    sb_ref[...] = jnp.where((lane >= ROPE_HALF) & (lane < ROPE_DIM), sin, 0.0)


def _rope_tables(positions):
    t = positions.size
    tm = 2048
    inv_freq = ROPE_THETA ** (-jnp.arange(0, ROPE_DIM, 2, dtype=F32) / ROPE_DIM)
    freq = jnp.concatenate([inv_freq, inv_freq, jnp.zeros((LANES - ROPE_DIM,), F32)]).reshape(1, LANES)
    pos = positions.astype(F32).reshape(t, 1)
    out = jax.ShapeDtypeStruct((t, LANES), F32)
    return pl.pallas_call(
        _rope_kernel,
        out_shape=(out, out, out),
        grid=(t // tm,),
        in_specs=[pl.BlockSpec((tm, 1), lambda i: (i, 0)),
                  pl.BlockSpec((1, LANES), lambda i: (0, 0))],
        out_specs=tuple(pl.BlockSpec((tm, LANES), lambda i: (i, 0)) for _ in range(3)),
        compiler_params=_cparams(("arbitrary",)),
        name="rope_tables",
    )(pos, freq)


def _norm_mod_kernel(x_ref, g_ref, sh_ref, sc_ref, xp_ref, u_ref, slab_ref):
    rows = x_ref.shape[0]
    sub = rows // STREAMS
    nslab = x_ref.shape[1] // LANES
    for c in range(nslab):
        slab_ref[c] = x_ref[:, c * LANES:(c + 1) * LANES]
    for j in range(STREAMS):
        x = jnp.concatenate([slab_ref[c, pl.ds(j, sub, stride=STREAMS), :] for c in range(nslab)], axis=1)
        xp_ref[0, j] = x
        y = x * lax.rsqrt(jnp.mean(x * x, axis=-1, keepdims=True) + NORM_EPS) * g_ref[...]
        u_ref[0, j] = (y * (1.0 + sc_ref[0]) + sh_ref[0]).astype(u_ref.dtype)


def _norm_modulate(x2d, g, shift, scale, seq):
    t, d = x2d.shape
    tm = 512
    per_b = seq // tm
    bsz = shift.shape[0]
    sub = tm // STREAMS
    out_spec = pl.BlockSpec((1, STREAMS, sub, d), lambda i: (i // per_b, 0, i % per_b, 0))
    xp, u = pl.pallas_call(
        _norm_mod_kernel,
        out_shape=(jax.ShapeDtypeStruct((bsz, STREAMS, seq // STREAMS, d), F32),
                   jax.ShapeDtypeStruct((bsz, STREAMS, seq // STREAMS, d), BF16)),
        grid=(t // tm,),
        in_specs=[pl.BlockSpec((tm, d), lambda i: (i, 0)),
                  pl.BlockSpec((1, d), lambda i: (0, 0)),
                  pl.BlockSpec((1, 1, d), lambda i: (i // per_b, 0, 0)),
                  pl.BlockSpec((1, 1, d), lambda i: (i // per_b, 0, 0))],
        out_specs=(out_spec, out_spec),
        scratch_shapes=[pltpu.VMEM((d // LANES, tm, LANES), F32)],
        compiler_params=_cparams(("arbitrary",)),
        name="norm_modulate",
    )(x2d, g.reshape(1, d), shift.reshape(bsz, 1, d), scale.reshape(bsz, 1, d))
    return xp.reshape(t, d), u.reshape(t, d)


def _cast_weight_tile(w_ref, wb_ref):
    rows = 256

    def body(r, carry):
        sl = pl.ds(pl.multiple_of(r * rows, rows), rows)
        wb_ref[sl, :] = w_ref[sl, :].astype(BF16)
        return carry

    lax.fori_loop(0, w_ref.shape[0] // rows, body, 0)


def _proj_kernel(a_ref, w_ref, o_ref, wb_ref):
    @pl.when(pl.program_id(1) == 0)
    def _():
        _cast_weight_tile(w_ref, wb_ref)

    o_ref[...] = jnp.dot(a_ref[...], wb_ref[...], preferred_element_type=F32).astype(o_ref.dtype)


def _proj_rope_kernel(a_ref, w_ref, c_ref, sa_ref, sb_ref, o_ref, wb_ref):
    @pl.when(pl.program_id(1) == 0)
    def _():
        _cast_weight_tile(w_ref, wb_ref)

    acc = jnp.dot(a_ref[...], wb_ref[...], preferred_element_type=F32)
    c, sa, sb = c_ref[...], sa_ref[...], sb_ref[...]
    for h in range(acc.shape[1] // HEAD_DIM):
        x = acc[:, h * HEAD_DIM:(h + 1) * HEAD_DIM]
        up = pltpu.roll(x, HEAD_DIM - ROPE_HALF, 1)
        dn = pltpu.roll(x, ROPE_HALF, 1)
        o_ref[:, h * HEAD_DIM:(h + 1) * HEAD_DIM] = (x * c + up * sa + dn * sb).astype(o_ref.dtype)


def _project(u, w, col0, ncols, out_dtype, rope=None):
    t, d = u.shape
    tm, tn = 1024, 1024
    j0 = col0 // tn
    in_specs = [pl.BlockSpec((tm, d), lambda j, i: (i, 0)),
                pl.BlockSpec((d, tn), lambda j, i: (0, j0 + j))]
    args = [u, w]
    kern = _proj_kernel
    if rope is not None:
        in_specs += [pl.BlockSpec((tm, LANES), lambda j, i: (i, 0)) for _ in range(3)]
        args += list(rope)
        kern = _proj_rope_kernel
    return pl.pallas_call(
        kern,
        out_shape=jax.ShapeDtypeStruct((t, ncols), out_dtype),
        grid=(ncols // tn, t // tm),
        in_specs=in_specs,
        out_specs=pl.BlockSpec((tm, tn), lambda j, i: (i, j)),
        scratch_shapes=[pltpu.VMEM((d, tn), BF16)],
        compiler_params=_cparams(("arbitrary", "arbitrary")),
        name="in_proj_rope" if rope is not None else "in_proj",
    )(*args)


def _key_pieces(qb, sub):
    L = ATTN_BLOCK
    pieces = [(0, 0, L * (qb + 1))]
    for r in range(1, STREAMS):
        back = 0 if qb == 0 else (L // 2 if r == 1 else L // 4)
        pieces.append((r, L * qb - back, L + back))
    return pieces


def _attention_bias(sub):
    L = ATTN_BLOCK
    nqb = sub // L
    kmax = max(sum(p[2] for p in _key_pieces(qb, sub)) for qb in range(nqb))
    qi = jnp.arange(L, dtype=jnp.int32)[:, None]
    out = []
    for j in range(STREAMS):
        row = []
        for qb in range(nqb):
            cols = []
            for r, start, size in _key_pieces(qb, sub):
                jp = (j + r) % STREAMS
                kn = start + jnp.arange(size, dtype=jnp.int32)[None, :]
                dt = STREAMS * (L * qb + qi - kn) + (j - jp)
                cnt = jnp.zeros(dt.shape, F32)
                for window, dil in DILATIONS:
                    cnt += ((dt >= 0) & (dt <= window) & (dt % dil == 0)).astype(F32)
                cols.append(jnp.log2(cnt))
            tile = jnp.concatenate(cols, axis=1)
            row.append(jnp.pad(tile, ((0, 0), (0, kmax - tile.shape[1])), constant_values=-jnp.inf))
        out.append(jnp.stack(row))
    return jnp.stack(out)


def _attn_kernel(q_ref, k_ref, v_ref, bias_ref, o_ref, s0, s1, m0, m1, p0, p1):
    L = ATTN_BLOCK
    sub = o_ref.shape[1] // STREAMS
    scale = HEAD_DIM ** -0.5 * 1.4426950408889634
    s_buf, m_buf, p_buf = (s0, s1), (m0, m1), (p0, p1)
    units = [(j, g, qb) for j in range(STREAMS) for g in range(HEADS_PER_STEP) for qb in range(sub // L)]

    def key_rows(j, qb):
        rows = [pl.ds(((j + r) % STREAMS) * sub + start, size) for r, start, size in _key_pieces(qb, sub)]
        return rows, sum(p[2] for p in _key_pieces(qb, sub))

    def score_stage(u):
        j, g, qb = units[u]
        lanes = slice(g * HEAD_DIM, (g + 1) * HEAD_DIM)
        rows, ktot = key_rows(j, qb)
        q = q_ref[0, pl.ds(j * sub + L * qb, L), lanes]
        k_all = jnp.concatenate([k_ref[0, rw, lanes] for rw in rows], axis=0)
        mx = None
        for c in range(0, ktot, 2 * LANES):
            w = min(2 * LANES, ktot - c)
            s = lax.dot_general(q, k_all[c:c + w], (((1,), (1,)), ((), ())), preferred_element_type=F32)
            s = s * scale + bias_ref[j, qb, :, c:c + w]
            s_buf[u % 2][:, c:c + w] = s
            for cc in range(0, w, LANES):
                part = s[:, cc:cc + LANES]
                mx = part if mx is None else jnp.maximum(mx, part)
        m_buf[u % 2][...] = jnp.broadcast_to(jnp.max(mx, axis=-1, keepdims=True), mx.shape)

    def prob_stage(u):
        j, _, qb = units[u]
        _, ktot = key_rows(j, qb)
        m = m_buf[u % 2][...]
        for c in range(0, ktot, LANES):
            p_buf[u % 2][:, c:c + LANES] = jnp.exp2(s_buf[u % 2][:, c:c + LANES] - m).astype(BF16)

    def out_stage(u):
        j, g, qb = units[u]
        lanes = slice(g * HEAD_DIM, (g + 1) * HEAD_DIM)
        rows, ktot = key_rows(j, qb)
        v_all = jnp.concatenate([v_ref[0, rw, lanes] for rw in rows], axis=0)
        v_ext = jnp.concatenate([v_all, jnp.ones_like(v_all)], axis=1)
        oe = jnp.dot(p_buf[u % 2][:, :ktot], v_ext, preferred_element_type=F32)
        y = oe[:, :HEAD_DIM] / oe[:, HEAD_DIM:]
        o_ref[0, pl.ds(j * sub + L * qb, L), lanes] = y.astype(o_ref.dtype)

    for t in range(len(units) + 2):
        if 0 <= t - 2:
            out_stage(t - 2)
        if 0 <= t - 1 < len(units):
            prob_stage(t - 1)
        if t < len(units):
            score_stage(t)


def _cols(src, bsz, seq, gw):
    a, col0 = src
    off = col0 // gw
    return a.reshape(bsz, seq, a.shape[1]), pl.BlockSpec((1, seq, gw), lambda b, h: (b, 0, off + h))


def _attention(q, k, v, width, bsz, seq):
    t = bsz * seq
    gw = HEADS_PER_STEP * HEAD_DIM
    sub = seq // STREAMS
    bias = _attention_bias(sub)
    (qa, qs), (ka, ks), (va, vs) = (_cols(s, bsz, seq, gw) for s in (q, k, v))
    out = pl.pallas_call(
        _attn_kernel,
        out_shape=jax.ShapeDtypeStruct((bsz, seq, width), BF16),
        grid=(bsz, width // gw),
        in_specs=[qs, ks, vs,
                  pl.BlockSpec(bias.shape, lambda b, h: (0, 0, 0, 0))],
        out_specs=pl.BlockSpec((1, seq, gw), lambda b, h: (b, 0, h)),
        scratch_shapes=[pltpu.VMEM((ATTN_BLOCK, bias.shape[-1]), F32)] * 2
        + [pltpu.VMEM((ATTN_BLOCK, LANES), F32)] * 2
        + [pltpu.VMEM((ATTN_BLOCK, bias.shape[-1]), BF16)] * 2,
        compiler_params=_cparams(("arbitrary", "arbitrary")),
        name="dilated_attention",
    )(qa, ka, va, bias)
    return out.reshape(t, width)


def _split3(x):
    hi = x.astype(BF16)
    r1 = x - hi.astype(F32)
    mid = r1.astype(BF16)
    lo = (r1 - mid.astype(F32)).astype(BF16)
    return hi, mid, lo


def _rec_scratch():
    C, K = REC_CHUNK, HEAD_DIM
    return [pltpu.VMEM((K, K), F32), pltpu.VMEM((C, K), F32), pltpu.VMEM((C, K), F32),
            pltpu.VMEM((C, C), BF16), pltpu.VMEM((C, K), BF16), pltpu.VMEM((K, K), F32),
            pltpu.VMEM((1, K), F32)]


def _rec_kernel(q_ref, f_ref, i_ref, g_ref, lb_ref, ng_ref, o_ref, *scratch):
    C = REC_CHUNK
    seq = q_ref.shape[1]
    sub = seq // STREAMS
    piece = C // STREAMS
    for g in range(REC_HEADS_PER_STEP):
        st_ref = scratch[g * (len(scratch) // REC_HEADS_PER_STEP)]
        st_ref[...] = jnp.zeros_like(st_ref)
    pi = lax.broadcasted_iota(jnp.int32, (C, C), 0)
    si = lax.broadcasted_iota(jnp.int32, (C, C), 1)
    time_of = lambda p: STREAMS * (p % piece) + p // piece
    causal = time_of(si) <= time_of(pi)
    tri = jnp.where(causal, 1.0, 0.0).astype(BF16)
    row_of = lambda tau: (tau % STREAMS) * piece + tau // STREAMS
    last = row_of(C - 1)
    anchor = row_of(C // 2 - 1)

    def rows(c, j):
        start = j * sub + c * piece
        return pl.ds(start if isinstance(start, int) else pl.multiple_of(start, piece), piece)

    def load(ref, c, lanes):
        return jnp.concatenate([ref[0, rows(c, j), lanes] for j in range(STREAMS)], axis=0)

    def gates(c, g):
        lanes, (_, b_s, kk_s, _, _, _, _) = head(g)
        lb = lb_ref[:, lanes]
        x = load(f_ref, c, lanes)
        z = jnp.exp(-jnp.abs(x))
        r = 1.0 / (1.0 + z)
        pos = x >= 0
        sig_p = jnp.where(pos, r, z * r)
        sig_n = jnp.where(pos, z * r, r)
        logf = jnp.log(lb + (1.0 - lb) * sig_p)
        hi, mid, lo = _split3(logf)
        b_s[...] = (jnp.dot(tri, hi, preferred_element_type=F32)
                    + jnp.dot(tri, mid, preferred_element_type=F32)
                    + jnp.dot(tri, lo, preferred_element_type=F32))
        kk_s[...] = (1.0 - lb) * sig_n

    def scores(c, g):
        lanes, (_, b_s, kk_s, sc_s, qin_s, upd_s, dec_s) = head(g)
        b, kk = b_s[...], kk_s[...]
        b_last = b[last:last + 1, :]
        b_mid = b[anchor:anchor + 1, :]
        qs = _silu(load(q_ref, c, lanes).astype(F32))
        v = load(i_ref, c, lanes)
        q_a = (qs * jnp.exp(b - b_mid)).astype(BF16)
        k_a = (kk * jnp.exp(b_mid - b)).astype(BF16)
        k_e = (kk * jnp.exp(b_last - b)).astype(BF16)
        sc = lax.dot_general(q_a, k_a, (((1,), (1,)), ((), ())), preferred_element_type=F32)
        sc_s[...] = jnp.where(causal, sc, 0.0).astype(BF16)
        qin_s[...] = (qs * jnp.exp(b)).astype(BF16)
        upd_s[...] = lax.dot_general(v, k_e, (((0,), (0,)), ((), ())), preferred_element_type=F32)
        dec_s[...] = jnp.exp(b_last)

    def output(c, g):
        lanes, (st_ref, _, _, sc_s, qin_s, upd_s, dec_s) = head(g)
        v = load(i_ref, c, lanes)
        st = st_ref[...]
        o = (jnp.dot(sc_s[...], v, preferred_element_type=F32)
             + lax.dot_general(qin_s[...], st.astype(BF16), (((1,), (1,)), ((), ())),
                               preferred_element_type=F32))
        st_ref[...] = dec_s[...] * st + upd_s[...]
        y = o * lax.rsqrt(jnp.mean(o * o, axis=-1, keepdims=True) + NORM_EPS) * ng_ref[...]
        y = (y * _silu(load(g_ref, c, lanes).astype(F32))).astype(o_ref.dtype)
        for j in range(STREAMS):
            o_ref[0, rows(c, j), lanes] = y[j * piece:(j + 1) * piece]

    per_head = len(scratch) // REC_HEADS_PER_STEP

    def head(g):
        return slice(g * HEAD_DIM, (g + 1) * HEAD_DIM), scratch[g * per_head:(g + 1) * per_head]

    def stages(c, run):
        for g in range(REC_HEADS_PER_STEP):
            if run[2]:
                output(c - 2, g)
            if run[1]:
                scores(c - 1, g)
            if run[0]:
                gates(c, g)

    nc = seq // C
    stages(0, (True, False, False))
    stages(1, (True, True, False))

    def body(c, carry):
        stages(c, (True, True, True))
        return carry

    lax.fori_loop(2, nc, body, 0)
    stages(nc, (False, True, True))
    stages(nc + 1, (False, False, True))


def _recurrence(q_r, f_r, i_r, g_r, lower_bound, norm_g, bsz, seq):
    t = bsz * seq
    width = lower_bound.shape[0]
    gw = REC_HEADS_PER_STEP * HEAD_DIM
    spec = pl.BlockSpec((1, seq, gw), lambda b, h: (b, 0, h))
    (qa, qs), (fa, fs), (ia, isp), (ga, gs) = (_cols(s, bsz, seq, gw) for s in (q_r, f_r, i_r, g_r))
    out = pl.pallas_call(
        _rec_kernel,
        out_shape=jax.ShapeDtypeStruct((bsz, seq, width), BF16),
        grid=(bsz, width // gw),
        in_specs=[qs, fs, isp, gs,
                  pl.BlockSpec((1, gw), lambda b, h: (0, h)),
                  pl.BlockSpec((1, HEAD_DIM), lambda b, h: (0, 0))],
        out_specs=spec,
        scratch_shapes=[s for _ in range(REC_HEADS_PER_STEP) for s in _rec_scratch()],
        compiler_params=_cparams(("arbitrary", "arbitrary")),
        name="hgrn2_recurrence",
    )(qa, fa, ia, ga, lower_bound.reshape(1, width), norm_g.reshape(1, HEAD_DIM))
    return out.reshape(t, width)


def _merge_kernel(ya_ref, yr_ref, wa_ref, wr_ref, ga_ref, gr_ref, o_ref):
    a = jnp.dot(ya_ref[...], wa_ref[...], preferred_element_type=F32)
    r = jnp.dot(yr_ref[...], wr_ref[...], preferred_element_type=F32)
    m = _sigmoid(ga_ref[...].astype(F32)) * a + _sigmoid(gr_ref[...].astype(F32)) * r
    o_ref[...] = m.astype(o_ref.dtype)


def _merge(ya, yr, wa, wr, ga, gr):
    t, d = ya.shape
    n = wa.shape[1]
    tm, tn = 1024, 512
    row = pl.BlockSpec((tm, d), lambda i, j: (i, 0))
    col = pl.BlockSpec((d, tn), lambda i, j: (0, j))
    tile = pl.BlockSpec((tm, tn), lambda i, j: (i, j))
    ga_off, gr_off = ga[1] // tn, gr[1] // tn
    return pl.pallas_call(
        _merge_kernel,
        out_shape=jax.ShapeDtypeStruct((t, n), BF16),
        grid=(t // tm, n // tn),
        in_specs=[row, row, col, col,
                  pl.BlockSpec((tm, tn), lambda i, j: (i, ga_off + j)),
                  pl.BlockSpec((tm, tn), lambda i, j: (i, gr_off + j))],
        out_specs=tile,
        compiler_params=_cparams(("arbitrary", "arbitrary")),
        name="branch_merge",
    )(ya, yr, wa, wr, ga[0], gr[0])


def _mixout_kernel(m_ref, w_ref, x_ref, gt_ref, g_ref, sh_ref, sc_ref, rw_ref, rb_ref,
                   h_ref, u_ref, route_ref):
    mix = jnp.dot(m_ref[...], w_ref[...], preferred_element_type=F32)
    h = x_ref[...] + gt_ref[0] * mix
    h_ref[...] = h
    u = h * lax.rsqrt(jnp.mean(h * h, axis=-1, keepdims=True) + NORM_EPS) * g_ref[...]
    u = u * (1.0 + sc_ref[0]) + sh_ref[0]
    wc = 2 * u_ref.shape[2]
    for c in range(u_ref.shape[0]):
        u_ref[c] = _pack_bf16_pairs(u[:, c * wc:(c + 1) * wc])
    u_hi = u.astype(BF16)
    u_lo = (u - u_hi.astype(F32)).astype(BF16)
    rw = rw_ref[...]
    w_hi = rw.astype(BF16)
    w_lo = (rw - w_hi.astype(F32)).astype(BF16)
    logits = (jnp.dot(u_hi, w_hi, preferred_element_type=F32)
              + jnp.dot(u_lo, w_hi, preferred_element_type=F32)
              + jnp.dot(u_hi, w_lo, preferred_element_type=F32)) + rb_ref[...]
    lane = lax.broadcasted_iota(jnp.int32, logits.shape, 1).astype(F32)
    big = float(LANES)
    neg = -jnp.inf
    lg = jnp.where(lane < N_GROUPS, logits, neg)
    mg = jnp.max(lg, axis=-1, keepdims=True)
    g_sel = jnp.min(jnp.where(lg == mg, lane, big), axis=-1, keepdims=True)
    p_group = 1.0 / jnp.sum(jnp.exp(lg - mg), axis=-1, keepdims=True)
    lo = N_GROUPS + EXPERTS_PER_GROUP * g_sel
    le = jnp.where((lane >= lo) & (lane < lo + EXPERTS_PER_GROUP), logits, neg)
    t1 = jnp.max(le, axis=-1, keepdims=True)
    i1 = jnp.min(jnp.where(le == t1, lane, big), axis=-1, keepdims=True)
    le2 = jnp.where(lane == i1, neg, le)
    t2 = jnp.max(le2, axis=-1, keepdims=True)
    i2 = jnp.min(jnp.where(le2 == t2, lane, big), axis=-1, keepdims=True)
    e21 = jnp.exp(t2 - t1)
    w1 = p_group / (1.0 + e21)
    w2 = p_group * e21 / (1.0 + e21)
    route = jnp.where(lane == 0, i1 - N_GROUPS,
                      jnp.where(lane == 1, i2 - N_GROUPS,
                                jnp.where(lane == 2, w1, jnp.where(lane == 3, w2, 0.0))))
    route_ref[...] = route


def _mix_out(merged, w_out, x2d, gate, g, shift, scale, rw, rb, seq):
    t, d = x2d.shape
    tm = 512
    per_b = seq // tm
    bsz = gate.shape[0]
    row = lambda dt: pl.BlockSpec((tm, d), lambda i: (i, 0))
    per_batch = pl.BlockSpec((1, 1, d), lambda i: (i // per_b, 0, 0))
    const = lambda shape: pl.BlockSpec(shape, lambda i: (0,) * len(shape))
    wc = d // GATHER_CHUNKS // 2
    outs = pl.pallas_call(
        _mixout_kernel,
        out_shape=(jax.ShapeDtypeStruct((t, d), F32),
                   jax.ShapeDtypeStruct((GATHER_CHUNKS, t, wc), jnp.uint32),
                   jax.ShapeDtypeStruct((t, LANES), F32)),
        grid=(t // tm,),
        in_specs=[row(BF16), const((d, d)), row(F32), per_batch, const((1, d)), per_batch, per_batch,
                  const((d, LANES)), const((1, LANES))],
        out_specs=(row(F32), pl.BlockSpec((GATHER_CHUNKS, tm, wc), lambda i: (0, i, 0)),
                   pl.BlockSpec((tm, LANES), lambda i: (i, 0))),
        compiler_params=_cparams(("arbitrary",)),
        name="mix_out_router",
    )(merged, w_out, x2d, gate.reshape(bsz, 1, d), g.reshape(1, d),
      shift.reshape(bsz, 1, d), scale.reshape(bsz, 1, d), rw, rb)
    return outs


def _expert_kernel(be_ref, nx_ref, nu_ref, *refs):
    x_ref = refs[0]
    w_hbm = refs[1:4]
    ws_ref, o_ref = refs[4], refs[5]
    stage = refs[6:9]
    wb = refs[9:12]
    sem = refs[12]
    nch = x_ref.shape[0]
    wc = 2 * o_ref.shape[2]
    i = pl.program_id(0)
    e = be_ref[i]
    nxt = nx_ref[i]
    active = i < nu_ref[0]
    first = i == 0
    run_start = jnp.logical_or(first, e != be_ref[jnp.maximum(i - 1, 0)])

    def weight_copies(expert):
        return [pltpu.make_async_copy(w_hbm[k].at[expert], stage[k], sem.at[k]) for k in range(3)]

    @pl.when(jnp.logical_and(active, first))
    def _():
        for cp in weight_copies(e):
            cp.start()

    @pl.when(jnp.logical_and(active, run_start))
    def _():
        for cp in weight_copies(e):
            cp.wait()
        for k in range(3):
            _cast_weight_tile(stage[k], wb[k])

        @pl.when(nxt >= 0)
        def _():
            for cp in weight_copies(nxt):
                cp.start()

    @pl.when(active)
    def _():
        x = jnp.concatenate([_unpack_bf16_pairs(x_ref[c]) for c in range(nch)], axis=1).astype(BF16)
        hg = jnp.dot(x, wb[0][...], preferred_element_type=F32)
        hu = jnp.dot(x, wb[1][...], preferred_element_type=F32)
        hdn = (_silu(hg) * hu).astype(BF16)
        y = jnp.dot(hdn, wb[2][...], preferred_element_type=F32)
        y = y * ws_ref[...]
        for c in range(nch):
            o_ref[c] = _pack_bf16_pairs(y[:, c * wc:(c + 1) * wc])

    @pl.when(jnp.logical_not(active))
    def _():
        o_ref[...] = jnp.zeros_like(o_ref)


def _expert_ffn(xs, w_gate, w_up, w_down, w_slot, block_expert, next_expert, n_used):
    nch, n_slots, wc = xs.shape
    d = 2 * wc * nch
    hid = w_gate.shape[2]
    bm = MOE_ROWS
    chunk = pl.BlockSpec((nch, bm, wc), lambda i, be, nx, nu: (0, i, 0))
    hbm = pl.BlockSpec(memory_space=pl.ANY)
    grid_spec = pltpu.PrefetchScalarGridSpec(
        num_scalar_prefetch=3,
        grid=(n_slots // bm,),
        in_specs=[chunk, hbm, hbm, hbm,
                  pl.BlockSpec((bm, 1), lambda i, be, nx, nu: (i, 0))],
        out_specs=chunk,
        scratch_shapes=[pltpu.VMEM((d, hid), F32), pltpu.VMEM((d, hid), F32), pltpu.VMEM((hid, d), F32),
                        pltpu.VMEM((d, hid), BF16), pltpu.VMEM((d, hid), BF16), pltpu.VMEM((hid, d), BF16),
                        pltpu.SemaphoreType.DMA((3,))],
    )
    return pl.pallas_call(
        _expert_kernel,
        out_shape=jax.ShapeDtypeStruct((nch, n_slots, wc), jnp.uint32),
        grid_spec=grid_spec,
        compiler_params=_cparams(("arbitrary",)),
        name="expert_ffn",
    )(block_expert, next_expert, n_used, xs, w_gate, w_up, w_down, w_slot)


def _gather_rows(chunks, idx):
    nch, n, d = chunks.shape
    table = chunks.reshape(nch * n, d)
    idx = (idx[None, :] + (jnp.arange(nch, dtype=jnp.int32) * n)[:, None]).reshape(-1)
    m = idx.shape[0]
    window = LANES
    mesh = plsc.VectorSubcoreMesh(core_axis_name="core", subcore_axis_name="subcore")

    @pl.kernel(out_type=jax.ShapeDtypeStruct((m, d), table.dtype), mesh=mesh, scratch_types=[])
    def gather(x_hbm, i_hbm, o_hbm):
        def body(i_vmem, o_vmem):
            pltpu.sync_copy(x_hbm.at[i_vmem.at[0]], o_vmem)

        pltpu.emit_pipeline(
            body,
            grid=(m // window,),
            in_specs=[pl.BlockSpec((1, window), lambda i: (0, i))],
            out_specs=[pl.BlockSpec((window, d), lambda i: (i, 0))],
            core_axis_name=("core", "subcore"),
            dimension_semantics=(pltpu.PARALLEL,),
        )(i_hbm, o_hbm)

    return gather(table, idx.reshape(1, m)).reshape(nch, m // nch, d)


def _final_kernel(h_ref, *rest):
    y_refs = rest[:TOP_K]
    gt_ref, g_ref, o_ref, slab_ref = rest[TOP_K:]
    sub = h_ref.shape[2]
    nslab = h_ref.shape[3] // LANES
    for j in range(STREAMS):
        ffn = sum(jnp.concatenate([_unpack_bf16_pairs(y_refs[k][c, 0, 0, j])
                                   for c in range(GATHER_CHUNKS)], axis=1) for k in range(TOP_K))
        h = h_ref[0, j] + gt_ref[0] * ffn
        y = h * lax.rsqrt(jnp.mean(h * h, axis=-1, keepdims=True) + NORM_EPS) * g_ref[...]
        for c in range(nslab):
            slab_ref[c, pl.ds(j, sub, stride=STREAMS), :] = y[:, c * LANES:(c + 1) * LANES]
    for c in range(nslab):
        o_ref[:, c * LANES:(c + 1) * LANES] = slab_ref[c]


def _final(h, y2, gate, g, seq):
    t, d = h.shape
    tm = 512
    per_b = seq // tm
    bsz = gate.shape[0]
    sub = tm // STREAMS
    spec = pl.BlockSpec((1, STREAMS, sub, d), lambda i: (i // per_b, 0, i % per_b, 0))
    wc = d // GATHER_CHUNKS // 2
    y6 = y2.reshape(GATHER_CHUNKS, TOP_K, bsz, STREAMS, seq // STREAMS, wc)

    def yspec(k):
        return pl.BlockSpec((GATHER_CHUNKS, 1, 1, STREAMS, sub, wc),
                            lambda i: (0, k, i // per_b, 0, i % per_b, 0))

    return pl.pallas_call(
        _final_kernel,
        out_shape=jax.ShapeDtypeStruct((t, d), F32),
        grid=(t // tm,),
        in_specs=[spec, *[yspec(k) for k in range(TOP_K)],
                  pl.BlockSpec((1, 1, d), lambda i: (i // per_b, 0, 0)),
                  pl.BlockSpec((1, d), lambda i: (0, 0))],
        out_specs=pl.BlockSpec((tm, d), lambda i: (i, 0)),
        scratch_shapes=[pltpu.VMEM((d // LANES, tm, LANES), F32)],
        compiler_params=_cparams(("arbitrary",)),
        name="final_norm",
    )(h.reshape(bsz, STREAMS, seq // STREAMS, d), *([y6] * TOP_K), gate.reshape(bsz, 1, d), g.reshape(1, d))


def _dispatch_plan(expert_idx, weights):
    n_assign = expert_idx.size
    n_blocks = n_assign // MOE_ROWS + N_EXPERTS
    n_slots = n_blocks * MOE_ROWS
    flat_e = expert_idx.reshape(-1)
    flat_w = weights.reshape(-1)
    ids = jnp.arange(n_assign, dtype=jnp.int32)
    eids = jnp.arange(N_EXPERTS, dtype=jnp.int32)[None, :]
    _, order = lax.sort((flat_e, ids), num_keys=1, is_stable=True)
    _, rank_sorted = lax.sort((order, ids), num_keys=1)
    hot_a = (flat_e[:, None] == eids).astype(jnp.int32)
    counts = jnp.sum(hot_a, axis=0)
    padded = ((counts + MOE_ROWS - 1) // MOE_ROWS) * MOE_ROWS
    pad_end = jnp.cumsum(padded)
    pad_start = pad_end - padded
    start = jnp.cumsum(counts) - counts
    dest = (rank_sorted + jnp.sum(hot_a * (pad_start - start)[None, :], axis=1)).reshape(-1, TOP_K)
    blk0 = jnp.arange(n_blocks, dtype=jnp.int32) * MOE_ROWS
    block_expert = jnp.minimum(jnp.sum((pad_end[None, :] <= blk0[:, None]).astype(jnp.int32), axis=1),
                               N_EXPERTS - 1)
    hot_b = (block_expert[:, None] == eids).astype(jnp.int32)
    blk_shift = jnp.sum(hot_b * (start - pad_start)[None, :], axis=1)
    blk_count = jnp.sum(hot_b * (pad_start + counts)[None, :], axis=1)
    slot = jnp.arange(n_slots, dtype=jnp.int32).reshape(n_blocks, MOE_ROWS)
    valid = (slot < blk_count[:, None]).reshape(-1)
    src = jnp.clip(slot + blk_shift[:, None], 0, n_assign - 1).reshape(-1)
    assign = order[src]
    token_of_slot = jnp.where(valid, assign // TOP_K, slot.reshape(-1) % (n_assign // TOP_K))
    weight_of_slot = jnp.where(valid, flat_w[assign], 0.0)
    n_used = (pad_end[-1:] // MOE_ROWS).astype(jnp.int32)
    bi = jnp.arange(n_blocks, dtype=jnp.int32)
    later = ((bi[None, :] > bi[:, None]) & (block_expert[None, :] != block_expert[:, None])
             & (bi[None, :] < n_used[0]))
    next_expert = jnp.where(jnp.any(later, axis=1), block_expert[jnp.argmax(later, axis=1)], -1)
    return token_of_slot, weight_of_slot, dest, block_expert, next_expert.astype(jnp.int32), n_used


def kernel(x, c, positions, ada_w, ada_b, mix_norm_g, w_in, w_attn_branch, w_rec_branch, w_mix_out,
           rec_norm_g, rec_lb_logits, ffn_norm_g, router_group_w, router_group_b, router_expert_w,
           router_expert_b, expert_w_gate, expert_w_up, expert_w_down, final_norm_g):
    bsz, seq, d = x.shape
    t = bsz * seq
    depth = ada_w.shape[0]
    assert depth == 1, "final norm is fused after the single layer"
    lower_bounds = jnp.cumsum(jax.nn.softmax(rec_lb_logits.astype(F32), axis=0), axis=0)
    pos_streams = positions.reshape(bsz, seq // STREAMS, STREAMS).transpose(0, 2, 1)
    rope = _rope_tables(pos_streams)
    h = x.reshape(t, d)
    for layer in range(depth):
        mod = _modulation(c, ada_w[layer], ada_b[layer])
        sh_m, sc_m, gt_m, sh_f, sc_f, gt_f = jnp.split(mod, 6, axis=-1)
        h, u = _norm_modulate(h, mix_norm_g[layer], sh_m, sc_m, seq)
        w = w_in[layer]
        qk = _project(u, w, 0, 2 * d, BF16, rope=rope)
        rest = _project(u, w, 2 * d, 2 * d, BF16)
        f_r = _project(u, w, 4 * d, d, F32)
        tail = _project(u, w, 5 * d, 4 * d, BF16)
        y_attn = _attention((qk, 0), (qk, d), (rest, 0), d, bsz, seq)
        y_rec = _recurrence((rest, d), (f_r, 0), (tail, 0), (tail, d), lower_bounds[layer],
                            rec_norm_g[layer], bsz, seq)
        merged = _merge(y_attn, y_rec, w_attn_branch[layer].astype(BF16),
                        w_rec_branch[layer].astype(BF16), (tail, 2 * d), (tail, 3 * d))
        rw = jnp.concatenate([router_group_w[layer], router_expert_w[layer],
                              jnp.zeros((d, LANES - N_GROUPS - N_EXPERTS), F32)], axis=1)
        rb = jnp.concatenate([router_group_b[layer], router_expert_b[layer],
                              jnp.zeros((LANES - N_GROUPS - N_EXPERTS,), F32)]).reshape(1, LANES)
        h, u2, route = _mix_out(merged, w_mix_out[layer].astype(BF16), h, gt_m, ffn_norm_g[layer],
                                sh_f, sc_f, rw, rb, seq)
        expert_idx = route[:, :TOP_K].astype(jnp.int32)
        weights = route[:, TOP_K:2 * TOP_K]
        tok, w_slot, dest, block_expert, next_expert, n_used = _dispatch_plan(expert_idx, weights)
        xs = _gather_rows(u2, tok)
        ys = _expert_ffn(xs, expert_w_gate[layer], expert_w_up[layer], expert_w_down[layer],
                         w_slot.reshape(-1, 1), block_expert, next_expert, n_used)
        dest_kt = dest.T.reshape(-1)
        y2 = _gather_rows(ys, dest_kt)
        h = _final(h, y2, gt_f, final_norm_g, seq)
    return h.reshape(bsz, seq, d)
```

```python
import functools

import jax
import jax.numpy as jnp
from jax import lax
from jax.experimental import pallas as pl
from jax.experimental.pallas import tpu as pltpu
from jax.experimental.pallas import tpu_sc as plsc

F32 = jnp.float32
BF16 = jnp.bfloat16

D_MODEL = 2048
HEAD_DIM = 128
N_HEADS = D_MODEL // HEAD_DIM
ROPE_DIM = HEAD_DIM // 4
ROPE_HALF = ROPE_DIM // 2
ROPE_THETA = 500000.0
ATTN_SPAN = 128
ATTN_BLOCK = 128
REC_CHUNK = 64
N_GROUPS = 4
EXPERTS_PER_GROUP = 8
N_EXPERTS = N_GROUPS * EXPERTS_PER_GROUP
EXPERT_HIDDEN = D_MODEL // 2
TOP_K = 2
NORM_EPS = 1e-6
IN_WIDTH = 9 * D_MODEL

LANES = 128
VMEM_LIMIT = 56 * 1024 * 1024

MOE_ROWS = 256
HEADS_PER_STEP = 2
REC_HEADS_PER_STEP = 4
GATHER_CHUNKS = 4
STREAMS = 4
DILATIONS = ((128, 1), (512, 4), (2048, 16))


def _cparams(sem):
    return pltpu.CompilerParams(dimension_semantics=sem, vmem_limit_bytes=VMEM_LIMIT)


def _sigmoid(x):
    return 1.0 / (1.0 + jnp.exp(-x))


def _silu(x):
    return x * _sigmoid(x)


def _pack_bf16_pairs(x):
    w = x.shape[1] // 2
    lo = lax.bitcast_convert_type(x[:, :w].astype(BF16).astype(F32), jnp.uint32)
    hi = lax.bitcast_convert_type(x[:, w:].astype(BF16).astype(F32), jnp.uint32)
    return (lo >> 16) | (hi & jnp.uint32(0xFFFF0000))


def _unpack_bf16_pairs(words):
    lo = lax.bitcast_convert_type(words << 16, F32)
    hi = lax.bitcast_convert_type(words & jnp.uint32(0xFFFF0000), F32)
    return jnp.concatenate([lo, hi], axis=1)


def _mod_kernel(c_ref, w_ref, b_ref, o_ref):
    cond = _silu(c_ref[...])
    o_ref[...] = jnp.dot(cond, w_ref[...], precision=lax.Precision.HIGHEST,
                         preferred_element_type=F32) + b_ref[...]


def _modulation(c, w, b):
    bsz, d = c.shape
    n = w.shape[1]
    tn = 1536
    return pl.pallas_call(
        _mod_kernel,
        out_shape=jax.ShapeDtypeStruct((bsz, n), F32),
        grid=(n // tn,),
        in_specs=[pl.BlockSpec((bsz, d), lambda j: (0, 0)),
                  pl.BlockSpec((d, tn), lambda j: (0, j)),
                  pl.BlockSpec((1, tn), lambda j: (0, j))],
        out_specs=pl.BlockSpec((bsz, tn), lambda j: (0, j)),
        compiler_params=_cparams(("arbitrary",)),
        name="adaln_mod",
    )(c, w, b.reshape(1, n))


def _rope_kernel(pos_ref, freq_ref, c_ref, sa_ref, sb_ref):
    ang = pos_ref[...] * freq_ref[...]
    lane = lax.broadcasted_iota(jnp.int32, ang.shape, 1)
    cos, sin = jnp.cos(ang), jnp.sin(ang)
    c_ref[...] = jnp.where(lane < ROPE_DIM, cos, 1.0)
    sa_ref[...] = jnp.where(lane < ROPE_HALF, -sin, 0.0)
    sb_ref[...] = jnp.where((lane >= ROPE_HALF) & (lane < ROPE_DIM), sin, 0.0)


def _rope_tables(positions):
    t = positions.size
    tm = 2048
    inv_freq = ROPE_THETA ** (-jnp.arange(0, ROPE_DIM, 2, dtype=F32) / ROPE_DIM)
    freq = jnp.concatenate([inv_freq, inv_freq, jnp.zeros((LANES - ROPE_DIM,), F32)]).reshape(1, LANES)
    pos = positions.astype(F32).reshape(t, 1)
    out = jax.ShapeDtypeStruct((t, LANES), F32)
    return pl.pallas_call(
        _rope_kernel,
        out_shape=(out, out, out),
        grid=(t // tm,),
        in_specs=[pl.BlockSpec((tm, 1), lambda i: (i, 0)),
                  pl.BlockSpec((1, LANES), lambda i: (0, 0))],
        out_specs=tuple(pl.BlockSpec((tm, LANES), lambda i: (i, 0)) for _ in range(3)),
        compiler_params=_cparams(("arbitrary",)),
        name="rope_tables",
    )(pos, freq)


def _norm_mod_kernel(x_ref, g_ref, sh_ref, sc_ref, xp_ref, u_ref, slab_ref):
    rows = x_ref.shape[0]
    sub = rows // STREAMS
    nslab = x_ref.shape[1] // LANES
    for c in range(nslab):
        slab_ref[c] = x_ref[:, c * LANES:(c + 1) * LANES]
    for j in range(STREAMS):
        x = jnp.concatenate([slab_ref[c, pl.ds(j, sub, stride=STREAMS), :] for c in range(nslab)], axis=1)
        xp_ref[0, j] = x
        y = x * lax.rsqrt(jnp.mean(x * x, axis=-1, keepdims=True) + NORM_EPS) * g_ref[...]
        u_ref[0, j] = (y * (1.0 + sc_ref[0]) + sh_ref[0]).astype(u_ref.dtype)


def _norm_modulate(x2d, g, shift, scale, seq):
    t, d = x2d.shape
    tm = 512
    per_b = seq // tm
    bsz = shift.shape[0]
    sub = tm // STREAMS
    out_spec = pl.BlockSpec((1, STREAMS, sub, d), lambda i: (i // per_b, 0, i % per_b, 0))
    xp, u = pl.pallas_call(
        _norm_mod_kernel,
        out_shape=(jax.ShapeDtypeStruct((bsz, STREAMS, seq // STREAMS, d), F32),
                   jax.ShapeDtypeStruct((bsz, STREAMS, seq // STREAMS, d), BF16)),
        grid=(t // tm,),
        in_specs=[pl.BlockSpec((tm, d), lambda i: (i, 0)),
                  pl.BlockSpec((1, d), lambda i: (0, 0)),
                  pl.BlockSpec((1, 1, d), lambda i: (i // per_b, 0, 0)),
                  pl.BlockSpec((1, 1, d), lambda i: (i // per_b, 0, 0))],
        out_specs=(out_spec, out_spec),
        scratch_shapes=[pltpu.VMEM((d // LANES, tm, LANES), F32)],
        compiler_params=_cparams(("arbitrary",)),
        name="norm_modulate",
    )(x2d, g.reshape(1, d), shift.reshape(bsz, 1, d), scale.reshape(bsz, 1, d))
    return xp.reshape(t, d), u.reshape(t, d)


def _cast_weight_tile(w_ref, wb_ref):
    rows = 256

    def body(r, carry):
        sl = pl.ds(pl.multiple_of(r * rows, rows), rows)
        wb_ref[sl, :] = w_ref[sl, :].astype(BF16)
        return carry

    lax.fori_loop(0, w_ref.shape[0] // rows, body, 0)


def _proj_kernel(a_ref, w_ref, o_ref, wb_ref):
    @pl.when(pl.program_id(1) == 0)
    def _():
        _cast_weight_tile(w_ref, wb_ref)

    o_ref[...] = jnp.dot(a_ref[...], wb_ref[...], preferred_element_type=F32).astype(o_ref.dtype)


def _proj_rope_kernel(a_ref, w_ref, c_ref, sa_ref, sb_ref, o_ref, wb_ref):
    @pl.when(pl.program_id(1) == 0)
    def _():
        _cast_weight_tile(w_ref, wb_ref)

    c, sa, sb = c_ref[...], sa_ref[...], sb_ref[...]
    a = a_ref[...]
    pair = 2 * HEAD_DIM
    for t in range(o_ref.shape[1] // pair):
        acc = jnp.dot(a, wb_ref[:, t * pair:(t + 1) * pair], preferred_element_type=F32)
        for h in range(2):
            x = acc[:, h * HEAD_DIM:(h + 1) * HEAD_DIM]
            up = pltpu.roll(x, HEAD_DIM - ROPE_HALF, 1)
            dn = pltpu.roll(x, ROPE_HALF, 1)
            col = t * pair + h * HEAD_DIM
            o_ref[:, col:col + HEAD_DIM] = (x * c + up * sa + dn * sb).astype(o_ref.dtype)


def _project(u, w, col0, ncols, out_dtype, rope=None):
    t, d = u.shape
    tm, tn = 1024, 1024
    j0 = col0 // tn
    in_specs = [pl.BlockSpec((tm, d), lambda j, i: (i, 0)),
                pl.BlockSpec((d, tn), lambda j, i: (0, j0 + j))]
    args = [u, w]
    kern = _proj_kernel
    if rope is not None:
        in_specs += [pl.BlockSpec((tm, LANES), lambda j, i: (i, 0)) for _ in range(3)]
        args += list(rope)
        kern = _proj_rope_kernel
    return pl.pallas_call(
        kern,
        out_shape=jax.ShapeDtypeStruct((t, ncols), out_dtype),
        grid=(ncols // tn, t // tm),
        in_specs=in_specs,
        out_specs=pl.BlockSpec((tm, tn), lambda j, i: (i, j)),
        scratch_shapes=[pltpu.VMEM((d, tn), BF16)],
        compiler_params=_cparams(("arbitrary", "arbitrary")),
        name="in_proj_rope" if rope is not None else "in_proj",
    )(*args)


def _key_pieces(qb, sub):
    L = ATTN_BLOCK
    pieces = [(0, 0, L * (qb + 1))]
    for r in range(1, STREAMS):
        back = 0 if qb == 0 else (L // 2 if r == 1 else L // 4)
        pieces.append((r, L * qb - back, L + back))
    return pieces


def _attention_bias(sub):
    L = ATTN_BLOCK
    nqb = sub // L
    kmax = max(sum(p[2] for p in _key_pieces(qb, sub)) for qb in range(nqb))
    qi = jnp.arange(L, dtype=jnp.int32)[:, None]
    out = []
    for j in range(STREAMS):
        row = []
        for qb in range(nqb):
            cols = []
            for r, start, size in _key_pieces(qb, sub):
                jp = (j + r) % STREAMS
                kn = start + jnp.arange(size, dtype=jnp.int32)[None, :]
                dt = STREAMS * (L * qb + qi - kn) + (j - jp)
                cnt = jnp.zeros(dt.shape, F32)
                for window, dil in DILATIONS:
                    cnt += ((dt >= 0) & (dt <= window) & (dt % dil == 0)).astype(F32)
                cols.append(jnp.log2(cnt))
            tile = jnp.concatenate(cols, axis=1)
            row.append(jnp.pad(tile, ((0, 0), (0, kmax - tile.shape[1])), constant_values=-jnp.inf))
        out.append(jnp.stack(row))
    return jnp.stack(out)


def _attn_kernel(q_ref, k_ref, v_ref, bias_ref, o_ref, s0, s1, m0, m1, p0, p1):
    L = ATTN_BLOCK
    sub = o_ref.shape[1] // STREAMS
    scale = HEAD_DIM ** -0.5 * 1.4426950408889634
    s_buf, m_buf, p_buf = (s0, s1), (m0, m1), (p0, p1)
    units = [(j, g, qb) for j in range(STREAMS) for g in range(HEADS_PER_STEP) for qb in range(sub // L)]

    def key_rows(j, qb):
        rows = [pl.ds(((j + r) % STREAMS) * sub + start, size) for r, start, size in _key_pieces(qb, sub)]
        return rows, sum(p[2] for p in _key_pieces(qb, sub))

    def score_stage(u):
        j, g, qb = units[u]
        lanes = slice(g * HEAD_DIM, (g + 1) * HEAD_DIM)
        rows, ktot = key_rows(j, qb)
        q = q_ref[0, pl.ds(j * sub + L * qb, L), lanes]
        k_all = jnp.concatenate([k_ref[0, rw, lanes] for rw in rows], axis=0)
        mx = None
        for c in range(0, ktot, 2 * LANES):
            w = min(2 * LANES, ktot - c)
            s = lax.dot_general(q, k_all[c:c + w], (((1,), (1,)), ((), ())), preferred_element_type=F32)
            s = s * scale + bias_ref[j, qb, :, c:c + w]
            s_buf[u % 2][:, c:c + w] = s
            for cc in range(0, w, LANES):
                part = s[:, cc:cc + LANES]
                mx = part if mx is None else jnp.maximum(mx, part)
        m_buf[u % 2][...] = jnp.broadcast_to(jnp.max(mx, axis=-1, keepdims=True), mx.shape)

    def prob_stage(u):
        j, _, qb = units[u]
        _, ktot = key_rows(j, qb)
        m = m_buf[u % 2][...]
        for c in range(0, ktot, LANES):
            p_buf[u % 2][:, c:c + LANES] = jnp.exp2(s_buf[u % 2][:, c:c + LANES] - m).astype(BF16)

    def out_stage(u):
        j, g, qb = units[u]
        lanes = slice(g * HEAD_DIM, (g + 1) * HEAD_DIM)
        rows, ktot = key_rows(j, qb)
        v_all = jnp.concatenate([v_ref[0, rw, lanes] for rw in rows], axis=0)
        v_ext = jnp.concatenate([v_all, jnp.ones_like(v_all)], axis=1)
        oe = jnp.dot(p_buf[u % 2][:, :ktot], v_ext, preferred_element_type=F32)
        y = oe[:, :HEAD_DIM] / oe[:, HEAD_DIM:]
        o_ref[0, pl.ds(j * sub + L * qb, L), lanes] = y.astype(o_ref.dtype)

    for t in range(len(units) + 2):
        if 0 <= t - 2:
            out_stage(t - 2)
        if 0 <= t - 1 < len(units):
            prob_stage(t - 1)
        if t < len(units):
            score_stage(t)


def _cols(src, bsz, seq, gw):
    a, col0 = src
    off = col0 // gw
    return a.reshape(bsz, seq, a.shape[1]), pl.BlockSpec((1, seq, gw), lambda b, h: (b, 0, off + h))


def _attention(q, k, v, width, bsz, seq):
    t = bsz * seq
    gw = HEADS_PER_STEP * HEAD_DIM
    sub = seq // STREAMS
    bias = _attention_bias(sub)
    (qa, qs), (ka, ks), (va, vs) = (_cols(s, bsz, seq, gw) for s in (q, k, v))
    out = pl.pallas_call(
        _attn_kernel,
        out_shape=jax.ShapeDtypeStruct((bsz, seq, width), BF16),
        grid=(bsz, width // gw),
        in_specs=[qs, ks, vs,
                  pl.BlockSpec(bias.shape, lambda b, h: (0, 0, 0, 0))],
        out_specs=pl.BlockSpec((1, seq, gw), lambda b, h: (b, 0, h)),
        scratch_shapes=[pltpu.VMEM((ATTN_BLOCK, bias.shape[-1]), F32)] * 2
        + [pltpu.VMEM((ATTN_BLOCK, LANES), F32)] * 2
        + [pltpu.VMEM((ATTN_BLOCK, bias.shape[-1]), BF16)] * 2,
        compiler_params=_cparams(("arbitrary", "arbitrary")),
        name="dilated_attention",
    )(qa, ka, va, bias)
    return out.reshape(t, width)


def _rec_scratch():
    C, K = REC_CHUNK, HEAD_DIM
    return [pltpu.VMEM((K, K), F32), pltpu.VMEM((C, K), F32), pltpu.VMEM((C, K), F32),
            pltpu.VMEM((C, C), BF16), pltpu.VMEM((C, K), BF16), pltpu.VMEM((K, K), F32),
            pltpu.VMEM((1, K), F32)]


def _rec_kernel(q_ref, f_ref, i_ref, g_ref, lb_ref, ng_ref, o_ref, *scratch):
    C = REC_CHUNK
    seq = q_ref.shape[1]
    sub = seq // STREAMS
    piece = C // STREAMS
    for g in range(REC_HEADS_PER_STEP):
        st_ref = scratch[g * (len(scratch) // REC_HEADS_PER_STEP)]
        st_ref[...] = jnp.zeros_like(st_ref)
    pi = lax.broadcasted_iota(jnp.int32, (C, C), 0)
    si = lax.broadcasted_iota(jnp.int32, (C, C), 1)
    time_of = lambda p: STREAMS * (p % piece) + p // piece
    causal = time_of(si) <= time_of(pi)
    tri = jnp.where(causal, 1.0, 0.0).astype(BF16)
    row_of = lambda tau: (tau % STREAMS) * piece + tau // STREAMS
    last = row_of(C - 1)
    anchor = row_of(C // 2 - 1)

    def rows(c, j):
        start = j * sub + c * piece
        return pl.ds(start if isinstance(start, int) else pl.multiple_of(start, piece), piece)

    def load(ref, c, lanes):
        return jnp.concatenate([ref[0, rows(c, j), lanes] for j in range(STREAMS)], axis=0)

    def gates(c, g):
        lanes, (_, b_s, kk_s, _, _, _, _) = head(g)
        lb = lb_ref[:, lanes]
        x = load(f_ref, c, lanes)
        z = jnp.exp(-jnp.abs(x))
        r = 1.0 / (1.0 + z)
        pos = x >= 0
        sig_p = jnp.where(pos, r, z * r)
        sig_n = jnp.where(pos, z * r, r)
        logf = jnp.log(lb + (1.0 - lb) * sig_p)
        hi = logf.astype(BF16)
        lo = (logf - hi.astype(F32)).astype(BF16)
        b_s[...] = (jnp.dot(tri, hi, preferred_element_type=F32)
                    + jnp.dot(tri, lo, preferred_element_type=F32))
        kk_s[...] = (1.0 - lb) * sig_n

    def scores(c, g):
        lanes, (_, b_s, kk_s, sc_s, qin_s, upd_s, dec_s) = head(g)
        b, kk = b_s[...], kk_s[...]
        b_last = b[last:last + 1, :]
        b_mid = b[anchor:anchor + 1, :]
        qs = _silu(load(q_ref, c, lanes).astype(F32))
        v = load(i_ref, c, lanes)
        q_a = (qs * jnp.exp(b - b_mid)).astype(BF16)
        k_a = (kk * jnp.exp(b_mid - b)).astype(BF16)
        k_e = (kk * jnp.exp(b_last - b)).astype(BF16)
        sc = lax.dot_general(q_a, k_a, (((1,), (1,)), ((), ())), preferred_element_type=F32)
        sc_s[...] = jnp.where(causal, sc, 0.0).astype(BF16)
        qin_s[...] = (qs * jnp.exp(b)).astype(BF16)
        upd_s[...] = lax.dot_general(v, k_e, (((0,), (0,)), ((), ())), preferred_element_type=F32)
        dec_s[...] = jnp.exp(b_last)

    def output(c, g):
        lanes, (st_ref, _, _, sc_s, qin_s, upd_s, dec_s) = head(g)
        v = load(i_ref, c, lanes)
        st = st_ref[...]
        o = (jnp.dot(sc_s[...], v, preferred_element_type=F32)
             + lax.dot_general(qin_s[...], st.astype(BF16), (((1,), (1,)), ((), ())),
                               preferred_element_type=F32))
        st_ref[...] = dec_s[...] * st + upd_s[...]
        y = o * lax.rsqrt(jnp.mean(o * o, axis=-1, keepdims=True) + NORM_EPS) * ng_ref[...]
        y = (y * _silu(load(g_ref, c, lanes).astype(F32))).astype(o_ref.dtype)
        for j in range(STREAMS):
            o_ref[0, rows(c, j), lanes] = y[j * piece:(j + 1) * piece]

    per_head = len(scratch) // REC_HEADS_PER_STEP

    def head(g):
        return slice(g * HEAD_DIM, (g + 1) * HEAD_DIM), scratch[g * per_head:(g + 1) * per_head]

    def stages(c, run):
        for g in range(REC_HEADS_PER_STEP):
            if run[2]:
                output(c - 2, g)
            if run[1]:
                scores(c - 1, g)
            if run[0]:
                gates(c, g)

    nc = seq // C
    stages(0, (True, False, False))
    stages(1, (True, True, False))

    def body(c, carry):
        stages(c, (True, True, True))
        return carry

    lax.fori_loop(2, nc, body, 0)
    stages(nc, (False, True, True))
    stages(nc + 1, (False, False, True))


def _recurrence(q_r, f_r, i_r, g_r, lower_bound, norm_g, bsz, seq):
    t = bsz * seq
    width = lower_bound.shape[0]
    gw = REC_HEADS_PER_STEP * HEAD_DIM
    spec = pl.BlockSpec((1, seq, gw), lambda b, h: (b, 0, h))
    (qa, qs), (fa, fs), (ia, isp), (ga, gs) = (_cols(s, bsz, seq, gw) for s in (q_r, f_r, i_r, g_r))
    out = pl.pallas_call(
        _rec_kernel,
        out_shape=jax.ShapeDtypeStruct((bsz, seq, width), BF16),
        grid=(bsz, width // gw),
        in_specs=[qs, fs, isp, gs,
                  pl.BlockSpec((1, gw), lambda b, h: (0, h)),
                  pl.BlockSpec((1, HEAD_DIM), lambda b, h: (0, 0))],
        out_specs=spec,
        scratch_shapes=[s for _ in range(REC_HEADS_PER_STEP) for s in _rec_scratch()],
        compiler_params=_cparams(("arbitrary", "arbitrary")),
        name="hgrn2_recurrence",
    )(qa, fa, ia, ga, lower_bound.reshape(1, width), norm_g.reshape(1, HEAD_DIM))
    return out.reshape(t, width)


def _merge_kernel(ya_ref, yr_ref, wa_ref, wr_ref, ga_ref, gr_ref, o_ref):
    a = jnp.dot(ya_ref[...], wa_ref[...], preferred_element_type=F32)
    r = jnp.dot(yr_ref[...], wr_ref[...], preferred_element_type=F32)
    m = _sigmoid(ga_ref[...].astype(F32)) * a + _sigmoid(gr_ref[...].astype(F32)) * r
    o_ref[...] = m.astype(o_ref.dtype)


def _merge(ya, yr, wa, wr, ga, gr):
    t, d = ya.shape
    n = wa.shape[1]
    tm, tn = 1024, 512
    row = pl.BlockSpec((tm, d), lambda i, j: (i, 0))
    col = pl.BlockSpec((d, tn), lambda i, j: (0, j))
    tile = pl.BlockSpec((tm, tn), lambda i, j: (i, j))
    ga_off, gr_off = ga[1] // tn, gr[1] // tn
    return pl.pallas_call(
        _merge_kernel,
        out_shape=jax.ShapeDtypeStruct((t, n), BF16),
        grid=(t // tm, n // tn),
        in_specs=[row, row, col, col,
                  pl.BlockSpec((tm, tn), lambda i, j: (i, ga_off + j)),
                  pl.BlockSpec((tm, tn), lambda i, j: (i, gr_off + j))],
        out_specs=tile,
        compiler_params=_cparams(("arbitrary", "arbitrary")),
        name="branch_merge",
    )(ya, yr, wa, wr, ga[0], gr[0])


def _mixout_kernel(m_ref, w_ref, x_ref, gt_ref, g_ref, sh_ref, sc_ref, rw_ref, rb_ref,
                   h_ref, u_ref, route_ref):
    mix = jnp.dot(m_ref[...], w_ref[...], preferred_element_type=F32)
    h = x_ref[...] + gt_ref[0] * mix
    h_ref[...] = h
    u = h * lax.rsqrt(jnp.mean(h * h, axis=-1, keepdims=True) + NORM_EPS) * g_ref[...]
    u = u * (1.0 + sc_ref[0]) + sh_ref[0]
    wc = 2 * u_ref.shape[2]
    for c in range(u_ref.shape[0]):
        u_ref[c] = _pack_bf16_pairs(u[:, c * wc:(c + 1) * wc])
    u_hi = u.astype(BF16)
    u_lo = (u - u_hi.astype(F32)).astype(BF16)
    rw = rw_ref[...]
    w_hi = rw.astype(BF16)
    w_lo = (rw - w_hi.astype(F32)).astype(BF16)
    logits = (jnp.dot(u_hi, w_hi, preferred_element_type=F32)
              + jnp.dot(u_lo, w_hi, preferred_element_type=F32)
              + jnp.dot(u_hi, w_lo, preferred_element_type=F32)) + rb_ref[...]
    lane = lax.broadcasted_iota(jnp.int32, logits.shape, 1).astype(F32)
    big = float(LANES)
    neg = -jnp.inf
    lg = jnp.where(lane < N_GROUPS, logits, neg)
    mg = jnp.max(lg, axis=-1, keepdims=True)
    g_sel = jnp.min(jnp.where(lg == mg, lane, big), axis=-1, keepdims=True)
    p_group = 1.0 / jnp.sum(jnp.exp(lg - mg), axis=-1, keepdims=True)
    lo = N_GROUPS + EXPERTS_PER_GROUP * g_sel
    le = jnp.where((lane >= lo) & (lane < lo + EXPERTS_PER_GROUP), logits, neg)
    t1 = jnp.max(le, axis=-1, keepdims=True)
    i1 = jnp.min(jnp.where(le == t1, lane, big), axis=-1, keepdims=True)
    le2 = jnp.where(lane == i1, neg, le)
    t2 = jnp.max(le2, axis=-1, keepdims=True)
    i2 = jnp.min(jnp.where(le2 == t2, lane, big), axis=-1, keepdims=True)
    e21 = jnp.exp(t2 - t1)
    w1 = p_group / (1.0 + e21)
    w2 = p_group * e21 / (1.0 + e21)
    route = jnp.where(lane == 0, i1 - N_GROUPS,
                      jnp.where(lane == 1, i2 - N_GROUPS,
                                jnp.where(lane == 2, w1, jnp.where(lane == 3, w2, 0.0))))
    route_ref[...] = route


def _mix_out(merged, w_out, x2d, gate, g, shift, scale, rw, rb, seq):
    t, d = x2d.shape
    tm = 512
    per_b = seq // tm
    bsz = gate.shape[0]
    row = lambda dt: pl.BlockSpec((tm, d), lambda i: (i, 0))
    per_batch = pl.BlockSpec((1, 1, d), lambda i: (i // per_b, 0, 0))
    const = lambda shape: pl.BlockSpec(shape, lambda i: (0,) * len(shape))
    wc = d // GATHER_CHUNKS // 2
    outs = pl.pallas_call(
        _mixout_kernel,
        out_shape=(jax.ShapeDtypeStruct((t, d), F32),
                   jax.ShapeDtypeStruct((GATHER_CHUNKS, t, wc), jnp.uint32),
                   jax.ShapeDtypeStruct((t, LANES), F32)),
        grid=(t // tm,),
        in_specs=[row(BF16), const((d, d)), row(F32), per_batch, const((1, d)), per_batch, per_batch,
                  const((d, LANES)), const((1, LANES))],
        out_specs=(row(F32), pl.BlockSpec((GATHER_CHUNKS, tm, wc), lambda i: (0, i, 0)),
                   pl.BlockSpec((tm, LANES), lambda i: (i, 0))),
        compiler_params=_cparams(("arbitrary",)),
        name="mix_out_router",
    )(merged, w_out, x2d, gate.reshape(bsz, 1, d), g.reshape(1, d),
      shift.reshape(bsz, 1, d), scale.reshape(bsz, 1, d), rw, rb)
    return outs


def _expert_kernel(be_ref, nx_ref, nu_ref, *refs):
    x_ref = refs[0]
    w_hbm = refs[1:4]
    o_ref = refs[4]
    stage = refs[5:8]
    wb = refs[8:11]
    sem = refs[11]
    nch = x_ref.shape[0]
    wc = 2 * o_ref.shape[2]
    i = pl.program_id(0)
    e = be_ref[i]
    nxt = nx_ref[i]
    active = i < nu_ref[0]
    first = i == 0
    run_start = jnp.logical_or(first, e != be_ref[jnp.maximum(i - 1, 0)])

    def weight_copies(expert):
        return [pltpu.make_async_copy(w_hbm[k].at[expert], stage[k], sem.at[k]) for k in range(3)]

    @pl.when(jnp.logical_and(active, first))
    def _():
        for cp in weight_copies(e):
            cp.start()

    @pl.when(jnp.logical_and(active, run_start))
    def _():
        for cp in weight_copies(e):
            cp.wait()
        for k in range(3):
            _cast_weight_tile(stage[k], wb[k])

        @pl.when(nxt >= 0)
        def _():
            for cp in weight_copies(nxt):
                cp.start()

    @pl.when(active)
    def _():
        x = jnp.concatenate([_unpack_bf16_pairs(x_ref[c]) for c in range(nch)], axis=1).astype(BF16)
        hg = jnp.dot(x, wb[0][...], preferred_element_type=F32)
        hu = jnp.dot(x, wb[1][...], preferred_element_type=F32)
        hdn = (_silu(hg) * hu).astype(BF16)
        y = jnp.dot(hdn, wb[2][...], preferred_element_type=F32)
        for c in range(nch):
            o_ref[c] = _pack_bf16_pairs(y[:, c * wc:(c + 1) * wc])

    @pl.when(jnp.logical_not(active))
    def _():
        o_ref[...] = jnp.zeros_like(o_ref)


def _expert_ffn(xs, w_gate, w_up, w_down, block_expert, next_expert, n_used):
    nch, n_slots, wc = xs.shape
    d = 2 * wc * nch
    hid = w_gate.shape[2]
    bm = MOE_ROWS
    chunk = pl.BlockSpec((nch, bm, wc), lambda i, be, nx, nu: (0, i, 0))
    hbm = pl.BlockSpec(memory_space=pl.ANY)
    grid_spec = pltpu.PrefetchScalarGridSpec(
        num_scalar_prefetch=3,
        grid=(n_slots // bm,),
        in_specs=[chunk, hbm, hbm, hbm],
        out_specs=chunk,
        scratch_shapes=[pltpu.VMEM((d, hid), F32), pltpu.VMEM((d, hid), F32), pltpu.VMEM((hid, d), F32),
                        pltpu.VMEM((d, hid), BF16), pltpu.VMEM((d, hid), BF16), pltpu.VMEM((hid, d), BF16),
                        pltpu.SemaphoreType.DMA((3,))],
    )
    return pl.pallas_call(
        _expert_kernel,
        out_shape=jax.ShapeDtypeStruct((nch, n_slots, wc), jnp.uint32),
        grid_spec=grid_spec,
        compiler_params=_cparams(("arbitrary",)),
        name="expert_ffn",
    )(block_expert, next_expert, n_used, xs, w_gate, w_up, w_down)


def _gather_rows(chunks, idx):
    nch, n, d = chunks.shape
    table = chunks.reshape(nch * n, d)
    idx = (idx[None, :] + (jnp.arange(nch, dtype=jnp.int32) * n)[:, None]).reshape(-1)
    m = idx.shape[0]
    window = LANES
    mesh = plsc.VectorSubcoreMesh(core_axis_name="core", subcore_axis_name="subcore")

    @pl.kernel(out_type=jax.ShapeDtypeStruct((m, d), table.dtype), mesh=mesh, scratch_types=[])
    def gather(x_hbm, i_hbm, o_hbm):
        def body(i_vmem, o_vmem):
            pltpu.sync_copy(x_hbm.at[i_vmem.at[0]], o_vmem)

        pltpu.emit_pipeline(
            body,
            grid=(m // window,),
            in_specs=[pl.BlockSpec((1, window), lambda i: (0, i))],
            out_specs=[pl.BlockSpec((window, d), lambda i: (i, 0))],
            core_axis_name=("core", "subcore"),
            dimension_semantics=(pltpu.PARALLEL,),
        )(i_hbm, o_hbm)

    return gather(table, idx.reshape(1, m)).reshape(nch, m // nch, d)


def _final_kernel(h_ref, *rest):
    y_refs = rest[:TOP_K]
    rt_ref, gt_ref, g_ref, o_ref, slab_ref = rest[TOP_K:]
    sub = h_ref.shape[2]
    nslab = h_ref.shape[3] // LANES
    for j in range(STREAMS):
        route = rt_ref[0, j]
        ffn = sum(route[:, TOP_K + k:TOP_K + k + 1]
                  * jnp.concatenate([_unpack_bf16_pairs(y_refs[k][c, 0, 0, j])
                                     for c in range(GATHER_CHUNKS)], axis=1) for k in range(TOP_K))
        h = h_ref[0, j] + gt_ref[0] * ffn
        y = h * lax.rsqrt(jnp.mean(h * h, axis=-1, keepdims=True) + NORM_EPS) * g_ref[...]
        for c in range(nslab):
            slab_ref[c, pl.ds(j, sub, stride=STREAMS), :] = y[:, c * LANES:(c + 1) * LANES]
    for c in range(nslab):
        o_ref[:, c * LANES:(c + 1) * LANES] = slab_ref[c]


def _final(h, y2, route, gate, g, seq):
    t, d = h.shape
    tm = 512
    per_b = seq // tm
    bsz = gate.shape[0]
    sub = tm // STREAMS
    spec = pl.BlockSpec((1, STREAMS, sub, d), lambda i: (i // per_b, 0, i % per_b, 0))
    wc = d // GATHER_CHUNKS // 2
    y6 = y2.reshape(GATHER_CHUNKS, TOP_K, bsz, STREAMS, seq // STREAMS, wc)

    def yspec(k):
        return pl.BlockSpec((GATHER_CHUNKS, 1, 1, STREAMS, sub, wc),
                            lambda i: (0, k, i // per_b, 0, i % per_b, 0))

    return pl.pallas_call(
        _final_kernel,
        out_shape=jax.ShapeDtypeStruct((t, d), F32),
        grid=(t // tm,),
        in_specs=[spec, *[yspec(k) for k in range(TOP_K)],
                  pl.BlockSpec((1, STREAMS, sub, LANES), lambda i: (i // per_b, 0, i % per_b, 0)),
                  pl.BlockSpec((1, 1, d), lambda i: (i // per_b, 0, 0)),
                  pl.BlockSpec((1, d), lambda i: (0, 0))],
        out_specs=pl.BlockSpec((tm, d), lambda i: (i, 0)),
        scratch_shapes=[pltpu.VMEM((d // LANES, tm, LANES), F32)],
        compiler_params=_cparams(("arbitrary",)),
        name="final_norm",
    )(h.reshape(bsz, STREAMS, seq // STREAMS, d), *([y6] * TOP_K),
      route.reshape(bsz, STREAMS, seq // STREAMS, LANES), gate.reshape(bsz, 1, d), g.reshape(1, d))


def _dispatch_plan(expert_idx):
    n_assign = expert_idx.size
    n_blocks = n_assign // MOE_ROWS + N_EXPERTS
    n_slots = n_blocks * MOE_ROWS
    flat_e = expert_idx.reshape(-1)
    ids = jnp.arange(n_assign, dtype=jnp.int32)
    eids = jnp.arange(N_EXPERTS, dtype=jnp.int32)[None, :]
    _, order = lax.sort((flat_e, ids), num_keys=1, is_stable=True)
    _, rank_sorted = lax.sort((order, ids), num_keys=1)
    hot_a = (flat_e[:, None] == eids).astype(jnp.int32)
    counts = jnp.sum(hot_a, axis=0)
    padded = ((counts + MOE_ROWS - 1) // MOE_ROWS) * MOE_ROWS
    pad_end = jnp.cumsum(padded)
    pad_start = pad_end - padded
    start = jnp.cumsum(counts) - counts
    dest = (rank_sorted + jnp.sum(hot_a * (pad_start - start)[None, :], axis=1)).reshape(-1, TOP_K)
    blk0 = jnp.arange(n_blocks, dtype=jnp.int32) * MOE_ROWS
    block_expert = jnp.minimum(jnp.sum((pad_end[None, :] <= blk0[:, None]).astype(jnp.int32), axis=1),
                               N_EXPERTS - 1)
    hot_b = (block_expert[:, None] == eids).astype(jnp.int32)
    blk_shift = jnp.sum(hot_b * (start - pad_start)[None, :], axis=1)
    blk_count = jnp.sum(hot_b * (pad_start + counts)[None, :], axis=1)
    slot = jnp.arange(n_slots, dtype=jnp.int32).reshape(n_blocks, MOE_ROWS)
    valid = (slot < blk_count[:, None]).reshape(-1)
    src = jnp.clip(slot + blk_shift[:, None], 0, n_assign - 1).reshape(-1)
    assign = order[src]
    token_of_slot = jnp.where(valid, assign // TOP_K, slot.reshape(-1) % (n_assign // TOP_K))
    n_used = (pad_end[-1:] // MOE_ROWS).astype(jnp.int32)
    bi = jnp.arange(n_blocks, dtype=jnp.int32)
    later = ((bi[None, :] > bi[:, None]) & (block_expert[None, :] != block_expert[:, None])
             & (bi[None, :] < n_used[0]))
    next_expert = jnp.where(jnp.any(later, axis=1), block_expert[jnp.argmax(later, axis=1)], -1)
    return token_of_slot, dest, block_expert, next_expert.astype(jnp.int32), n_used


def kernel(x, c, positions, ada_w, ada_b, mix_norm_g, w_in, w_attn_branch, w_rec_branch, w_mix_out,
           rec_norm_g, rec_lb_logits, ffn_norm_g, router_group_w, router_group_b, router_expert_w,
           router_expert_b, expert_w_gate, expert_w_up, expert_w_down, final_norm_g):
    bsz, seq, d = x.shape
    t = bsz * seq
    depth = ada_w.shape[0]
    assert depth == 1, "final norm is fused after the single layer"
    lower_bounds = jnp.cumsum(jax.nn.softmax(rec_lb_logits.astype(F32), axis=0), axis=0)
    pos_streams = positions.reshape(bsz, seq // STREAMS, STREAMS).transpose(0, 2, 1)
    rope = _rope_tables(pos_streams)
    h = x.reshape(t, d)
    for layer in range(depth):
        mod = _modulation(c, ada_w[layer], ada_b[layer])
        sh_m, sc_m, gt_m, sh_f, sc_f, gt_f = jnp.split(mod, 6, axis=-1)
        h, u = _norm_modulate(h, mix_norm_g[layer], sh_m, sc_m, seq)
        w = w_in[layer]
        qk = _project(u, w, 0, 2 * d, BF16, rope=rope)
        rest = _project(u, w, 2 * d, 2 * d, BF16)
        f_r = _project(u, w, 4 * d, d, F32)
        tail = _project(u, w, 5 * d, 4 * d, BF16)
        y_attn = _attention((qk, 0), (qk, d), (rest, 0), d, bsz, seq)
        y_rec = _recurrence((rest, d), (f_r, 0), (tail, 0), (tail, d), lower_bounds[layer],
                            rec_norm_g[layer], bsz, seq)
        merged = _merge(y_attn, y_rec, w_attn_branch[layer].astype(BF16),
                        w_rec_branch[layer].astype(BF16), (tail, 2 * d), (tail, 3 * d))
        rw = jnp.concatenate([router_group_w[layer], router_expert_w[layer],
                              jnp.zeros((d, LANES - N_GROUPS - N_EXPERTS), F32)], axis=1)
        rb = jnp.concatenate([router_group_b[layer], router_expert_b[layer],
                              jnp.zeros((LANES - N_GROUPS - N_EXPERTS,), F32)]).reshape(1, LANES)
        h, u2, route = _mix_out(merged, w_mix_out[layer].astype(BF16), h, gt_m, ffn_norm_g[layer],
                                sh_f, sc_f, rw, rb, seq)
        expert_idx = route[:, :TOP_K].astype(jnp.int32)
        tok, dest, block_expert, next_expert, n_used = _dispatch_plan(expert_idx)
        xs = _gather_rows(u2, tok)
        ys = _expert_ffn(xs, expert_w_gate[layer], expert_w_up[layer], expert_w_down[layer],
                         block_expert, next_expert, n_used)
        dest_kt = dest.T.reshape(-1)
        y2 = _gather_rows(ys, dest_kt)
        h = _final(h, y2, route, gt_f, final_norm_g, seq)
    return h.reshape(bsz, seq, d)
```

```python
import functools

import jax
import jax.numpy as jnp
from jax import lax
from jax.experimental import pallas as pl
from jax.experimental.pallas import tpu as pltpu
from jax.experimental.pallas import tpu_sc as plsc

F32 = jnp.float32
BF16 = jnp.bfloat16

D_MODEL = 2048
HEAD_DIM = 128
N_HEADS = D_MODEL // HEAD_DIM
ROPE_DIM = HEAD_DIM // 4
ROPE_HALF = ROPE_DIM // 2
ROPE_THETA = 500000.0
ATTN_SPAN = 128
ATTN_BLOCK = 128
REC_CHUNK = 64
N_GROUPS = 4
EXPERTS_PER_GROUP = 8
N_EXPERTS = N_GROUPS * EXPERTS_PER_GROUP
EXPERT_HIDDEN = D_MODEL // 2
TOP_K = 2
NORM_EPS = 1e-6
IN_WIDTH = 9 * D_MODEL

LANES = 128
VMEM_LIMIT = 56 * 1024 * 1024

MOE_ROWS = 256
HEADS_PER_STEP = 2
REC_HEADS_PER_STEP = 4
REC_CHUNKS_PER_STEP = 8
GATHER_CHUNKS = 4
STREAMS = 4
DILATIONS = ((128, 1), (512, 4), (2048, 16))


def _cparams(sem):
    return pltpu.CompilerParams(dimension_semantics=sem, vmem_limit_bytes=VMEM_LIMIT)


def _sigmoid(x):
    return 1.0 / (1.0 + jnp.exp(-x))


def _silu(x):
    return x * _sigmoid(x)


def _pack_bf16_pairs(x):
    w = x.shape[1] // 2
    lo = lax.bitcast_convert_type(x[:, :w].astype(BF16).astype(F32), jnp.uint32)
    hi = lax.bitcast_convert_type(x[:, w:].astype(BF16).astype(F32), jnp.uint32)
    return (lo >> 16) | (hi & jnp.uint32(0xFFFF0000))


def _unpack_bf16_pairs(words):
    lo = lax.bitcast_convert_type(words << 16, F32)
    hi = lax.bitcast_convert_type(words & jnp.uint32(0xFFFF0000), F32)
    return jnp.concatenate([lo, hi], axis=1)


def _mod_kernel(c_ref, w_ref, b_ref, o_ref):
    cond = _silu(c_ref[...])
    o_ref[...] = jnp.dot(cond, w_ref[...], precision=lax.Precision.HIGHEST,
                         preferred_element_type=F32) + b_ref[...]


def _modulation(c, w, b):
    bsz, d = c.shape
    n = w.shape[1]
    tn = 1536
    return pl.pallas_call(
        _mod_kernel,
        out_shape=jax.ShapeDtypeStruct((bsz, n), F32),
        grid=(n // tn,),
        in_specs=[pl.BlockSpec((bsz, d), lambda j: (0, 0)),
                  pl.BlockSpec((d, tn), lambda j: (0, j)),
                  pl.BlockSpec((1, tn), lambda j: (0, j))],
        out_specs=pl.BlockSpec((bsz, tn), lambda j: (0, j)),
        compiler_params=_cparams(("arbitrary",)),
        name="adaln_mod",
    )(c, w, b.reshape(1, n))


def _rope_kernel(pos_ref, freq_ref, c_ref, sa_ref, sb_ref):
    ang = pos_ref[...] * freq_ref[...]
    lane = lax.broadcasted_iota(jnp.int32, ang.shape, 1)
    cos, sin = jnp.cos(ang), jnp.sin(ang)
    c_ref[...] = jnp.where(lane < ROPE_DIM, cos, 1.0)
    sa_ref[...] = jnp.where(lane < ROPE_HALF, -sin, 0.0)
    sb_ref[...] = jnp.where((lane >= ROPE_HALF) & (lane < ROPE_DIM), sin, 0.0)


def _rope_tables(positions):
    t = positions.size
    tm = 2048
    inv_freq = ROPE_THETA ** (-jnp.arange(0, ROPE_DIM, 2, dtype=F32) / ROPE_DIM)
    freq = jnp.concatenate([inv_freq, inv_freq, jnp.zeros((LANES - ROPE_DIM,), F32)]).reshape(1, LANES)
    pos = positions.astype(F32).reshape(t, 1)
    out = jax.ShapeDtypeStruct((t, LANES), F32)
    return pl.pallas_call(
        _rope_kernel,
        out_shape=(out, out, out),
        grid=(t // tm,),
        in_specs=[pl.BlockSpec((tm, 1), lambda i: (i, 0)),
                  pl.BlockSpec((1, LANES), lambda i: (0, 0))],
        out_specs=tuple(pl.BlockSpec((tm, LANES), lambda i: (i, 0)) for _ in range(3)),
        compiler_params=_cparams(("arbitrary",)),
        name="rope_tables",
    )(pos, freq)


def _norm_mod_kernel(x_ref, g_ref, sh_ref, sc_ref, xp_ref, u_ref, slab_ref):
    rows = x_ref.shape[0]
    sub = rows // STREAMS
    nslab = x_ref.shape[1] // LANES
    for c in range(nslab):
        slab_ref[c] = x_ref[:, c * LANES:(c + 1) * LANES]
    for j in range(STREAMS):
        x = jnp.concatenate([slab_ref[c, pl.ds(j, sub, stride=STREAMS), :] for c in range(nslab)], axis=1)
        xp_ref[0, j] = x
        y = x * lax.rsqrt(jnp.mean(x * x, axis=-1, keepdims=True) + NORM_EPS) * g_ref[...]
        u_ref[0, j] = (y * (1.0 + sc_ref[0]) + sh_ref[0]).astype(u_ref.dtype)


def _norm_modulate(x2d, g, shift, scale, seq):
    t, d = x2d.shape
    tm = 512
    per_b = seq // tm
    bsz = shift.shape[0]
    sub = tm // STREAMS
    out_spec = pl.BlockSpec((1, STREAMS, sub, d), lambda i: (i // per_b, 0, i % per_b, 0))
    xp, u = pl.pallas_call(
        _norm_mod_kernel,
        out_shape=(jax.ShapeDtypeStruct((bsz, STREAMS, seq // STREAMS, d), F32),
                   jax.ShapeDtypeStruct((bsz, STREAMS, seq // STREAMS, d), BF16)),
        grid=(t // tm,),
        in_specs=[pl.BlockSpec((tm, d), lambda i: (i, 0)),
                  pl.BlockSpec((1, d), lambda i: (0, 0)),
                  pl.BlockSpec((1, 1, d), lambda i: (i // per_b, 0, 0)),
                  pl.BlockSpec((1, 1, d), lambda i: (i // per_b, 0, 0))],
        out_specs=(out_spec, out_spec),
        scratch_shapes=[pltpu.VMEM((d // LANES, tm, LANES), F32)],
        compiler_params=_cparams(("arbitrary",)),
        name="norm_modulate",
    )(x2d, g.reshape(1, d), shift.reshape(bsz, 1, d), scale.reshape(bsz, 1, d))
    return xp.reshape(t, d), u.reshape(t, d)


def _cast_weight_tile(w_ref, wb_ref):
    rows = 256

    def body(r, carry):
        sl = pl.ds(pl.multiple_of(r * rows, rows), rows)
        wb_ref[sl, :] = w_ref[sl, :].astype(BF16)
        return carry

    lax.fori_loop(0, w_ref.shape[0] // rows, body, 0)


def _proj_kernel(a_ref, w_ref, o_ref, wb_ref):
    @pl.when(pl.program_id(1) == 0)
    def _():
        _cast_weight_tile(w_ref, wb_ref)

    o_ref[...] = jnp.dot(a_ref[...], wb_ref[...], preferred_element_type=F32).astype(o_ref.dtype)


def _rope_project(a_ref, wb_ref, c_ref, sa_ref, sb_ref, o_ref, t):
    c, sa, sb = c_ref[...], sa_ref[...], sb_ref[...]
    pair = 2 * HEAD_DIM
    acc = jnp.dot(a_ref[...], wb_ref[:, t * pair:(t + 1) * pair], preferred_element_type=F32)
    for h in range(2):
        x = acc[:, h * HEAD_DIM:(h + 1) * HEAD_DIM]
        up = pltpu.roll(x, HEAD_DIM - ROPE_HALF, 1)
        dn = pltpu.roll(x, ROPE_HALF, 1)
        col = t * pair + h * HEAD_DIM
        o_ref[:, col:col + HEAD_DIM] = (x * c + up * sa + dn * sb).astype(o_ref.dtype)


def _project(u, w, col0, ncols, out_dtype):
    t, d = u.shape
    tm, tn = 1024, 1024
    j0 = col0 // tn
    return pl.pallas_call(
        _proj_kernel,
        out_shape=jax.ShapeDtypeStruct((t, ncols), out_dtype),
        grid=(ncols // tn, t // tm),
        in_specs=[pl.BlockSpec((tm, d), lambda j, i: (i, 0)),
                  pl.BlockSpec((d, tn), lambda j, i: (0, j0 + j))],
        out_specs=pl.BlockSpec((tm, tn), lambda j, i: (i, j)),
        scratch_shapes=[pltpu.VMEM((d, tn), BF16)],
        compiler_params=_cparams(("arbitrary", "arbitrary")),
        name="in_proj",
    )(u, w)


def _key_pieces(qb, sub):
    L = ATTN_BLOCK
    pieces = [(0, 0, L * (qb + 1))]
    for r in range(1, STREAMS):
        back = 0 if qb == 0 else (L // 2 if r == 1 else L // 4)
        pieces.append((r, L * qb - back, L + back))
    return pieces


def _attention_bias(sub):
    L = ATTN_BLOCK
    nqb = sub // L
    kmax = max(sum(p[2] for p in _key_pieces(qb, sub)) for qb in range(nqb))
    qi = jnp.arange(L, dtype=jnp.int32)[:, None]
    out = []
    for j in range(STREAMS):
        row = []
        for qb in range(nqb):
            cols = []
            for r, start, size in _key_pieces(qb, sub):
                jp = (j + r) % STREAMS
                kn = start + jnp.arange(size, dtype=jnp.int32)[None, :]
                dt = STREAMS * (L * qb + qi - kn) + (j - jp)
                cnt = jnp.zeros(dt.shape, F32)
                for window, dil in DILATIONS:
                    cnt += ((dt >= 0) & (dt <= window) & (dt % dil == 0)).astype(F32)
                cols.append(jnp.log2(cnt))
            tile = jnp.concatenate(cols, axis=1)
            row.append(jnp.pad(tile, ((0, 0), (0, kmax - tile.shape[1])), constant_values=-jnp.inf))
        out.append(jnp.stack(row))
    return jnp.stack(out)


def _attn_kernel(q_ref, k_ref, v_ref, bias_ref, o_ref, s0, s1, m0, m1, p0, p1):
    L = ATTN_BLOCK
    sub = o_ref.shape[1] // STREAMS
    scale = HEAD_DIM ** -0.5 * 1.4426950408889634
    s_buf, m_buf, p_buf = (s0, s1), (m0, m1), (p0, p1)
    units = [(j, g, qb) for j in range(STREAMS) for g in range(HEADS_PER_STEP) for qb in range(sub // L)]

    def key_rows(j, qb):
        rows = [pl.ds(((j + r) % STREAMS) * sub + start, size) for r, start, size in _key_pieces(qb, sub)]
        return rows, sum(p[2] for p in _key_pieces(qb, sub))

    def score_stage(u):
        j, g, qb = units[u]
        lanes = slice(g * HEAD_DIM, (g + 1) * HEAD_DIM)
        rows, ktot = key_rows(j, qb)
        q = q_ref[0, pl.ds(j * sub + L * qb, L), lanes]
        k_all = jnp.concatenate([k_ref[0, rw, lanes] for rw in rows], axis=0)
        mx = None
        for c in range(0, ktot, 2 * LANES):
            w = min(2 * LANES, ktot - c)
            s = lax.dot_general(q, k_all[c:c + w], (((1,), (1,)), ((), ())), preferred_element_type=F32)
            s = s * scale + bias_ref[j, qb, :, c:c + w]
            s_buf[u % 2][:, c:c + w] = s
            for cc in range(0, w, LANES):
                part = s[:, cc:cc + LANES]
                mx = part if mx is None else jnp.maximum(mx, part)
        m_buf[u % 2][...] = jnp.broadcast_to(jnp.max(mx, axis=-1, keepdims=True), mx.shape)

    def prob_stage(u):
        j, _, qb = units[u]
        _, ktot = key_rows(j, qb)
        m = m_buf[u % 2][...]
        for c in range(0, ktot, LANES):
            p_buf[u % 2][:, c:c + LANES] = jnp.exp2(s_buf[u % 2][:, c:c + LANES] - m).astype(BF16)

    def out_stage(u):
        j, g, qb = units[u]
        lanes = slice(g * HEAD_DIM, (g + 1) * HEAD_DIM)
        rows, ktot = key_rows(j, qb)
        v_all = jnp.concatenate([v_ref[0, rw, lanes] for rw in rows], axis=0)
        v_ext = jnp.concatenate([v_all, jnp.ones_like(v_all)], axis=1)
        oe = jnp.dot(p_buf[u % 2][:, :ktot], v_ext, preferred_element_type=F32)
        y = oe[:, :HEAD_DIM] / oe[:, HEAD_DIM:]
        o_ref[0, pl.ds(j * sub + L * qb, L), lanes] = y.astype(o_ref.dtype)

    for t in range(len(units) + 2):
        if 0 <= t - 2:
            out_stage(t - 2)
        if 0 <= t - 1 < len(units):
            prob_stage(t - 1)
        if t < len(units):
            score_stage(t)


def _cols(src, bsz, seq, gw):
    a, col0 = src
    off = col0 // gw
    return a.reshape(bsz, seq, a.shape[1]), pl.BlockSpec((1, seq, gw), lambda b, h: (b, 0, off + h))


def _attention(q, k, v, width, bsz, seq):
    t = bsz * seq
    gw = HEADS_PER_STEP * HEAD_DIM
    sub = seq // STREAMS
    bias = _attention_bias(sub)
    (qa, qs), (ka, ks), (va, vs) = (_cols(s, bsz, seq, gw) for s in (q, k, v))
    out = pl.pallas_call(
        _attn_kernel,
        out_shape=jax.ShapeDtypeStruct((bsz, seq, width), BF16),
        grid=(bsz, width // gw),
        in_specs=[qs, ks, vs,
                  pl.BlockSpec(bias.shape, lambda b, h: (0, 0, 0, 0))],
        out_specs=pl.BlockSpec((1, seq, gw), lambda b, h: (b, 0, h)),
        scratch_shapes=[pltpu.VMEM((ATTN_BLOCK, bias.shape[-1]), F32)] * 2
        + [pltpu.VMEM((ATTN_BLOCK, LANES), F32)] * 2
        + [pltpu.VMEM((ATTN_BLOCK, bias.shape[-1]), BF16)] * 2,
        compiler_params=_cparams(("arbitrary", "arbitrary")),
        name="dilated_attention",
    )(qa, ka, va, bias)
    return out.reshape(t, width)


def _rec_scratch():
    C, K = REC_CHUNK, HEAD_DIM
    return [pltpu.VMEM((K, K), F32), pltpu.VMEM((C, K), F32), pltpu.VMEM((C, K), F32),
            pltpu.VMEM((C, C), BF16), pltpu.VMEM((C, K), BF16), pltpu.VMEM((K, K), F32),
            pltpu.VMEM((1, K), F32)]


def _rec_pipeline(q_ref, f_ref, i_ref, g_ref, lb_ref, ng_ref, o_ref, scratch, c0, n, interleave):
    C = REC_CHUNK
    seq = q_ref.shape[1]
    sub = seq // STREAMS
    piece = C // STREAMS
    pi = lax.broadcasted_iota(jnp.int32, (C, C), 0)
    si = lax.broadcasted_iota(jnp.int32, (C, C), 1)
    time_of = lambda p: STREAMS * (p % piece) + p // piece
    causal = time_of(si) <= time_of(pi)
    tri = jnp.where(causal, 1.0, 0.0).astype(BF16)
    row_of = lambda tau: (tau % STREAMS) * piece + tau // STREAMS
    last = row_of(C - 1)
    anchor = row_of(C // 2 - 1)
    per_head = len(scratch) // REC_HEADS_PER_STEP

    def head(g):
        return slice(g * HEAD_DIM, (g + 1) * HEAD_DIM), scratch[g * per_head:(g + 1) * per_head]

    def rows(c, j):
        start = j * sub + c * piece
        return pl.ds(start if isinstance(start, int) else pl.multiple_of(start, piece), piece)

    def load(ref, c, lanes):
        return jnp.concatenate([ref[0, rows(c, j), lanes] for j in range(STREAMS)], axis=0)

    def gates(c, g):
        lanes, (_, b_s, kk_s, _, _, _, _) = head(g)
        lb = lb_ref[:, lanes]
        x = load(f_ref, c, lanes)
        z = jnp.exp(-jnp.abs(x))
        r = 1.0 / (1.0 + z)
        pos = x >= 0
        sig_p = jnp.where(pos, r, z * r)
        sig_n = jnp.where(pos, z * r, r)
        logf = jnp.log(lb + (1.0 - lb) * sig_p)
        hi = logf.astype(BF16)
        lo = (logf - hi.astype(F32)).astype(BF16)
        b_s[...] = (jnp.dot(tri, hi, preferred_element_type=F32)
                    + jnp.dot(tri, lo, preferred_element_type=F32))
        kk_s[...] = (1.0 - lb) * sig_n

    def scores(c, g):
        lanes, (_, b_s, kk_s, sc_s, qin_s, upd_s, dec_s) = head(g)
        b, kk = b_s[...], kk_s[...]
        b_last = b[last:last + 1, :]
        b_mid = b[anchor:anchor + 1, :]
        qs = _silu(load(q_ref, c, lanes).astype(F32))
        v = load(i_ref, c, lanes)
        q_a = (qs * jnp.exp(b - b_mid)).astype(BF16)
        k_a = (kk * jnp.exp(b_mid - b)).astype(BF16)
        k_e = (kk * jnp.exp(b_last - b)).astype(BF16)
        sc = lax.dot_general(q_a, k_a, (((1,), (1,)), ((), ())), preferred_element_type=F32)
        sc_s[...] = jnp.where(causal, sc, 0.0).astype(BF16)
        qin_s[...] = (qs * jnp.exp(b)).astype(BF16)
        upd_s[...] = lax.dot_general(v, k_e, (((0,), (0,)), ((), ())), preferred_element_type=F32)
        dec_s[...] = jnp.exp(b_last)

    def output(c, g):
        lanes, (st_ref, _, _, sc_s, qin_s, upd_s, dec_s) = head(g)
        v = load(i_ref, c, lanes)
        st = st_ref[...]
        o = (jnp.dot(sc_s[...], v, preferred_element_type=F32)
             + lax.dot_general(qin_s[...], st.astype(BF16), (((1,), (1,)), ((), ())),
                               preferred_element_type=F32))
        st_ref[...] = dec_s[...] * st + upd_s[...]
        y = o * lax.rsqrt(jnp.mean(o * o, axis=-1, keepdims=True) + NORM_EPS) * ng_ref[...]
        y = (y * _silu(load(g_ref, c, lanes).astype(F32))).astype(o_ref.dtype)
        for j in range(STREAMS):
            o_ref[0, rows(c, j), lanes] = y[j * piece:(j + 1) * piece]

    for t in range(n + 2):
        if t in interleave:
            interleave[t]()
        for g in range(REC_HEADS_PER_STEP):
            if t >= 2:
                output(c0 + (t - 2), g)
            if 1 <= t <= n:
                scores(c0 + (t - 1), g)
            if t < n:
                gates(c0 + t, g)


def _qk_rec_kernel(a_ref, w_ref, c_ref, sa_ref, sb_ref, q_ref, f_ref, i_ref, g_ref, lb_ref, ng_ref,
                   o_ref, y_ref, wb_ref, *scratch):
    n_i = pl.num_programs(1)
    i = pl.program_id(1)
    steps_per_seq = q_ref.shape[1] // (REC_CHUNK * REC_CHUNKS_PER_STEP)
    group = (pl.program_id(0) * n_i + i) % steps_per_seq

    @pl.when(i == 0)
    def _():
        _cast_weight_tile(w_ref, wb_ref)

    @pl.when(group == 0)
    def _():
        per_head = len(scratch) // REC_HEADS_PER_STEP
        for g in range(REC_HEADS_PER_STEP):
            scratch[g * per_head][...] = jnp.zeros_like(scratch[g * per_head])

    pairs = o_ref.shape[1] // (2 * HEAD_DIM)
    every = (REC_CHUNKS_PER_STEP + 2) // pairs
    interleave = {p * every: functools.partial(_rope_project, a_ref, wb_ref, c_ref, sa_ref, sb_ref, o_ref, p)
                  for p in range(pairs)}
    _rec_pipeline(q_ref, f_ref, i_ref, g_ref, lb_ref, ng_ref, y_ref, scratch,
                  group * REC_CHUNKS_PER_STEP, REC_CHUNKS_PER_STEP, interleave)


def _qk_project_and_recurrence(u, w, rope, q_r, f_r, i_r, g_r, lower_bound, norm_g, bsz, seq):
    t, d = u.shape
    ncols = 2 * N_HEADS * HEAD_DIM
    width = lower_bound.shape[0]
    gw = REC_HEADS_PER_STEP * HEAD_DIM
    tm, tn = 1024, 512
    n_j, n_i = ncols // tn, t // tm
    n_hg = width // gw
    steps_per_seq = seq // (REC_CHUNK * REC_CHUNKS_PER_STEP)
    assert n_j * n_i == bsz * n_hg * steps_per_seq, "one recurrence slice per projection tile"

    def rec_pos(j, i):
        s = (j * n_i + i) // steps_per_seq
        return s // n_hg, s % n_hg

    def rec_cols(src):
        a, col0 = src
        off = col0 // gw
        return a.reshape(bsz, seq, a.shape[1]), pl.BlockSpec(
            (1, seq, gw), lambda j, i: (rec_pos(j, i)[0], 0, off + rec_pos(j, i)[1]))

    (qa, qs), (fa, fs), (ia, isp), (ga, gs) = (rec_cols(s) for s in (q_r, f_r, i_r, g_r))
    qk, y = pl.pallas_call(
        _qk_rec_kernel,
        out_shape=(jax.ShapeDtypeStruct((t, ncols), BF16), jax.ShapeDtypeStruct((bsz, seq, width), BF16)),
        grid=(n_j, n_i),
        in_specs=[pl.BlockSpec((tm, d), lambda j, i: (i, 0)),
                  pl.BlockSpec((d, tn), lambda j, i: (0, j)),
                  *[pl.BlockSpec((tm, LANES), lambda j, i: (i, 0)) for _ in range(3)],
                  qs, fs, isp, gs,
                  pl.BlockSpec((1, gw), lambda j, i: (0, rec_pos(j, i)[1])),
                  pl.BlockSpec((1, HEAD_DIM), lambda j, i: (0, 0))],
        out_specs=(pl.BlockSpec((tm, tn), lambda j, i: (i, j)),
                   pl.BlockSpec((1, seq, gw), lambda j, i: (rec_pos(j, i)[0], 0, rec_pos(j, i)[1]))),
        scratch_shapes=[pltpu.VMEM((d, tn), BF16)]
        + [s for _ in range(REC_HEADS_PER_STEP) for s in _rec_scratch()],
        compiler_params=_cparams(("arbitrary", "arbitrary")),
        name="qk_proj_hgrn2",
    )(u, w, *rope, qa, fa, ia, ga, lower_bound.reshape(1, width), norm_g.reshape(1, HEAD_DIM))
    return qk, y.reshape(t, width)


def _merge_kernel(ya_ref, yr_ref, wa_ref, wr_ref, ga_ref, gr_ref, o_ref):
    a = jnp.dot(ya_ref[...], wa_ref[...], preferred_element_type=F32)
    r = jnp.dot(yr_ref[...], wr_ref[...], preferred_element_type=F32)
    m = _sigmoid(ga_ref[...].astype(F32)) * a + _sigmoid(gr_ref[...].astype(F32)) * r
    o_ref[...] = m.astype(o_ref.dtype)


def _merge(ya, yr, wa, wr, ga, gr):
    t, d = ya.shape
    n = wa.shape[1]
    tm, tn = 1024, 512
    row = pl.BlockSpec((tm, d), lambda i, j: (i, 0))
    col = pl.BlockSpec((d, tn), lambda i, j: (0, j))
    tile = pl.BlockSpec((tm, tn), lambda i, j: (i, j))
    ga_off, gr_off = ga[1] // tn, gr[1] // tn
    return pl.pallas_call(
        _merge_kernel,
        out_shape=jax.ShapeDtypeStruct((t, n), BF16),
        grid=(t // tm, n // tn),
        in_specs=[row, row, col, col,
                  pl.BlockSpec((tm, tn), lambda i, j: (i, ga_off + j)),
                  pl.BlockSpec((tm, tn), lambda i, j: (i, gr_off + j))],
        out_specs=tile,
        compiler_params=_cparams(("arbitrary", "arbitrary")),
        name="branch_merge",
    )(ya, yr, wa, wr, ga[0], gr[0])


def _mixout_kernel(m_ref, w_ref, x_ref, gt_ref, g_ref, sh_ref, sc_ref, rw_ref, rb_ref,
                   h_ref, u_ref, route_ref):
    mix = jnp.dot(m_ref[...], w_ref[...], preferred_element_type=F32)
    h = x_ref[...] + gt_ref[0] * mix
    h_ref[...] = h
    u = h * lax.rsqrt(jnp.mean(h * h, axis=-1, keepdims=True) + NORM_EPS) * g_ref[...]
    u = u * (1.0 + sc_ref[0]) + sh_ref[0]
    wc = 2 * u_ref.shape[2]
    for c in range(u_ref.shape[0]):
        u_ref[c] = _pack_bf16_pairs(u[:, c * wc:(c + 1) * wc])
    u_hi = u.astype(BF16)
    u_lo = (u - u_hi.astype(F32)).astype(BF16)
    rw = rw_ref[...]
    w_hi = rw.astype(BF16)
    w_lo = (rw - w_hi.astype(F32)).astype(BF16)
    logits = (jnp.dot(u_hi, w_hi, preferred_element_type=F32)
              + jnp.dot(u_lo, w_hi, preferred_element_type=F32)
              + jnp.dot(u_hi, w_lo, preferred_element_type=F32)) + rb_ref[...]
    lane = lax.broadcasted_iota(jnp.int32, logits.shape, 1).astype(F32)
    big = float(LANES)
    neg = -jnp.inf
    lg = jnp.where(lane < N_GROUPS, logits, neg)
    mg = jnp.max(lg, axis=-1, keepdims=True)
    g_sel = jnp.min(jnp.where(lg == mg, lane, big), axis=-1, keepdims=True)
    p_group = 1.0 / jnp.sum(jnp.exp(lg - mg), axis=-1, keepdims=True)
    lo = N_GROUPS + EXPERTS_PER_GROUP * g_sel
    le = jnp.where((lane >= lo) & (lane < lo + EXPERTS_PER_GROUP), logits, neg)
    t1 = jnp.max(le, axis=-1, keepdims=True)
    i1 = jnp.min(jnp.where(le == t1, lane, big), axis=-1, keepdims=True)
    le2 = jnp.where(lane == i1, neg, le)
    t2 = jnp.max(le2, axis=-1, keepdims=True)
    i2 = jnp.min(jnp.where(le2 == t2, lane, big), axis=-1, keepdims=True)
    e21 = jnp.exp(t2 - t1)
    w1 = p_group / (1.0 + e21)
    w2 = p_group * e21 / (1.0 + e21)
    route = jnp.where(lane == 0, i1 - N_GROUPS,
                      jnp.where(lane == 1, i2 - N_GROUPS,
                                jnp.where(lane == 2, w1, jnp.where(lane == 3, w2, 0.0))))
    route_ref[...] = route


def _mix_out(merged, w_out, x2d, gate, g, shift, scale, rw, rb, seq):
    t, d = x2d.shape
    tm = 512
    per_b = seq // tm
    bsz = gate.shape[0]
    row = lambda dt: pl.BlockSpec((tm, d), lambda i: (i, 0))
    per_batch = pl.BlockSpec((1, 1, d), lambda i: (i // per_b, 0, 0))
    const = lambda shape: pl.BlockSpec(shape, lambda i: (0,) * len(shape))
    wc = d // GATHER_CHUNKS // 2
    outs = pl.pallas_call(
        _mixout_kernel,
        out_shape=(jax.ShapeDtypeStruct((t, d), F32),
                   jax.ShapeDtypeStruct((GATHER_CHUNKS, t, wc), jnp.uint32),
                   jax.ShapeDtypeStruct((t, LANES), F32)),
        grid=(t // tm,),
        in_specs=[row(BF16), const((d, d)), row(F32), per_batch, const((1, d)), per_batch, per_batch,
                  const((d, LANES)), const((1, LANES))],
        out_specs=(row(F32), pl.BlockSpec((GATHER_CHUNKS, tm, wc), lambda i: (0, i, 0)),
                   pl.BlockSpec((tm, LANES), lambda i: (i, 0))),
        compiler_params=_cparams(("arbitrary",)),
        name="mix_out_router",
    )(merged, w_out, x2d, gate.reshape(bsz, 1, d), g.reshape(1, d),
      shift.reshape(bsz, 1, d), scale.reshape(bsz, 1, d), rw, rb)
    return outs


def _expert_kernel(be_ref, nx_ref, nu_ref, *refs):
    x_ref = refs[0]
    w_hbm = refs[1:4]
    o_ref = refs[4]
    stage = refs[5:8]
    wb = refs[8:11]
    sem = refs[11]
    nch = x_ref.shape[0]
    wc = 2 * o_ref.shape[2]
    i = pl.program_id(0)
    e = be_ref[i]
    nxt = nx_ref[i]
    active = i < nu_ref[0]
    first = i == 0
    run_start = jnp.logical_or(first, e != be_ref[jnp.maximum(i - 1, 0)])

    def weight_copies(expert):
        return [pltpu.make_async_copy(w_hbm[k].at[expert], stage[k], sem.at[k]) for k in range(3)]

    @pl.when(jnp.logical_and(active, first))
    def _():
        for cp in weight_copies(e):
            cp.start()

    @pl.when(jnp.logical_and(active, run_start))
    def _():
        for cp in weight_copies(e):
            cp.wait()
        for k in range(3):
            _cast_weight_tile(stage[k], wb[k])

        @pl.when(nxt >= 0)
        def _():
            for cp in weight_copies(nxt):
                cp.start()

    @pl.when(active)
    def _():
        x = jnp.concatenate([_unpack_bf16_pairs(x_ref[c]) for c in range(nch)], axis=1).astype(BF16)
        hg = jnp.dot(x, wb[0][...], preferred_element_type=F32)
        hu = jnp.dot(x, wb[1][...], preferred_element_type=F32)
        hdn = (_silu(hg) * hu).astype(BF16)
        y = jnp.dot(hdn, wb[2][...], preferred_element_type=F32)
        for c in range(nch):
            o_ref[c] = _pack_bf16_pairs(y[:, c * wc:(c + 1) * wc])

    @pl.when(jnp.logical_not(active))
    def _():
        o_ref[...] = jnp.zeros_like(o_ref)


def _expert_ffn(xs, w_gate, w_up, w_down, block_expert, next_expert, n_used):
    nch, n_slots, wc = xs.shape
    d = 2 * wc * nch
    hid = w_gate.shape[2]
    bm = MOE_ROWS
    chunk = pl.BlockSpec((nch, bm, wc), lambda i, be, nx, nu: (0, i, 0))
    hbm = pl.BlockSpec(memory_space=pl.ANY)
    grid_spec = pltpu.PrefetchScalarGridSpec(
        num_scalar_prefetch=3,
        grid=(n_slots // bm,),
        in_specs=[chunk, hbm, hbm, hbm],
        out_specs=chunk,
        scratch_shapes=[pltpu.VMEM((d, hid), F32), pltpu.VMEM((d, hid), F32), pltpu.VMEM((hid, d), F32),
                        pltpu.VMEM((d, hid), BF16), pltpu.VMEM((d, hid), BF16), pltpu.VMEM((hid, d), BF16),
                        pltpu.SemaphoreType.DMA((3,))],
    )
    return pl.pallas_call(
        _expert_kernel,
        out_shape=jax.ShapeDtypeStruct((nch, n_slots, wc), jnp.uint32),
        grid_spec=grid_spec,
        compiler_params=_cparams(("arbitrary",)),
        name="expert_ffn",
    )(block_expert, next_expert, n_used, xs, w_gate, w_up, w_down)


def _gather_rows(chunks, idx):
    nch, n, d = chunks.shape
    table = chunks.reshape(nch * n, d)
    idx = (idx[None, :] + (jnp.arange(nch, dtype=jnp.int32) * n)[:, None]).reshape(-1)
    m = idx.shape[0]
    window = LANES
    mesh = plsc.VectorSubcoreMesh(core_axis_name="core", subcore_axis_name="subcore")

    @pl.kernel(out_type=jax.ShapeDtypeStruct((m, d), table.dtype), mesh=mesh, scratch_types=[])
    def gather(x_hbm, i_hbm, o_hbm):
        def body(i_vmem, o_vmem):
            pltpu.sync_copy(x_hbm.at[i_vmem.at[0]], o_vmem)

        pltpu.emit_pipeline(
            body,
            grid=(m // window,),
            in_specs=[pl.BlockSpec((1, window), lambda i: (0, i))],
            out_specs=[pl.BlockSpec((window, d), lambda i: (i, 0))],
            core_axis_name=("core", "subcore"),
            dimension_semantics=(pltpu.PARALLEL,),
        )(i_hbm, o_hbm)

    return gather(table, idx.reshape(1, m)).reshape(nch, m // nch, d)


def _final_kernel(h_ref, *rest):
    y_refs = rest[:TOP_K]
    rt_ref, gt_ref, g_ref, o_ref, slab_ref = rest[TOP_K:]
    sub = h_ref.shape[2]
    nslab = h_ref.shape[3] // LANES
    for j in range(STREAMS):
        route = rt_ref[0, j]
        ffn = sum(route[:, TOP_K + k:TOP_K + k + 1]
                  * jnp.concatenate([_unpack_bf16_pairs(y_refs[k][c, 0, 0, j])
                                     for c in range(GATHER_CHUNKS)], axis=1) for k in range(TOP_K))
        h = h_ref[0, j] + gt_ref[0] * ffn
        y = h * lax.rsqrt(jnp.mean(h * h, axis=-1, keepdims=True) + NORM_EPS) * g_ref[...]
        for c in range(nslab):
            slab_ref[c, pl.ds(j, sub, stride=STREAMS), :] = y[:, c * LANES:(c + 1) * LANES]
    for c in range(nslab):
        o_ref[:, c * LANES:(c + 1) * LANES] = slab_ref[c]


def _final(h, y2, route, gate, g, seq):
    t, d = h.shape
    tm = 512
    per_b = seq // tm
    bsz = gate.shape[0]
    sub = tm // STREAMS
    spec = pl.BlockSpec((1, STREAMS, sub, d), lambda i: (i // per_b, 0, i % per_b, 0))
    wc = d // GATHER_CHUNKS // 2
    y6 = y2.reshape(GATHER_CHUNKS, TOP_K, bsz, STREAMS, seq // STREAMS, wc)

    def yspec(k):
        return pl.BlockSpec((GATHER_CHUNKS, 1, 1, STREAMS, sub, wc),
                            lambda i: (0, k, i // per_b, 0, i % per_b, 0))

    return pl.pallas_call(
        _final_kernel,
        out_shape=jax.ShapeDtypeStruct((t, d), F32),
        grid=(t // tm,),
        in_specs=[spec, *[yspec(k) for k in range(TOP_K)],
                  pl.BlockSpec((1, STREAMS, sub, LANES), lambda i: (i // per_b, 0, i % per_b, 0)),
                  pl.BlockSpec((1, 1, d), lambda i: (i // per_b, 0, 0)),
                  pl.BlockSpec((1, d), lambda i: (0, 0))],
        out_specs=pl.BlockSpec((tm, d), lambda i: (i, 0)),
        scratch_shapes=[pltpu.VMEM((d // LANES, tm, LANES), F32)],
        compiler_params=_cparams(("arbitrary",)),
        name="final_norm",
    )(h.reshape(bsz, STREAMS, seq // STREAMS, d), *([y6] * TOP_K),
      route.reshape(bsz, STREAMS, seq // STREAMS, LANES), gate.reshape(bsz, 1, d), g.reshape(1, d))


def _dispatch_plan(expert_idx):
    n_assign = expert_idx.size
    n_blocks = n_assign // MOE_ROWS + N_EXPERTS
    n_slots = n_blocks * MOE_ROWS
    flat_e = expert_idx.reshape(-1)
    ids = jnp.arange(n_assign, dtype=jnp.int32)
    eids = jnp.arange(N_EXPERTS, dtype=jnp.int32)[None, :]
    _, order = lax.sort((flat_e, ids), num_keys=1, is_stable=True)
    _, rank_sorted = lax.sort((order, ids), num_keys=1)
    hot_a = (flat_e[:, None] == eids).astype(jnp.int32)
    counts = jnp.sum(hot_a, axis=0)
    padded = ((counts + MOE_ROWS - 1) // MOE_ROWS) * MOE_ROWS
    pad_end = jnp.cumsum(padded)
    pad_start = pad_end - padded
    start = jnp.cumsum(counts) - counts
    dest = (rank_sorted + jnp.sum(hot_a * (pad_start - start)[None, :], axis=1)).reshape(-1, TOP_K)
    blk0 = jnp.arange(n_blocks, dtype=jnp.int32) * MOE_ROWS
    block_expert = jnp.minimum(jnp.sum((pad_end[None, :] <= blk0[:, None]).astype(jnp.int32), axis=1),
                               N_EXPERTS - 1)
    hot_b = (block_expert[:, None] == eids).astype(jnp.int32)
    blk_shift = jnp.sum(hot_b * (start - pad_start)[None, :], axis=1)
    blk_count = jnp.sum(hot_b * (pad_start + counts)[None, :], axis=1)
    slot = jnp.arange(n_slots, dtype=jnp.int32).reshape(n_blocks, MOE_ROWS)
    valid = (slot < blk_count[:, None]).reshape(-1)
    src = jnp.clip(slot + blk_shift[:, None], 0, n_assign - 1).reshape(-1)
    assign = order[src]
    token_of_slot = jnp.where(valid, assign // TOP_K, slot.reshape(-1) % (n_assign // TOP_K))
    n_used = (pad_end[-1:] // MOE_ROWS).astype(jnp.int32)
    bi = jnp.arange(n_blocks, dtype=jnp.int32)
    later = ((bi[None, :] > bi[:, None]) & (block_expert[None, :] != block_expert[:, None])
             & (bi[None, :] < n_used[0]))
    next_expert = jnp.where(jnp.any(later, axis=1), block_expert[jnp.argmax(later, axis=1)], -1)
    return token_of_slot, dest, block_expert, next_expert.astype(jnp.int32), n_used


def kernel(x, c, positions, ada_w, ada_b, mix_norm_g, w_in, w_attn_branch, w_rec_branch, w_mix_out,
           rec_norm_g, rec_lb_logits, ffn_norm_g, router_group_w, router_group_b, router_expert_w,
           router_expert_b, expert_w_gate, expert_w_up, expert_w_down, final_norm_g):
    bsz, seq, d = x.shape
    t = bsz * seq
    depth = ada_w.shape[0]
    assert depth == 1, "final norm is fused after the single layer"
    lower_bounds = jnp.cumsum(jax.nn.softmax(rec_lb_logits.astype(F32), axis=0), axis=0)
    pos_streams = positions.reshape(bsz, seq // STREAMS, STREAMS).transpose(0, 2, 1)
    rope = _rope_tables(pos_streams)
    h = x.reshape(t, d)
    for layer in range(depth):
        mod = _modulation(c, ada_w[layer], ada_b[layer])
        sh_m, sc_m, gt_m, sh_f, sc_f, gt_f = jnp.split(mod, 6, axis=-1)
        h, u = _norm_modulate(h, mix_norm_g[layer], sh_m, sc_m, seq)
        w = w_in[layer]
        rest = _project(u, w, 2 * d, 2 * d, BF16)
        f_r = _project(u, w, 4 * d, d, F32)
        tail = _project(u, w, 5 * d, 4 * d, BF16)
        qk, y_rec = _qk_project_and_recurrence(u, w, rope, (rest, d), (f_r, 0), (tail, 0), (tail, d),
                                               lower_bounds[layer], rec_norm_g[layer], bsz, seq)
        y_attn = _attention((qk, 0), (qk, d), (rest, 0), d, bsz, seq)
        merged = _merge(y_attn, y_rec, w_attn_branch[layer].astype(BF16),
                        w_rec_branch[layer].astype(BF16), (tail, 2 * d), (tail, 3 * d))
        rw = jnp.concatenate([router_group_w[layer], router_expert_w[layer],
                              jnp.zeros((d, LANES - N_GROUPS - N_EXPERTS), F32)], axis=1)
        rb = jnp.concatenate([router_group_b[layer], router_expert_b[layer],
                              jnp.zeros((LANES - N_GROUPS - N_EXPERTS,), F32)]).reshape(1, LANES)
        h, u2, route = _mix_out(merged, w_mix_out[layer].astype(BF16), h, gt_m, ffn_norm_g[layer],
                                sh_f, sc_f, rw, rb, seq)
        expert_idx = route[:, :TOP_K].astype(jnp.int32)
        tok, dest, block_expert, next_expert, n_used = _dispatch_plan(expert_idx)
        xs = _gather_rows(u2, tok)
        ys = _expert_ffn(xs, expert_w_gate[layer], expert_w_up[layer], expert_w_down[layer],
                         block_expert, next_expert, n_used)
        dest_kt = dest.T.reshape(-1)
        y2 = _gather_rows(ys, dest_kt)
        h = _final(h, y2, route, gt_f, final_norm_g, seq)
    return h.reshape(bsz, seq, d)
```

```python
import functools

import jax
import jax.numpy as jnp
from jax import lax
from jax.experimental import pallas as pl
from jax.experimental.pallas import tpu as pltpu
from jax.experimental.pallas import tpu_sc as plsc

F32 = jnp.float32
BF16 = jnp.bfloat16

D_MODEL = 2048
HEAD_DIM = 128
N_HEADS = D_MODEL // HEAD_DIM
ROPE_DIM = HEAD_DIM // 4
ROPE_HALF = ROPE_DIM // 2
ROPE_THETA = 500000.0
ATTN_SPAN = 128
ATTN_BLOCK = 128
REC_CHUNK = 64
N_GROUPS = 4
EXPERTS_PER_GROUP = 8
N_EXPERTS = N_GROUPS * EXPERTS_PER_GROUP
EXPERT_HIDDEN = D_MODEL // 2
TOP_K = 2
NORM_EPS = 1e-6
IN_WIDTH = 9 * D_MODEL

LANES = 128
VMEM_LIMIT = 56 * 1024 * 1024

MOE_ROWS = 256
HEADS_PER_STEP = 2
REC_HEADS_PER_STEP = 4
REC_CHUNKS_PER_STEP = 8
GATHER_CHUNKS = 4
STREAMS = 4
DILATIONS = ((128, 1), (512, 4), (2048, 16))


def _cparams(sem):
    return pltpu.CompilerParams(dimension_semantics=sem, vmem_limit_bytes=VMEM_LIMIT)


def _sigmoid(x):
    return 1.0 / (1.0 + jnp.exp(-x))


def _silu(x):
    return x * _sigmoid(x)


def _pack_bf16_pairs(x):
    w = x.shape[1] // 2
    lo = lax.bitcast_convert_type(x[:, :w].astype(BF16).astype(F32), jnp.uint32)
    hi = lax.bitcast_convert_type(x[:, w:].astype(BF16).astype(F32), jnp.uint32)
    return (lo >> 16) | (hi & jnp.uint32(0xFFFF0000))


def _unpack_bf16_pairs(words):
    lo = lax.bitcast_convert_type(words << 16, F32)
    hi = lax.bitcast_convert_type(words & jnp.uint32(0xFFFF0000), F32)
    return jnp.concatenate([lo, hi], axis=1)


def _mod_kernel(c_ref, w_ref, b_ref, o_ref):
    cond = _silu(c_ref[...])
    o_ref[...] = jnp.dot(cond, w_ref[...], precision=lax.Precision.HIGHEST,
                         preferred_element_type=F32) + b_ref[...]


def _modulation(c, w, b):
    bsz, d = c.shape
    n = w.shape[1]
    tn = 1536
    return pl.pallas_call(
        _mod_kernel,
        out_shape=jax.ShapeDtypeStruct((bsz, n), F32),
        grid=(n // tn,),
        in_specs=[pl.BlockSpec((bsz, d), lambda j: (0, 0)),
                  pl.BlockSpec((d, tn), lambda j: (0, j)),
                  pl.BlockSpec((1, tn), lambda j: (0, j))],
        out_specs=pl.BlockSpec((bsz, tn), lambda j: (0, j)),
        compiler_params=_cparams(("arbitrary",)),
        name="adaln_mod",
    )(c, w, b.reshape(1, n))


def _rope_kernel(pos_ref, freq_ref, c_ref, sa_ref, sb_ref):
    ang = pos_ref[...] * freq_ref[...]
    lane = lax.broadcasted_iota(jnp.int32, ang.shape, 1)
    cos, sin = jnp.cos(ang), jnp.sin(ang)
    c_ref[...] = jnp.where(lane < ROPE_DIM, cos, 1.0)
    sa_ref[...] = jnp.where(lane < ROPE_HALF, -sin, 0.0)
    sb_ref[...] = jnp.where((lane >= ROPE_HALF) & (lane < ROPE_DIM), sin, 0.0)


def _rope_tables(positions):
    t = positions.size
    tm = 2048
    inv_freq = ROPE_THETA ** (-jnp.arange(0, ROPE_DIM, 2, dtype=F32) / ROPE_DIM)
    freq = jnp.concatenate([inv_freq, inv_freq, jnp.zeros((LANES - ROPE_DIM,), F32)]).reshape(1, LANES)
    pos = positions.astype(F32).reshape(t, 1)
    out = jax.ShapeDtypeStruct((t, LANES), F32)
    return pl.pallas_call(
        _rope_kernel,
        out_shape=(out, out, out),
        grid=(t // tm,),
        in_specs=[pl.BlockSpec((tm, 1), lambda i: (i, 0)),
                  pl.BlockSpec((1, LANES), lambda i: (0, 0))],
        out_specs=tuple(pl.BlockSpec((tm, LANES), lambda i: (i, 0)) for _ in range(3)),
        compiler_params=_cparams(("arbitrary",)),
        name="rope_tables",
    )(pos, freq)


def _norm_mod_kernel(x_ref, g_ref, sh_ref, sc_ref, xp_ref, u_ref, slab_ref):
    rows = x_ref.shape[0]
    sub = rows // STREAMS
    nslab = x_ref.shape[1] // LANES
    for c in range(nslab):
        slab_ref[c] = x_ref[:, c * LANES:(c + 1) * LANES]
    for j in range(STREAMS):
        x = jnp.concatenate([slab_ref[c, pl.ds(j, sub, stride=STREAMS), :] for c in range(nslab)], axis=1)
        xp_ref[0, j] = x
        y = x * lax.rsqrt(jnp.mean(x * x, axis=-1, keepdims=True) + NORM_EPS) * g_ref[...]
        u_ref[0, j] = (y * (1.0 + sc_ref[0]) + sh_ref[0]).astype(u_ref.dtype)


def _norm_modulate(x2d, g, shift, scale, seq):
    t, d = x2d.shape
    tm = 512
    per_b = seq // tm
    bsz = shift.shape[0]
    sub = tm // STREAMS
    out_spec = pl.BlockSpec((1, STREAMS, sub, d), lambda i: (i // per_b, 0, i % per_b, 0))
    xp, u = pl.pallas_call(
        _norm_mod_kernel,
        out_shape=(jax.ShapeDtypeStruct((bsz, STREAMS, seq // STREAMS, d), F32),
                   jax.ShapeDtypeStruct((bsz, STREAMS, seq // STREAMS, d), BF16)),
        grid=(t // tm,),
        in_specs=[pl.BlockSpec((tm, d), lambda i: (i, 0)),
                  pl.BlockSpec((1, d), lambda i: (0, 0)),
                  pl.BlockSpec((1, 1, d), lambda i: (i // per_b, 0, 0)),
                  pl.BlockSpec((1, 1, d), lambda i: (i // per_b, 0, 0))],
        out_specs=(out_spec, out_spec),
        scratch_shapes=[pltpu.VMEM((d // LANES, tm, LANES), F32)],
        compiler_params=_cparams(("arbitrary",)),
        name="norm_modulate",
    )(x2d, g.reshape(1, d), shift.reshape(bsz, 1, d), scale.reshape(bsz, 1, d))
    return xp.reshape(t, d), u.reshape(t, d)


def _cast_weight_tile(w_ref, wb_ref):
    rows = 256

    def body(r, carry):
        sl = pl.ds(pl.multiple_of(r * rows, rows), rows)
        wb_ref[sl, :] = w_ref[sl, :].astype(BF16)
        return carry

    lax.fori_loop(0, w_ref.shape[0] // rows, body, 0)


def _proj_kernel(a_ref, w_ref, o_ref, wb_ref):
    @pl.when(pl.program_id(1) == 0)
    def _():
        _cast_weight_tile(w_ref, wb_ref)

    o_ref[...] = jnp.dot(a_ref[...], wb_ref[...], preferred_element_type=F32).astype(o_ref.dtype)


def _rope_project(a_ref, wb_ref, c_ref, sa_ref, sb_ref, o_ref, t):
    c, sa, sb = c_ref[...], sa_ref[...], sb_ref[...]
    pair = 2 * HEAD_DIM
    acc = jnp.dot(a_ref[...], wb_ref[:, t * pair:(t + 1) * pair], preferred_element_type=F32)
    for h in range(2):
        x = acc[:, h * HEAD_DIM:(h + 1) * HEAD_DIM]
        up = pltpu.roll(x, HEAD_DIM - ROPE_HALF, 1)
        dn = pltpu.roll(x, ROPE_HALF, 1)
        col = t * pair + h * HEAD_DIM
        o_ref[:, col:col + HEAD_DIM] = (x * c + up * sa + dn * sb).astype(o_ref.dtype)


def _plain_project(a_ref, wb_ref, o_ref, t):
    cols = slice(t * 2 * LANES, (t + 1) * 2 * LANES)
    o_ref[:, cols] = jnp.dot(a_ref[...], wb_ref[:, cols], preferred_element_type=F32).astype(o_ref.dtype)


def _proj_rope_kernel(a_ref, w_ref, c_ref, sa_ref, sb_ref, o_ref, wb_ref):
    @pl.when(pl.program_id(1) == 0)
    def _():
        _cast_weight_tile(w_ref, wb_ref)

    for t in range(o_ref.shape[1] // (2 * HEAD_DIM)):
        _rope_project(a_ref, wb_ref, c_ref, sa_ref, sb_ref, o_ref, t)


def _project(u, w, col0, ncols, rope=None):
    t, d = u.shape
    tm, tn = 1024, 1024
    j0 = col0 // tn
    in_specs = [pl.BlockSpec((tm, d), lambda j, i: (i, 0)),
                pl.BlockSpec((d, tn), lambda j, i: (0, j0 + j))]
    args = [u, w]
    if rope is not None:
        in_specs += [pl.BlockSpec((tm, LANES), lambda j, i: (i, 0)) for _ in range(3)]
        args += list(rope)
    return pl.pallas_call(
        _proj_kernel if rope is None else _proj_rope_kernel,
        out_shape=jax.ShapeDtypeStruct((t, ncols), BF16),
        grid=(ncols // tn, t // tm),
        in_specs=in_specs,
        out_specs=pl.BlockSpec((tm, tn), lambda j, i: (i, j)),
        scratch_shapes=[pltpu.VMEM((d, tn), BF16)],
        compiler_params=_cparams(("arbitrary", "arbitrary")),
        name="in_proj" if rope is None else "in_proj_rope",
    )(*args)


def _key_pieces(qb, sub):
    L = ATTN_BLOCK
    pieces = [(0, 0, L * (qb + 1))]
    for r in range(1, STREAMS):
        back = 0 if qb == 0 else (L // 2 if r == 1 else L // 4)
        pieces.append((r, L * qb - back, L + back))
    return pieces


def _attention_bias(sub):
    L = ATTN_BLOCK
    nqb = sub // L
    kmax = max(sum(p[2] for p in _key_pieces(qb, sub)) for qb in range(nqb))
    qi = jnp.arange(L, dtype=jnp.int32)[:, None]
    out = []
    for j in range(STREAMS):
        row = []
        for qb in range(nqb):
            cols = []
            for r, start, size in _key_pieces(qb, sub):
                jp = (j + r) % STREAMS
                kn = start + jnp.arange(size, dtype=jnp.int32)[None, :]
                dt = STREAMS * (L * qb + qi - kn) + (j - jp)
                cnt = jnp.zeros(dt.shape, F32)
                for window, dil in DILATIONS:
                    cnt += ((dt >= 0) & (dt <= window) & (dt % dil == 0)).astype(F32)
                cols.append(jnp.log2(cnt))
            tile = jnp.concatenate(cols, axis=1)
            row.append(jnp.pad(tile, ((0, 0), (0, kmax - tile.shape[1])), constant_values=-jnp.inf))
        out.append(jnp.stack(row))
    return jnp.stack(out)


def _attn_pipeline(q_ref, k_ref, v_ref, bias_ref, o_ref, bufs, j, interleave):
    L = ATTN_BLOCK
    sub = o_ref.shape[1] // STREAMS
    scale = HEAD_DIM ** -0.5 * 1.4426950408889634
    s_buf, m_buf, p_buf = bufs[0:2], bufs[2:4], bufs[4:6]
    units = [(g, qb) for g in range(HEADS_PER_STEP) for qb in range(sub // L)]
    own = pl.multiple_of(j * sub, sub)

    def key_rows(qb):
        rows = []
        for r, start, size in _key_pieces(qb, sub):
            base = own if r == 0 else pl.multiple_of(((j + r) % STREAMS) * sub, sub)
            rows.append(pl.ds(base + start, size))
        return rows, sum(p[2] for p in _key_pieces(qb, sub))

    def score_stage(u):
        g, qb = units[u]
        lanes = slice(g * HEAD_DIM, (g + 1) * HEAD_DIM)
        rows, ktot = key_rows(qb)
        q = q_ref[0, pl.ds(own + L * qb, L), lanes]
        k_all = jnp.concatenate([k_ref[0, rw, lanes] for rw in rows], axis=0)
        mx = None
        for c in range(0, ktot, 2 * LANES):
            w = min(2 * LANES, ktot - c)
            s = lax.dot_general(q, k_all[c:c + w], (((1,), (1,)), ((), ())), preferred_element_type=F32)
            s = s * scale + bias_ref[j, qb, :, c:c + w]
            s_buf[u % 2][:, c:c + w] = s
            for cc in range(0, w, LANES):
                part = s[:, cc:cc + LANES]
                mx = part if mx is None else jnp.maximum(mx, part)
        m_buf[u % 2][...] = jnp.broadcast_to(jnp.max(mx, axis=-1, keepdims=True), mx.shape)

    def prob_stage(u):
        _, qb = units[u]
        _, ktot = key_rows(qb)
        m = m_buf[u % 2][...]
        for c in range(0, ktot, LANES):
            p_buf[u % 2][:, c:c + LANES] = jnp.exp2(s_buf[u % 2][:, c:c + LANES] - m).astype(BF16)

    def out_stage(u):
        g, qb = units[u]
        lanes = slice(g * HEAD_DIM, (g + 1) * HEAD_DIM)
        rows, ktot = key_rows(qb)
        v_all = jnp.concatenate([v_ref[0, rw, lanes] for rw in rows], axis=0)
        v_ext = jnp.concatenate([v_all, jnp.ones_like(v_all)], axis=1)
        oe = jnp.dot(p_buf[u % 2][:, :ktot], v_ext, preferred_element_type=F32)
        y = oe[:, :HEAD_DIM] / oe[:, HEAD_DIM:]
        o_ref[0, pl.ds(own + L * qb, L), lanes] = y.astype(o_ref.dtype)

    for t in range(len(units) + 2):
        if t in interleave:
            interleave[t]()
        if t >= 2:
            out_stage(t - 2)
        if 1 <= t <= len(units):
            prob_stage(t - 1)
        if t < len(units):
            score_stage(t)


def _spread(n_steps, pieces, fn):
    every = max(n_steps // pieces, 1)
    return {p * every: functools.partial(fn, p) for p in range(pieces)}


def _proj_attn_kernel(a_ref, w_ref, q_ref, k_ref, v_ref, bias_ref, o_ref, y_ref, wb_ref, *bufs):
    i = pl.program_id(1)
    stream = (pl.program_id(0) * pl.num_programs(1) + i) % STREAMS

    @pl.when(i == 0)
    def _():
        _cast_weight_tile(w_ref, wb_ref)

    n_units = HEADS_PER_STEP * (y_ref.shape[1] // STREAMS // ATTN_BLOCK)
    pieces = o_ref.shape[1] // (2 * LANES)
    interleave = _spread(n_units + 2, pieces, functools.partial(_plain_project, a_ref, wb_ref, o_ref))
    _attn_pipeline(q_ref, k_ref, v_ref, bias_ref, y_ref, bufs, stream, interleave)


def _project_and_attention(u, w, col0, ncols, q, k, v, width, bsz, seq):
    t, d = u.shape
    gw = HEADS_PER_STEP * HEAD_DIM
    sub = seq // STREAMS
    tm, tn = 1024, 512
    n_j, n_i = ncols // tn, t // tm
    n_hg = width // gw
    assert n_j * n_i == bsz * n_hg * STREAMS, "one attention stream per projection tile"
    j0 = col0 // tn
    bias = _attention_bias(sub)

    def attn_pos(j, i):
        s = (j * n_i + i) // STREAMS
        return s // n_hg, s % n_hg

    def attn_cols(src):
        a, c0 = src
        off = c0 // gw
        return a.reshape(bsz, seq, a.shape[1]), pl.BlockSpec(
            (1, seq, gw), lambda j, i: (attn_pos(j, i)[0], 0, off + attn_pos(j, i)[1]))

    (qa, qs), (ka, ks), (va, vs) = (attn_cols(s) for s in (q, k, v))
    proj, y = pl.pallas_call(
        _proj_attn_kernel,
        out_shape=(jax.ShapeDtypeStruct((t, ncols), BF16), jax.ShapeDtypeStruct((bsz, seq, width), BF16)),
        grid=(n_j, n_i),
        in_specs=[pl.BlockSpec((tm, d), lambda j, i: (i, 0)),
                  pl.BlockSpec((d, tn), lambda j, i: (0, j0 + j)),
                  qs, ks, vs,
                  pl.BlockSpec(bias.shape, lambda j, i: (0, 0, 0, 0))],
        out_specs=(pl.BlockSpec((tm, tn), lambda j, i: (i, j)),
                   pl.BlockSpec((1, seq, gw), lambda j, i: (attn_pos(j, i)[0], 0, attn_pos(j, i)[1]))),
        scratch_shapes=[pltpu.VMEM((d, tn), BF16)]
        + [pltpu.VMEM((ATTN_BLOCK, bias.shape[-1]), F32)] * 2
        + [pltpu.VMEM((ATTN_BLOCK, LANES), F32)] * 2
        + [pltpu.VMEM((ATTN_BLOCK, bias.shape[-1]), BF16)] * 2,
        compiler_params=_cparams(("arbitrary", "arbitrary")),
        name="proj_attention",
    )(u, w, qa, ka, va, bias)
    return proj, y.reshape(t, width)


def _rec_scratch():
    C, K = REC_CHUNK, HEAD_DIM
    return [pltpu.VMEM((K, K), F32), pltpu.VMEM((C, K), F32), pltpu.VMEM((C, K), F32),
            pltpu.VMEM((C, C), BF16), pltpu.VMEM((C, K), BF16), pltpu.VMEM((K, K), F32),
            pltpu.VMEM((1, K), F32)]


def _rec_pipeline(q_ref, f_ref, i_ref, g_ref, lb_ref, ng_ref, o_ref, scratch, c0, n, interleave):
    C = REC_CHUNK
    seq = q_ref.shape[1]
    sub = seq // STREAMS
    piece = C // STREAMS
    pi = lax.broadcasted_iota(jnp.int32, (C, C), 0)
    si = lax.broadcasted_iota(jnp.int32, (C, C), 1)
    time_of = lambda p: STREAMS * (p % piece) + p // piece
    causal = time_of(si) <= time_of(pi)
    tri = jnp.where(causal, 1.0, 0.0).astype(BF16)
    row_of = lambda tau: (tau % STREAMS) * piece + tau // STREAMS
    last = row_of(C - 1)
    anchor = row_of(C // 2 - 1)
    per_head = len(scratch) // REC_HEADS_PER_STEP

    def head(g):
        return slice(g * HEAD_DIM, (g + 1) * HEAD_DIM), scratch[g * per_head:(g + 1) * per_head]

    def rows(c, j):
        start = j * sub + c * piece
        return pl.ds(start if isinstance(start, int) else pl.multiple_of(start, piece), piece)

    def load(ref, c, lanes):
        return jnp.concatenate([ref[0, rows(c, j), lanes] for j in range(STREAMS)], axis=0)

    def gates(c, g):
        lanes, (_, b_s, kk_s, _, _, _, _) = head(g)
        lb = lb_ref[:, lanes]
        x = load(f_ref, c, lanes).astype(F32)
        z = jnp.exp(-jnp.abs(x))
        r = 1.0 / (1.0 + z)
        pos = x >= 0
        sig_p = jnp.where(pos, r, z * r)
        sig_n = jnp.where(pos, z * r, r)
        logf = jnp.log(lb + (1.0 - lb) * sig_p)
        hi = logf.astype(BF16)
        lo = (logf - hi.astype(F32)).astype(BF16)
        b_s[...] = (jnp.dot(tri, hi, preferred_element_type=F32)
                    + jnp.dot(tri, lo, preferred_element_type=F32))
        kk_s[...] = (1.0 - lb) * sig_n

    def scores(c, g):
        lanes, (_, b_s, kk_s, sc_s, qin_s, upd_s, dec_s) = head(g)
        b, kk = b_s[...], kk_s[...]
        b_last = b[last:last + 1, :]
        b_mid = b[anchor:anchor + 1, :]
        qs = _silu(load(q_ref, c, lanes).astype(F32))
        v = load(i_ref, c, lanes)
        q_a = (qs * jnp.exp(b - b_mid)).astype(BF16)
        k_a = (kk * jnp.exp(b_mid - b)).astype(BF16)
        k_e = (kk * jnp.exp(b_last - b)).astype(BF16)
        sc = lax.dot_general(q_a, k_a, (((1,), (1,)), ((), ())), preferred_element_type=F32)
        sc_s[...] = jnp.where(causal, sc, 0.0).astype(BF16)
        qin_s[...] = (qs * jnp.exp(b)).astype(BF16)
        upd_s[...] = lax.dot_general(v, k_e, (((0,), (0,)), ((), ())), preferred_element_type=F32)
        dec_s[...] = jnp.exp(b_last)

    def output(c, g):
        lanes, (st_ref, _, _, sc_s, qin_s, upd_s, dec_s) = head(g)
        v = load(i_ref, c, lanes)
        st = st_ref[...]
        o = (jnp.dot(sc_s[...], v, preferred_element_type=F32)
             + lax.dot_general(qin_s[...], st.astype(BF16), (((1,), (1,)), ((), ())),
                               preferred_element_type=F32))
        st_ref[...] = dec_s[...] * st + upd_s[...]
        y = o * lax.rsqrt(jnp.mean(o * o, axis=-1, keepdims=True) + NORM_EPS) * ng_ref[...]
        y = (y * _silu(load(g_ref, c, lanes).astype(F32))).astype(o_ref.dtype)
        for j in range(STREAMS):
            o_ref[0, rows(c, j), lanes] = y[j * piece:(j + 1) * piece]

    for t in range(n + 2):
        if t in interleave:
            interleave[t]()
        for g in range(REC_HEADS_PER_STEP):
            if t >= 2:
                output(c0 + (t - 2), g)
            if 1 <= t <= n:
                scores(c0 + (t - 1), g)
            if t < n:
                gates(c0 + t, g)


def _proj_rec_kernel(a_ref, w_ref, q_ref, f_ref, i_ref, g_ref, lb_ref, ng_ref, o_ref, y_ref, wb_ref, *scratch):
    n_i = pl.num_programs(1)
    i = pl.program_id(1)
    steps_per_seq = q_ref.shape[1] // (REC_CHUNK * REC_CHUNKS_PER_STEP)
    group = (pl.program_id(0) * n_i + i) % steps_per_seq

    @pl.when(i == 0)
    def _():
        _cast_weight_tile(w_ref, wb_ref)

    @pl.when(group == 0)
    def _():
        per_head = len(scratch) // REC_HEADS_PER_STEP
        for g in range(REC_HEADS_PER_STEP):
            scratch[g * per_head][...] = jnp.zeros_like(scratch[g * per_head])

    interleave = _spread(REC_CHUNKS_PER_STEP + 2, o_ref.shape[1] // (2 * LANES),
                         functools.partial(_plain_project, a_ref, wb_ref, o_ref))
    _rec_pipeline(q_ref, f_ref, i_ref, g_ref, lb_ref, ng_ref, y_ref, scratch,
                  group * REC_CHUNKS_PER_STEP, REC_CHUNKS_PER_STEP, interleave)


def _project_and_recurrence(u, w, col0, ncols, q_r, f_r, i_r, g_r, lower_bound, norm_g, bsz, seq):
    t, d = u.shape
    width = lower_bound.shape[0]
    gw = REC_HEADS_PER_STEP * HEAD_DIM
    tm, tn = 1024, 512
    j0 = col0 // tn
    n_j, n_i = ncols // tn, t // tm
    n_hg = width // gw
    steps_per_seq = seq // (REC_CHUNK * REC_CHUNKS_PER_STEP)
    assert n_j * n_i == bsz * n_hg * steps_per_seq, "one recurrence slice per projection tile"

    def rec_pos(j, i):
        s = (j * n_i + i) // steps_per_seq
        return s // n_hg, s % n_hg

    def rec_cols(src):
        a, col0 = src
        off = col0 // gw
        return a.reshape(bsz, seq, a.shape[1]), pl.BlockSpec(
            (1, seq, gw), lambda j, i: (rec_pos(j, i)[0], 0, off + rec_pos(j, i)[1]))

    (qa, qs), (fa, fs), (ia, isp), (ga, gs) = (rec_cols(s) for s in (q_r, f_r, i_r, g_r))
    proj, y = pl.pallas_call(
        _proj_rec_kernel,
        out_shape=(jax.ShapeDtypeStruct((t, ncols), BF16), jax.ShapeDtypeStruct((bsz, seq, width), BF16)),
        grid=(n_j, n_i),
        in_specs=[pl.BlockSpec((tm, d), lambda j, i: (i, 0)),
                  pl.BlockSpec((d, tn), lambda j, i: (0, j0 + j)),
                  qs, fs, isp, gs,
                  pl.BlockSpec((1, gw), lambda j, i: (0, rec_pos(j, i)[1])),
                  pl.BlockSpec((1, HEAD_DIM), lambda j, i: (0, 0))],
        out_specs=(pl.BlockSpec((tm, tn), lambda j, i: (i, j)),
                   pl.BlockSpec((1, seq, gw), lambda j, i: (rec_pos(j, i)[0], 0, rec_pos(j, i)[1]))),
        scratch_shapes=[pltpu.VMEM((d, tn), BF16)]
        + [s for _ in range(REC_HEADS_PER_STEP) for s in _rec_scratch()],
        compiler_params=_cparams(("arbitrary", "arbitrary")),
        name="proj_hgrn2",
    )(u, w, qa, fa, ia, ga, lower_bound.reshape(1, width), norm_g.reshape(1, HEAD_DIM))
    return proj, y.reshape(t, width)


def _merge_kernel(ya_ref, yr_ref, wa_ref, wr_ref, ga_ref, gr_ref, o_ref):
    a = jnp.dot(ya_ref[...], wa_ref[...], preferred_element_type=F32)
    r = jnp.dot(yr_ref[...], wr_ref[...], preferred_element_type=F32)
    m = _sigmoid(ga_ref[...].astype(F32)) * a + _sigmoid(gr_ref[...].astype(F32)) * r
    o_ref[...] = m.astype(o_ref.dtype)


def _merge(ya, yr, wa, wr, ga, gr):
    t, d = ya.shape
    n = wa.shape[1]
    tm, tn = 1024, 512
    row = pl.BlockSpec((tm, d), lambda i, j: (i, 0))
    col = pl.BlockSpec((d, tn), lambda i, j: (0, j))
    tile = pl.BlockSpec((tm, tn), lambda i, j: (i, j))
    ga_off, gr_off = ga[1] // tn, gr[1] // tn
    return pl.pallas_call(
        _merge_kernel,
        out_shape=jax.ShapeDtypeStruct((t, n), BF16),
        grid=(t // tm, n // tn),
        in_specs=[row, row, col, col,
                  pl.BlockSpec((tm, tn), lambda i, j: (i, ga_off + j)),
                  pl.BlockSpec((tm, tn), lambda i, j: (i, gr_off + j))],
        out_specs=tile,
        compiler_params=_cparams(("arbitrary", "arbitrary")),
        name="branch_merge",
    )(ya, yr, wa, wr, ga[0], gr[0])


def _mixout_kernel(m_ref, w_ref, x_ref, gt_ref, g_ref, sh_ref, sc_ref, rw_ref, rb_ref,
                   h_ref, u_ref, route_ref):
    mix = jnp.dot(m_ref[...], w_ref[...], preferred_element_type=F32)
    h = x_ref[...] + gt_ref[0] * mix
    h_ref[...] = h
    u = h * lax.rsqrt(jnp.mean(h * h, axis=-1, keepdims=True) + NORM_EPS) * g_ref[...]
    u = u * (1.0 + sc_ref[0]) + sh_ref[0]
    wc = 2 * u_ref.shape[2]
    for c in range(u_ref.shape[0]):
        u_ref[c] = _pack_bf16_pairs(u[:, c * wc:(c + 1) * wc])
    u_hi = u.astype(BF16)
    u_lo = (u - u_hi.astype(F32)).astype(BF16)
    rw = rw_ref[...]
    w_hi = rw.astype(BF16)
    w_lo = (rw - w_hi.astype(F32)).astype(BF16)
    logits = (jnp.dot(u_hi, w_hi, preferred_element_type=F32)
              + jnp.dot(u_lo, w_hi, preferred_element_type=F32)
              + jnp.dot(u_hi, w_lo, preferred_element_type=F32)) + rb_ref[...]
    lane = lax.broadcasted_iota(jnp.int32, logits.shape, 1).astype(F32)
    big = float(LANES)
    neg = -jnp.inf
    lg = jnp.where(lane < N_GROUPS, logits, neg)
    mg = jnp.max(lg, axis=-1, keepdims=True)
    g_sel = jnp.min(jnp.where(lg == mg, lane, big), axis=-1, keepdims=True)
    p_group = 1.0 / jnp.sum(jnp.exp(lg - mg), axis=-1, keepdims=True)
    lo = N_GROUPS + EXPERTS_PER_GROUP * g_sel
    le = jnp.where((lane >= lo) & (lane < lo + EXPERTS_PER_GROUP), logits, neg)
    t1 = jnp.max(le, axis=-1, keepdims=True)
    i1 = jnp.min(jnp.where(le == t1, lane, big), axis=-1, keepdims=True)
    le2 = jnp.where(lane == i1, neg, le)
    t2 = jnp.max(le2, axis=-1, keepdims=True)
    i2 = jnp.min(jnp.where(le2 == t2, lane, big), axis=-1, keepdims=True)
    e21 = jnp.exp(t2 - t1)
    w1 = p_group / (1.0 + e21)
    w2 = p_group * e21 / (1.0 + e21)
    route = jnp.where(lane == 0, i1 - N_GROUPS,
                      jnp.where(lane == 1, i2 - N_GROUPS,
                                jnp.where(lane == 2, w1, jnp.where(lane == 3, w2, 0.0))))
    route_ref[...] = route


def _mix_out(merged, w_out, x2d, gate, g, shift, scale, rw, rb, seq):
    t, d = x2d.shape
    tm = 512
    per_b = seq // tm
    bsz = gate.shape[0]
    row = lambda dt: pl.BlockSpec((tm, d), lambda i: (i, 0))
    per_batch = pl.BlockSpec((1, 1, d), lambda i: (i // per_b, 0, 0))
    const = lambda shape: pl.BlockSpec(shape, lambda i: (0,) * len(shape))
    wc = d // GATHER_CHUNKS // 2
    outs = pl.pallas_call(
        _mixout_kernel,
        out_shape=(jax.ShapeDtypeStruct((t, d), F32),
                   jax.ShapeDtypeStruct((GATHER_CHUNKS, t, wc), jnp.uint32),
                   jax.ShapeDtypeStruct((t, LANES), F32)),
        grid=(t // tm,),
        in_specs=[row(BF16), const((d, d)), row(F32), per_batch, const((1, d)), per_batch, per_batch,
                  const((d, LANES)), const((1, LANES))],
        out_specs=(row(F32), pl.BlockSpec((GATHER_CHUNKS, tm, wc), lambda i: (0, i, 0)),
                   pl.BlockSpec((tm, LANES), lambda i: (i, 0))),
        compiler_params=_cparams(("arbitrary",)),
        name="mix_out_router",
    )(merged, w_out, x2d, gate.reshape(bsz, 1, d), g.reshape(1, d),
      shift.reshape(bsz, 1, d), scale.reshape(bsz, 1, d), rw, rb)
    return outs


def _expert_kernel(be_ref, nx_ref, nu_ref, *refs):
    x_ref = refs[0]
    w_hbm = refs[1:4]
    o_ref = refs[4]
    stage = refs[5:8]
    wb = refs[8:11]
    sem = refs[11]
    nch = x_ref.shape[0]
    wc = 2 * o_ref.shape[2]
    i = pl.program_id(0)
    e = be_ref[i]
    nxt = nx_ref[i]
    active = i < nu_ref[0]
    first = i == 0
    run_start = jnp.logical_or(first, e != be_ref[jnp.maximum(i - 1, 0)])

    def weight_copies(expert):
        return [pltpu.make_async_copy(w_hbm[k].at[expert], stage[k], sem.at[k]) for k in range(3)]

    @pl.when(jnp.logical_and(active, first))
    def _():
        for cp in weight_copies(e):
            cp.start()

    @pl.when(jnp.logical_and(active, run_start))
    def _():
        for cp in weight_copies(e):
            cp.wait()
        for k in range(3):
            _cast_weight_tile(stage[k], wb[k])

        @pl.when(nxt >= 0)
        def _():
            for cp in weight_copies(nxt):
                cp.start()

    @pl.when(active)
    def _():
        x = jnp.concatenate([_unpack_bf16_pairs(x_ref[c]) for c in range(nch)], axis=1).astype(BF16)
        hg = jnp.dot(x, wb[0][...], preferred_element_type=F32)
        hu = jnp.dot(x, wb[1][...], preferred_element_type=F32)
        hdn = (_silu(hg) * hu).astype(BF16)
        y = jnp.dot(hdn, wb[2][...], preferred_element_type=F32)
        for c in range(nch):
            o_ref[c] = _pack_bf16_pairs(y[:, c * wc:(c + 1) * wc])

    @pl.when(jnp.logical_not(active))
    def _():
        o_ref[...] = jnp.zeros_like(o_ref)


def _expert_ffn(xs, w_gate, w_up, w_down, block_expert, next_expert, n_used):
    nch, n_slots, wc = xs.shape
    d = 2 * wc * nch
    hid = w_gate.shape[2]
    bm = MOE_ROWS
    chunk = pl.BlockSpec((nch, bm, wc), lambda i, be, nx, nu: (0, i, 0))
    hbm = pl.BlockSpec(memory_space=pl.ANY)
    grid_spec = pltpu.PrefetchScalarGridSpec(
        num_scalar_prefetch=3,
        grid=(n_slots // bm,),
        in_specs=[chunk, hbm, hbm, hbm],
        out_specs=chunk,
        scratch_shapes=[pltpu.VMEM((d, hid), F32), pltpu.VMEM((d, hid), F32), pltpu.VMEM((hid, d), F32),
                        pltpu.VMEM((d, hid), BF16), pltpu.VMEM((d, hid), BF16), pltpu.VMEM((hid, d), BF16),
                        pltpu.SemaphoreType.DMA((3,))],
    )
    return pl.pallas_call(
        _expert_kernel,
        out_shape=jax.ShapeDtypeStruct((nch, n_slots, wc), jnp.uint32),
        grid_spec=grid_spec,
        compiler_params=_cparams(("arbitrary",)),
        name="expert_ffn",
    )(block_expert, next_expert, n_used, xs, w_gate, w_up, w_down)


def _gather_rows(chunks, idx):
    nch, n, d = chunks.shape
    table = chunks.reshape(nch * n, d)
    idx = (idx[None, :] + (jnp.arange(nch, dtype=jnp.int32) * n)[:, None]).reshape(-1)
    m = idx.shape[0]
    window = LANES
    mesh = plsc.VectorSubcoreMesh(core_axis_name="core", subcore_axis_name="subcore")

    @pl.kernel(out_type=jax.ShapeDtypeStruct((m, d), table.dtype), mesh=mesh, scratch_types=[])
    def gather(x_hbm, i_hbm, o_hbm):
        def body(i_vmem, o_vmem):
            pltpu.sync_copy(x_hbm.at[i_vmem.at[0]], o_vmem)

        pltpu.emit_pipeline(
            body,
            grid=(m // window,),
            in_specs=[pl.BlockSpec((1, window), lambda i: (0, i))],
            out_specs=[pl.BlockSpec((window, d), lambda i: (i, 0))],
            core_axis_name=("core", "subcore"),
            dimension_semantics=(pltpu.PARALLEL,),
        )(i_hbm, o_hbm)

    return gather(table, idx.reshape(1, m)).reshape(nch, m // nch, d)


def _final_kernel(h_ref, *rest):
    y_refs = rest[:TOP_K]
    rt_ref, gt_ref, g_ref, o_ref, slab_ref = rest[TOP_K:]
    sub = h_ref.shape[2]
    nslab = h_ref.shape[3] // LANES
    for j in range(STREAMS):
        route = rt_ref[0, j]
        ffn = sum(route[:, TOP_K + k:TOP_K + k + 1]
                  * jnp.concatenate([_unpack_bf16_pairs(y_refs[k][c, 0, 0, j])
                                     for c in range(GATHER_CHUNKS)], axis=1) for k in range(TOP_K))
        h = h_ref[0, j] + gt_ref[0] * ffn
        y = h * lax.rsqrt(jnp.mean(h * h, axis=-1, keepdims=True) + NORM_EPS) * g_ref[...]
        for c in range(nslab):
            slab_ref[c, pl.ds(j, sub, stride=STREAMS), :] = y[:, c * LANES:(c + 1) * LANES]
    for c in range(nslab):
        o_ref[:, c * LANES:(c + 1) * LANES] = slab_ref[c]


def _final(h, y2, route, gate, g, seq):
    t, d = h.shape
    tm = 512
    per_b = seq // tm
    bsz = gate.shape[0]
    sub = tm // STREAMS
    spec = pl.BlockSpec((1, STREAMS, sub, d), lambda i: (i // per_b, 0, i % per_b, 0))
    wc = d // GATHER_CHUNKS // 2
    y6 = y2.reshape(GATHER_CHUNKS, TOP_K, bsz, STREAMS, seq // STREAMS, wc)

    def yspec(k):
        return pl.BlockSpec((GATHER_CHUNKS, 1, 1, STREAMS, sub, wc),
                            lambda i: (0, k, i // per_b, 0, i % per_b, 0))

    return pl.pallas_call(
        _final_kernel,
        out_shape=jax.ShapeDtypeStruct((t, d), F32),
        grid=(t // tm,),
        in_specs=[spec, *[yspec(k) for k in range(TOP_K)],
                  pl.BlockSpec((1, STREAMS, sub, LANES), lambda i: (i // per_b, 0, i % per_b, 0)),
                  pl.BlockSpec((1, 1, d), lambda i: (i // per_b, 0, 0)),
                  pl.BlockSpec((1, d), lambda i: (0, 0))],
        out_specs=pl.BlockSpec((tm, d), lambda i: (i, 0)),
        scratch_shapes=[pltpu.VMEM((d // LANES, tm, LANES), F32)],
        compiler_params=_cparams(("arbitrary",)),
        name="final_norm",
    )(h.reshape(bsz, STREAMS, seq // STREAMS, d), *([y6] * TOP_K),
      route.reshape(bsz, STREAMS, seq // STREAMS, LANES), gate.reshape(bsz, 1, d), g.reshape(1, d))


def _dispatch_plan(expert_idx):
    n_assign = expert_idx.size
    n_blocks = n_assign // MOE_ROWS + N_EXPERTS
    n_slots = n_blocks * MOE_ROWS
    flat_e = expert_idx.reshape(-1)
    ids = jnp.arange(n_assign, dtype=jnp.int32)
    eids = jnp.arange(N_EXPERTS, dtype=jnp.int32)[None, :]
    _, order = lax.sort((flat_e, ids), num_keys=1, is_stable=True)
    _, rank_sorted = lax.sort((order, ids), num_keys=1)
    hot_a = (flat_e[:, None] == eids).astype(jnp.int32)
    counts = jnp.sum(hot_a, axis=0)
    padded = ((counts + MOE_ROWS - 1) // MOE_ROWS) * MOE_ROWS
    pad_end = jnp.cumsum(padded)
    pad_start = pad_end - padded
    start = jnp.cumsum(counts) - counts
    dest = (rank_sorted + jnp.sum(hot_a * (pad_start - start)[None, :], axis=1)).reshape(-1, TOP_K)
    blk0 = jnp.arange(n_blocks, dtype=jnp.int32) * MOE_ROWS
    block_expert = jnp.minimum(jnp.sum((pad_end[None, :] <= blk0[:, None]).astype(jnp.int32), axis=1),
                               N_EXPERTS - 1)
    hot_b = (block_expert[:, None] == eids).astype(jnp.int32)
    blk_shift = jnp.sum(hot_b * (start - pad_start)[None, :], axis=1)
    blk_count = jnp.sum(hot_b * (pad_start + counts)[None, :], axis=1)
    slot = jnp.arange(n_slots, dtype=jnp.int32).reshape(n_blocks, MOE_ROWS)
    valid = (slot < blk_count[:, None]).reshape(-1)
    src = jnp.clip(slot + blk_shift[:, None], 0, n_assign - 1).reshape(-1)
    assign = order[src]
    token_of_slot = jnp.where(valid, assign // TOP_K, slot.reshape(-1) % (n_assign // TOP_K))
    n_used = (pad_end[-1:] // MOE_ROWS).astype(jnp.int32)
    bi = jnp.arange(n_blocks, dtype=jnp.int32)
    later = ((bi[None, :] > bi[:, None]) & (block_expert[None, :] != block_expert[:, None])
             & (bi[None, :] < n_used[0]))
    next_expert = jnp.where(jnp.any(later, axis=1), block_expert[jnp.argmax(later, axis=1)], -1)
    return token_of_slot, dest, block_expert, next_expert.astype(jnp.int32), n_used


def kernel(x, c, positions, ada_w, ada_b, mix_norm_g, w_in, w_attn_branch, w_rec_branch, w_mix_out,
           rec_norm_g, rec_lb_logits, ffn_norm_g, router_group_w, router_group_b, router_expert_w,
           router_expert_b, expert_w_gate, expert_w_up, expert_w_down, final_norm_g):
    bsz, seq, d = x.shape
    t = bsz * seq
    depth = ada_w.shape[0]
    assert depth == 1, "final norm is fused after the single layer"
    lower_bounds = jnp.cumsum(jax.nn.softmax(rec_lb_logits.astype(F32), axis=0), axis=0)
    pos_streams = positions.reshape(bsz, seq // STREAMS, STREAMS).transpose(0, 2, 1)
    rope = _rope_tables(pos_streams)
    h = x.reshape(t, d)
    for layer in range(depth):
        mod = _modulation(c, ada_w[layer], ada_b[layer])
        sh_m, sc_m, gt_m, sh_f, sc_f, gt_f = jnp.split(mod, 6, axis=-1)
        h, u = _norm_modulate(h, mix_norm_g[layer], sh_m, sc_m, seq)
        w = w_in[layer]
        qk = _project(u, w, 0, 2 * d, rope=rope)
        v_a = _project(u, w, 2 * d, d)
        rec_in, y_attn = _project_and_attention(u, w, 3 * d, 4 * d, (qk, 0), (qk, d), (v_a, 0),
                                                d, bsz, seq)
        gates, y_rec = _project_and_recurrence(u, w, 7 * d, 2 * d, (rec_in, 0), (rec_in, d), (rec_in, 2 * d),
                                               (rec_in, 3 * d), lower_bounds[layer], rec_norm_g[layer],
                                               bsz, seq)
        merged = _merge(y_attn, y_rec, w_attn_branch[layer].astype(BF16),
                        w_rec_branch[layer].astype(BF16), (gates, 0), (gates, d))
        rw = jnp.concatenate([router_group_w[layer], router_expert_w[layer],
                              jnp.zeros((d, LANES - N_GROUPS - N_EXPERTS), F32)], axis=1)
        rb = jnp.concatenate([router_group_b[layer], router_expert_b[layer],
                              jnp.zeros((LANES - N_GROUPS - N_EXPERTS,), F32)]).reshape(1, LANES)
        h, u2, route = _mix_out(merged, w_mix_out[layer].astype(BF16), h, gt_m, ffn_norm_g[layer],
                                sh_f, sc_f, rw, rb, seq)
        expert_idx = route[:, :TOP_K].astype(jnp.int32)
        tok, dest, block_expert, next_expert, n_used = _dispatch_plan(expert_idx)
        xs = _gather_rows(u2, tok)
        ys = _expert_ffn(xs, expert_w_gate[layer], expert_w_up[layer], expert_w_down[layer],
                         block_expert, next_expert, n_used)
        dest_kt = dest.T.reshape(-1)
        y2 = _gather_rows(ys, dest_kt)
        h = _final(h, y2, route, gt_f, final_norm_g, seq)
    return h.reshape(bsz, seq, d)
```

```python
import functools

import jax
import jax.numpy as jnp
from jax import lax
from jax.experimental import pallas as pl
from jax.experimental.pallas import tpu as pltpu
from jax.experimental.pallas import tpu_sc as plsc

F32 = jnp.float32
BF16 = jnp.bfloat16

D_MODEL = 2048
HEAD_DIM = 128
N_HEADS = D_MODEL // HEAD_DIM
ROPE_DIM = HEAD_DIM // 4
ROPE_HALF = ROPE_DIM // 2
ROPE_THETA = 500000.0
ATTN_SPAN = 128
ATTN_BLOCK = 128
REC_CHUNK = 64
N_GROUPS = 4
EXPERTS_PER_GROUP = 8
N_EXPERTS = N_GROUPS * EXPERTS_PER_GROUP
EXPERT_HIDDEN = D_MODEL // 2
TOP_K = 2
NORM_EPS = 1e-6
IN_WIDTH = 9 * D_MODEL

LANES = 128
VMEM_LIMIT = 56 * 1024 * 1024

MOE_ROWS = 256
HEADS_PER_STEP = 2
ATTN_STREAMS_PER_STEP = 2
REC_HEADS_PER_STEP = 4
REC_CHUNKS_PER_STEP = 16
GATHER_CHUNKS = 4
STREAMS = 4
DILATIONS = ((128, 1), (512, 4), (2048, 16))


def _cparams(sem):
    return pltpu.CompilerParams(dimension_semantics=sem, vmem_limit_bytes=VMEM_LIMIT)


def _sigmoid(x):
    return 1.0 / (1.0 + jnp.exp(-x))


def _silu(x):
    return x * _sigmoid(x)


def _pack_bf16_pairs(x):
    w = x.shape[1] // 2
    lo = lax.bitcast_convert_type(x[:, :w].astype(BF16).astype(F32), jnp.uint32)
    hi = lax.bitcast_convert_type(x[:, w:].astype(BF16).astype(F32), jnp.uint32)
    return (lo >> 16) | (hi & jnp.uint32(0xFFFF0000))


def _unpack_bf16_pairs(words):
    lo = lax.bitcast_convert_type(words << 16, F32)
    hi = lax.bitcast_convert_type(words & jnp.uint32(0xFFFF0000), F32)
    return jnp.concatenate([lo, hi], axis=1)


def _mod_kernel(c_ref, w_ref, b_ref, o_ref):
    cond = _silu(c_ref[...])
    o_ref[...] = jnp.dot(cond, w_ref[...], precision=lax.Precision.HIGHEST,
                         preferred_element_type=F32) + b_ref[...]


def _modulation(c, w, b):
    bsz, d = c.shape
    n = w.shape[1]
    tn = 1536
    return pl.pallas_call(
        _mod_kernel,
        out_shape=jax.ShapeDtypeStruct((bsz, n), F32),
        grid=(n // tn,),
        in_specs=[pl.BlockSpec((bsz, d), lambda j: (0, 0)),
                  pl.BlockSpec((d, tn), lambda j: (0, j)),
                  pl.BlockSpec((1, tn), lambda j: (0, j))],
        out_specs=pl.BlockSpec((bsz, tn), lambda j: (0, j)),
        compiler_params=_cparams(("arbitrary",)),
        name="adaln_mod",
    )(c, w, b.reshape(1, n))


def _rope_kernel(pos_ref, freq_ref, c_ref, sa_ref, sb_ref):
    ang = pos_ref[...] * freq_ref[...]
    lane = lax.broadcasted_iota(jnp.int32, ang.shape, 1)
    cos, sin = jnp.cos(ang), jnp.sin(ang)
    c_ref[...] = jnp.where(lane < ROPE_DIM, cos, 1.0)
    sa_ref[...] = jnp.where(lane < ROPE_HALF, -sin, 0.0)
    sb_ref[...] = jnp.where((lane >= ROPE_HALF) & (lane < ROPE_DIM), sin, 0.0)


def _rope_tables(positions):
    t = positions.size
    tm = 2048
    inv_freq = ROPE_THETA ** (-jnp.arange(0, ROPE_DIM, 2, dtype=F32) / ROPE_DIM)
    freq = jnp.concatenate([inv_freq, inv_freq, jnp.zeros((LANES - ROPE_DIM,), F32)]).reshape(1, LANES)
    pos = positions.astype(F32).reshape(t, 1)
    out = jax.ShapeDtypeStruct((t, LANES), F32)
    return pl.pallas_call(
        _rope_kernel,
        out_shape=(out, out, out),
        grid=(t // tm,),
        in_specs=[pl.BlockSpec((tm, 1), lambda i: (i, 0)),
                  pl.BlockSpec((1, LANES), lambda i: (0, 0))],
        out_specs=tuple(pl.BlockSpec((tm, LANES), lambda i: (i, 0)) for _ in range(3)),
        compiler_params=_cparams(("arbitrary",)),
        name="rope_tables",
    )(pos, freq)


def _norm_mod_kernel(x_ref, g_ref, sh_ref, sc_ref, xp_ref, u_ref, slab_ref):
    rows = x_ref.shape[0]
    sub = rows // STREAMS
    nslab = x_ref.shape[1] // LANES
    for c in range(nslab):
        slab_ref[c] = x_ref[:, c * LANES:(c + 1) * LANES]
    for j in range(STREAMS):
        x = jnp.concatenate([slab_ref[c, pl.ds(j, sub, stride=STREAMS), :] for c in range(nslab)], axis=1)
        xp_ref[0, j] = x
        y = x * lax.rsqrt(jnp.mean(x * x, axis=-1, keepdims=True) + NORM_EPS) * g_ref[...]
        u_ref[0, j] = (y * (1.0 + sc_ref[0]) + sh_ref[0]).astype(u_ref.dtype)


def _norm_modulate(x2d, g, shift, scale, seq):
    t, d = x2d.shape
    tm = 512
    per_b = seq // tm
    bsz = shift.shape[0]
    sub = tm // STREAMS
    out_spec = pl.BlockSpec((1, STREAMS, sub, d), lambda i: (i // per_b, 0, i % per_b, 0))
    xp, u = pl.pallas_call(
        _norm_mod_kernel,
        out_shape=(jax.ShapeDtypeStruct((bsz, STREAMS, seq // STREAMS, d), F32),
                   jax.ShapeDtypeStruct((bsz, STREAMS, seq // STREAMS, d), BF16)),
        grid=(t // tm,),
        in_specs=[pl.BlockSpec((tm, d), lambda i: (i, 0)),
                  pl.BlockSpec((1, d), lambda i: (0, 0)),
                  pl.BlockSpec((1, 1, d), lambda i: (i // per_b, 0, 0)),
                  pl.BlockSpec((1, 1, d), lambda i: (i // per_b, 0, 0))],
        out_specs=(out_spec, out_spec),
        scratch_shapes=[pltpu.VMEM((d // LANES, tm, LANES), F32)],
        compiler_params=_cparams(("arbitrary",)),
        name="norm_modulate",
    )(x2d, g.reshape(1, d), shift.reshape(bsz, 1, d), scale.reshape(bsz, 1, d))
    return xp.reshape(t, d), u.reshape(t, d)


def _cast_weight_tile(w_ref, wb_ref):
    rows = 256

    def body(r, carry):
        sl = pl.ds(pl.multiple_of(r * rows, rows), rows)
        wb_ref[sl, :] = w_ref[sl, :].astype(BF16)
        return carry

    lax.fori_loop(0, w_ref.shape[0] // rows, body, 0)


def _proj_kernel(a_ref, w_ref, o_ref, wb_ref):
    @pl.when(pl.program_id(1) == 0)
    def _():
        _cast_weight_tile(w_ref, wb_ref)

    o_ref[...] = jnp.dot(a_ref[...], wb_ref[...], preferred_element_type=F32).astype(o_ref.dtype)


def _rope_project(a_ref, wb_ref, c_ref, sa_ref, sb_ref, o_ref, t):
    c, sa, sb = c_ref[...], sa_ref[...], sb_ref[...]
    pair = 2 * HEAD_DIM
    acc = jnp.dot(a_ref[...], wb_ref[:, t * pair:(t + 1) * pair], preferred_element_type=F32)
    for h in range(2):
        x = acc[:, h * HEAD_DIM:(h + 1) * HEAD_DIM]
        up = pltpu.roll(x, HEAD_DIM - ROPE_HALF, 1)
        dn = pltpu.roll(x, ROPE_HALF, 1)
        col = t * pair + h * HEAD_DIM
        o_ref[:, col:col + HEAD_DIM] = (x * c + up * sa + dn * sb).astype(o_ref.dtype)


def _plain_project(a_ref, wb_ref, o_ref, t):
    cols = slice(t * 2 * LANES, (t + 1) * 2 * LANES)
    o_ref[:, cols] = jnp.dot(a_ref[...], wb_ref[:, cols], preferred_element_type=F32).astype(o_ref.dtype)


def _proj_rope_kernel(a_ref, w_ref, c_ref, sa_ref, sb_ref, o_ref, wb_ref):
    @pl.when(pl.program_id(1) == 0)
    def _():
        _cast_weight_tile(w_ref, wb_ref)

    for t in range(o_ref.shape[1] // (2 * HEAD_DIM)):
        _rope_project(a_ref, wb_ref, c_ref, sa_ref, sb_ref, o_ref, t)


def _project(u, w, col0, ncols, rope=None):
    t, d = u.shape
    tm, tn = 1024, 1024
    j0 = col0 // tn
    in_specs = [pl.BlockSpec((tm, d), lambda j, i: (i, 0)),
                pl.BlockSpec((d, tn), lambda j, i: (0, j0 + j))]
    args = [u, w]
    if rope is not None:
        in_specs += [pl.BlockSpec((tm, LANES), lambda j, i: (i, 0)) for _ in range(3)]
        args += list(rope)
    return pl.pallas_call(
        _proj_kernel if rope is None else _proj_rope_kernel,
        out_shape=jax.ShapeDtypeStruct((t, ncols), BF16),
        grid=(ncols // tn, t // tm),
        in_specs=in_specs,
        out_specs=pl.BlockSpec((tm, tn), lambda j, i: (i, j)),
        scratch_shapes=[pltpu.VMEM((d, tn), BF16)],
        compiler_params=_cparams(("arbitrary", "arbitrary")),
        name="in_proj" if rope is None else "in_proj_rope",
    )(*args)


def _key_pieces(qb, sub):
    L = ATTN_BLOCK
    pieces = [(0, 0, L * (qb + 1))]
    for r in range(1, STREAMS):
        back = 0 if qb == 0 else (L // 2 if r == 1 else L // 4)
        pieces.append((r, L * qb - back, L + back))
    return pieces


def _attention_bias(sub):
    L = ATTN_BLOCK
    nqb = sub // L
    kmax = max(sum(p[2] for p in _key_pieces(qb, sub)) for qb in range(nqb))
    qi = jnp.arange(L, dtype=jnp.int32)[:, None]
    out = []
    for j in range(STREAMS):
        row = []
        for qb in range(nqb):
            cols = []
            for r, start, size in _key_pieces(qb, sub):
                jp = (j + r) % STREAMS
                kn = start + jnp.arange(size, dtype=jnp.int32)[None, :]
                dt = STREAMS * (L * qb + qi - kn) + (j - jp)
                cnt = jnp.zeros(dt.shape, F32)
                for window, dil in DILATIONS:
                    cnt += ((dt >= 0) & (dt <= window) & (dt % dil == 0)).astype(F32)
                cols.append(jnp.log2(cnt))
            tile = jnp.concatenate(cols, axis=1)
            row.append(jnp.pad(tile, ((0, 0), (0, kmax - tile.shape[1])), constant_values=-jnp.inf))
        out.append(jnp.stack(row))
    return jnp.stack(out)


def _attn_pipeline(q_ref, k_ref, v_ref, bias_ref, o_ref, bufs, j, interleave):
    L = ATTN_BLOCK
    sub = o_ref.shape[1] // STREAMS
    scale = HEAD_DIM ** -0.5 * 1.4426950408889634
    s_buf, m_buf, p_buf = bufs[0:2], bufs[2:4], bufs[4:6]
    units = [(g, qb) for g in range(HEADS_PER_STEP) for qb in range(sub // L)]
    own = pl.multiple_of(j * sub, sub)

    def key_rows(qb):
        rows = []
        for r, start, size in _key_pieces(qb, sub):
            base = own if r == 0 else pl.multiple_of(((j + r) % STREAMS) * sub, sub)
            rows.append(pl.ds(base + start, size))
        return rows, sum(p[2] for p in _key_pieces(qb, sub))

    def score_stage(u):
        g, qb = units[u]
        lanes = slice(g * HEAD_DIM, (g + 1) * HEAD_DIM)
        rows, ktot = key_rows(qb)
        q = q_ref[0, pl.ds(own + L * qb, L), lanes]
        k_all = jnp.concatenate([k_ref[0, rw, lanes] for rw in rows], axis=0)
        mx = None
        for c in range(0, ktot, 2 * LANES):
            w = min(2 * LANES, ktot - c)
            s = lax.dot_general(q, k_all[c:c + w], (((1,), (1,)), ((), ())), preferred_element_type=F32)
            s = s * scale + bias_ref[j, qb, :, c:c + w]
            s_buf[u % 2][:, c:c + w] = s
            for cc in range(0, w, LANES):
                part = s[:, cc:cc + LANES]
                mx = part if mx is None else jnp.maximum(mx, part)
        m_buf[u % 2][...] = jnp.broadcast_to(jnp.max(mx, axis=-1, keepdims=True), mx.shape)

    def prob_stage(u):
        _, qb = units[u]
        _, ktot = key_rows(qb)
        m = m_buf[u % 2][...]
        for c in range(0, ktot, LANES):
            p_buf[u % 2][:, c:c + LANES] = jnp.exp2(s_buf[u % 2][:, c:c + LANES] - m).astype(BF16)

    def out_stage(u):
        g, qb = units[u]
        lanes = slice(g * HEAD_DIM, (g + 1) * HEAD_DIM)
        rows, ktot = key_rows(qb)
        v_all = jnp.concatenate([v_ref[0, rw, lanes] for rw in rows], axis=0)
        v_ext = jnp.concatenate([v_all, jnp.ones_like(v_all)], axis=1)
        oe = jnp.dot(p_buf[u % 2][:, :ktot], v_ext, preferred_element_type=F32)
        y = oe[:, :HEAD_DIM] / oe[:, HEAD_DIM:]
        o_ref[0, pl.ds(own + L * qb, L), lanes] = y.astype(o_ref.dtype)

    for t in range(len(units) + 2):
        if t in interleave:
            interleave[t]()
        if t >= 2:
            out_stage(t - 2)
        if 1 <= t <= len(units):
            prob_stage(t - 1)
        if t < len(units):
            score_stage(t)


def _spread(n_steps, pieces, fn):
    every = max(n_steps // pieces, 1)
    return {p * every: functools.partial(fn, p) for p in range(pieces)}


def _proj_attn_kernel(a_ref, w_ref, q_ref, k_ref, v_ref, bias_ref, o_ref, y_ref, wb_ref, *bufs):
    i = pl.program_id(1)
    steps_per_group = STREAMS // ATTN_STREAMS_PER_STEP
    first = ((pl.program_id(0) * pl.num_programs(1) + i) % steps_per_group) * ATTN_STREAMS_PER_STEP

    @pl.when(i == 0)
    def _():
        _cast_weight_tile(w_ref, wb_ref)

    n_units = HEADS_PER_STEP * (y_ref.shape[1] // STREAMS // ATTN_BLOCK)
    pieces = o_ref.shape[1] // (2 * LANES) // ATTN_STREAMS_PER_STEP
    for l in range(ATTN_STREAMS_PER_STEP):
        interleave = _spread(n_units + 2, pieces,
                             lambda p, l=l: _plain_project(a_ref, wb_ref, o_ref, l * pieces + p))
        _attn_pipeline(q_ref, k_ref, v_ref, bias_ref, y_ref, bufs, first + l, interleave)


def _project_and_attention(u, w, col0, ncols, q, k, v, width, bsz, seq):
    t, d = u.shape
    gw = HEADS_PER_STEP * HEAD_DIM
    sub = seq // STREAMS
    tm, tn = 1024, 1024
    n_j, n_i = ncols // tn, t // tm
    n_hg = width // gw
    steps_per_group = STREAMS // ATTN_STREAMS_PER_STEP
    assert n_j * n_i == bsz * n_hg * steps_per_group, "ATTN_STREAMS_PER_STEP streams per projection tile"
    j0 = col0 // tn
    bias = _attention_bias(sub)

    def attn_pos(j, i):
        s = (j * n_i + i) // steps_per_group
        return s // n_hg, s % n_hg

    def attn_cols(src):
        a, c0 = src
        off = c0 // gw
        return a.reshape(bsz, seq, a.shape[1]), pl.BlockSpec(
            (1, seq, gw), lambda j, i: (attn_pos(j, i)[0], 0, off + attn_pos(j, i)[1]))

    (qa, qs), (ka, ks), (va, vs) = (attn_cols(s) for s in (q, k, v))
    proj, y = pl.pallas_call(
        _proj_attn_kernel,
        out_shape=(jax.ShapeDtypeStruct((t, ncols), BF16), jax.ShapeDtypeStruct((bsz, seq, width), BF16)),
        grid=(n_j, n_i),
        in_specs=[pl.BlockSpec((tm, d), lambda j, i: (i, 0)),
                  pl.BlockSpec((d, tn), lambda j, i: (0, j0 + j)),
                  qs, ks, vs,
                  pl.BlockSpec(bias.shape, lambda j, i: (0, 0, 0, 0), pipeline_mode=pl.Buffered(1))],
        out_specs=(pl.BlockSpec((tm, tn), lambda j, i: (i, j)),
                   pl.BlockSpec((1, seq, gw), lambda j, i: (attn_pos(j, i)[0], 0, attn_pos(j, i)[1]))),
        scratch_shapes=[pltpu.VMEM((d, tn), BF16)]
        + [pltpu.VMEM((ATTN_BLOCK, bias.shape[-1]), F32)] * 2
        + [pltpu.VMEM((ATTN_BLOCK, LANES), F32)] * 2
        + [pltpu.VMEM((ATTN_BLOCK, bias.shape[-1]), BF16)] * 2,
        compiler_params=_cparams(("arbitrary", "arbitrary")),
        name="proj_attention",
    )(u, w, qa, ka, va, bias)
    return proj, y.reshape(t, width)


def _rec_scratch():
    C, K = REC_CHUNK, HEAD_DIM
    return [pltpu.VMEM((K, K), F32), pltpu.VMEM((C, K), F32), pltpu.VMEM((C, K), F32),
            pltpu.VMEM((C, C), BF16), pltpu.VMEM((C, K), BF16), pltpu.VMEM((K, K), F32),
            pltpu.VMEM((1, K), F32)]


def _rec_pipeline(q_ref, f_ref, i_ref, g_ref, lb_ref, ng_ref, o_ref, scratch, c0, n, interleave):
    C = REC_CHUNK
    seq = q_ref.shape[1]
    sub = seq // STREAMS
    piece = C // STREAMS
    pi = lax.broadcasted_iota(jnp.int32, (C, C), 0)
    si = lax.broadcasted_iota(jnp.int32, (C, C), 1)
    time_of = lambda p: STREAMS * (p % piece) + p // piece
    causal = time_of(si) <= time_of(pi)
    tri = jnp.where(causal, 1.0, 0.0).astype(BF16)
    row_of = lambda tau: (tau % STREAMS) * piece + tau // STREAMS
    last = row_of(C - 1)
    anchor = row_of(C // 2 - 1)
    per_head = len(scratch) // REC_HEADS_PER_STEP

    def head(g):
        return slice(g * HEAD_DIM, (g + 1) * HEAD_DIM), scratch[g * per_head:(g + 1) * per_head]

    def rows(c, j):
        start = j * sub + c * piece
        return pl.ds(start if isinstance(start, int) else pl.multiple_of(start, piece), piece)

    def load(ref, c, lanes):
        return jnp.concatenate([ref[0, rows(c, j), lanes] for j in range(STREAMS)], axis=0)

    def gates(c, g):
        lanes, (_, b_s, kk_s, _, _, _, _) = head(g)
        lb = lb_ref[:, lanes]
        x = load(f_ref, c, lanes).astype(F32)
        z = jnp.exp(-jnp.abs(x))
        r = 1.0 / (1.0 + z)
        pos = x >= 0
        sig_p = jnp.where(pos, r, z * r)
        sig_n = jnp.where(pos, z * r, r)
        logf = jnp.log(lb + (1.0 - lb) * sig_p)
        hi = logf.astype(BF16)
        lo = (logf - hi.astype(F32)).astype(BF16)
        b_s[...] = (jnp.dot(tri, hi, preferred_element_type=F32)
                    + jnp.dot(tri, lo, preferred_element_type=F32))
        kk_s[...] = (1.0 - lb) * sig_n

    def scores(c, g):
        lanes, (_, b_s, kk_s, sc_s, qin_s, upd_s, dec_s) = head(g)
        b, kk = b_s[...], kk_s[...]
        b_last = b[last:last + 1, :]
        b_mid = b[anchor:anchor + 1, :]
        qs = _silu(load(q_ref, c, lanes).astype(F32))
        v = load(i_ref, c, lanes)
        q_a = (qs * jnp.exp(b - b_mid)).astype(BF16)
        k_a = (kk * jnp.exp(b_mid - b)).astype(BF16)
        k_e = (kk * jnp.exp(b_last - b)).astype(BF16)
        sc = lax.dot_general(q_a, k_a, (((1,), (1,)), ((), ())), preferred_element_type=F32)
        sc_s[...] = jnp.where(causal, sc, 0.0).astype(BF16)
        qin_s[...] = (qs * jnp.exp(b)).astype(BF16)
        upd_s[...] = lax.dot_general(v, k_e, (((0,), (0,)), ((), ())), preferred_element_type=F32)
        dec_s[...] = jnp.exp(b_last)

    def output(c, g):
        lanes, (st_ref, _, _, sc_s, qin_s, upd_s, dec_s) = head(g)
        v = load(i_ref, c, lanes)
        st = st_ref[...]
        o = (jnp.dot(sc_s[...], v, preferred_element_type=F32)
             + lax.dot_general(qin_s[...], st.astype(BF16), (((1,), (1,)), ((), ())),
                               preferred_element_type=F32))
        st_ref[...] = dec_s[...] * st + upd_s[...]
        y = o * lax.rsqrt(jnp.mean(o * o, axis=-1, keepdims=True) + NORM_EPS) * ng_ref[...]
        y = (y * _silu(load(g_ref, c, lanes).astype(F32))).astype(o_ref.dtype)
        for j in range(STREAMS):
            o_ref[0, rows(c, j), lanes] = y[j * piece:(j + 1) * piece]

    for t in range(n + 2):
        if t in interleave:
            interleave[t]()
        for g in range(REC_HEADS_PER_STEP):
            if t >= 2:
                output(c0 + (t - 2), g)
            if 1 <= t <= n:
                scores(c0 + (t - 1), g)
            if t < n:
                gates(c0 + t, g)


def _proj_rec_kernel(a_ref, w_ref, q_ref, f_ref, i_ref, g_ref, lb_ref, ng_ref, o_ref, y_ref, wb_ref, *scratch):
    n_i = pl.num_programs(1)
    i = pl.program_id(1)
    steps_per_seq = q_ref.shape[1] // (REC_CHUNK * REC_CHUNKS_PER_STEP)
    group = (pl.program_id(0) * n_i + i) % steps_per_seq

    @pl.when(i == 0)
    def _():
        _cast_weight_tile(w_ref, wb_ref)

    @pl.when(group == 0)
    def _():
        per_head = len(scratch) // REC_HEADS_PER_STEP
        for g in range(REC_HEADS_PER_STEP):
            scratch[g * per_head][...] = jnp.zeros_like(scratch[g * per_head])

    interleave = _spread(REC_CHUNKS_PER_STEP + 2, o_ref.shape[1] // (2 * LANES),
                         functools.partial(_plain_project, a_ref, wb_ref, o_ref))
    _rec_pipeline(q_ref, f_ref, i_ref, g_ref, lb_ref, ng_ref, y_ref, scratch,
                  group * REC_CHUNKS_PER_STEP, REC_CHUNKS_PER_STEP, interleave)


def _project_and_recurrence(u, w, col0, ncols, q_r, f_r, i_r, g_r, lower_bound, norm_g, bsz, seq):
    t, d = u.shape
    width = lower_bound.shape[0]
    gw = REC_HEADS_PER_STEP * HEAD_DIM
    tm, tn = 1024, 1024
    j0 = col0 // tn
    n_j, n_i = ncols // tn, t // tm
    n_hg = width // gw
    steps_per_seq = seq // (REC_CHUNK * REC_CHUNKS_PER_STEP)
    assert n_j * n_i == bsz * n_hg * steps_per_seq, "one recurrence slice per projection tile"

    def rec_pos(j, i):
        s = (j * n_i + i) // steps_per_seq
        return s // n_hg, s % n_hg

    def rec_cols(src):
        a, col0 = src
        off = col0 // gw
        return a.reshape(bsz, seq, a.shape[1]), pl.BlockSpec(
            (1, seq, gw), lambda j, i: (rec_pos(j, i)[0], 0, off + rec_pos(j, i)[1]))

    (qa, qs), (fa, fs), (ia, isp), (ga, gs) = (rec_cols(s) for s in (q_r, f_r, i_r, g_r))
    proj, y = pl.pallas_call(
        _proj_rec_kernel,
        out_shape=(jax.ShapeDtypeStruct((t, ncols), BF16), jax.ShapeDtypeStruct((bsz, seq, width), BF16)),
        grid=(n_j, n_i),
        in_specs=[pl.BlockSpec((tm, d), lambda j, i: (i, 0)),
                  pl.BlockSpec((d, tn), lambda j, i: (0, j0 + j)),
                  qs, fs, isp, gs,
                  pl.BlockSpec((1, gw), lambda j, i: (0, rec_pos(j, i)[1])),
                  pl.BlockSpec((1, HEAD_DIM), lambda j, i: (0, 0))],
        out_specs=(pl.BlockSpec((tm, tn), lambda j, i: (i, j)),
                   pl.BlockSpec((1, seq, gw), lambda j, i: (rec_pos(j, i)[0], 0, rec_pos(j, i)[1]))),
        scratch_shapes=[pltpu.VMEM((d, tn), BF16)]
        + [s for _ in range(REC_HEADS_PER_STEP) for s in _rec_scratch()],
        compiler_params=_cparams(("arbitrary", "arbitrary")),
        name="proj_hgrn2",
    )(u, w, qa, fa, ia, ga, lower_bound.reshape(1, width), norm_g.reshape(1, HEAD_DIM))
    return proj, y.reshape(t, width)


def _merge_kernel(ya_ref, yr_ref, wa_ref, wr_ref, ga_ref, gr_ref, o_ref):
    a = jnp.dot(ya_ref[...], wa_ref[...], preferred_element_type=F32)
    r = jnp.dot(yr_ref[...], wr_ref[...], preferred_element_type=F32)
    m = _sigmoid(ga_ref[...].astype(F32)) * a + _sigmoid(gr_ref[...].astype(F32)) * r
    o_ref[...] = m.astype(o_ref.dtype)


def _merge(ya, yr, wa, wr, ga, gr):
    t, d = ya.shape
    n = wa.shape[1]
    tm, tn = 1024, 512
    row = pl.BlockSpec((tm, d), lambda i, j: (i, 0))
    col = pl.BlockSpec((d, tn), lambda i, j: (0, j))
    tile = pl.BlockSpec((tm, tn), lambda i, j: (i, j))
    ga_off, gr_off = ga[1] // tn, gr[1] // tn
    return pl.pallas_call(
        _merge_kernel,
        out_shape=jax.ShapeDtypeStruct((t, n), BF16),
        grid=(t // tm, n // tn),
        in_specs=[row, row, col, col,
                  pl.BlockSpec((tm, tn), lambda i, j: (i, ga_off + j)),
                  pl.BlockSpec((tm, tn), lambda i, j: (i, gr_off + j))],
        out_specs=tile,
        compiler_params=_cparams(("arbitrary", "arbitrary")),
        name="branch_merge",
    )(ya, yr, wa, wr, ga[0], gr[0])


def _mixout_kernel(m_ref, w_ref, x_ref, gt_ref, g_ref, sh_ref, sc_ref, rw_ref, rb_ref,
                   h_ref, u_ref, route_ref):
    mix = jnp.dot(m_ref[...], w_ref[...], preferred_element_type=F32)
    h = x_ref[...] + gt_ref[0] * mix
    h_ref[...] = h
    u = h * lax.rsqrt(jnp.mean(h * h, axis=-1, keepdims=True) + NORM_EPS) * g_ref[...]
    u = u * (1.0 + sc_ref[0]) + sh_ref[0]
    wc = 2 * u_ref.shape[2]
    for c in range(u_ref.shape[0]):
        u_ref[c] = _pack_bf16_pairs(u[:, c * wc:(c + 1) * wc])
    u_hi = u.astype(BF16)
    u_lo = (u - u_hi.astype(F32)).astype(BF16)
    rw = rw_ref[...]
    w_hi = rw.astype(BF16)
    w_lo = (rw - w_hi.astype(F32)).astype(BF16)
    logits = (jnp.dot(u_hi, w_hi, preferred_element_type=F32)
              + jnp.dot(u_lo, w_hi, preferred_element_type=F32)
              + jnp.dot(u_hi, w_lo, preferred_element_type=F32)) + rb_ref[...]
    lane = lax.broadcasted_iota(jnp.int32, logits.shape, 1).astype(F32)
    big = float(LANES)
    neg = -jnp.inf
    lg = jnp.where(lane < N_GROUPS, logits, neg)
    mg = jnp.max(lg, axis=-1, keepdims=True)
    g_sel = jnp.min(jnp.where(lg == mg, lane, big), axis=-1, keepdims=True)
    p_group = 1.0 / jnp.sum(jnp.exp(lg - mg), axis=-1, keepdims=True)
    lo = N_GROUPS + EXPERTS_PER_GROUP * g_sel
    le = jnp.where((lane >= lo) & (lane < lo + EXPERTS_PER_GROUP), logits, neg)
    t1 = jnp.max(le, axis=-1, keepdims=True)
    i1 = jnp.min(jnp.where(le == t1, lane, big), axis=-1, keepdims=True)
    le2 = jnp.where(lane == i1, neg, le)
    t2 = jnp.max(le2, axis=-1, keepdims=True)
    i2 = jnp.min(jnp.where(le2 == t2, lane, big), axis=-1, keepdims=True)
    e21 = jnp.exp(t2 - t1)
    w1 = p_group / (1.0 + e21)
    w2 = p_group * e21 / (1.0 + e21)
    route = jnp.where(lane == 0, i1 - N_GROUPS,
                      jnp.where(lane == 1, i2 - N_GROUPS,
                                jnp.where(lane == 2, w1, jnp.where(lane == 3, w2, 0.0))))
    route_ref[...] = route


def _mix_out(merged, w_out, x2d, gate, g, shift, scale, rw, rb, seq):
    t, d = x2d.shape
    tm = 512
    per_b = seq // tm
    bsz = gate.shape[0]
    row = lambda dt: pl.BlockSpec((tm, d), lambda i: (i, 0))
    per_batch = pl.BlockSpec((1, 1, d), lambda i: (i // per_b, 0, 0))
    const = lambda shape: pl.BlockSpec(shape, lambda i: (0,) * len(shape))
    wc = d // GATHER_CHUNKS // 2
    outs = pl.pallas_call(
        _mixout_kernel,
        out_shape=(jax.ShapeDtypeStruct((t, d), F32),
                   jax.ShapeDtypeStruct((GATHER_CHUNKS, t, wc), jnp.uint32),
                   jax.ShapeDtypeStruct((t, LANES), F32)),
        grid=(t // tm,),
        in_specs=[row(BF16), const((d, d)), row(F32), per_batch, const((1, d)), per_batch, per_batch,
                  const((d, LANES)), const((1, LANES))],
        out_specs=(row(F32), pl.BlockSpec((GATHER_CHUNKS, tm, wc), lambda i: (0, i, 0)),
                   pl.BlockSpec((tm, LANES), lambda i: (i, 0))),
        compiler_params=_cparams(("arbitrary",)),
        name="mix_out_router",
    )(merged, w_out, x2d, gate.reshape(bsz, 1, d), g.reshape(1, d),
      shift.reshape(bsz, 1, d), scale.reshape(bsz, 1, d), rw, rb)
    return outs


def _expert_kernel(be_ref, nx_ref, nu_ref, *refs):
    x_ref = refs[0]
    w_hbm = refs[1:4]
    o_ref = refs[4]
    stage = refs[5:8]
    wb = refs[8:11]
    sem = refs[11]
    nch = x_ref.shape[0]
    wc = 2 * o_ref.shape[2]
    i = pl.program_id(0)
    e = be_ref[i]
    nxt = nx_ref[i]
    active = i < nu_ref[0]
    first = i == 0
    run_start = jnp.logical_or(first, e != be_ref[jnp.maximum(i - 1, 0)])

    def weight_copies(expert):
        return [pltpu.make_async_copy(w_hbm[k].at[expert], stage[k], sem.at[k]) for k in range(3)]

    @pl.when(jnp.logical_and(active, first))
    def _():
        for cp in weight_copies(e):
            cp.start()

    @pl.when(jnp.logical_and(active, run_start))
    def _():
        for cp in weight_copies(e):
            cp.wait()
        for k in range(3):
            _cast_weight_tile(stage[k], wb[k])

        @pl.when(nxt >= 0)
        def _():
            for cp in weight_copies(nxt):
                cp.start()

    @pl.when(active)
    def _():
        x = jnp.concatenate([_unpack_bf16_pairs(x_ref[c]) for c in range(nch)], axis=1).astype(BF16)
        hg = jnp.dot(x, wb[0][...], preferred_element_type=F32)
        hu = jnp.dot(x, wb[1][...], preferred_element_type=F32)
        hdn = (_silu(hg) * hu).astype(BF16)
        y = jnp.dot(hdn, wb[2][...], preferred_element_type=F32)
        for c in range(nch):
            o_ref[c] = _pack_bf16_pairs(y[:, c * wc:(c + 1) * wc])

    @pl.when(jnp.logical_not(active))
    def _():
        o_ref[...] = jnp.zeros_like(o_ref)


def _expert_ffn(xs, w_gate, w_up, w_down, block_expert, next_expert, n_used):
    nch, n_slots, wc = xs.shape
    d = 2 * wc * nch
    hid = w_gate.shape[2]
    bm = MOE_ROWS
    chunk = pl.BlockSpec((nch, bm, wc), lambda i, be, nx, nu: (0, i, 0))
    hbm = pl.BlockSpec(memory_space=pl.ANY)
    grid_spec = pltpu.PrefetchScalarGridSpec(
        num_scalar_prefetch=3,
        grid=(n_slots // bm,),
        in_specs=[chunk, hbm, hbm, hbm],
        out_specs=chunk,
        scratch_shapes=[pltpu.VMEM((d, hid), F32), pltpu.VMEM((d, hid), F32), pltpu.VMEM((hid, d), F32),
                        pltpu.VMEM((d, hid), BF16), pltpu.VMEM((d, hid), BF16), pltpu.VMEM((hid, d), BF16),
                        pltpu.SemaphoreType.DMA((3,))],
    )
    return pl.pallas_call(
        _expert_kernel,
        out_shape=jax.ShapeDtypeStruct((nch, n_slots, wc), jnp.uint32),
        grid_spec=grid_spec,
        compiler_params=_cparams(("arbitrary",)),
        name="expert_ffn",
    )(block_expert, next_expert, n_used, xs, w_gate, w_up, w_down)


def _gather_rows(chunks, idx):
    nch, n, d = chunks.shape
    table = chunks.reshape(nch * n, d)
    idx = (idx[None, :] + (jnp.arange(nch, dtype=jnp.int32) * n)[:, None]).reshape(-1)
    m = idx.shape[0]
    window = LANES
    mesh = plsc.VectorSubcoreMesh(core_axis_name="core", subcore_axis_name="subcore")

    @pl.kernel(out_type=jax.ShapeDtypeStruct((m, d), table.dtype), mesh=mesh, scratch_types=[])
    def gather(x_hbm, i_hbm, o_hbm):
        def body(i_vmem, o_vmem):
            pltpu.sync_copy(x_hbm.at[i_vmem.at[0]], o_vmem)

        pltpu.emit_pipeline(
            body,
            grid=(m // window,),
            in_specs=[pl.BlockSpec((1, window), lambda i: (0, i))],
            out_specs=[pl.BlockSpec((window, d), lambda i: (i, 0))],
            core_axis_name=("core", "subcore"),
            dimension_semantics=(pltpu.PARALLEL,),
        )(i_hbm, o_hbm)

    return gather(table, idx.reshape(1, m)).reshape(nch, m // nch, d)


def _final_kernel(h_ref, *rest):
    y_refs = rest[:TOP_K]
    rt_ref, gt_ref, g_ref, o_ref, slab_ref = rest[TOP_K:]
    sub = h_ref.shape[2]
    nslab = h_ref.shape[3] // LANES
    for j in range(STREAMS):
        route = rt_ref[0, j]
        ffn = sum(route[:, TOP_K + k:TOP_K + k + 1]
                  * jnp.concatenate([_unpack_bf16_pairs(y_refs[k][c, 0, 0, j])
                                     for c in range(GATHER_CHUNKS)], axis=1) for k in range(TOP_K))
        h = h_ref[0, j] + gt_ref[0] * ffn
        y = h * lax.rsqrt(jnp.mean(h * h, axis=-1, keepdims=True) + NORM_EPS) * g_ref[...]
        for c in range(nslab):
            slab_ref[c, pl.ds(j, sub, stride=STREAMS), :] = y[:, c * LANES:(c + 1) * LANES]
    for c in range(nslab):
        o_ref[:, c * LANES:(c + 1) * LANES] = slab_ref[c]


def _final(h, y2, route, gate, g, seq):
    t, d = h.shape
    tm = 512
    per_b = seq // tm
    bsz = gate.shape[0]
    sub = tm // STREAMS
    spec = pl.BlockSpec((1, STREAMS, sub, d), lambda i: (i // per_b, 0, i % per_b, 0))
    wc = d // GATHER_CHUNKS // 2
    y6 = y2.reshape(GATHER_CHUNKS, TOP_K, bsz, STREAMS, seq // STREAMS, wc)

    def yspec(k):
        return pl.BlockSpec((GATHER_CHUNKS, 1, 1, STREAMS, sub, wc),
                            lambda i: (0, k, i // per_b, 0, i % per_b, 0))

    return pl.pallas_call(
        _final_kernel,
        out_shape=jax.ShapeDtypeStruct((t, d), F32),
        grid=(t // tm,),
        in_specs=[spec, *[yspec(k) for k in range(TOP_K)],
                  pl.BlockSpec((1, STREAMS, sub, LANES), lambda i: (i // per_b, 0, i % per_b, 0)),
                  pl.BlockSpec((1, 1, d), lambda i: (i // per_b, 0, 0)),
                  pl.BlockSpec((1, d), lambda i: (0, 0))],
        out_specs=pl.BlockSpec((tm, d), lambda i: (i, 0)),
        scratch_shapes=[pltpu.VMEM((d // LANES, tm, LANES), F32)],
        compiler_params=_cparams(("arbitrary",)),
        name="final_norm",
    )(h.reshape(bsz, STREAMS, seq // STREAMS, d), *([y6] * TOP_K),
      route.reshape(bsz, STREAMS, seq // STREAMS, LANES), gate.reshape(bsz, 1, d), g.reshape(1, d))


def _dispatch_plan(expert_idx):
    n_assign = expert_idx.size
    n_blocks = n_assign // MOE_ROWS + N_EXPERTS
    n_slots = n_blocks * MOE_ROWS
    flat_e = expert_idx.reshape(-1)
    ids = jnp.arange(n_assign, dtype=jnp.int32)
    eids = jnp.arange(N_EXPERTS, dtype=jnp.int32)[None, :]
    _, order = lax.sort((flat_e, ids), num_keys=1, is_stable=True)
    _, rank_sorted = lax.sort((order, ids), num_keys=1)
    hot_a = (flat_e[:, None] == eids).astype(jnp.int32)
    counts = jnp.sum(hot_a, axis=0)
    padded = ((counts + MOE_ROWS - 1) // MOE_ROWS) * MOE_ROWS
    pad_end = jnp.cumsum(padded)
    pad_start = pad_end - padded
    start = jnp.cumsum(counts) - counts
    dest = (rank_sorted + jnp.sum(hot_a * (pad_start - start)[None, :], axis=1)).reshape(-1, TOP_K)
    blk0 = jnp.arange(n_blocks, dtype=jnp.int32) * MOE_ROWS
    block_expert = jnp.minimum(jnp.sum((pad_end[None, :] <= blk0[:, None]).astype(jnp.int32), axis=1),
                               N_EXPERTS - 1)
    hot_b = (block_expert[:, None] == eids).astype(jnp.int32)
    blk_shift = jnp.sum(hot_b * (start - pad_start)[None, :], axis=1)
    blk_count = jnp.sum(hot_b * (pad_start + counts)[None, :], axis=1)
    slot = jnp.arange(n_slots, dtype=jnp.int32).reshape(n_blocks, MOE_ROWS)
    valid = (slot < blk_count[:, None]).reshape(-1)
    src = jnp.clip(slot + blk_shift[:, None], 0, n_assign - 1).reshape(-1)
    assign = order[src]
    token_of_slot = jnp.where(valid, assign // TOP_K, slot.reshape(-1) % (n_assign // TOP_K))
    n_used = (pad_end[-1:] // MOE_ROWS).astype(jnp.int32)
    bi = jnp.arange(n_blocks, dtype=jnp.int32)
    later = ((bi[None, :] > bi[:, None]) & (block_expert[None, :] != block_expert[:, None])
             & (bi[None, :] < n_used[0]))
    next_expert = jnp.where(jnp.any(later, axis=1), block_expert[jnp.argmax(later, axis=1)], -1)
    return token_of_slot, dest, block_expert, next_expert.astype(jnp.int32), n_used


def kernel(x, c, positions, ada_w, ada_b, mix_norm_g, w_in, w_attn_branch, w_rec_branch, w_mix_out,
           rec_norm_g, rec_lb_logits, ffn_norm_g, router_group_w, router_group_b, router_expert_w,
           router_expert_b, expert_w_gate, expert_w_up, expert_w_down, final_norm_g):
    bsz, seq, d = x.shape
    t = bsz * seq
    depth = ada_w.shape[0]
    assert depth == 1, "final norm is fused after the single layer"
    lower_bounds = jnp.cumsum(jax.nn.softmax(rec_lb_logits.astype(F32), axis=0), axis=0)
    pos_streams = positions.reshape(bsz, seq // STREAMS, STREAMS).transpose(0, 2, 1)
    rope = _rope_tables(pos_streams)
    h = x.reshape(t, d)
    for layer in range(depth):
        mod = _modulation(c, ada_w[layer], ada_b[layer])
        sh_m, sc_m, gt_m, sh_f, sc_f, gt_f = jnp.split(mod, 6, axis=-1)
        h, u = _norm_modulate(h, mix_norm_g[layer], sh_m, sc_m, seq)
        w = w_in[layer]
        qk = _project(u, w, 0, 2 * d, rope=rope)
        v_a = _project(u, w, 2 * d, d)
        rec_in, y_attn = _project_and_attention(u, w, 3 * d, 4 * d, (qk, 0), (qk, d), (v_a, 0),
                                                d, bsz, seq)
        gates, y_rec = _project_and_recurrence(u, w, 7 * d, 2 * d, (rec_in, 0), (rec_in, d), (rec_in, 2 * d),
                                               (rec_in, 3 * d), lower_bounds[layer], rec_norm_g[layer],
                                               bsz, seq)
        merged = _merge(y_attn, y_rec, w_attn_branch[layer].astype(BF16),
                        w_rec_branch[layer].astype(BF16), (gates, 0), (gates, d))
        rw = jnp.concatenate([router_group_w[layer], router_expert_w[layer],
                              jnp.zeros((d, LANES - N_GROUPS - N_EXPERTS), F32)], axis=1)
        rb = jnp.concatenate([router_group_b[layer], router_expert_b[layer],
                              jnp.zeros((LANES - N_GROUPS - N_EXPERTS,), F32)]).reshape(1, LANES)
        h, u2, route = _mix_out(merged, w_mix_out[layer].astype(BF16), h, gt_m, ffn_norm_g[layer],
                                sh_f, sc_f, rw, rb, seq)
        expert_idx = route[:, :TOP_K].astype(jnp.int32)
        tok, dest, block_expert, next_expert, n_used = _dispatch_plan(expert_idx)
        xs = _gather_rows(u2, tok)
        ys = _expert_ffn(xs, expert_w_gate[layer], expert_w_up[layer], expert_w_down[layer],
                         block_expert, next_expert, n_used)
        dest_kt = dest.T.reshape(-1)
        y2 = _gather_rows(ys, dest_kt)
        h = _final(h, y2, route, gt_f, final_norm_g, seq)
    return h.reshape(bsz, seq, d)
```

```python
import functools

import jax
import jax.numpy as jnp
from jax import lax
from jax.experimental import pallas as pl
from jax.experimental.pallas import tpu as pltpu
from jax.experimental.pallas import tpu_sc as plsc

F32 = jnp.float32
BF16 = jnp.bfloat16

D_MODEL = 2048
HEAD_DIM = 128
N_HEADS = D_MODEL // HEAD_DIM
ROPE_DIM = HEAD_DIM // 4
ROPE_HALF = ROPE_DIM // 2
ROPE_THETA = 500000.0
ATTN_SPAN = 128
ATTN_BLOCK = 128
REC_CHUNK = 64
N_GROUPS = 4
EXPERTS_PER_GROUP = 8
N_EXPERTS = N_GROUPS * EXPERTS_PER_GROUP
EXPERT_HIDDEN = D_MODEL // 2
TOP_K = 2
NORM_EPS = 1e-6
IN_WIDTH = 9 * D_MODEL

LANES = 128
VMEM_LIMIT = 56 * 1024 * 1024

MOE_ROWS = 256
HEADS_PER_STEP = 2
ATTN_STREAMS_PER_STEP = 2
REC_HEADS_PER_STEP = 4
REC_CHUNKS_PER_STEP = 16
GATHER_CHUNKS = 4
STREAMS = 4
DILATIONS = ((128, 1), (512, 4), (2048, 16))


def _cparams(sem):
    return pltpu.CompilerParams(dimension_semantics=sem, vmem_limit_bytes=VMEM_LIMIT)


def _sigmoid(x):
    return 1.0 / (1.0 + jnp.exp(-x))


def _silu(x):
    return x * _sigmoid(x)


def _pack_bf16_pairs(x):
    w = x.shape[1] // 2
    lo = lax.bitcast_convert_type(x[:, :w].astype(BF16).astype(F32), jnp.uint32)
    hi = lax.bitcast_convert_type(x[:, w:].astype(BF16).astype(F32), jnp.uint32)
    return (lo >> 16) | (hi & jnp.uint32(0xFFFF0000))


def _unpack_bf16_pairs(words):
    lo = lax.bitcast_convert_type(words << 16, F32)
    hi = lax.bitcast_convert_type(words & jnp.uint32(0xFFFF0000), F32)
    return jnp.concatenate([lo, hi], axis=1)


def _mod_kernel(c_ref, w_ref, b_ref, o_ref):
    cond = _silu(c_ref[...])
    w = w_ref[...]
    c_hi = cond.astype(BF16)
    c_lo = (cond - c_hi.astype(F32)).astype(BF16)
    w_hi = w.astype(BF16)
    w_lo = (w - w_hi.astype(F32)).astype(BF16)
    o_ref[...] = (jnp.dot(c_hi, w_hi, preferred_element_type=F32)
                  + jnp.dot(c_lo, w_hi, preferred_element_type=F32)
                  + jnp.dot(c_hi, w_lo, preferred_element_type=F32)) + b_ref[...]


def _modulation(c, w, b):
    bsz, d = c.shape
    n = w.shape[1]
    tn = 1536
    return pl.pallas_call(
        _mod_kernel,
        out_shape=jax.ShapeDtypeStruct((bsz, n), F32),
        grid=(n // tn,),
        in_specs=[pl.BlockSpec((bsz, d), lambda j: (0, 0)),
                  pl.BlockSpec((d, tn), lambda j: (0, j)),
                  pl.BlockSpec((1, tn), lambda j: (0, j))],
        out_specs=pl.BlockSpec((bsz, tn), lambda j: (0, j)),
        compiler_params=_cparams(("arbitrary",)),
        name="adaln_mod",
    )(c, w, b.reshape(1, n))


def _rope_kernel(pos_ref, freq_ref, c_ref, sa_ref, sb_ref):
    ang = pos_ref[...] * freq_ref[...]
    lane = lax.broadcasted_iota(jnp.int32, ang.shape, 1)
    cos, sin = jnp.cos(ang), jnp.sin(ang)
    c_ref[...] = jnp.where(lane < ROPE_DIM, cos, 1.0)
    sa_ref[...] = jnp.where(lane < ROPE_HALF, -sin, 0.0)
    sb_ref[...] = jnp.where((lane >= ROPE_HALF) & (lane < ROPE_DIM), sin, 0.0)


def _rope_tables(positions):
    t = positions.size
    tm = 2048
    inv_freq = ROPE_THETA ** (-jnp.arange(0, ROPE_DIM, 2, dtype=F32) / ROPE_DIM)
    freq = jnp.concatenate([inv_freq, inv_freq, jnp.zeros((LANES - ROPE_DIM,), F32)]).reshape(1, LANES)
    pos = positions.astype(F32).reshape(t, 1)
    out = jax.ShapeDtypeStruct((t, LANES), F32)
    return pl.pallas_call(
        _rope_kernel,
        out_shape=(out, out, out),
        grid=(t // tm,),
        in_specs=[pl.BlockSpec((tm, 1), lambda i: (i, 0)),
                  pl.BlockSpec((1, LANES), lambda i: (0, 0))],
        out_specs=tuple(pl.BlockSpec((tm, LANES), lambda i: (i, 0)) for _ in range(3)),
        compiler_params=_cparams(("arbitrary",)),
        name="rope_tables",
    )(pos, freq)


def _norm_mod_kernel(x_ref, g_ref, sh_ref, sc_ref, xp_ref, u_ref, slab_ref):
    rows = x_ref.shape[0]
    sub = rows // STREAMS
    nslab = x_ref.shape[1] // LANES
    for c in range(nslab):
        slab_ref[c] = x_ref[:, c * LANES:(c + 1) * LANES]
    for j in range(STREAMS):
        x = jnp.concatenate([slab_ref[c, pl.ds(j, sub, stride=STREAMS), :] for c in range(nslab)], axis=1)
        xp_ref[0, j] = x
        y = x * lax.rsqrt(jnp.mean(x * x, axis=-1, keepdims=True) + NORM_EPS) * g_ref[...]
        u_ref[0, j] = (y * (1.0 + sc_ref[0]) + sh_ref[0]).astype(u_ref.dtype)


def _norm_modulate(x2d, g, shift, scale, seq):
    t, d = x2d.shape
    tm = 512
    per_b = seq // tm
    bsz = shift.shape[0]
    sub = tm // STREAMS
    out_spec = pl.BlockSpec((1, STREAMS, sub, d), lambda i: (i // per_b, 0, i % per_b, 0))
    xp, u = pl.pallas_call(
        _norm_mod_kernel,
        out_shape=(jax.ShapeDtypeStruct((bsz, STREAMS, seq // STREAMS, d), F32),
                   jax.ShapeDtypeStruct((bsz, STREAMS, seq // STREAMS, d), BF16)),
        grid=(t // tm,),
        in_specs=[pl.BlockSpec((tm, d), lambda i: (i, 0)),
                  pl.BlockSpec((1, d), lambda i: (0, 0)),
                  pl.BlockSpec((1, 1, d), lambda i: (i // per_b, 0, 0)),
                  pl.BlockSpec((1, 1, d), lambda i: (i // per_b, 0, 0))],
        out_specs=(out_spec, out_spec),
        scratch_shapes=[pltpu.VMEM((d // LANES, tm, LANES), F32)],
        compiler_params=_cparams(("arbitrary",)),
        name="norm_modulate",
    )(x2d, g.reshape(1, d), shift.reshape(bsz, 1, d), scale.reshape(bsz, 1, d))
    return xp.reshape(t, d), u.reshape(t, d)


def _cast_weight_tile(w_ref, wb_ref):
    rows = 256

    def body(r, carry):
        sl = pl.ds(pl.multiple_of(r * rows, rows), rows)
        wb_ref[sl, :] = w_ref[sl, :].astype(BF16)
        return carry

    lax.fori_loop(0, w_ref.shape[0] // rows, body, 0)


def _proj_kernel(a_ref, w_ref, o_ref, wb_ref):
    @pl.when(pl.program_id(1) == 0)
    def _():
        _cast_weight_tile(w_ref, wb_ref)

    o_ref[...] = jnp.dot(a_ref[...], wb_ref[...], preferred_element_type=F32).astype(o_ref.dtype)


def _rope_project(a_ref, wb_ref, c_ref, sa_ref, sb_ref, o_ref, t):
    c, sa, sb = c_ref[...], sa_ref[...], sb_ref[...]
    pair = 2 * HEAD_DIM
    acc = jnp.dot(a_ref[...], wb_ref[:, t * pair:(t + 1) * pair], preferred_element_type=F32)
    for h in range(2):
        x = acc[:, h * HEAD_DIM:(h + 1) * HEAD_DIM]
        up = pltpu.roll(x, HEAD_DIM - ROPE_HALF, 1)
        dn = pltpu.roll(x, ROPE_HALF, 1)
        col = t * pair + h * HEAD_DIM
        o_ref[:, col:col + HEAD_DIM] = (x * c + up * sa + dn * sb).astype(o_ref.dtype)


def _plain_project(a_ref, wb_ref, o_ref, t):
    cols = slice(t * 2 * LANES, (t + 1) * 2 * LANES)
    o_ref[:, cols] = jnp.dot(a_ref[...], wb_ref[:, cols], preferred_element_type=F32).astype(o_ref.dtype)


def _proj_rope_kernel(a_ref, w_ref, c_ref, sa_ref, sb_ref, o_ref, wb_ref):
    @pl.when(pl.program_id(1) == 0)
    def _():
        _cast_weight_tile(w_ref, wb_ref)

    for t in range(o_ref.shape[1] // (2 * HEAD_DIM)):
        _rope_project(a_ref, wb_ref, c_ref, sa_ref, sb_ref, o_ref, t)


def _project(u, w, col0, ncols, rope=None):
    t, d = u.shape
    tm, tn = 1024, 1024
    j0 = col0 // tn
    in_specs = [pl.BlockSpec((tm, d), lambda j, i: (i, 0)),
                pl.BlockSpec((d, tn), lambda j, i: (0, j0 + j))]
    args = [u, w]
    if rope is not None:
        in_specs += [pl.BlockSpec((tm, LANES), lambda j, i: (i, 0)) for _ in range(3)]
        args += list(rope)
    return pl.pallas_call(
        _proj_kernel if rope is None else _proj_rope_kernel,
        out_shape=jax.ShapeDtypeStruct((t, ncols), BF16),
        grid=(ncols // tn, t // tm),
        in_specs=in_specs,
        out_specs=pl.BlockSpec((tm, tn), lambda j, i: (i, j)),
        scratch_shapes=[pltpu.VMEM((d, tn), BF16)],
        compiler_params=_cparams(("arbitrary", "arbitrary")),
        name="in_proj" if rope is None else "in_proj_rope",
    )(*args)


def _key_pieces(qb, sub):
    L = ATTN_BLOCK
    pieces = [(0, 0, L * (qb + 1))]
    for r in range(1, STREAMS):
        back = 0 if qb == 0 else (L // 2 if r == 1 else L // 4)
        pieces.append((r, L * qb - back, L + back))
    return pieces


def _attention_bias(sub):
    L = ATTN_BLOCK
    nqb = sub // L
    kmax = max(sum(p[2] for p in _key_pieces(qb, sub)) for qb in range(nqb))
    qi = jnp.arange(L, dtype=jnp.int32)[:, None]
    out = []
    for j in range(STREAMS):
        row = []
        for qb in range(nqb):
            cols = []
            for r, start, size in _key_pieces(qb, sub):
                jp = (j + r) % STREAMS
                kn = start + jnp.arange(size, dtype=jnp.int32)[None, :]
                dt = STREAMS * (L * qb + qi - kn) + (j - jp)
                cnt = jnp.zeros(dt.shape, F32)
                for window, dil in DILATIONS:
                    cnt += ((dt >= 0) & (dt <= window) & (dt % dil == 0)).astype(F32)
                cols.append(jnp.log2(cnt))
            tile = jnp.concatenate(cols, axis=1)
            row.append(jnp.pad(tile, ((0, 0), (0, kmax - tile.shape[1])), constant_values=-jnp.inf))
        out.append(jnp.stack(row))
    return jnp.stack(out)


def _attn_pipeline(q_ref, k_ref, v_ref, bias_ref, o_ref, bufs, j, interleave):
    L = ATTN_BLOCK
    sub = o_ref.shape[1] // STREAMS
    scale = HEAD_DIM ** -0.5 * 1.4426950408889634
    s_buf, m_buf, p_buf = bufs[0:2], bufs[2:4], bufs[4:6]
    units = [(g, qb) for g in range(HEADS_PER_STEP) for qb in range(sub // L)]
    own = pl.multiple_of(j * sub, sub)

    def key_rows(qb):
        rows = []
        for r, start, size in _key_pieces(qb, sub):
            base = own if r == 0 else pl.multiple_of(((j + r) % STREAMS) * sub, sub)
            rows.append(pl.ds(base + start, size))
        return rows, sum(p[2] for p in _key_pieces(qb, sub))

    def score_stage(u):
        g, qb = units[u]
        lanes = slice(g * HEAD_DIM, (g + 1) * HEAD_DIM)
        rows, ktot = key_rows(qb)
        q = q_ref[0, pl.ds(own + L * qb, L), lanes]
        k_all = jnp.concatenate([k_ref[0, rw, lanes] for rw in rows], axis=0)
        mx = None
        for c in range(0, ktot, 2 * LANES):
            w = min(2 * LANES, ktot - c)
            s = lax.dot_general(q, k_all[c:c + w], (((1,), (1,)), ((), ())), preferred_element_type=F32)
            s = s * scale + bias_ref[j, qb, :, c:c + w]
            s_buf[u % 2][:, c:c + w] = s
            for cc in range(0, w, LANES):
                part = s[:, cc:cc + LANES]
                mx = part if mx is None else jnp.maximum(mx, part)
        m_buf[u % 2][...] = jnp.broadcast_to(jnp.max(mx, axis=-1, keepdims=True), mx.shape)

    def prob_stage(u):
        _, qb = units[u]
        _, ktot = key_rows(qb)
        m = m_buf[u % 2][...]
        for c in range(0, ktot, LANES):
            p_buf[u % 2][:, c:c + LANES] = jnp.exp2(s_buf[u % 2][:, c:c + LANES] - m).astype(BF16)

    def out_stage(u):
        g, qb = units[u]
        lanes = slice(g * HEAD_DIM, (g + 1) * HEAD_DIM)
        rows, ktot = key_rows(qb)
        v_all = jnp.concatenate([v_ref[0, rw, lanes] for rw in rows], axis=0)
        v_ext = jnp.concatenate([v_all, jnp.ones_like(v_all)], axis=1)
        oe = jnp.dot(p_buf[u % 2][:, :ktot], v_ext, preferred_element_type=F32)
        y = oe[:, :HEAD_DIM] / oe[:, HEAD_DIM:]
        o_ref[0, pl.ds(own + L * qb, L), lanes] = y.astype(o_ref.dtype)

    for t in range(len(units) + 2):
        if t in interleave:
            interleave[t]()
        if t >= 2:
            out_stage(t - 2)
        if 1 <= t <= len(units):
            prob_stage(t - 1)
        if t < len(units):
            score_stage(t)


def _spread(n_steps, pieces, fn):
    every = max(n_steps // pieces, 1)
    return {p * every: functools.partial(fn, p) for p in range(pieces)}


def _proj_attn_kernel(a_ref, w_ref, q_ref, k_ref, v_ref, bias_ref, o_ref, y_ref, wb_ref, *bufs):
    i = pl.program_id(1)
    steps_per_group = STREAMS // ATTN_STREAMS_PER_STEP
    first = ((pl.program_id(0) * pl.num_programs(1) + i) % steps_per_group) * ATTN_STREAMS_PER_STEP

    @pl.when(i == 0)
    def _():
        _cast_weight_tile(w_ref, wb_ref)

    n_units = HEADS_PER_STEP * (y_ref.shape[1] // STREAMS // ATTN_BLOCK)
    pieces = o_ref.shape[1] // (2 * LANES) // ATTN_STREAMS_PER_STEP
    for l in range(ATTN_STREAMS_PER_STEP):
        interleave = _spread(n_units + 2, pieces,
                             lambda p, l=l: _plain_project(a_ref, wb_ref, o_ref, l * pieces + p))
        _attn_pipeline(q_ref, k_ref, v_ref, bias_ref, y_ref, bufs, first + l, interleave)


def _project_and_attention(u, w, col0, ncols, q, k, v, width, bsz, seq):
    t, d = u.shape
    gw = HEADS_PER_STEP * HEAD_DIM
    sub = seq // STREAMS
    tm, tn = 1024, 1024
    n_j, n_i = ncols // tn, t // tm
    n_hg = width // gw
    steps_per_group = STREAMS // ATTN_STREAMS_PER_STEP
    assert n_j * n_i == bsz * n_hg * steps_per_group, "ATTN_STREAMS_PER_STEP streams per projection tile"
    j0 = col0 // tn
    bias = _attention_bias(sub)

    def attn_pos(j, i):
        s = (j * n_i + i) // steps_per_group
        return s // n_hg, s % n_hg

    def attn_cols(src):
        a, c0 = src
        off = c0 // gw
        return a.reshape(bsz, seq, a.shape[1]), pl.BlockSpec(
            (1, seq, gw), lambda j, i: (attn_pos(j, i)[0], 0, off + attn_pos(j, i)[1]))

    (qa, qs), (ka, ks), (va, vs) = (attn_cols(s) for s in (q, k, v))
    proj, y = pl.pallas_call(
        _proj_attn_kernel,
        out_shape=(jax.ShapeDtypeStruct((t, ncols), BF16), jax.ShapeDtypeStruct((bsz, seq, width), BF16)),
        grid=(n_j, n_i),
        in_specs=[pl.BlockSpec((tm, d), lambda j, i: (i, 0)),
                  pl.BlockSpec((d, tn), lambda j, i: (0, j0 + j)),
                  qs, ks, vs,
                  pl.BlockSpec(bias.shape, lambda j, i: (0, 0, 0, 0), pipeline_mode=pl.Buffered(1))],
        out_specs=(pl.BlockSpec((tm, tn), lambda j, i: (i, j)),
                   pl.BlockSpec((1, seq, gw), lambda j, i: (attn_pos(j, i)[0], 0, attn_pos(j, i)[1]))),
        scratch_shapes=[pltpu.VMEM((d, tn), BF16)]
        + [pltpu.VMEM((ATTN_BLOCK, bias.shape[-1]), F32)] * 2
        + [pltpu.VMEM((ATTN_BLOCK, LANES), F32)] * 2
        + [pltpu.VMEM((ATTN_BLOCK, bias.shape[-1]), BF16)] * 2,
        compiler_params=_cparams(("arbitrary", "arbitrary")),
        name="proj_attention",
    )(u, w, qa, ka, va, bias)
    return proj, y.reshape(t, width)


def _rec_scratch():
    C, K = REC_CHUNK, HEAD_DIM
    return [pltpu.VMEM((K, K), F32), pltpu.VMEM((C, K), F32), pltpu.VMEM((C, K), F32),
            pltpu.VMEM((C, C), BF16), pltpu.VMEM((C, K), BF16), pltpu.VMEM((K, K), F32),
            pltpu.VMEM((1, K), F32)]


def _rec_pipeline(q_ref, f_ref, i_ref, g_ref, lb_ref, ng_ref, o_ref, scratch, c0, n, interleave):
    C = REC_CHUNK
    seq = q_ref.shape[1]
    sub = seq // STREAMS
    piece = C // STREAMS
    pi = lax.broadcasted_iota(jnp.int32, (C, C), 0)
    si = lax.broadcasted_iota(jnp.int32, (C, C), 1)
    time_of = lambda p: STREAMS * (p % piece) + p // piece
    causal = time_of(si) <= time_of(pi)
    tri = jnp.where(causal, 1.0, 0.0).astype(BF16)
    row_of = lambda tau: (tau % STREAMS) * piece + tau // STREAMS
    last = row_of(C - 1)
    anchor = row_of(C // 2 - 1)
    per_head = len(scratch) // REC_HEADS_PER_STEP

    def head(g):
        return slice(g * HEAD_DIM, (g + 1) * HEAD_DIM), scratch[g * per_head:(g + 1) * per_head]

    def rows(c, j):
        start = j * sub + c * piece
        return pl.ds(start if isinstance(start, int) else pl.multiple_of(start, piece), piece)

    def load(ref, c, lanes):
        return jnp.concatenate([ref[0, rows(c, j), lanes] for j in range(STREAMS)], axis=0)

    def gates(c, g):
        lanes, (_, b_s, kk_s, _, _, _, _) = head(g)
        lb = lb_ref[:, lanes]
        x = load(f_ref, c, lanes).astype(F32)
        z = jnp.exp(-jnp.abs(x))
        r = 1.0 / (1.0 + z)
        pos = x >= 0
        sig_p = jnp.where(pos, r, z * r)
        sig_n = jnp.where(pos, z * r, r)
        logf = jnp.log(lb + (1.0 - lb) * sig_p)
        hi = logf.astype(BF16)
        lo = (logf - hi.astype(F32)).astype(BF16)
        b_s[...] = (jnp.dot(tri, hi, preferred_element_type=F32)
                    + jnp.dot(tri, lo, preferred_element_type=F32))
        kk_s[...] = (1.0 - lb) * sig_n

    def scores(c, g):
        lanes, (_, b_s, kk_s, sc_s, qin_s, upd_s, dec_s) = head(g)
        b, kk = b_s[...], kk_s[...]
        b_last = b[last:last + 1, :]
        b_mid = b[anchor:anchor + 1, :]
        qs = _silu(load(q_ref, c, lanes).astype(F32))
        v = load(i_ref, c, lanes)
        q_a = (qs * jnp.exp(b - b_mid)).astype(BF16)
        k_a = (kk * jnp.exp(b_mid - b)).astype(BF16)
        k_e = (kk * jnp.exp(b_last - b)).astype(BF16)
        sc = lax.dot_general(q_a, k_a, (((1,), (1,)), ((), ())), preferred_element_type=F32)
        sc_s[...] = jnp.where(causal, sc, 0.0).astype(BF16)
        qin_s[...] = (qs * jnp.exp(b)).astype(BF16)
        upd_s[...] = lax.dot_general(v, k_e, (((0,), (0,)), ((), ())), preferred_element_type=F32)
        dec_s[...] = jnp.exp(b_last)

    def output(c, g):
        lanes, (st_ref, _, _, sc_s, qin_s, upd_s, dec_s) = head(g)
        v = load(i_ref, c, lanes)
        st = st_ref[...]
        o = (jnp.dot(sc_s[...], v, preferred_element_type=F32)
             + lax.dot_general(qin_s[...], st.astype(BF16), (((1,), (1,)), ((), ())),
                               preferred_element_type=F32))
        st_ref[...] = dec_s[...] * st + upd_s[...]
        y = o * lax.rsqrt(jnp.mean(o * o, axis=-1, keepdims=True) + NORM_EPS) * ng_ref[...]
        y = (y * _silu(load(g_ref, c, lanes).astype(F32))).astype(o_ref.dtype)
        for j in range(STREAMS):
            o_ref[0, rows(c, j), lanes] = y[j * piece:(j + 1) * piece]

    for t in range(n + 2):
        if t in interleave:
            interleave[t]()
        for g in range(REC_HEADS_PER_STEP):
            if t >= 2:
                output(c0 + (t - 2), g)
            if 1 <= t <= n:
                scores(c0 + (t - 1), g)
            if t < n:
                gates(c0 + t, g)


def _proj_rec_kernel(a_ref, w_ref, q_ref, f_ref, i_ref, g_ref, lb_ref, ng_ref, o_ref, y_ref, wb_ref, *scratch):
    n_i = pl.num_programs(1)
    i = pl.program_id(1)
    steps_per_seq = q_ref.shape[1] // (REC_CHUNK * REC_CHUNKS_PER_STEP)
    group = (pl.program_id(0) * n_i + i) % steps_per_seq

    @pl.when(i == 0)
    def _():
        _cast_weight_tile(w_ref, wb_ref)

    @pl.when(group == 0)
    def _():
        per_head = len(scratch) // REC_HEADS_PER_STEP
        for g in range(REC_HEADS_PER_STEP):
            scratch[g * per_head][...] = jnp.zeros_like(scratch[g * per_head])

    interleave = _spread(REC_CHUNKS_PER_STEP + 2, o_ref.shape[1] // (2 * LANES),
                         functools.partial(_plain_project, a_ref, wb_ref, o_ref))
    _rec_pipeline(q_ref, f_ref, i_ref, g_ref, lb_ref, ng_ref, y_ref, scratch,
                  group * REC_CHUNKS_PER_STEP, REC_CHUNKS_PER_STEP, interleave)


def _project_and_recurrence(u, w, col0, ncols, q_r, f_r, i_r, g_r, lower_bound, norm_g, bsz, seq):
    t, d = u.shape
    width = lower_bound.shape[0]
    gw = REC_HEADS_PER_STEP * HEAD_DIM
    tm, tn = 1024, 1024
    j0 = col0 // tn
    n_j, n_i = ncols // tn, t // tm
    n_hg = width // gw
    steps_per_seq = seq // (REC_CHUNK * REC_CHUNKS_PER_STEP)
    assert n_j * n_i == bsz * n_hg * steps_per_seq, "one recurrence slice per projection tile"

    def rec_pos(j, i):
        s = (j * n_i + i) // steps_per_seq
        return s // n_hg, s % n_hg

    def rec_cols(src):
        a, col0 = src
        off = col0 // gw
        return a.reshape(bsz, seq, a.shape[1]), pl.BlockSpec(
            (1, seq, gw), lambda j, i: (rec_pos(j, i)[0], 0, off + rec_pos(j, i)[1]))

    (qa, qs), (fa, fs), (ia, isp), (ga, gs) = (rec_cols(s) for s in (q_r, f_r, i_r, g_r))
    proj, y = pl.pallas_call(
        _proj_rec_kernel,
        out_shape=(jax.ShapeDtypeStruct((t, ncols), BF16), jax.ShapeDtypeStruct((bsz, seq, width), BF16)),
        grid=(n_j, n_i),
        in_specs=[pl.BlockSpec((tm, d), lambda j, i: (i, 0)),
                  pl.BlockSpec((d, tn), lambda j, i: (0, j0 + j)),
                  qs, fs, isp, gs,
                  pl.BlockSpec((1, gw), lambda j, i: (0, rec_pos(j, i)[1])),
                  pl.BlockSpec((1, HEAD_DIM), lambda j, i: (0, 0))],
        out_specs=(pl.BlockSpec((tm, tn), lambda j, i: (i, j)),
                   pl.BlockSpec((1, seq, gw), lambda j, i: (rec_pos(j, i)[0], 0, rec_pos(j, i)[1]))),
        scratch_shapes=[pltpu.VMEM((d, tn), BF16)]
        + [s for _ in range(REC_HEADS_PER_STEP) for s in _rec_scratch()],
        compiler_params=_cparams(("arbitrary", "arbitrary")),
        name="proj_hgrn2",
    )(u, w, qa, fa, ia, ga, lower_bound.reshape(1, width), norm_g.reshape(1, HEAD_DIM))
    return proj, y.reshape(t, width)


def _merge_kernel(ya_ref, yr_ref, wa_ref, wr_ref, ga_ref, gr_ref, o_ref, wab_ref, wrb_ref):
    @pl.when(pl.program_id(1) == 0)
    def _():
        _cast_weight_tile(wa_ref, wab_ref)
        _cast_weight_tile(wr_ref, wrb_ref)

    a = jnp.dot(ya_ref[...], wab_ref[...], preferred_element_type=F32)
    r = jnp.dot(yr_ref[...], wrb_ref[...], preferred_element_type=F32)
    m = _sigmoid(ga_ref[...].astype(F32)) * a + _sigmoid(gr_ref[...].astype(F32)) * r
    o_ref[...] = m.astype(o_ref.dtype)


def _merge(ya, yr, wa, wr, ga, gr):
    t, d = ya.shape
    n = wa.shape[1]
    tm, tn = 1024, 512
    row = pl.BlockSpec((tm, d), lambda j, i: (i, 0))
    col = pl.BlockSpec((d, tn), lambda j, i: (0, j))
    ga_off, gr_off = ga[1] // tn, gr[1] // tn
    return pl.pallas_call(
        _merge_kernel,
        out_shape=jax.ShapeDtypeStruct((t, n), BF16),
        grid=(n // tn, t // tm),
        in_specs=[row, row, col, col,
                  pl.BlockSpec((tm, tn), lambda j, i: (i, ga_off + j)),
                  pl.BlockSpec((tm, tn), lambda j, i: (i, gr_off + j))],
        out_specs=pl.BlockSpec((tm, tn), lambda j, i: (i, j)),
        scratch_shapes=[pltpu.VMEM((d, tn), BF16)] * 2,
        compiler_params=_cparams(("arbitrary", "arbitrary")),
        name="branch_merge",
    )(ya, yr, wa, wr, ga[0], gr[0])


def _mixout_kernel(m_ref, w_ref, x_ref, gt_ref, g_ref, sh_ref, sc_ref, rw_ref, rb_ref,
                   h_ref, u_ref, route_ref, wb_ref):
    @pl.when(pl.program_id(0) == 0)
    def _():
        _cast_weight_tile(w_ref, wb_ref)

    mix = jnp.dot(m_ref[...], wb_ref[...], preferred_element_type=F32)
    h = x_ref[...] + gt_ref[0] * mix
    h_ref[...] = h
    u = h * lax.rsqrt(jnp.mean(h * h, axis=-1, keepdims=True) + NORM_EPS) * g_ref[...]
    u = u * (1.0 + sc_ref[0]) + sh_ref[0]
    wc = 2 * u_ref.shape[2]
    for c in range(u_ref.shape[0]):
        u_ref[c] = _pack_bf16_pairs(u[:, c * wc:(c + 1) * wc])
    u_hi = u.astype(BF16)
    u_lo = (u - u_hi.astype(F32)).astype(BF16)
    rw = rw_ref[...]
    w_hi = rw.astype(BF16)
    w_lo = (rw - w_hi.astype(F32)).astype(BF16)
    logits = (jnp.dot(u_hi, w_hi, preferred_element_type=F32)
              + jnp.dot(u_lo, w_hi, preferred_element_type=F32)
              + jnp.dot(u_hi, w_lo, preferred_element_type=F32)) + rb_ref[...]
    lane = lax.broadcasted_iota(jnp.int32, logits.shape, 1).astype(F32)
    big = float(LANES)
    neg = -jnp.inf
    lg = jnp.where(lane < N_GROUPS, logits, neg)
    mg = jnp.max(lg, axis=-1, keepdims=True)
    g_sel = jnp.min(jnp.where(lg == mg, lane, big), axis=-1, keepdims=True)
    p_group = 1.0 / jnp.sum(jnp.exp(lg - mg), axis=-1, keepdims=True)
    lo = N_GROUPS + EXPERTS_PER_GROUP * g_sel
    le = jnp.where((lane >= lo) & (lane < lo + EXPERTS_PER_GROUP), logits, neg)
    t1 = jnp.max(le, axis=-1, keepdims=True)
    i1 = jnp.min(jnp.where(le == t1, lane, big), axis=-1, keepdims=True)
    le2 = jnp.where(lane == i1, neg, le)
    t2 = jnp.max(le2, axis=-1, keepdims=True)
    i2 = jnp.min(jnp.where(le2 == t2, lane, big), axis=-1, keepdims=True)
    e21 = jnp.exp(t2 - t1)
    w1 = p_group / (1.0 + e21)
    w2 = p_group * e21 / (1.0 + e21)
    route = jnp.where(lane == 0, i1 - N_GROUPS,
                      jnp.where(lane == 1, i2 - N_GROUPS,
                                jnp.where(lane == 2, w1, jnp.where(lane == 3, w2, 0.0))))
    route_ref[...] = route


def _mix_out(merged, w_out, x2d, gate, g, shift, scale, rw, rb, seq):
    t, d = x2d.shape
    tm = 512
    per_b = seq // tm
    bsz = gate.shape[0]
    row = lambda dt: pl.BlockSpec((tm, d), lambda i: (i, 0))
    per_batch = pl.BlockSpec((1, 1, d), lambda i: (i // per_b, 0, 0))
    const = lambda shape: pl.BlockSpec(shape, lambda i: (0,) * len(shape))
    wc = d // GATHER_CHUNKS // 2
    outs = pl.pallas_call(
        _mixout_kernel,
        out_shape=(jax.ShapeDtypeStruct((t, d), F32),
                   jax.ShapeDtypeStruct((GATHER_CHUNKS, t, wc), jnp.uint32),
                   jax.ShapeDtypeStruct((t, LANES), F32)),
        grid=(t // tm,),
        in_specs=[row(BF16), pl.BlockSpec((d, d), lambda i: (0, 0), pipeline_mode=pl.Buffered(1)),
                  row(F32), per_batch, const((1, d)), per_batch, per_batch,
                  const((d, LANES)), const((1, LANES))],
        out_specs=(row(F32), pl.BlockSpec((GATHER_CHUNKS, tm, wc), lambda i: (0, i, 0)),
                   pl.BlockSpec((tm, LANES), lambda i: (i, 0))),
        scratch_shapes=[pltpu.VMEM((d, d), BF16)],
        compiler_params=_cparams(("arbitrary",)),
        name="mix_out_router",
    )(merged, w_out, x2d, gate.reshape(bsz, 1, d), g.reshape(1, d),
      shift.reshape(bsz, 1, d), scale.reshape(bsz, 1, d), rw, rb)
    return outs


def _expert_kernel(be_ref, nx_ref, nu_ref, *refs):
    x_ref = refs[0]
    w_hbm = refs[1:4]
    o_ref = refs[4]
    stage = refs[5:8]
    wb = refs[8:11]
    sem = refs[11]
    nch = x_ref.shape[0]
    wc = 2 * o_ref.shape[2]
    i = pl.program_id(0)
    e = be_ref[i]
    nxt = nx_ref[i]
    active = i < nu_ref[0]
    first = i == 0
    run_start = jnp.logical_or(first, e != be_ref[jnp.maximum(i - 1, 0)])

    def weight_copies(expert):
        return [pltpu.make_async_copy(w_hbm[k].at[expert], stage[k], sem.at[k]) for k in range(3)]

    @pl.when(jnp.logical_and(active, first))
    def _():
        for cp in weight_copies(e):
            cp.start()

    @pl.when(jnp.logical_and(active, run_start))
    def _():
        for cp in weight_copies(e):
            cp.wait()
        for k in range(3):
            _cast_weight_tile(stage[k], wb[k])

        @pl.when(nxt >= 0)
        def _():
            for cp in weight_copies(nxt):
                cp.start()

    @pl.when(active)
    def _():
        x = jnp.concatenate([_unpack_bf16_pairs(x_ref[c]) for c in range(nch)], axis=1).astype(BF16)
        hg = jnp.dot(x, wb[0][...], preferred_element_type=F32)
        hu = jnp.dot(x, wb[1][...], preferred_element_type=F32)
        hdn = (_silu(hg) * hu).astype(BF16)
        y = jnp.dot(hdn, wb[2][...], preferred_element_type=F32)
        for c in range(nch):
            o_ref[c] = _pack_bf16_pairs(y[:, c * wc:(c + 1) * wc])

    @pl.when(jnp.logical_not(active))
    def _():
        o_ref[...] = jnp.zeros_like(o_ref)


def _expert_ffn(xs, w_gate, w_up, w_down, block_expert, next_expert, n_used):
    nch, n_slots, wc = xs.shape
    d = 2 * wc * nch
    hid = w_gate.shape[2]
    bm = MOE_ROWS
    chunk = pl.BlockSpec((nch, bm, wc), lambda i, be, nx, nu: (0, i, 0))
    hbm = pl.BlockSpec(memory_space=pl.ANY)
    grid_spec = pltpu.PrefetchScalarGridSpec(
        num_scalar_prefetch=3,
        grid=(n_slots // bm,),
        in_specs=[chunk, hbm, hbm, hbm],
        out_specs=chunk,
        scratch_shapes=[pltpu.VMEM((d, hid), F32), pltpu.VMEM((d, hid), F32), pltpu.VMEM((hid, d), F32),
                        pltpu.VMEM((d, hid), BF16), pltpu.VMEM((d, hid), BF16), pltpu.VMEM((hid, d), BF16),
                        pltpu.SemaphoreType.DMA((3,))],
    )
    return pl.pallas_call(
        _expert_kernel,
        out_shape=jax.ShapeDtypeStruct((nch, n_slots, wc), jnp.uint32),
        grid_spec=grid_spec,
        compiler_params=_cparams(("arbitrary",)),
        name="expert_ffn",
    )(block_expert, next_expert, n_used, xs, w_gate, w_up, w_down)


def _gather_rows(chunks, idx):
    nch, n, d = chunks.shape
    table = chunks.reshape(nch * n, d)
    idx = (idx[None, :] + (jnp.arange(nch, dtype=jnp.int32) * n)[:, None]).reshape(-1)
    m = idx.shape[0]
    window = LANES
    mesh = plsc.VectorSubcoreMesh(core_axis_name="core", subcore_axis_name="subcore")

    @pl.kernel(out_type=jax.ShapeDtypeStruct((m, d), table.dtype), mesh=mesh, scratch_types=[])
    def gather(x_hbm, i_hbm, o_hbm):
        def body(i_vmem, o_vmem):
            pltpu.sync_copy(x_hbm.at[i_vmem.at[0]], o_vmem)

        pltpu.emit_pipeline(
            body,
            grid=(m // window,),
            in_specs=[pl.BlockSpec((1, window), lambda i: (0, i))],
            out_specs=[pl.BlockSpec((window, d), lambda i: (i, 0))],
            core_axis_name=("core", "subcore"),
            dimension_semantics=(pltpu.PARALLEL,),
        )(i_hbm, o_hbm)

    return gather(table, idx.reshape(1, m)).reshape(nch, m // nch, d)


def _final_kernel(h_ref, *rest):
    y_refs = rest[:TOP_K]
    rt_ref, gt_ref, g_ref, o_ref, slab_ref = rest[TOP_K:]
    sub = h_ref.shape[2]
    nslab = h_ref.shape[3] // LANES
    for j in range(STREAMS):
        route = rt_ref[0, j]
        ffn = sum(route[:, TOP_K + k:TOP_K + k + 1]
                  * jnp.concatenate([_unpack_bf16_pairs(y_refs[k][c, 0, 0, j])
                                     for c in range(GATHER_CHUNKS)], axis=1) for k in range(TOP_K))
        h = h_ref[0, j] + gt_ref[0] * ffn
        y = h * lax.rsqrt(jnp.mean(h * h, axis=-1, keepdims=True) + NORM_EPS) * g_ref[...]
        for c in range(nslab):
            slab_ref[c, pl.ds(j, sub, stride=STREAMS), :] = y[:, c * LANES:(c + 1) * LANES]
    for c in range(nslab):
        o_ref[:, c * LANES:(c + 1) * LANES] = slab_ref[c]


def _final(h, y2, route, gate, g, seq):
    t, d = h.shape
    tm = 512
    per_b = seq // tm
    bsz = gate.shape[0]
    sub = tm // STREAMS
    spec = pl.BlockSpec((1, STREAMS, sub, d), lambda i: (i // per_b, 0, i % per_b, 0))
    wc = d // GATHER_CHUNKS // 2
    y6 = y2.reshape(GATHER_CHUNKS, TOP_K, bsz, STREAMS, seq // STREAMS, wc)

    def yspec(k):
        return pl.BlockSpec((GATHER_CHUNKS, 1, 1, STREAMS, sub, wc),
                            lambda i: (0, k, i // per_b, 0, i % per_b, 0))

    return pl.pallas_call(
        _final_kernel,
        out_shape=jax.ShapeDtypeStruct((t, d), F32),
        grid=(t // tm,),
        in_specs=[spec, *[yspec(k) for k in range(TOP_K)],
                  pl.BlockSpec((1, STREAMS, sub, LANES), lambda i: (i // per_b, 0, i % per_b, 0)),
                  pl.BlockSpec((1, 1, d), lambda i: (i // per_b, 0, 0)),
                  pl.BlockSpec((1, d), lambda i: (0, 0))],
        out_specs=pl.BlockSpec((tm, d), lambda i: (i, 0)),
        scratch_shapes=[pltpu.VMEM((d // LANES, tm, LANES), F32)],
        compiler_params=_cparams(("arbitrary",)),
        name="final_norm",
    )(h.reshape(bsz, STREAMS, seq // STREAMS, d), *([y6] * TOP_K),
      route.reshape(bsz, STREAMS, seq // STREAMS, LANES), gate.reshape(bsz, 1, d), g.reshape(1, d))


def _dispatch_plan(expert_idx):
    n_assign = expert_idx.size
    n_blocks = n_assign // MOE_ROWS + N_EXPERTS
    n_slots = n_blocks * MOE_ROWS
    flat_e = expert_idx.reshape(-1)
    ids = jnp.arange(n_assign, dtype=jnp.int32)
    eids = jnp.arange(N_EXPERTS, dtype=jnp.int32)[None, :]
    _, order = lax.sort((flat_e, ids), num_keys=1, is_stable=True)
    _, rank_sorted = lax.sort((order, ids), num_keys=1)
    hot_a = (flat_e[:, None] == eids).astype(jnp.int32)
    counts = jnp.sum(hot_a, axis=0)
    padded = ((counts + MOE_ROWS - 1) // MOE_ROWS) * MOE_ROWS
    pad_end = jnp.cumsum(padded)
    pad_start = pad_end - padded
    start = jnp.cumsum(counts) - counts
    dest = (rank_sorted + jnp.sum(hot_a * (pad_start - start)[None, :], axis=1)).reshape(-1, TOP_K)
    blk0 = jnp.arange(n_blocks, dtype=jnp.int32) * MOE_ROWS
    block_expert = jnp.minimum(jnp.sum((pad_end[None, :] <= blk0[:, None]).astype(jnp.int32), axis=1),
                               N_EXPERTS - 1)
    hot_b = (block_expert[:, None] == eids).astype(jnp.int32)
    blk_shift = jnp.sum(hot_b * (start - pad_start)[None, :], axis=1)
    blk_count = jnp.sum(hot_b * (pad_start + counts)[None, :], axis=1)
    slot = jnp.arange(n_slots, dtype=jnp.int32).reshape(n_blocks, MOE_ROWS)
    valid = (slot < blk_count[:, None]).reshape(-1)
    src = jnp.clip(slot + blk_shift[:, None], 0, n_assign - 1).reshape(-1)
    assign = order[src]
    token_of_slot = jnp.where(valid, assign // TOP_K, slot.reshape(-1) % (n_assign // TOP_K))
    n_used = (pad_end[-1:] // MOE_ROWS).astype(jnp.int32)
    bi = jnp.arange(n_blocks, dtype=jnp.int32)
    later = ((bi[None, :] > bi[:, None]) & (block_expert[None, :] != block_expert[:, None])
             & (bi[None, :] < n_used[0]))
    next_expert = jnp.where(jnp.any(later, axis=1), block_expert[jnp.argmax(later, axis=1)], -1)
    return token_of_slot, dest, block_expert, next_expert.astype(jnp.int32), n_used


def kernel(x, c, positions, ada_w, ada_b, mix_norm_g, w_in, w_attn_branch, w_rec_branch, w_mix_out,
           rec_norm_g, rec_lb_logits, ffn_norm_g, router_group_w, router_group_b, router_expert_w,
           router_expert_b, expert_w_gate, expert_w_up, expert_w_down, final_norm_g):
    bsz, seq, d = x.shape
    t = bsz * seq
    depth = ada_w.shape[0]
    assert depth == 1, "final norm is fused after the single layer"
    lower_bounds = jnp.cumsum(jax.nn.softmax(rec_lb_logits.astype(F32), axis=0), axis=0)
    pos_streams = positions.reshape(bsz, seq // STREAMS, STREAMS).transpose(0, 2, 1)
    rope = _rope_tables(pos_streams)
    h = x.reshape(t, d)
    for layer in range(depth):
        mod = _modulation(c, ada_w[layer], ada_b[layer])
        sh_m, sc_m, gt_m, sh_f, sc_f, gt_f = jnp.split(mod, 6, axis=-1)
        h, u = _norm_modulate(h, mix_norm_g[layer], sh_m, sc_m, seq)
        w = w_in[layer]
        qk = _project(u, w, 0, 2 * d, rope=rope)
        v_a = _project(u, w, 2 * d, d)
        rec_in, y_attn = _project_and_attention(u, w, 3 * d, 4 * d, (qk, 0), (qk, d), (v_a, 0),
                                                d, bsz, seq)
        gates, y_rec = _project_and_recurrence(u, w, 7 * d, 2 * d, (rec_in, 0), (rec_in, d), (rec_in, 2 * d),
                                               (rec_in, 3 * d), lower_bounds[layer], rec_norm_g[layer],
                                               bsz, seq)
        merged = _merge(y_attn, y_rec, w_attn_branch[layer], w_rec_branch[layer], (gates, 0), (gates, d))
        rw = jnp.concatenate([router_group_w[layer], router_expert_w[layer],
                              jnp.zeros((d, LANES - N_GROUPS - N_EXPERTS), F32)], axis=1)
        rb = jnp.concatenate([router_group_b[layer], router_expert_b[layer],
                              jnp.zeros((LANES - N_GROUPS - N_EXPERTS,), F32)]).reshape(1, LANES)
        h, u2, route = _mix_out(merged, w_mix_out[layer], h, gt_m, ffn_norm_g[layer],
                                sh_f, sc_f, rw, rb, seq)
        expert_idx = route[:, :TOP_K].astype(jnp.int32)
        tok, dest, block_expert, next_expert, n_used = _dispatch_plan(expert_idx)
        xs = _gather_rows(u2, tok)
        ys = _expert_ffn(xs, expert_w_gate[layer], expert_w_up[layer], expert_w_down[layer],
                         block_expert, next_expert, n_used)
        dest_kt = dest.T.reshape(-1)
        y2 = _gather_rows(ys, dest_kt)
        h = _final(h, y2, route, gt_f, final_norm_g, seq)
    return h.reshape(bsz, seq, d)
```

```python
import functools

import jax
import jax.numpy as jnp
from jax import lax
from jax.experimental import pallas as pl
from jax.experimental.pallas import tpu as pltpu
from jax.experimental.pallas import tpu_sc as plsc

F32 = jnp.float32
BF16 = jnp.bfloat16

HEAD_DIM = 128
ROPE_DIM = HEAD_DIM // 4
ROPE_HALF = ROPE_DIM // 2
ROPE_THETA = 500000.0
ATTN_BLOCK = 128
REC_CHUNK = 64
N_GROUPS = 4
EXPERTS_PER_GROUP = 8
N_EXPERTS = N_GROUPS * EXPERTS_PER_GROUP
TOP_K = 2
NORM_EPS = 1e-6

LANES = 128
VMEM_LIMIT = 56 * 1024 * 1024

MOE_ROWS = 256
HEADS_PER_STEP = 2
ATTN_STREAMS_PER_STEP = 2
REC_HEADS_PER_STEP = 4
REC_CHUNKS_PER_STEP = 16
GATHER_CHUNKS = 4
STREAMS = 4
DILATIONS = ((128, 1), (512, 4), (2048, 16))


def _cparams(sem):
    return pltpu.CompilerParams(dimension_semantics=sem, vmem_limit_bytes=VMEM_LIMIT)


def _sigmoid(x):
    return 1.0 / (1.0 + jnp.exp(-x))


def _silu(x):
    return x * _sigmoid(x)


def _pack_bf16_pairs(x):
    w = x.shape[1] // 2
    lo = lax.bitcast_convert_type(x[:, :w].astype(BF16).astype(F32), jnp.uint32)
    hi = lax.bitcast_convert_type(x[:, w:].astype(BF16).astype(F32), jnp.uint32)
    return (lo >> 16) | (hi & jnp.uint32(0xFFFF0000))


def _unpack_bf16_pairs(words):
    lo = lax.bitcast_convert_type(words << 16, F32)
    hi = lax.bitcast_convert_type(words & jnp.uint32(0xFFFF0000), F32)
    return jnp.concatenate([lo, hi], axis=1)


def _mod_kernel(c_ref, w_ref, b_ref, o_ref):
    cond = _silu(c_ref[...])
    w = w_ref[...]
    c_hi = cond.astype(BF16)
    c_lo = (cond - c_hi.astype(F32)).astype(BF16)
    w_hi = w.astype(BF16)
    w_lo = (w - w_hi.astype(F32)).astype(BF16)
    o_ref[...] = (jnp.dot(c_hi, w_hi, preferred_element_type=F32)
                  + jnp.dot(c_lo, w_hi, preferred_element_type=F32)
                  + jnp.dot(c_hi, w_lo, preferred_element_type=F32)) + b_ref[...]


def _modulation(c, w, b):
    bsz, d = c.shape
    n = w.shape[1]
    tn = 1536
    return pl.pallas_call(
        _mod_kernel,
        out_shape=jax.ShapeDtypeStruct((bsz, n), F32),
        grid=(n // tn,),
        in_specs=[pl.BlockSpec((bsz, d), lambda j: (0, 0)),
                  pl.BlockSpec((d, tn), lambda j: (0, j)),
                  pl.BlockSpec((1, tn), lambda j: (0, j))],
        out_specs=pl.BlockSpec((bsz, tn), lambda j: (0, j)),
        compiler_params=_cparams(("arbitrary",)),
        name="adaln_mod",
    )(c, w, b.reshape(1, n))


def _rope_kernel(pos_ref, freq_ref, c_ref, sa_ref, sb_ref):
    ang = pos_ref[...] * freq_ref[...]
    lane = lax.broadcasted_iota(jnp.int32, ang.shape, 1)
    cos, sin = jnp.cos(ang), jnp.sin(ang)
    c_ref[...] = jnp.where(lane < ROPE_DIM, cos, 1.0)
    sa_ref[...] = jnp.where(lane < ROPE_HALF, -sin, 0.0)
    sb_ref[...] = jnp.where((lane >= ROPE_HALF) & (lane < ROPE_DIM), sin, 0.0)


def _rope_tables(positions):
    t = positions.size
    tm = 2048
    inv_freq = ROPE_THETA ** (-jnp.arange(0, ROPE_DIM, 2, dtype=F32) / ROPE_DIM)
    freq = jnp.concatenate([inv_freq, inv_freq, jnp.zeros((LANES - ROPE_DIM,), F32)]).reshape(1, LANES)
    pos = positions.astype(F32).reshape(t, 1)
    out = jax.ShapeDtypeStruct((t, LANES), F32)
    return pl.pallas_call(
        _rope_kernel,
        out_shape=(out, out, out),
        grid=(t // tm,),
        in_specs=[pl.BlockSpec((tm, 1), lambda i: (i, 0)),
                  pl.BlockSpec((1, LANES), lambda i: (0, 0))],
        out_specs=tuple(pl.BlockSpec((tm, LANES), lambda i: (i, 0)) for _ in range(3)),
        compiler_params=_cparams(("arbitrary",)),
        name="rope_tables",
    )(pos, freq)


def _norm_mod_kernel(x_ref, g_ref, sh_ref, sc_ref, xp_ref, u_ref, slab_ref):
    rows = x_ref.shape[0]
    sub = rows // STREAMS
    nslab = x_ref.shape[1] // LANES
    for c in range(nslab):
        slab_ref[c] = x_ref[:, c * LANES:(c + 1) * LANES]
    for j in range(STREAMS):
        x = jnp.concatenate([slab_ref[c, pl.ds(j, sub, stride=STREAMS), :] for c in range(nslab)], axis=1)
        xp_ref[0, j] = x
        y = x * lax.rsqrt(jnp.mean(x * x, axis=-1, keepdims=True) + NORM_EPS) * g_ref[...]
        u_ref[0, j] = (y * (1.0 + sc_ref[0]) + sh_ref[0]).astype(u_ref.dtype)


def _norm_modulate(x2d, g, shift, scale, seq):
    t, d = x2d.shape
    tm = 512
    per_b = seq // tm
    bsz = shift.shape[0]
    sub = tm // STREAMS
    out_spec = pl.BlockSpec((1, STREAMS, sub, d), lambda i: (i // per_b, 0, i % per_b, 0))
    xp, u = pl.pallas_call(
        _norm_mod_kernel,
        out_shape=(jax.ShapeDtypeStruct((bsz, STREAMS, seq // STREAMS, d), F32),
                   jax.ShapeDtypeStruct((bsz, STREAMS, seq // STREAMS, d), BF16)),
        grid=(t // tm,),
        in_specs=[pl.BlockSpec((tm, d), lambda i: (i, 0)),
                  pl.BlockSpec((1, d), lambda i: (0, 0)),
                  pl.BlockSpec((1, 1, d), lambda i: (i // per_b, 0, 0)),
                  pl.BlockSpec((1, 1, d), lambda i: (i // per_b, 0, 0))],
        out_specs=(out_spec, out_spec),
        scratch_shapes=[pltpu.VMEM((d // LANES, tm, LANES), F32)],
        compiler_params=_cparams(("arbitrary",)),
        name="norm_modulate",
    )(x2d, g.reshape(1, d), shift.reshape(bsz, 1, d), scale.reshape(bsz, 1, d))
    return xp.reshape(t, d), u.reshape(t, d)


def _cast_weight_tile(w_ref, wb_ref):
    rows = 256

    def body(r, carry):
        sl = pl.ds(pl.multiple_of(r * rows, rows), rows)
        wb_ref[sl, :] = w_ref[sl, :].astype(BF16)
        return carry

    lax.fori_loop(0, w_ref.shape[0] // rows, body, 0)


def _proj_kernel(a_ref, w_ref, o_ref, wb_ref):
    @pl.when(pl.program_id(1) == 0)
    def _():
        _cast_weight_tile(w_ref, wb_ref)

    o_ref[...] = jnp.dot(a_ref[...], wb_ref[...], preferred_element_type=F32).astype(o_ref.dtype)


def _rope_project(a_ref, wb_ref, c_ref, sa_ref, sb_ref, o_ref, t):
    c, sa, sb = c_ref[...], sa_ref[...], sb_ref[...]
    pair = 2 * HEAD_DIM
    acc = jnp.dot(a_ref[...], wb_ref[:, t * pair:(t + 1) * pair], preferred_element_type=F32)
    for h in range(2):
        x = acc[:, h * HEAD_DIM:(h + 1) * HEAD_DIM]
        up = pltpu.roll(x, HEAD_DIM - ROPE_HALF, 1)
        dn = pltpu.roll(x, ROPE_HALF, 1)
        col = t * pair + h * HEAD_DIM
        o_ref[:, col:col + HEAD_DIM] = (x * c + up * sa + dn * sb).astype(o_ref.dtype)


def _plain_project(a_ref, wb_ref, o_ref, t):
    cols = slice(t * 2 * LANES, (t + 1) * 2 * LANES)
    o_ref[:, cols] = jnp.dot(a_ref[...], wb_ref[:, cols], preferred_element_type=F32).astype(o_ref.dtype)


def _proj_rope_kernel(a_ref, w_ref, c_ref, sa_ref, sb_ref, o_ref, wb_ref):
    @pl.when(pl.program_id(1) == 0)
    def _():
        _cast_weight_tile(w_ref, wb_ref)

    for t in range(o_ref.shape[1] // (2 * HEAD_DIM)):
        _rope_project(a_ref, wb_ref, c_ref, sa_ref, sb_ref, o_ref, t)


def _project(u, w, col0, ncols, rope=None):
    t, d = u.shape
    tm, tn = 1024, 1024
    j0 = col0 // tn
    in_specs = [pl.BlockSpec((tm, d), lambda j, i: (i, 0)),
                pl.BlockSpec((d, tn), lambda j, i: (0, j0 + j))]
    args = [u, w]
    if rope is not None:
        in_specs += [pl.BlockSpec((tm, LANES), lambda j, i: (i, 0)) for _ in range(3)]
        args += list(rope)
    return pl.pallas_call(
        _proj_kernel if rope is None else _proj_rope_kernel,
        out_shape=jax.ShapeDtypeStruct((t, ncols), BF16),
        grid=(ncols // tn, t // tm),
        in_specs=in_specs,
        out_specs=pl.BlockSpec((tm, tn), lambda j, i: (i, j)),
        scratch_shapes=[pltpu.VMEM((d, tn), BF16)],
        compiler_params=_cparams(("arbitrary", "arbitrary")),
        name="in_proj" if rope is None else "in_proj_rope",
    )(*args)


def _key_pieces(qb, sub):
    L = ATTN_BLOCK
    pieces = [(0, 0, L * (qb + 1))]
    for r in range(1, STREAMS):
        back = 0 if qb == 0 else (L // 2 if r == 1 else L // 4)
        pieces.append((r, L * qb - back, L + back))
    return pieces


def _attention_bias(sub):
    L = ATTN_BLOCK
    nqb = sub // L
    kmax = max(sum(p[2] for p in _key_pieces(qb, sub)) for qb in range(nqb))
    qi = jnp.arange(L, dtype=jnp.int32)[:, None]
    out = []
    for j in range(STREAMS):
        row = []
        for qb in range(nqb):
            cols = []
            for r, start, size in _key_pieces(qb, sub):
                jp = (j + r) % STREAMS
                kn = start + jnp.arange(size, dtype=jnp.int32)[None, :]
                dt = STREAMS * (L * qb + qi - kn) + (j - jp)
                cnt = jnp.zeros(dt.shape, F32)
                for window, dil in DILATIONS:
                    cnt += ((dt >= 0) & (dt <= window) & (dt % dil == 0)).astype(F32)
                cols.append(jnp.log2(cnt))
            tile = jnp.concatenate(cols, axis=1)
            row.append(jnp.pad(tile, ((0, 0), (0, kmax - tile.shape[1])), constant_values=-jnp.inf))
        out.append(jnp.stack(row))
    return jnp.stack(out)


def _attn_pipeline(q_ref, k_ref, v_ref, bias_ref, o_ref, bufs, j, interleave):
    L = ATTN_BLOCK
    sub = o_ref.shape[1] // STREAMS
    scale = HEAD_DIM ** -0.5 * 1.4426950408889634
    s_buf, m_buf, p_buf = bufs[0:2], bufs[2:4], bufs[4:6]
    units = [(g, qb) for g in range(HEADS_PER_STEP) for qb in range(sub // L)]
    own = pl.multiple_of(j * sub, sub)

    def key_rows(qb):
        rows = []
        for r, start, size in _key_pieces(qb, sub):
            base = own if r == 0 else pl.multiple_of(((j + r) % STREAMS) * sub, sub)
            rows.append(pl.ds(base + start, size))
        return rows, sum(p[2] for p in _key_pieces(qb, sub))

    def score_stage(u):
        g, qb = units[u]
        lanes = slice(g * HEAD_DIM, (g + 1) * HEAD_DIM)
        rows, ktot = key_rows(qb)
        q = q_ref[0, pl.ds(own + L * qb, L), lanes]
        k_all = jnp.concatenate([k_ref[0, rw, lanes] for rw in rows], axis=0)
        mx = None
        for c in range(0, ktot, 2 * LANES):
            w = min(2 * LANES, ktot - c)
            s = lax.dot_general(q, k_all[c:c + w], (((1,), (1,)), ((), ())), preferred_element_type=F32)
            s = s * scale + bias_ref[j, qb, :, c:c + w]
            s_buf[u % 2][:, c:c + w] = s
            for cc in range(0, w, LANES):
                part = s[:, cc:cc + LANES]
                mx = part if mx is None else jnp.maximum(mx, part)
        m_buf[u % 2][...] = jnp.broadcast_to(jnp.max(mx, axis=-1, keepdims=True), mx.shape)

    def prob_stage(u):
        _, qb = units[u]
        _, ktot = key_rows(qb)
        m = m_buf[u % 2][...]
        for c in range(0, ktot, LANES):
            p_buf[u % 2][:, c:c + LANES] = jnp.exp2(s_buf[u % 2][:, c:c + LANES] - m).astype(BF16)

    def out_stage(u):
        g, qb = units[u]
        lanes = slice(g * HEAD_DIM, (g + 1) * HEAD_DIM)
        rows, ktot = key_rows(qb)
        v_all = jnp.concatenate([v_ref[0, rw, lanes] for rw in rows], axis=0)
        v_ext = jnp.concatenate([v_all, jnp.ones_like(v_all)], axis=1)
        oe = jnp.dot(p_buf[u % 2][:, :ktot], v_ext, preferred_element_type=F32)
        y = oe[:, :HEAD_DIM] / oe[:, HEAD_DIM:]
        o_ref[0, pl.ds(own + L * qb, L), lanes] = y.astype(o_ref.dtype)

    for t in range(len(units) + 2):
        if t in interleave:
            interleave[t]()
        if t >= 2:
            out_stage(t - 2)
        if 1 <= t <= len(units):
            prob_stage(t - 1)
        if t < len(units):
            score_stage(t)


def _spread(n_steps, pieces, fn):
    every = max(n_steps // pieces, 1)
    return {p * every: functools.partial(fn, p) for p in range(pieces)}


def _proj_attn_kernel(a_ref, w_ref, q_ref, k_ref, v_ref, bias_ref, o_ref, y_ref, wb_ref, *bufs):
    i = pl.program_id(1)
    steps_per_group = STREAMS // ATTN_STREAMS_PER_STEP
    first = ((pl.program_id(0) * pl.num_programs(1) + i) % steps_per_group) * ATTN_STREAMS_PER_STEP

    @pl.when(i == 0)
    def _():
        _cast_weight_tile(w_ref, wb_ref)

    n_units = HEADS_PER_STEP * (y_ref.shape[1] // STREAMS // ATTN_BLOCK)
    pieces = o_ref.shape[1] // (2 * LANES) // ATTN_STREAMS_PER_STEP
    for l in range(ATTN_STREAMS_PER_STEP):
        interleave = _spread(n_units + 2, pieces,
                             lambda p, l=l: _plain_project(a_ref, wb_ref, o_ref, l * pieces + p))
        _attn_pipeline(q_ref, k_ref, v_ref, bias_ref, y_ref, bufs, first + l, interleave)


def _project_and_attention(u, w, col0, ncols, q, k, v, width, bsz, seq):
    t, d = u.shape
    gw = HEADS_PER_STEP * HEAD_DIM
    sub = seq // STREAMS
    tm, tn = 1024, 1024
    n_j, n_i = ncols // tn, t // tm
    n_hg = width // gw
    steps_per_group = STREAMS // ATTN_STREAMS_PER_STEP
    assert n_j * n_i == bsz * n_hg * steps_per_group, "ATTN_STREAMS_PER_STEP streams per projection tile"
    j0 = col0 // tn
    bias = _attention_bias(sub)

    def attn_pos(j, i):
        s = (j * n_i + i) // steps_per_group
        return s // n_hg, s % n_hg

    def attn_cols(src):
        a, c0 = src
        off = c0 // gw
        return a.reshape(bsz, seq, a.shape[1]), pl.BlockSpec(
            (1, seq, gw), lambda j, i: (attn_pos(j, i)[0], 0, off + attn_pos(j, i)[1]))

    (qa, qs), (ka, ks), (va, vs) = (attn_cols(s) for s in (q, k, v))
    proj, y = pl.pallas_call(
        _proj_attn_kernel,
        out_shape=(jax.ShapeDtypeStruct((t, ncols), BF16), jax.ShapeDtypeStruct((bsz, seq, width), BF16)),
        grid=(n_j, n_i),
        in_specs=[pl.BlockSpec((tm, d), lambda j, i: (i, 0)),
                  pl.BlockSpec((d, tn), lambda j, i: (0, j0 + j)),
                  qs, ks, vs,
                  pl.BlockSpec(bias.shape, lambda j, i: (0, 0, 0, 0), pipeline_mode=pl.Buffered(1))],
        out_specs=(pl.BlockSpec((tm, tn), lambda j, i: (i, j)),
                   pl.BlockSpec((1, seq, gw), lambda j, i: (attn_pos(j, i)[0], 0, attn_pos(j, i)[1]))),
        scratch_shapes=[pltpu.VMEM((d, tn), BF16)]
        + [pltpu.VMEM((ATTN_BLOCK, bias.shape[-1]), F32)] * 2
        + [pltpu.VMEM((ATTN_BLOCK, LANES), F32)] * 2
        + [pltpu.VMEM((ATTN_BLOCK, bias.shape[-1]), BF16)] * 2,
        compiler_params=_cparams(("arbitrary", "arbitrary")),
        name="proj_attention",
    )(u, w, qa, ka, va, bias)
    return proj, y.reshape(t, width)


def _rec_scratch():
    C, K = REC_CHUNK, HEAD_DIM
    return [pltpu.VMEM((K, K), F32), pltpu.VMEM((C, K), F32), pltpu.VMEM((C, K), F32),
            pltpu.VMEM((C, C), BF16), pltpu.VMEM((C, K), BF16), pltpu.VMEM((K, K), F32),
            pltpu.VMEM((1, K), F32)]


def _rec_pipeline(q_ref, f_ref, i_ref, g_ref, lb_ref, ng_ref, o_ref, scratch, c0, n, interleave):
    C = REC_CHUNK
    seq = q_ref.shape[1]
    sub = seq // STREAMS
    piece = C // STREAMS
    pi = lax.broadcasted_iota(jnp.int32, (C, C), 0)
    si = lax.broadcasted_iota(jnp.int32, (C, C), 1)
    time_of = lambda p: STREAMS * (p % piece) + p // piece
    causal = time_of(si) <= time_of(pi)
    tri = jnp.where(causal, 1.0, 0.0).astype(BF16)
    row_of = lambda tau: (tau % STREAMS) * piece + tau // STREAMS
    last = row_of(C - 1)
    anchor = row_of(C // 2 - 1)
    per_head = len(scratch) // REC_HEADS_PER_STEP
    tri2 = jnp.concatenate([tri, tri], axis=1)

    def head(g):
        return slice(g * HEAD_DIM, (g + 1) * HEAD_DIM), scratch[g * per_head:(g + 1) * per_head]

    def rows(c, j):
        start = j * sub + c * piece
        return pl.ds(start if isinstance(start, int) else pl.multiple_of(start, piece), piece)

    def load(ref, c, lanes):
        return jnp.concatenate([ref[0, rows(c, j), lanes] for j in range(STREAMS)], axis=0)

    def gates(c, g):
        lanes, (_, b_s, kk_s, _, _, _, _) = head(g)
        lb = lb_ref[:, lanes]
        x = load(f_ref, c, lanes).astype(F32)
        z = jnp.exp(-jnp.abs(x))
        r = 1.0 / (1.0 + z)
        pos = x >= 0
        sig_p = jnp.where(pos, r, z * r)
        sig_n = jnp.where(pos, z * r, r)
        logf = jnp.log(lb + (1.0 - lb) * sig_p)
        hi = logf.astype(BF16)
        lo = (logf - hi.astype(F32)).astype(BF16)
        b_s[...] = jnp.dot(tri2, jnp.concatenate([hi, lo], axis=0), preferred_element_type=F32)
        kk_s[...] = (1.0 - lb) * sig_n

    def scores(c, g):
        lanes, (_, b_s, kk_s, sc_s, qin_s, upd_s, dec_s) = head(g)
        b, kk = b_s[...], kk_s[...]
        b_last = b[last:last + 1, :]
        b_mid = b[anchor:anchor + 1, :]
        qs = _silu(load(q_ref, c, lanes).astype(F32))
        v = load(i_ref, c, lanes)
        q_a = (qs * jnp.exp(b - b_mid)).astype(BF16)
        k_a = (kk * jnp.exp(b_mid - b)).astype(BF16)
        k_e = (kk * jnp.exp(b_last - b)).astype(BF16)
        sc = lax.dot_general(q_a, k_a, (((1,), (1,)), ((), ())), preferred_element_type=F32)
        sc_s[...] = jnp.where(causal, sc, 0.0).astype(BF16)
        qin_s[...] = (qs * jnp.exp(b)).astype(BF16)
        upd_s[...] = lax.dot_general(v, k_e, (((0,), (0,)), ((), ())), preferred_element_type=F32)
        dec_s[...] = jnp.exp(b_last)

    def output(c, g):
        lanes, (st_ref, _, _, sc_s, qin_s, upd_s, dec_s) = head(g)
        v = load(i_ref, c, lanes)
        st = st_ref[...]
        o = (jnp.dot(sc_s[...], v, preferred_element_type=F32)
             + lax.dot_general(qin_s[...], st.astype(BF16), (((1,), (1,)), ((), ())),
                               preferred_element_type=F32))
        st_ref[...] = dec_s[...] * st + upd_s[...]
        y = o * lax.rsqrt(jnp.mean(o * o, axis=-1, keepdims=True) + NORM_EPS) * ng_ref[...]
        y = (y * _silu(load(g_ref, c, lanes).astype(F32))).astype(o_ref.dtype)
        for j in range(STREAMS):
            o_ref[0, rows(c, j), lanes] = y[j * piece:(j + 1) * piece]

    for t in range(n + 2):
        if t in interleave:
            interleave[t]()
        for g in range(REC_HEADS_PER_STEP):
            if t >= 2:
                output(c0 + (t - 2), g)
            if 1 <= t <= n:
                scores(c0 + (t - 1), g)
            if t < n:
                gates(c0 + t, g)


def _proj_rec_kernel(a_ref, w_ref, q_ref, f_ref, i_ref, g_ref, lb_ref, ng_ref, o_ref, y_ref, wb_ref, *scratch):
    n_i = pl.num_programs(1)
    i = pl.program_id(1)
    steps_per_seq = q_ref.shape[1] // (REC_CHUNK * REC_CHUNKS_PER_STEP)
    group = (pl.program_id(0) * n_i + i) % steps_per_seq

    @pl.when(i == 0)
    def _():
        _cast_weight_tile(w_ref, wb_ref)

    @pl.when(group == 0)
    def _():
        per_head = len(scratch) // REC_HEADS_PER_STEP
        for g in range(REC_HEADS_PER_STEP):
            scratch[g * per_head][...] = jnp.zeros_like(scratch[g * per_head])

    interleave = _spread(REC_CHUNKS_PER_STEP + 2, o_ref.shape[1] // (2 * LANES),
                         functools.partial(_plain_project, a_ref, wb_ref, o_ref))
    _rec_pipeline(q_ref, f_ref, i_ref, g_ref, lb_ref, ng_ref, y_ref, scratch,
                  group * REC_CHUNKS_PER_STEP, REC_CHUNKS_PER_STEP, interleave)


def _project_and_recurrence(u, w, col0, ncols, q_r, f_r, i_r, g_r, lower_bound, norm_g, bsz, seq):
    t, d = u.shape
    width = lower_bound.shape[0]
    gw = REC_HEADS_PER_STEP * HEAD_DIM
    tm, tn = 1024, 1024
    j0 = col0 // tn
    n_j, n_i = ncols // tn, t // tm
    n_hg = width // gw
    steps_per_seq = seq // (REC_CHUNK * REC_CHUNKS_PER_STEP)
    assert n_j * n_i == bsz * n_hg * steps_per_seq, "one recurrence slice per projection tile"

    def rec_pos(j, i):
        s = (j * n_i + i) // steps_per_seq
        return s // n_hg, s % n_hg

    def rec_cols(src):
        a, col0 = src
        off = col0 // gw
        return a.reshape(bsz, seq, a.shape[1]), pl.BlockSpec(
            (1, seq, gw), lambda j, i: (rec_pos(j, i)[0], 0, off + rec_pos(j, i)[1]))

    (qa, qs), (fa, fs), (ia, isp), (ga, gs) = (rec_cols(s) for s in (q_r, f_r, i_r, g_r))
    proj, y = pl.pallas_call(
        _proj_rec_kernel,
        out_shape=(jax.ShapeDtypeStruct((t, ncols), BF16), jax.ShapeDtypeStruct((bsz, seq, width), BF16)),
        grid=(n_j, n_i),
        in_specs=[pl.BlockSpec((tm, d), lambda j, i: (i, 0)),
                  pl.BlockSpec((d, tn), lambda j, i: (0, j0 + j)),
                  qs, fs, isp, gs,
                  pl.BlockSpec((1, gw), lambda j, i: (0, rec_pos(j, i)[1])),
                  pl.BlockSpec((1, HEAD_DIM), lambda j, i: (0, 0))],
        out_specs=(pl.BlockSpec((tm, tn), lambda j, i: (i, j)),
                   pl.BlockSpec((1, seq, gw), lambda j, i: (rec_pos(j, i)[0], 0, rec_pos(j, i)[1]))),
        scratch_shapes=[pltpu.VMEM((d, tn), BF16)]
        + [s for _ in range(REC_HEADS_PER_STEP) for s in _rec_scratch()],
        compiler_params=_cparams(("arbitrary", "arbitrary")),
        name="proj_hgrn2",
    )(u, w, qa, fa, ia, ga, lower_bound.reshape(1, width), norm_g.reshape(1, HEAD_DIM))
    return proj, y.reshape(t, width)


def _merge_kernel(ya_ref, yr_ref, wa_ref, wr_ref, ga_ref, gr_ref, o_ref, wab_ref, wrb_ref):
    @pl.when(pl.program_id(1) == 0)
    def _():
        _cast_weight_tile(wa_ref, wab_ref)
        _cast_weight_tile(wr_ref, wrb_ref)

    a = jnp.dot(ya_ref[...], wab_ref[...], preferred_element_type=F32)
    r = jnp.dot(yr_ref[...], wrb_ref[...], preferred_element_type=F32)
    m = _sigmoid(ga_ref[...].astype(F32)) * a + _sigmoid(gr_ref[...].astype(F32)) * r
    o_ref[...] = m.astype(o_ref.dtype)


def _merge(ya, yr, wa, wr, ga, gr):
    t, d = ya.shape
    n = wa.shape[1]
    tm, tn = 1024, 512
    row = pl.BlockSpec((tm, d), lambda j, i: (i, 0))
    col = pl.BlockSpec((d, tn), lambda j, i: (0, j))
    ga_off, gr_off = ga[1] // tn, gr[1] // tn
    return pl.pallas_call(
        _merge_kernel,
        out_shape=jax.ShapeDtypeStruct((t, n), BF16),
        grid=(n // tn, t // tm),
        in_specs=[row, row, col, col,
                  pl.BlockSpec((tm, tn), lambda j, i: (i, ga_off + j)),
                  pl.BlockSpec((tm, tn), lambda j, i: (i, gr_off + j))],
        out_specs=pl.BlockSpec((tm, tn), lambda j, i: (i, j)),
        scratch_shapes=[pltpu.VMEM((d, tn), BF16)] * 2,
        compiler_params=_cparams(("arbitrary", "arbitrary")),
        name="branch_merge",
    )(ya, yr, wa, wr, ga[0], gr[0])


def _mixout_kernel(m_ref, w_ref, x_ref, gt_ref, g_ref, sh_ref, sc_ref, rw_ref, rb_ref,
                   h_ref, u_ref, route_ref, wb_ref):
    @pl.when(pl.program_id(0) == 0)
    def _():
        _cast_weight_tile(w_ref, wb_ref)

    mix = jnp.dot(m_ref[...], wb_ref[...], preferred_element_type=F32)
    h = x_ref[...] + gt_ref[0] * mix
    h_ref[...] = h
    u = h * lax.rsqrt(jnp.mean(h * h, axis=-1, keepdims=True) + NORM_EPS) * g_ref[...]
    u = u * (1.0 + sc_ref[0]) + sh_ref[0]
    wc = 2 * u_ref.shape[2]
    for c in range(u_ref.shape[0]):
        u_ref[c] = _pack_bf16_pairs(u[:, c * wc:(c + 1) * wc])
    u_hi = u.astype(BF16)
    u_lo = (u - u_hi.astype(F32)).astype(BF16)
    rw = rw_ref[...]
    w_hi = rw.astype(BF16)
    w_lo = (rw - w_hi.astype(F32)).astype(BF16)
    logits = (jnp.dot(u_hi, w_hi, preferred_element_type=F32)
              + jnp.dot(u_lo, w_hi, preferred_element_type=F32)
              + jnp.dot(u_hi, w_lo, preferred_element_type=F32)) + rb_ref[...]
    lane = lax.broadcasted_iota(jnp.int32, logits.shape, 1).astype(F32)
    big = float(LANES)
    neg = -jnp.inf
    lg = jnp.where(lane < N_GROUPS, logits, neg)
    mg = jnp.max(lg, axis=-1, keepdims=True)
    g_sel = jnp.min(jnp.where(lg == mg, lane, big), axis=-1, keepdims=True)
    p_group = 1.0 / jnp.sum(jnp.exp(lg - mg), axis=-1, keepdims=True)
    lo = N_GROUPS + EXPERTS_PER_GROUP * g_sel
    le = jnp.where((lane >= lo) & (lane < lo + EXPERTS_PER_GROUP), logits, neg)
    t1 = jnp.max(le, axis=-1, keepdims=True)
    i1 = jnp.min(jnp.where(le == t1, lane, big), axis=-1, keepdims=True)
    le2 = jnp.where(lane == i1, neg, le)
    t2 = jnp.max(le2, axis=-1, keepdims=True)
    i2 = jnp.min(jnp.where(le2 == t2, lane, big), axis=-1, keepdims=True)
    e21 = jnp.exp(t2 - t1)
    w1 = p_group / (1.0 + e21)
    w2 = p_group * e21 / (1.0 + e21)
    route = jnp.where(lane == 0, i1 - N_GROUPS,
                      jnp.where(lane == 1, i2 - N_GROUPS,
                                jnp.where(lane == 2, w1, jnp.where(lane == 3, w2, 0.0))))
    route_ref[...] = route


def _mix_out(merged, w_out, x2d, gate, g, shift, scale, rw, rb, seq):
    t, d = x2d.shape
    tm = 512
    per_b = seq // tm
    bsz = gate.shape[0]
    row = lambda dt: pl.BlockSpec((tm, d), lambda i: (i, 0))
    per_batch = pl.BlockSpec((1, 1, d), lambda i: (i // per_b, 0, 0))
    const = lambda shape: pl.BlockSpec(shape, lambda i: (0,) * len(shape))
    wc = d // GATHER_CHUNKS // 2
    outs = pl.pallas_call(
        _mixout_kernel,
        out_shape=(jax.ShapeDtypeStruct((t, d), F32),
                   jax.ShapeDtypeStruct((GATHER_CHUNKS, t, wc), jnp.uint32),
                   jax.ShapeDtypeStruct((t, LANES), F32)),
        grid=(t // tm,),
        in_specs=[row(BF16), pl.BlockSpec((d, d), lambda i: (0, 0), pipeline_mode=pl.Buffered(1)),
                  row(F32), per_batch, const((1, d)), per_batch, per_batch,
                  const((d, LANES)), const((1, LANES))],
        out_specs=(row(F32), pl.BlockSpec((GATHER_CHUNKS, tm, wc), lambda i: (0, i, 0)),
                   pl.BlockSpec((tm, LANES), lambda i: (i, 0))),
        scratch_shapes=[pltpu.VMEM((d, d), BF16)],
        compiler_params=_cparams(("arbitrary",)),
        name="mix_out_router",
    )(merged, w_out, x2d, gate.reshape(bsz, 1, d), g.reshape(1, d),
      shift.reshape(bsz, 1, d), scale.reshape(bsz, 1, d), rw, rb)
    return outs


def _expert_kernel(be_ref, nx_ref, nu_ref, *refs):
    x_ref = refs[0]
    w_hbm = refs[1:4]
    o_ref = refs[4]
    stage = refs[5:8]
    wb = refs[8:11]
    sem = refs[11]
    nch = x_ref.shape[0]
    wc = 2 * o_ref.shape[2]
    i = pl.program_id(0)
    e = be_ref[i]
    nxt = nx_ref[i]
    active = i < nu_ref[0]
    first = i == 0
    run_start = jnp.logical_or(first, e != be_ref[jnp.maximum(i - 1, 0)])

    def weight_copies(expert):
        return [pltpu.make_async_copy(w_hbm[k].at[expert], stage[k], sem.at[k]) for k in range(3)]

    @pl.when(jnp.logical_and(active, first))
    def _():
        for cp in weight_copies(e):
            cp.start()

    @pl.when(jnp.logical_and(active, run_start))
    def _():
        for cp in weight_copies(e):
            cp.wait()
        for k in range(3):
            _cast_weight_tile(stage[k], wb[k])

        @pl.when(nxt >= 0)
        def _():
            for cp in weight_copies(nxt):
                cp.start()

    @pl.when(active)
    def _():
        x = jnp.concatenate([_unpack_bf16_pairs(x_ref[c]) for c in range(nch)], axis=1).astype(BF16)
        hg = jnp.dot(x, wb[0][...], preferred_element_type=F32)
        hu = jnp.dot(x, wb[1][...], preferred_element_type=F32)
        hdn = (_silu(hg) * hu).astype(BF16)
        y = jnp.dot(hdn, wb[2][...], preferred_element_type=F32)
        for c in range(nch):
            o_ref[c] = _pack_bf16_pairs(y[:, c * wc:(c + 1) * wc])

    @pl.when(jnp.logical_not(active))
    def _():
        o_ref[...] = jnp.zeros_like(o_ref)


def _expert_ffn(xs, w_gate, w_up, w_down, block_expert, next_expert, n_used):
    nch, n_slots, wc = xs.shape
    d = 2 * wc * nch
    hid = w_gate.shape[2]
    bm = MOE_ROWS
    chunk = pl.BlockSpec((nch, bm, wc), lambda i, be, nx, nu: (0, i, 0))
    hbm = pl.BlockSpec(memory_space=pl.ANY)
    grid_spec = pltpu.PrefetchScalarGridSpec(
        num_scalar_prefetch=3,
        grid=(n_slots // bm,),
        in_specs=[chunk, hbm, hbm, hbm],
        out_specs=chunk,
        scratch_shapes=[pltpu.VMEM((d, hid), F32), pltpu.VMEM((d, hid), F32), pltpu.VMEM((hid, d), F32),
                        pltpu.VMEM((d, hid), BF16), pltpu.VMEM((d, hid), BF16), pltpu.VMEM((hid, d), BF16),
                        pltpu.SemaphoreType.DMA((3,))],
    )
    return pl.pallas_call(
        _expert_kernel,
        out_shape=jax.ShapeDtypeStruct((nch, n_slots, wc), jnp.uint32),
        grid_spec=grid_spec,
        compiler_params=_cparams(("arbitrary",)),
        name="expert_ffn",
    )(block_expert, next_expert, n_used, xs, w_gate, w_up, w_down)


def _gather_rows(chunks, idx):
    nch, n, d = chunks.shape
    table = chunks.reshape(nch * n, d)
    idx = (idx[None, :] + (jnp.arange(nch, dtype=jnp.int32) * n)[:, None]).reshape(-1)
    m = idx.shape[0]
    window = LANES
    mesh = plsc.VectorSubcoreMesh(core_axis_name="core", subcore_axis_name="subcore")

    @pl.kernel(out_type=jax.ShapeDtypeStruct((m, d), table.dtype), mesh=mesh, scratch_types=[])
    def gather(x_hbm, i_hbm, o_hbm):
        def body(i_vmem, o_vmem):
            pltpu.sync_copy(x_hbm.at[i_vmem.at[0]], o_vmem)

        pltpu.emit_pipeline(
            body,
            grid=(m // window,),
            in_specs=[pl.BlockSpec((1, window), lambda i: (0, i))],
            out_specs=[pl.BlockSpec((window, d), lambda i: (i, 0))],
            core_axis_name=("core", "subcore"),
            dimension_semantics=(pltpu.PARALLEL,),
        )(i_hbm, o_hbm)

    return gather(table, idx.reshape(1, m)).reshape(nch, m // nch, d)


def _final_kernel(h_ref, *rest):
    y_refs = rest[:TOP_K]
    rt_ref, gt_ref, g_ref = rest[TOP_K:TOP_K + 3]
    o_ref, slab_ref = rest[-2:]
    sub = h_ref.shape[2]
    nslab = h_ref.shape[3] // LANES
    for j in range(STREAMS):
        route = rt_ref[0, j]
        ffn = sum(route[:, TOP_K + k:TOP_K + k + 1]
                  * jnp.concatenate([_unpack_bf16_pairs(y_refs[k][c, 0, 0, j])
                                     for c in range(GATHER_CHUNKS)], axis=1) for k in range(TOP_K))
        h = h_ref[0, j] + gt_ref[0] * ffn
        y = h * lax.rsqrt(jnp.mean(h * h, axis=-1, keepdims=True) + NORM_EPS) * g_ref[...]
        for c in range(nslab):
            slab_ref[c, pl.ds(j, sub, stride=STREAMS), :] = y[:, c * LANES:(c + 1) * LANES]
    for c in range(nslab):
        o_ref[:, c * LANES:(c + 1) * LANES] = slab_ref[c]


def _final(h, y2, route, gate, g, seq, part, parts, prev=None):
    t, d = h.shape
    tm = 512
    per_b = seq // tm
    bsz = gate.shape[0]
    sub = tm // STREAMS
    steps = t // tm // parts
    first = part * steps
    spec = pl.BlockSpec((1, STREAMS, sub, d), lambda i: ((first + i) // per_b, 0, (first + i) % per_b, 0))
    wc = d // GATHER_CHUNKS // 2
    y6 = y2.reshape(GATHER_CHUNKS, TOP_K, bsz // parts, STREAMS, seq // STREAMS, wc)

    def yspec(k):
        return pl.BlockSpec((GATHER_CHUNKS, 1, 1, STREAMS, sub, wc),
                            lambda i: (0, k, i // per_b, 0, i % per_b, 0))

    in_specs = [spec, *[yspec(k) for k in range(TOP_K)],
                pl.BlockSpec((1, STREAMS, sub, LANES),
                             lambda i: ((first + i) // per_b, 0, (first + i) % per_b, 0)),
                pl.BlockSpec((1, 1, d), lambda i: ((first + i) // per_b, 0, 0)),
                pl.BlockSpec((1, d), lambda i: (0, 0))]
    args = [h.reshape(bsz, STREAMS, seq // STREAMS, d), *([y6] * TOP_K),
            route.reshape(bsz, STREAMS, seq // STREAMS, LANES), gate.reshape(bsz, 1, d), g.reshape(1, d)]
    aliases = {}
    if prev is not None:
        in_specs.append(pl.BlockSpec(memory_space=pl.ANY))
        args.append(prev)
        aliases = {len(args) - 1: 0}
    return pl.pallas_call(
        _final_kernel,
        out_shape=jax.ShapeDtypeStruct((t, d), F32),
        grid=(steps,),
        in_specs=in_specs,
        out_specs=pl.BlockSpec((tm, d), lambda i: (first + i, 0)),
        scratch_shapes=[pltpu.VMEM((d // LANES, tm, LANES), F32)],
        input_output_aliases=aliases,
        compiler_params=_cparams(("arbitrary",)),
        name="final_norm",
    )(*args)


def _dispatch_plan(expert_idx):
    n_assign = expert_idx.size
    n_blocks = n_assign // MOE_ROWS + N_EXPERTS
    n_slots = n_blocks * MOE_ROWS
    flat_e = expert_idx.reshape(-1)
    ids = jnp.arange(n_assign, dtype=jnp.int32)
    eids = jnp.arange(N_EXPERTS, dtype=jnp.int32)[None, :]
    _, order = lax.sort((flat_e, ids), num_keys=1, is_stable=True)
    _, rank_sorted = lax.sort((order, ids), num_keys=1)
    hot_a = (flat_e[:, None] == eids).astype(jnp.int32)
    counts = jnp.sum(hot_a, axis=0)
    padded = ((counts + MOE_ROWS - 1) // MOE_ROWS) * MOE_ROWS
    pad_end = jnp.cumsum(padded)
    pad_start = pad_end - padded
    start = jnp.cumsum(counts) - counts
    dest = (rank_sorted + jnp.sum(hot_a * (pad_start - start)[None, :], axis=1)).reshape(-1, TOP_K)
    blk0 = jnp.arange(n_blocks, dtype=jnp.int32) * MOE_ROWS
    block_expert = jnp.minimum(jnp.sum((pad_end[None, :] <= blk0[:, None]).astype(jnp.int32), axis=1),
                               N_EXPERTS - 1)
    hot_b = (block_expert[:, None] == eids).astype(jnp.int32)
    blk_shift = jnp.sum(hot_b * (start - pad_start)[None, :], axis=1)
    blk_count = jnp.sum(hot_b * (pad_start + counts)[None, :], axis=1)
    slot = jnp.arange(n_slots, dtype=jnp.int32).reshape(n_blocks, MOE_ROWS)
    valid = (slot < blk_count[:, None]).reshape(-1)
    src = jnp.clip(slot + blk_shift[:, None], 0, n_assign - 1).reshape(-1)
    assign = order[src]
    token_of_slot = jnp.where(valid, assign // TOP_K, slot.reshape(-1) % (n_assign // TOP_K))
    n_used = (pad_end[-1:] // MOE_ROWS).astype(jnp.int32)
    bi = jnp.arange(n_blocks, dtype=jnp.int32)
    later = ((bi[None, :] > bi[:, None]) & (block_expert[None, :] != block_expert[:, None])
             & (bi[None, :] < n_used[0]))
    next_expert = jnp.where(jnp.any(later, axis=1), block_expert[jnp.argmax(later, axis=1)], -1)
    return token_of_slot, dest, block_expert, next_expert.astype(jnp.int32), n_used


def kernel(x, c, positions, ada_w, ada_b, mix_norm_g, w_in, w_attn_branch, w_rec_branch, w_mix_out,
           rec_norm_g, rec_lb_logits, ffn_norm_g, router_group_w, router_group_b, router_expert_w,
           router_expert_b, expert_w_gate, expert_w_up, expert_w_down, final_norm_g):
    bsz, seq, d = x.shape
    t = bsz * seq
    depth = ada_w.shape[0]
    assert depth == 1, "final norm is fused after the single layer"
    lower_bounds = jnp.cumsum(jax.nn.softmax(rec_lb_logits.astype(F32), axis=0), axis=0)
    pos_streams = positions.reshape(bsz, seq // STREAMS, STREAMS).transpose(0, 2, 1)
    rope = _rope_tables(pos_streams)
    h = x.reshape(t, d)
    for layer in range(depth):
        mod = _modulation(c, ada_w[layer], ada_b[layer])
        sh_m, sc_m, gt_m, sh_f, sc_f, gt_f = jnp.split(mod, 6, axis=-1)
        h, u = _norm_modulate(h, mix_norm_g[layer], sh_m, sc_m, seq)
        w = w_in[layer]
        qk = _project(u, w, 0, 2 * d, rope=rope)
        v_a = _project(u, w, 2 * d, d)
        rec_in, y_attn = _project_and_attention(u, w, 3 * d, 4 * d, (qk, 0), (qk, d), (v_a, 0),
                                                d, bsz, seq)
        gates, y_rec = _project_and_recurrence(u, w, 7 * d, 2 * d, (rec_in, 0), (rec_in, d), (rec_in, 2 * d),
                                               (rec_in, 3 * d), lower_bounds[layer], rec_norm_g[layer],
                                               bsz, seq)
        merged = _merge(y_attn, y_rec, w_attn_branch[layer], w_rec_branch[layer], (gates, 0), (gates, d))
        rw = jnp.concatenate([router_group_w[layer], router_expert_w[layer],
                              jnp.zeros((d, LANES - N_GROUPS - N_EXPERTS), F32)], axis=1)
        rb = jnp.concatenate([router_group_b[layer], router_expert_b[layer],
                              jnp.zeros((LANES - N_GROUPS - N_EXPERTS,), F32)]).reshape(1, LANES)
        h, u2, route = _mix_out(merged, w_mix_out[layer], h, gt_m, ffn_norm_g[layer],
                                sh_f, sc_f, rw, rb, seq)
        expert_idx = route[:, :TOP_K].astype(jnp.int32)
        tok, dest, block_expert, next_expert, n_used = _dispatch_plan(expert_idx)
        xs = _gather_rows(u2, tok)
        ys = _expert_ffn(xs, expert_w_gate[layer], expert_w_up[layer], expert_w_down[layer],
                         block_expert, next_expert, n_used)
        parts = 2
        out = None
        for part in range(parts):
            tok_part = dest[part * (t // parts):(part + 1) * (t // parts)]
            y2 = _gather_rows(ys, tok_part.T.reshape(-1))
            out = _final(h, y2, route, gt_f, final_norm_g, seq, part, parts, prev=out)
        h = out
    return h.reshape(bsz, seq, d)
```

```python
import functools

import jax
import jax.numpy as jnp
from jax import lax
from jax.experimental import pallas as pl
from jax.experimental.pallas import tpu as pltpu
from jax.experimental.pallas import tpu_sc as plsc

F32 = jnp.float32
BF16 = jnp.bfloat16

HEAD_DIM = 128
ROPE_DIM = HEAD_DIM // 4
ROPE_HALF = ROPE_DIM // 2
ROPE_THETA = 500000.0
ATTN_BLOCK = 128
REC_CHUNK = 64
N_GROUPS = 4
EXPERTS_PER_GROUP = 8
N_EXPERTS = N_GROUPS * EXPERTS_PER_GROUP
TOP_K = 2
NORM_EPS = 1e-6

LANES = 128
VMEM_LIMIT = 56 * 1024 * 1024

MOE_ROWS = 256
HEADS_PER_STEP = 2
ATTN_STREAMS_PER_STEP = 2
REC_HEADS_PER_STEP = 4
REC_CHUNKS_PER_STEP = 16
GATHER_CHUNKS = 4
STREAMS = 4
DILATIONS = ((128, 1), (512, 4), (2048, 16))


def _cparams(sem):
    return pltpu.CompilerParams(dimension_semantics=sem, vmem_limit_bytes=VMEM_LIMIT)


def _sigmoid(x):
    return 1.0 / (1.0 + jnp.exp(-x))


def _silu(x):
    return x * _sigmoid(x)


def _pack_bf16_pairs(x):
    w = x.shape[1] // 2
    lo = lax.bitcast_convert_type(x[:, :w].astype(BF16).astype(F32), jnp.uint32)
    hi = lax.bitcast_convert_type(x[:, w:].astype(BF16).astype(F32), jnp.uint32)
    return (lo >> 16) | (hi & jnp.uint32(0xFFFF0000))


def _unpack_bf16_pairs(words):
    lo = lax.bitcast_convert_type(words << 16, F32)
    hi = lax.bitcast_convert_type(words & jnp.uint32(0xFFFF0000), F32)
    return jnp.concatenate([lo, hi], axis=1)


def _mod_kernel(c_ref, w_ref, b_ref, o_ref):
    cond = _silu(c_ref[...])
    w = w_ref[...]
    c_hi = cond.astype(BF16)
    c_lo = (cond - c_hi.astype(F32)).astype(BF16)
    w_hi = w.astype(BF16)
    w_lo = (w - w_hi.astype(F32)).astype(BF16)
    o_ref[...] = (jnp.dot(c_hi, w_hi, preferred_element_type=F32)
                  + jnp.dot(c_lo, w_hi, preferred_element_type=F32)
                  + jnp.dot(c_hi, w_lo, preferred_element_type=F32)) + b_ref[...]


def _modulation(c, w, b):
    bsz, d = c.shape
    n = w.shape[1]
    tn = 1536
    return pl.pallas_call(
        _mod_kernel,
        out_shape=jax.ShapeDtypeStruct((bsz, n), F32),
        grid=(n // tn,),
        in_specs=[pl.BlockSpec((bsz, d), lambda j: (0, 0)),
                  pl.BlockSpec((d, tn), lambda j: (0, j)),
                  pl.BlockSpec((1, tn), lambda j: (0, j))],
        out_specs=pl.BlockSpec((bsz, tn), lambda j: (0, j)),
        compiler_params=_cparams(("arbitrary",)),
        name="adaln_mod",
    )(c, w, b.reshape(1, n))


def _rope_kernel(pos_ref, freq_ref, c_ref, sa_ref, sb_ref):
    ang = pos_ref[...] * freq_ref[...]
    lane = lax.broadcasted_iota(jnp.int32, ang.shape, 1)
    cos, sin = jnp.cos(ang), jnp.sin(ang)
    c_ref[...] = jnp.where(lane < ROPE_DIM, cos, 1.0)
    sa_ref[...] = jnp.where(lane < ROPE_HALF, -sin, 0.0)
    sb_ref[...] = jnp.where((lane >= ROPE_HALF) & (lane < ROPE_DIM), sin, 0.0)


def _rope_tables(positions):
    t = positions.size
    tm = 2048
    inv_freq = ROPE_THETA ** (-jnp.arange(0, ROPE_DIM, 2, dtype=F32) / ROPE_DIM)
    freq = jnp.concatenate([inv_freq, inv_freq, jnp.zeros((LANES - ROPE_DIM,), F32)]).reshape(1, LANES)
    pos = positions.astype(F32).reshape(t, 1)
    out = jax.ShapeDtypeStruct((t, LANES), F32)
    return pl.pallas_call(
        _rope_kernel,
        out_shape=(out, out, out),
        grid=(t // tm,),
        in_specs=[pl.BlockSpec((tm, 1), lambda i: (i, 0)),
                  pl.BlockSpec((1, LANES), lambda i: (0, 0))],
        out_specs=tuple(pl.BlockSpec((tm, LANES), lambda i: (i, 0)) for _ in range(3)),
        compiler_params=_cparams(("arbitrary",)),
        name="rope_tables",
    )(pos, freq)


def _norm_mod_kernel(x_ref, g_ref, sh_ref, sc_ref, xp_ref, u_ref, slab_ref):
    rows = x_ref.shape[0]
    sub = rows // STREAMS
    nslab = x_ref.shape[1] // LANES
    for c in range(nslab):
        slab_ref[c] = x_ref[:, c * LANES:(c + 1) * LANES]
    for j in range(STREAMS):
        x = jnp.concatenate([slab_ref[c, pl.ds(j, sub, stride=STREAMS), :] for c in range(nslab)], axis=1)
        xp_ref[0, j] = x
        y = x * lax.rsqrt(jnp.mean(x * x, axis=-1, keepdims=True) + NORM_EPS) * g_ref[...]
        u_ref[0, j] = (y * (1.0 + sc_ref[0]) + sh_ref[0]).astype(u_ref.dtype)


def _norm_modulate(x2d, g, shift, scale, seq):
    t, d = x2d.shape
    tm = 512
    per_b = seq // tm
    bsz = shift.shape[0]
    sub = tm // STREAMS
    out_spec = pl.BlockSpec((1, STREAMS, sub, d), lambda i: (i // per_b, 0, i % per_b, 0))
    xp, u = pl.pallas_call(
        _norm_mod_kernel,
        out_shape=(jax.ShapeDtypeStruct((bsz, STREAMS, seq // STREAMS, d), F32),
                   jax.ShapeDtypeStruct((bsz, STREAMS, seq // STREAMS, d), BF16)),
        grid=(t // tm,),
        in_specs=[pl.BlockSpec((tm, d), lambda i: (i, 0)),
                  pl.BlockSpec((1, d), lambda i: (0, 0)),
                  pl.BlockSpec((1, 1, d), lambda i: (i // per_b, 0, 0)),
                  pl.BlockSpec((1, 1, d), lambda i: (i // per_b, 0, 0))],
        out_specs=(out_spec, out_spec),
        scratch_shapes=[pltpu.VMEM((d // LANES, tm, LANES), F32)],
        compiler_params=_cparams(("arbitrary",)),
        name="norm_modulate",
    )(x2d, g.reshape(1, d), shift.reshape(bsz, 1, d), scale.reshape(bsz, 1, d))
    return xp.reshape(t, d), u.reshape(t, d)


def _cast_weight_tile(w_ref, wb_ref):
    rows = 256

    def body(r, carry):
        sl = pl.ds(pl.multiple_of(r * rows, rows), rows)
        wb_ref[sl, :] = w_ref[sl, :].astype(BF16)
        return carry

    lax.fori_loop(0, w_ref.shape[0] // rows, body, 0)


def _proj_kernel(a_ref, w_ref, o_ref, wb_ref):
    @pl.when(pl.program_id(1) == 0)
    def _():
        _cast_weight_tile(w_ref, wb_ref)

    o_ref[...] = jnp.dot(a_ref[...], wb_ref[...], preferred_element_type=F32).astype(o_ref.dtype)


def _rope_project(a_ref, wb_ref, c_ref, sa_ref, sb_ref, o_ref, t):
    c, sa, sb = c_ref[...], sa_ref[...], sb_ref[...]
    pair = 2 * HEAD_DIM
    acc = jnp.dot(a_ref[...], wb_ref[:, t * pair:(t + 1) * pair], preferred_element_type=F32)
    for h in range(2):
        x = acc[:, h * HEAD_DIM:(h + 1) * HEAD_DIM]
        up = pltpu.roll(x, HEAD_DIM - ROPE_HALF, 1)
        dn = pltpu.roll(x, ROPE_HALF, 1)
        col = t * pair + h * HEAD_DIM
        o_ref[:, col:col + HEAD_DIM] = (x * c + up * sa + dn * sb).astype(o_ref.dtype)


def _plain_project(a_ref, wb_ref, o_ref, t):
    cols = slice(t * 2 * LANES, (t + 1) * 2 * LANES)
    o_ref[:, cols] = jnp.dot(a_ref[...], wb_ref[:, cols], preferred_element_type=F32).astype(o_ref.dtype)


def _proj_rope_kernel(a_ref, w_ref, c_ref, sa_ref, sb_ref, o_ref, wb_ref):
    @pl.when(pl.program_id(1) == 0)
    def _():
        _cast_weight_tile(w_ref, wb_ref)

    for t in range(o_ref.shape[1] // (2 * HEAD_DIM)):
        _rope_project(a_ref, wb_ref, c_ref, sa_ref, sb_ref, o_ref, t)


def _project(u, w, col0, ncols, rope=None):
    t, d = u.shape
    tm, tn = 1024, 1024
    j0 = col0 // tn
    in_specs = [pl.BlockSpec((tm, d), lambda j, i: (i, 0)),
                pl.BlockSpec((d, tn), lambda j, i: (0, j0 + j))]
    args = [u, w]
    if rope is not None:
        in_specs += [pl.BlockSpec((tm, LANES), lambda j, i: (i, 0)) for _ in range(3)]
        args += list(rope)
    return pl.pallas_call(
        _proj_kernel if rope is None else _proj_rope_kernel,
        out_shape=jax.ShapeDtypeStruct((t, ncols), BF16),
        grid=(ncols // tn, t // tm),
        in_specs=in_specs,
        out_specs=pl.BlockSpec((tm, tn), lambda j, i: (i, j)),
        scratch_shapes=[pltpu.VMEM((d, tn), BF16)],
        compiler_params=_cparams(("arbitrary", "arbitrary")),
        name="in_proj" if rope is None else "in_proj_rope",
    )(*args)


def _key_pieces(qb, sub):
    L = ATTN_BLOCK
    pieces = [(0, 0, L * (qb + 1))]
    for r in range(1, STREAMS):
        back = 0 if qb == 0 else (L // 2 if r == 1 else L // 4)
        pieces.append((r, L * qb - back, L + back))
    return pieces


def _attention_bias(sub):
    L = ATTN_BLOCK
    nqb = sub // L
    kmax = max(sum(p[2] for p in _key_pieces(qb, sub)) for qb in range(nqb))
    qi = jnp.arange(L, dtype=jnp.int32)[:, None]
    out = []
    for j in range(STREAMS):
        row = []
        for qb in range(nqb):
            cols = []
            for r, start, size in _key_pieces(qb, sub):
                jp = (j + r) % STREAMS
                kn = start + jnp.arange(size, dtype=jnp.int32)[None, :]
                dt = STREAMS * (L * qb + qi - kn) + (j - jp)
                cnt = jnp.zeros(dt.shape, F32)
                for window, dil in DILATIONS:
                    cnt += ((dt >= 0) & (dt <= window) & (dt % dil == 0)).astype(F32)
                cols.append(jnp.log2(cnt))
            tile = jnp.concatenate(cols, axis=1)
            row.append(jnp.pad(tile, ((0, 0), (0, kmax - tile.shape[1])), constant_values=-jnp.inf))
        out.append(jnp.stack(row))
    return jnp.stack(out)


def _attn_pipeline(q_ref, k_ref, v_ref, bias_ref, o_ref, bufs, j, interleave):
    L = ATTN_BLOCK
    sub = o_ref.shape[1] // STREAMS
    scale = HEAD_DIM ** -0.5 * 1.4426950408889634
    s_buf, m_buf, p_buf = bufs[0:2], bufs[2:4], bufs[4:6]
    units = [(g, qb) for g in range(HEADS_PER_STEP) for qb in range(sub // L)]
    own = pl.multiple_of(j * sub, sub)

    def key_rows(qb):
        rows = []
        for r, start, size in _key_pieces(qb, sub):
            base = own if r == 0 else pl.multiple_of(((j + r) % STREAMS) * sub, sub)
            rows.append(pl.ds(base + start, size))
        return rows, sum(p[2] for p in _key_pieces(qb, sub))

    def score_stage(u):
        g, qb = units[u]
        lanes = slice(g * HEAD_DIM, (g + 1) * HEAD_DIM)
        rows, ktot = key_rows(qb)
        q = q_ref[0, pl.ds(own + L * qb, L), lanes]
        k_all = jnp.concatenate([k_ref[0, rw, lanes] for rw in rows], axis=0)
        mx = None
        for c in range(0, ktot, 2 * LANES):
            w = min(2 * LANES, ktot - c)
            s = lax.dot_general(q, k_all[c:c + w], (((1,), (1,)), ((), ())), preferred_element_type=F32)
            s = s * scale + bias_ref[j, qb, :, c:c + w]
            s_buf[u % 2][:, c:c + w] = s
            for cc in range(0, w, LANES):
                part = s[:, cc:cc + LANES]
                mx = part if mx is None else jnp.maximum(mx, part)
        m_buf[u % 2][...] = jnp.broadcast_to(jnp.max(mx, axis=-1, keepdims=True), mx.shape)

    def prob_stage(u):
        _, qb = units[u]
        _, ktot = key_rows(qb)
        m = m_buf[u % 2][...]
        for c in range(0, ktot, LANES):
            p_buf[u % 2][:, c:c + LANES] = jnp.exp2(s_buf[u % 2][:, c:c + LANES] - m).astype(BF16)

    def out_stage(u):
        g, qb = units[u]
        lanes = slice(g * HEAD_DIM, (g + 1) * HEAD_DIM)
        rows, ktot = key_rows(qb)
        v_all = jnp.concatenate([v_ref[0, rw, lanes] for rw in rows], axis=0)
        v_ext = jnp.concatenate([v_all, jnp.ones_like(v_all)], axis=1)
        oe = jnp.dot(p_buf[u % 2][:, :ktot], v_ext, preferred_element_type=F32)
        y = oe[:, :HEAD_DIM] / oe[:, HEAD_DIM:]
        o_ref[0, pl.ds(own + L * qb, L), lanes] = y.astype(o_ref.dtype)

    for t in range(len(units) + 2):
        if t in interleave:
            interleave[t]()
        if t >= 2:
            out_stage(t - 2)
        if 1 <= t <= len(units):
            prob_stage(t - 1)
        if t < len(units):
            score_stage(t)


def _spread(n_steps, pieces, fn):
    every = max(n_steps // pieces, 1)
    return {p * every: functools.partial(fn, p) for p in range(pieces)}


def _proj_attn_kernel(a_ref, w_ref, q_ref, k_ref, v_ref, bias_ref, o_ref, y_ref, wb_ref, *bufs):
    i = pl.program_id(1)
    steps_per_group = STREAMS // ATTN_STREAMS_PER_STEP
    first = ((pl.program_id(0) * pl.num_programs(1) + i) % steps_per_group) * ATTN_STREAMS_PER_STEP

    @pl.when(i == 0)
    def _():
        _cast_weight_tile(w_ref, wb_ref)

    n_units = HEADS_PER_STEP * (y_ref.shape[1] // STREAMS // ATTN_BLOCK)
    pieces = o_ref.shape[1] // (2 * LANES) // ATTN_STREAMS_PER_STEP
    for l in range(ATTN_STREAMS_PER_STEP):
        interleave = _spread(n_units + 2, pieces,
                             lambda p, l=l: _plain_project(a_ref, wb_ref, o_ref, l * pieces + p))
        _attn_pipeline(q_ref, k_ref, v_ref, bias_ref, y_ref, bufs, first + l, interleave)


def _project_and_attention(u, w, col0, ncols, q, k, v, width, bsz, seq):
    t, d = u.shape
    gw = HEADS_PER_STEP * HEAD_DIM
    sub = seq // STREAMS
    tm, tn = 1024, 1024
    n_j, n_i = ncols // tn, t // tm
    n_hg = width // gw
    steps_per_group = STREAMS // ATTN_STREAMS_PER_STEP
    assert n_j * n_i == bsz * n_hg * steps_per_group, "ATTN_STREAMS_PER_STEP streams per projection tile"
    j0 = col0 // tn
    bias = _attention_bias(sub)

    def attn_pos(j, i):
        s = (j * n_i + i) // steps_per_group
        return s // n_hg, s % n_hg

    def attn_cols(src):
        a, c0 = src
        off = c0 // gw
        return a.reshape(bsz, seq, a.shape[1]), pl.BlockSpec(
            (1, seq, gw), lambda j, i: (attn_pos(j, i)[0], 0, off + attn_pos(j, i)[1]))

    (qa, qs), (ka, ks), (va, vs) = (attn_cols(s) for s in (q, k, v))
    proj, y = pl.pallas_call(
        _proj_attn_kernel,
        out_shape=(jax.ShapeDtypeStruct((t, ncols), BF16), jax.ShapeDtypeStruct((bsz, seq, width), BF16)),
        grid=(n_j, n_i),
        in_specs=[pl.BlockSpec((tm, d), lambda j, i: (i, 0)),
                  pl.BlockSpec((d, tn), lambda j, i: (0, j0 + j)),
                  qs, ks, vs,
                  pl.BlockSpec(bias.shape, lambda j, i: (0, 0, 0, 0), pipeline_mode=pl.Buffered(1))],
        out_specs=(pl.BlockSpec((tm, tn), lambda j, i: (i, j)),
                   pl.BlockSpec((1, seq, gw), lambda j, i: (attn_pos(j, i)[0], 0, attn_pos(j, i)[1]))),
        scratch_shapes=[pltpu.VMEM((d, tn), BF16)]
        + [pltpu.VMEM((ATTN_BLOCK, bias.shape[-1]), F32)] * 2
        + [pltpu.VMEM((ATTN_BLOCK, LANES), F32)] * 2
        + [pltpu.VMEM((ATTN_BLOCK, bias.shape[-1]), BF16)] * 2,
        compiler_params=_cparams(("arbitrary", "arbitrary")),
        name="proj_attention",
    )(u, w, qa, ka, va, bias)
    return proj, y.reshape(t, width)


def _rec_scratch():
    C, K = REC_CHUNK, HEAD_DIM
    return [pltpu.VMEM((K, K), F32), pltpu.VMEM((C, K), F32), pltpu.VMEM((C, K), F32),
            pltpu.VMEM((C, C), BF16), pltpu.VMEM((C, K), BF16), pltpu.VMEM((K, K), F32),
            pltpu.VMEM((1, K), F32)]


def _rec_pipeline(q_ref, f_ref, i_ref, g_ref, lb_ref, ng_ref, o_ref, scratch, c0, n, interleave):
    C = REC_CHUNK
    seq = q_ref.shape[1]
    sub = seq // STREAMS
    piece = C // STREAMS
    pi = lax.broadcasted_iota(jnp.int32, (C, C), 0)
    si = lax.broadcasted_iota(jnp.int32, (C, C), 1)
    time_of = lambda p: STREAMS * (p % piece) + p // piece
    causal = time_of(si) <= time_of(pi)
    tri = jnp.where(causal, 1.0, 0.0).astype(BF16)
    row_of = lambda tau: (tau % STREAMS) * piece + tau // STREAMS
    last = row_of(C - 1)
    anchor = row_of(C // 2 - 1)
    per_head = len(scratch) // REC_HEADS_PER_STEP
    tri2 = jnp.concatenate([tri, tri], axis=1)

    def head(g):
        return slice(g * HEAD_DIM, (g + 1) * HEAD_DIM), scratch[g * per_head:(g + 1) * per_head]

    def rows(c, j):
        start = j * sub + c * piece
        return pl.ds(start if isinstance(start, int) else pl.multiple_of(start, piece), piece)

    def load(ref, c, lanes):
        return jnp.concatenate([ref[0, rows(c, j), lanes] for j in range(STREAMS)], axis=0)

    def gates(c, g):
        lanes, (_, b_s, kk_s, _, _, _, _) = head(g)
        lb = lb_ref[:, lanes]
        x = load(f_ref, c, lanes).astype(F32)
        z = jnp.exp(-jnp.abs(x))
        r = 1.0 / (1.0 + z)
        pos = x >= 0
        sig_p = jnp.where(pos, r, z * r)
        sig_n = jnp.where(pos, z * r, r)
        logf = jnp.log(lb + (1.0 - lb) * sig_p)
        hi = logf.astype(BF16)
        lo = (logf - hi.astype(F32)).astype(BF16)
        b_s[...] = jnp.dot(tri2, jnp.concatenate([hi, lo], axis=0), preferred_element_type=F32)
        kk_s[...] = (1.0 - lb) * sig_n

    def scores(c, g):
        lanes, (_, b_s, kk_s, sc_s, qin_s, upd_s, dec_s) = head(g)
        b, kk = b_s[...], kk_s[...]
        b_last = b[last:last + 1, :]
        b_mid = b[anchor:anchor + 1, :]
        qs = _silu(load(q_ref, c, lanes).astype(F32))
        v = load(i_ref, c, lanes)
        q_a = (qs * jnp.exp(b - b_mid)).astype(BF16)
        k_a = (kk * jnp.exp(b_mid - b)).astype(BF16)
        k_e = (kk * jnp.exp(b_last - b)).astype(BF16)
        sc = lax.dot_general(q_a, k_a, (((1,), (1,)), ((), ())), preferred_element_type=F32)
        sc_s[...] = jnp.where(causal, sc, 0.0).astype(BF16)
        qin_s[...] = (qs * jnp.exp(b)).astype(BF16)
        upd_s[...] = lax.dot_general(v, k_e, (((0,), (0,)), ((), ())), preferred_element_type=F32)
        dec_s[...] = jnp.exp(b_last)

    def output(c, g):
        lanes, (st_ref, _, _, sc_s, qin_s, upd_s, dec_s) = head(g)
        v = load(i_ref, c, lanes)
        st = st_ref[...]
        o = (jnp.dot(sc_s[...], v, preferred_element_type=F32)
             + lax.dot_general(qin_s[...], st.astype(BF16), (((1,), (1,)), ((), ())),
                               preferred_element_type=F32))
        st_ref[...] = dec_s[...] * st + upd_s[...]
        y = o * lax.rsqrt(jnp.mean(o * o, axis=-1, keepdims=True) + NORM_EPS) * ng_ref[...]
        y = (y * _silu(load(g_ref, c, lanes).astype(F32))).astype(o_ref.dtype)
        for j in range(STREAMS):
            o_ref[0, rows(c, j), lanes] = y[j * piece:(j + 1) * piece]

    for t in range(n + 2):
        if t in interleave:
            interleave[t]()
        for g in range(REC_HEADS_PER_STEP):
            if t >= 2:
                output(c0 + (t - 2), g)
            if 1 <= t <= n:
                scores(c0 + (t - 1), g)
            if t < n:
                gates(c0 + t, g)


def _proj_rec_kernel(a_ref, w_ref, q_ref, f_ref, i_ref, g_ref, lb_ref, ng_ref, o_ref, y_ref, wb_ref, *scratch):
    n_i = pl.num_programs(1)
    i = pl.program_id(1)
    steps_per_seq = q_ref.shape[1] // (REC_CHUNK * REC_CHUNKS_PER_STEP)
    group = (pl.program_id(0) * n_i + i) % steps_per_seq

    @pl.when(i == 0)
    def _():
        _cast_weight_tile(w_ref, wb_ref)

    @pl.when(group == 0)
    def _():
        per_head = len(scratch) // REC_HEADS_PER_STEP
        for g in range(REC_HEADS_PER_STEP):
            scratch[g * per_head][...] = jnp.zeros_like(scratch[g * per_head])

    interleave = _spread(REC_CHUNKS_PER_STEP + 2, o_ref.shape[1] // (2 * LANES),
                         functools.partial(_plain_project, a_ref, wb_ref, o_ref))
    _rec_pipeline(q_ref, f_ref, i_ref, g_ref, lb_ref, ng_ref, y_ref, scratch,
                  group * REC_CHUNKS_PER_STEP, REC_CHUNKS_PER_STEP, interleave)


def _project_and_recurrence(u, w, col0, ncols, q_r, f_r, i_r, g_r, lower_bound, norm_g, bsz, seq):
    t, d = u.shape
    width = lower_bound.shape[0]
    gw = REC_HEADS_PER_STEP * HEAD_DIM
    tm, tn = 1024, 1024
    j0 = col0 // tn
    n_j, n_i = ncols // tn, t // tm
    n_hg = width // gw
    steps_per_seq = seq // (REC_CHUNK * REC_CHUNKS_PER_STEP)
    assert n_j * n_i == bsz * n_hg * steps_per_seq, "one recurrence slice per projection tile"

    def rec_pos(j, i):
        s = (j * n_i + i) // steps_per_seq
        return s // n_hg, s % n_hg

    def rec_cols(src):
        a, col0 = src
        off = col0 // gw
        return a.reshape(bsz, seq, a.shape[1]), pl.BlockSpec(
            (1, seq, gw), lambda j, i: (rec_pos(j, i)[0], 0, off + rec_pos(j, i)[1]))

    (qa, qs), (fa, fs), (ia, isp), (ga, gs) = (rec_cols(s) for s in (q_r, f_r, i_r, g_r))
    proj, y = pl.pallas_call(
        _proj_rec_kernel,
        out_shape=(jax.ShapeDtypeStruct((t, ncols), BF16), jax.ShapeDtypeStruct((bsz, seq, width), BF16)),
        grid=(n_j, n_i),
        in_specs=[pl.BlockSpec((tm, d), lambda j, i: (i, 0)),
                  pl.BlockSpec((d, tn), lambda j, i: (0, j0 + j)),
                  qs, fs, isp, gs,
                  pl.BlockSpec((1, gw), lambda j, i: (0, rec_pos(j, i)[1])),
                  pl.BlockSpec((1, HEAD_DIM), lambda j, i: (0, 0))],
        out_specs=(pl.BlockSpec((tm, tn), lambda j, i: (i, j)),
                   pl.BlockSpec((1, seq, gw), lambda j, i: (rec_pos(j, i)[0], 0, rec_pos(j, i)[1]))),
        scratch_shapes=[pltpu.VMEM((d, tn), BF16)]
        + [s for _ in range(REC_HEADS_PER_STEP) for s in _rec_scratch()],
        compiler_params=_cparams(("arbitrary", "arbitrary")),
        name="proj_hgrn2",
    )(u, w, qa, fa, ia, ga, lower_bound.reshape(1, width), norm_g.reshape(1, HEAD_DIM))
    return proj, y.reshape(t, width)


def _merge_kernel(ya_ref, yr_ref, wa_ref, wr_ref, ga_ref, gr_ref, o_ref, wab_ref, wrb_ref):
    @pl.when(pl.program_id(1) == 0)
    def _():
        _cast_weight_tile(wa_ref, wab_ref)
        _cast_weight_tile(wr_ref, wrb_ref)

    a = jnp.dot(ya_ref[...], wab_ref[...], preferred_element_type=F32)
    r = jnp.dot(yr_ref[...], wrb_ref[...], preferred_element_type=F32)
    m = _sigmoid(ga_ref[...].astype(F32)) * a + _sigmoid(gr_ref[...].astype(F32)) * r
    o_ref[...] = m.astype(o_ref.dtype)


def _merge(ya, yr, wa, wr, ga, gr):
    t, d = ya.shape
    n = wa.shape[1]
    tm, tn = 1024, 512
    row = pl.BlockSpec((tm, d), lambda j, i: (i, 0))
    col = pl.BlockSpec((d, tn), lambda j, i: (0, j))
    ga_off, gr_off = ga[1] // tn, gr[1] // tn
    return pl.pallas_call(
        _merge_kernel,
        out_shape=jax.ShapeDtypeStruct((t, n), BF16),
        grid=(n // tn, t // tm),
        in_specs=[row, row, col, col,
                  pl.BlockSpec((tm, tn), lambda j, i: (i, ga_off + j)),
                  pl.BlockSpec((tm, tn), lambda j, i: (i, gr_off + j))],
        out_specs=pl.BlockSpec((tm, tn), lambda j, i: (i, j)),
        scratch_shapes=[pltpu.VMEM((d, tn), BF16)] * 2,
        compiler_params=_cparams(("arbitrary", "arbitrary")),
        name="branch_merge",
    )(ya, yr, wa, wr, ga[0], gr[0])


def _mixout_kernel(m_ref, w_ref, x_ref, gt_ref, g_ref, sh_ref, sc_ref, rw_ref, rb_ref,
                   h_ref, u_ref, route_ref, wb_ref):
    @pl.when(pl.program_id(0) == 0)
    def _():
        _cast_weight_tile(w_ref, wb_ref)

    mix = jnp.dot(m_ref[...], wb_ref[...], preferred_element_type=F32)
    h = x_ref[...] + gt_ref[0] * mix
    h_ref[...] = h
    u = h * lax.rsqrt(jnp.mean(h * h, axis=-1, keepdims=True) + NORM_EPS) * g_ref[...]
    u = u * (1.0 + sc_ref[0]) + sh_ref[0]
    wc = 2 * u_ref.shape[2]
    for c in range(u_ref.shape[0]):
        u_ref[c] = _pack_bf16_pairs(u[:, c * wc:(c + 1) * wc])
    u_hi = u.astype(BF16)
    u_lo = (u - u_hi.astype(F32)).astype(BF16)
    rw = rw_ref[...]
    w_hi = rw.astype(BF16)
    w_lo = (rw - w_hi.astype(F32)).astype(BF16)
    logits = (jnp.dot(u_hi, w_hi, preferred_element_type=F32)
              + jnp.dot(u_lo, w_hi, preferred_element_type=F32)
              + jnp.dot(u_hi, w_lo, preferred_element_type=F32)) + rb_ref[...]
    lane = lax.broadcasted_iota(jnp.int32, logits.shape, 1).astype(F32)
    big = float(LANES)
    neg = -jnp.inf
    lg = jnp.where(lane < N_GROUPS, logits, neg)
    mg = jnp.max(lg, axis=-1, keepdims=True)
    g_sel = jnp.min(jnp.where(lg == mg, lane, big), axis=-1, keepdims=True)
    p_group = 1.0 / jnp.sum(jnp.exp(lg - mg), axis=-1, keepdims=True)
    lo = N_GROUPS + EXPERTS_PER_GROUP * g_sel
    le = jnp.where((lane >= lo) & (lane < lo + EXPERTS_PER_GROUP), logits, neg)
    t1 = jnp.max(le, axis=-1, keepdims=True)
    i1 = jnp.min(jnp.where(le == t1, lane, big), axis=-1, keepdims=True)
    le2 = jnp.where(lane == i1, neg, le)
    t2 = jnp.max(le2, axis=-1, keepdims=True)
    i2 = jnp.min(jnp.where(le2 == t2, lane, big), axis=-1, keepdims=True)
    e21 = jnp.exp(t2 - t1)
    w1 = p_group / (1.0 + e21)
    w2 = p_group * e21 / (1.0 + e21)
    route = jnp.where(lane == 0, i1 - N_GROUPS,
                      jnp.where(lane == 1, i2 - N_GROUPS,
                                jnp.where(lane == 2, w1, jnp.where(lane == 3, w2, 0.0))))
    route_ref[...] = route


def _mix_out(merged, w_out, x2d, gate, g, shift, scale, rw, rb, seq):
    t, d = x2d.shape
    tm = 512
    per_b = seq // tm
    bsz = gate.shape[0]
    row = lambda dt: pl.BlockSpec((tm, d), lambda i: (i, 0))
    per_batch = pl.BlockSpec((1, 1, d), lambda i: (i // per_b, 0, 0))
    const = lambda shape: pl.BlockSpec(shape, lambda i: (0,) * len(shape))
    wc = d // GATHER_CHUNKS // 2
    outs = pl.pallas_call(
        _mixout_kernel,
        out_shape=(jax.ShapeDtypeStruct((t, d), F32),
                   jax.ShapeDtypeStruct((GATHER_CHUNKS, t, wc), jnp.uint32),
                   jax.ShapeDtypeStruct((t, LANES), F32)),
        grid=(t // tm,),
        in_specs=[row(BF16), pl.BlockSpec((d, d), lambda i: (0, 0), pipeline_mode=pl.Buffered(1)),
                  row(F32), per_batch, const((1, d)), per_batch, per_batch,
                  const((d, LANES)), const((1, LANES))],
        out_specs=(row(F32), pl.BlockSpec((GATHER_CHUNKS, tm, wc), lambda i: (0, i, 0)),
                   pl.BlockSpec((tm, LANES), lambda i: (i, 0))),
        scratch_shapes=[pltpu.VMEM((d, d), BF16)],
        compiler_params=_cparams(("arbitrary",)),
        name="mix_out_router",
    )(merged, w_out, x2d, gate.reshape(bsz, 1, d), g.reshape(1, d),
      shift.reshape(bsz, 1, d), scale.reshape(bsz, 1, d), rw, rb)
    return outs


def _expert_kernel(be_ref, nx_ref, nu_ref, *refs):
    x_ref = refs[0]
    w_hbm = refs[1:4]
    o_ref = refs[4]
    stage = refs[5:8]
    wb = refs[8:11]
    sem = refs[11]
    nch = x_ref.shape[0]
    wc = 2 * o_ref.shape[2]
    i = pl.program_id(0)
    e = be_ref[i]
    nxt = nx_ref[i]
    active = i < nu_ref[0]
    first = i == 0
    run_start = jnp.logical_or(first, e != be_ref[jnp.maximum(i - 1, 0)])

    def weight_copies(expert):
        return [pltpu.make_async_copy(w_hbm[k].at[expert], stage[k], sem.at[k]) for k in range(3)]

    @pl.when(jnp.logical_and(active, first))
    def _():
        for cp in weight_copies(e):
            cp.start()

    @pl.when(jnp.logical_and(active, run_start))
    def _():
        for cp in weight_copies(e):
            cp.wait()
        for k in range(3):
            _cast_weight_tile(stage[k], wb[k])

        @pl.when(nxt >= 0)
        def _():
            for cp in weight_copies(nxt):
                cp.start()

    @pl.when(active)
    def _():
        x = jnp.concatenate([_unpack_bf16_pairs(x_ref[c]) for c in range(nch)], axis=1).astype(BF16)
        hg = jnp.dot(x, wb[0][...], preferred_element_type=F32)
        hu = jnp.dot(x, wb[1][...], preferred_element_type=F32)
        hdn = (_silu(hg) * hu).astype(BF16)
        y = jnp.dot(hdn, wb[2][...], preferred_element_type=F32)
        for c in range(nch):
            o_ref[c] = _pack_bf16_pairs(y[:, c * wc:(c + 1) * wc])

    @pl.when(jnp.logical_not(active))
    def _():
        o_ref[...] = jnp.zeros_like(o_ref)


def _expert_ffn(xs, w_gate, w_up, w_down, block_expert, next_expert, n_used):
    nch, n_slots, wc = xs.shape
    d = 2 * wc * nch
    hid = w_gate.shape[2]
    bm = MOE_ROWS
    chunk = pl.BlockSpec((nch, bm, wc), lambda i, be, nx, nu: (0, i, 0))
    hbm = pl.BlockSpec(memory_space=pl.ANY)
    grid_spec = pltpu.PrefetchScalarGridSpec(
        num_scalar_prefetch=3,
        grid=(n_slots // bm,),
        in_specs=[chunk, hbm, hbm, hbm],
        out_specs=chunk,
        scratch_shapes=[pltpu.VMEM((d, hid), F32), pltpu.VMEM((d, hid), F32), pltpu.VMEM((hid, d), F32),
                        pltpu.VMEM((d, hid), BF16), pltpu.VMEM((d, hid), BF16), pltpu.VMEM((hid, d), BF16),
                        pltpu.SemaphoreType.DMA((3,))],
    )
    return pl.pallas_call(
        _expert_kernel,
        out_shape=jax.ShapeDtypeStruct((nch, n_slots, wc), jnp.uint32),
        grid_spec=grid_spec,
        compiler_params=_cparams(("arbitrary",)),
        name="expert_ffn",
    )(block_expert, next_expert, n_used, xs, w_gate, w_up, w_down)


def _gather_rows(chunks, idx):
    nch, n, d = chunks.shape
    table = chunks.reshape(nch * n, d)
    idx = (idx[None, :] + (jnp.arange(nch, dtype=jnp.int32) * n)[:, None]).reshape(-1)
    m = idx.shape[0]
    window = LANES
    mesh = plsc.VectorSubcoreMesh(core_axis_name="core", subcore_axis_name="subcore")

    @pl.kernel(out_type=jax.ShapeDtypeStruct((m, d), table.dtype), mesh=mesh, scratch_types=[])
    def gather(x_hbm, i_hbm, o_hbm):
        def body(i_vmem, o_vmem):
            pltpu.sync_copy(x_hbm.at[i_vmem.at[0]], o_vmem)

        pltpu.emit_pipeline(
            body,
            grid=(m // window,),
            in_specs=[pl.BlockSpec((1, window), lambda i: (0, i))],
            out_specs=[pl.BlockSpec((window, d), lambda i: (i, 0))],
            core_axis_name=("core", "subcore"),
            dimension_semantics=(pltpu.PARALLEL,),
        )(i_hbm, o_hbm)

    return gather(table, idx.reshape(1, m)).reshape(nch, m // nch, d)


def _final_kernel(h_ref, *rest):
    y_refs = rest[:TOP_K]
    rt_ref, gt_ref, g_ref = rest[TOP_K:TOP_K + 3]
    o_ref, slab_ref = rest[-2:]
    sub = h_ref.shape[2]
    nslab = h_ref.shape[3] // LANES
    for j in range(STREAMS):
        route = rt_ref[0, j]
        ffn = sum(route[:, TOP_K + k:TOP_K + k + 1]
                  * jnp.concatenate([_unpack_bf16_pairs(y_refs[k][c, 0, 0, j])
                                     for c in range(GATHER_CHUNKS)], axis=1) for k in range(TOP_K))
        h = h_ref[0, j] + gt_ref[0] * ffn
        y = h * lax.rsqrt(jnp.mean(h * h, axis=-1, keepdims=True) + NORM_EPS) * g_ref[...]
        for c in range(nslab):
            slab_ref[c, pl.ds(j, sub, stride=STREAMS), :] = y[:, c * LANES:(c + 1) * LANES]
    for c in range(nslab):
        o_ref[:, c * LANES:(c + 1) * LANES] = slab_ref[c]


def _final(h, y2, route, gate, g, seq, part, parts, prev=None):
    t, d = h.shape
    tm = 512
    per_b = seq // tm
    bsz = gate.shape[0]
    sub = tm // STREAMS
    steps = t // tm // parts
    first = part * steps
    spec = pl.BlockSpec((1, STREAMS, sub, d), lambda i: ((first + i) // per_b, 0, (first + i) % per_b, 0))
    wc = d // GATHER_CHUNKS // 2
    y6 = y2.reshape(GATHER_CHUNKS, TOP_K, bsz // parts, STREAMS, seq // STREAMS, wc)

    def yspec(k):
        return pl.BlockSpec((GATHER_CHUNKS, 1, 1, STREAMS, sub, wc),
                            lambda i: (0, k, i // per_b, 0, i % per_b, 0))

    in_specs = [spec, *[yspec(k) for k in range(TOP_K)],
                pl.BlockSpec((1, STREAMS, sub, LANES),
                             lambda i: ((first + i) // per_b, 0, (first + i) % per_b, 0)),
                pl.BlockSpec((1, 1, d), lambda i: ((first + i) // per_b, 0, 0)),
                pl.BlockSpec((1, d), lambda i: (0, 0))]
    args = [h.reshape(bsz, STREAMS, seq // STREAMS, d), *([y6] * TOP_K),
            route.reshape(bsz, STREAMS, seq // STREAMS, LANES), gate.reshape(bsz, 1, d), g.reshape(1, d)]
    aliases = {}
    if prev is not None:
        in_specs.append(pl.BlockSpec(memory_space=pl.ANY))
        args.append(prev)
        aliases = {len(args) - 1: 0}
    return pl.pallas_call(
        _final_kernel,
        out_shape=jax.ShapeDtypeStruct((t, d), F32),
        grid=(steps,),
        in_specs=in_specs,
        out_specs=pl.BlockSpec((tm, d), lambda i: (first + i, 0)),
        scratch_shapes=[pltpu.VMEM((d // LANES, tm, LANES), F32)],
        input_output_aliases=aliases,
        compiler_params=_cparams(("arbitrary",)),
        name="final_norm",
    )(*args)


def _dispatch_plan(expert_idx):
    n_assign = expert_idx.size
    n_blocks = n_assign // MOE_ROWS + N_EXPERTS
    n_slots = n_blocks * MOE_ROWS
    flat_e = expert_idx.reshape(-1)
    ids = jnp.arange(n_assign, dtype=jnp.int32)
    eids = jnp.arange(N_EXPERTS, dtype=jnp.int32)[None, :]
    _, order = lax.sort((flat_e, ids), num_keys=1, is_stable=True)
    _, rank_sorted = lax.sort((order, ids), num_keys=1)
    hot_a = (flat_e[:, None] == eids).astype(jnp.int32)
    counts = jnp.sum(hot_a, axis=0)
    padded = ((counts + MOE_ROWS - 1) // MOE_ROWS) * MOE_ROWS
    pad_end = jnp.cumsum(padded)
    pad_start = pad_end - padded
    start = jnp.cumsum(counts) - counts
    dest = (rank_sorted + jnp.sum(hot_a * (pad_start - start)[None, :], axis=1)).reshape(-1, TOP_K)
    blk0 = jnp.arange(n_blocks, dtype=jnp.int32) * MOE_ROWS
    block_expert = jnp.minimum(jnp.sum((pad_end[None, :] <= blk0[:, None]).astype(jnp.int32), axis=1),
                               N_EXPERTS - 1)
    hot_b = (block_expert[:, None] == eids).astype(jnp.int32)
    blk_shift = jnp.sum(hot_b * (start - pad_start)[None, :], axis=1)
    blk_count = jnp.sum(hot_b * (pad_start + counts)[None, :], axis=1)
    slot = jnp.arange(n_slots, dtype=jnp.int32).reshape(n_blocks, MOE_ROWS)
    valid = (slot < blk_count[:, None]).reshape(-1)
    src = jnp.clip(slot + blk_shift[:, None], 0, n_assign - 1).reshape(-1)
    assign = order[src]
    token_of_slot = jnp.where(valid, assign // TOP_K, slot.reshape(-1) % (n_assign // TOP_K))
    n_used = (pad_end[-1:] // MOE_ROWS).astype(jnp.int32)
    bi = jnp.arange(n_blocks, dtype=jnp.int32)
    later = ((bi[None, :] > bi[:, None]) & (block_expert[None, :] != block_expert[:, None])
             & (bi[None, :] < n_used[0]))
    next_expert = jnp.where(jnp.any(later, axis=1), block_expert[jnp.argmax(later, axis=1)], -1)
    return token_of_slot, dest, block_expert, next_expert.astype(jnp.int32), n_used


def kernel(x, c, positions, ada_w, ada_b, mix_norm_g, w_in, w_attn_branch, w_rec_branch, w_mix_out,
           rec_norm_g, rec_lb_logits, ffn_norm_g, router_group_w, router_group_b, router_expert_w,
           router_expert_b, expert_w_gate, expert_w_up, expert_w_down, final_norm_g):
    bsz, seq, d = x.shape
    t = bsz * seq
    depth = ada_w.shape[0]
    assert depth == 1, "final norm is fused after the single layer"
    lower_bounds = jnp.cumsum(jax.nn.softmax(rec_lb_logits.astype(F32), axis=0), axis=0)
    pos_streams = positions.reshape(bsz, seq // STREAMS, STREAMS).transpose(0, 2, 1)
    rope = _rope_tables(pos_streams)
    h = x.reshape(t, d)
    for layer in range(depth):
        mod = _modulation(c, ada_w[layer], ada_b[layer])
        sh_m, sc_m, gt_m, sh_f, sc_f, gt_f = jnp.split(mod, 6, axis=-1)
        h, u = _norm_modulate(h, mix_norm_g[layer], sh_m, sc_m, seq)
        w = w_in[layer]
        qk = _project(u, w, 0, 2 * d, rope=rope)
        v_a = _project(u, w, 2 * d, d)
        rec_in, y_attn = _project_and_attention(u, w, 3 * d, 4 * d, (qk, 0), (qk, d), (v_a, 0),
                                                d, bsz, seq)
        gates, y_rec = _project_and_recurrence(u, w, 7 * d, 2 * d, (rec_in, 0), (rec_in, d), (rec_in, 2 * d),
                                               (rec_in, 3 * d), lower_bounds[layer], rec_norm_g[layer],
                                               bsz, seq)
        merged = _merge(y_attn, y_rec, w_attn_branch[layer], w_rec_branch[layer], (gates, 0), (gates, d))
        rw = jnp.concatenate([router_group_w[layer], router_expert_w[layer],
                              jnp.zeros((d, LANES - N_GROUPS - N_EXPERTS), F32)], axis=1)
        rb = jnp.concatenate([router_group_b[layer], router_expert_b[layer],
                              jnp.zeros((LANES - N_GROUPS - N_EXPERTS,), F32)]).reshape(1, LANES)
        h, u2, route = _mix_out(merged, w_mix_out[layer], h, gt_m, ffn_norm_g[layer],
                                sh_f, sc_f, rw, rb, seq)
        expert_idx = route[:, :TOP_K].astype(jnp.int32)
        tok, dest, block_expert, next_expert, n_used = _dispatch_plan(expert_idx)
        xs = _gather_rows(u2, tok)
        ys = _expert_ffn(xs, expert_w_gate[layer], expert_w_up[layer], expert_w_down[layer],
                         block_expert, next_expert, n_used)
        parts = 4
        out = None
        for part in range(parts):
            tok_part = dest[part * (t // parts):(part + 1) * (t // parts)]
            y2 = _gather_rows(ys, tok_part.T.reshape(-1))
            out = _final(h, y2, route, gt_f, final_norm_g, seq, part, parts, prev=out)
        h = out
    return h.reshape(bsz, seq, d)
```

```python
import functools

import jax
import jax.numpy as jnp
from jax import lax
from jax.experimental import pallas as pl
from jax.experimental.pallas import tpu as pltpu
from jax.experimental.pallas import tpu_sc as plsc

F32 = jnp.float32
BF16 = jnp.bfloat16

HEAD_DIM = 128
ROPE_DIM = HEAD_DIM // 4
ROPE_HALF = ROPE_DIM // 2
ROPE_THETA = 500000.0
ATTN_BLOCK = 128
REC_CHUNK = 64
N_GROUPS = 4
EXPERTS_PER_GROUP = 8
N_EXPERTS = N_GROUPS * EXPERTS_PER_GROUP
TOP_K = 2
NORM_EPS = 1e-6

LANES = 128
VMEM_LIMIT = 56 * 1024 * 1024

MOE_ROWS = 256
EXPERT_PARTS = 2
HEADS_PER_STEP = 2
ATTN_STREAMS_PER_STEP = 2
REC_HEADS_PER_STEP = 4
REC_CHUNKS_PER_STEP = 16
GATHER_CHUNKS = 4
STREAMS = 4
DILATIONS = ((128, 1), (512, 4), (2048, 16))


def _cparams(sem):
    return pltpu.CompilerParams(dimension_semantics=sem, vmem_limit_bytes=VMEM_LIMIT)


def _sigmoid(x):
    return 1.0 / (1.0 + jnp.exp(-x))


def _silu(x):
    return x * _sigmoid(x)


def _pack_bf16_pairs(x):
    w = x.shape[1] // 2
    lo = lax.bitcast_convert_type(x[:, :w].astype(BF16).astype(F32), jnp.uint32)
    hi = lax.bitcast_convert_type(x[:, w:].astype(BF16).astype(F32), jnp.uint32)
    return (lo >> 16) | (hi & jnp.uint32(0xFFFF0000))


def _unpack_bf16_pairs(words):
    lo = lax.bitcast_convert_type(words << 16, F32)
    hi = lax.bitcast_convert_type(words & jnp.uint32(0xFFFF0000), F32)
    return jnp.concatenate([lo, hi], axis=1)


def _mod_kernel(c_ref, w_ref, b_ref, o_ref):
    cond = _silu(c_ref[...])
    w = w_ref[...]
    c_hi = cond.astype(BF16)
    c_lo = (cond - c_hi.astype(F32)).astype(BF16)
    w_hi = w.astype(BF16)
    w_lo = (w - w_hi.astype(F32)).astype(BF16)
    o_ref[...] = (jnp.dot(c_hi, w_hi, preferred_element_type=F32)
                  + jnp.dot(c_lo, w_hi, preferred_element_type=F32)
                  + jnp.dot(c_hi, w_lo, preferred_element_type=F32)) + b_ref[...]


def _modulation(c, w, b):
    bsz, d = c.shape
    n = w.shape[1]
    tn = 1536
    return pl.pallas_call(
        _mod_kernel,
        out_shape=jax.ShapeDtypeStruct((bsz, n), F32),
        grid=(n // tn,),
        in_specs=[pl.BlockSpec((bsz, d), lambda j: (0, 0)),
                  pl.BlockSpec((d, tn), lambda j: (0, j)),
                  pl.BlockSpec((1, tn), lambda j: (0, j))],
        out_specs=pl.BlockSpec((bsz, tn), lambda j: (0, j)),
        compiler_params=_cparams(("arbitrary",)),
        name="adaln_mod",
    )(c, w, b.reshape(1, n))


def _rope_kernel(pos_ref, freq_ref, c_ref, sa_ref, sb_ref):
    ang = pos_ref[...] * freq_ref[...]
    lane = lax.broadcasted_iota(jnp.int32, ang.shape, 1)
    cos, sin = jnp.cos(ang), jnp.sin(ang)
    c_ref[...] = jnp.where(lane < ROPE_DIM, cos, 1.0)
    sa_ref[...] = jnp.where(lane < ROPE_HALF, -sin, 0.0)
    sb_ref[...] = jnp.where((lane >= ROPE_HALF) & (lane < ROPE_DIM), sin, 0.0)


def _rope_tables(positions):
    t = positions.size
    tm = 2048
    inv_freq = ROPE_THETA ** (-jnp.arange(0, ROPE_DIM, 2, dtype=F32) / ROPE_DIM)
    freq = jnp.concatenate([inv_freq, inv_freq, jnp.zeros((LANES - ROPE_DIM,), F32)]).reshape(1, LANES)
    pos = positions.astype(F32).reshape(t, 1)
    out = jax.ShapeDtypeStruct((t, LANES), F32)
    return pl.pallas_call(
        _rope_kernel,
        out_shape=(out, out, out),
        grid=(t // tm,),
        in_specs=[pl.BlockSpec((tm, 1), lambda i: (i, 0)),
                  pl.BlockSpec((1, LANES), lambda i: (0, 0))],
        out_specs=tuple(pl.BlockSpec((tm, LANES), lambda i: (i, 0)) for _ in range(3)),
        compiler_params=_cparams(("arbitrary",)),
        name="rope_tables",
    )(pos, freq)


def _norm_mod_kernel(x_ref, g_ref, sh_ref, sc_ref, xp_ref, u_ref, slab_ref):
    rows = x_ref.shape[0]
    sub = rows // STREAMS
    nslab = x_ref.shape[1] // LANES
    for c in range(nslab):
        slab_ref[c] = x_ref[:, c * LANES:(c + 1) * LANES]
    for j in range(STREAMS):
        x = jnp.concatenate([slab_ref[c, pl.ds(j, sub, stride=STREAMS), :] for c in range(nslab)], axis=1)
        xp_ref[0, j] = x
        y = x * lax.rsqrt(jnp.mean(x * x, axis=-1, keepdims=True) + NORM_EPS) * g_ref[...]
        u_ref[0, j] = (y * (1.0 + sc_ref[0]) + sh_ref[0]).astype(u_ref.dtype)


def _norm_modulate(x2d, g, shift, scale, seq):
    t, d = x2d.shape
    tm = 512
    per_b = seq // tm
    bsz = shift.shape[0]
    sub = tm // STREAMS
    out_spec = pl.BlockSpec((1, STREAMS, sub, d), lambda i: (i // per_b, 0, i % per_b, 0))
    xp, u = pl.pallas_call(
        _norm_mod_kernel,
        out_shape=(jax.ShapeDtypeStruct((bsz, STREAMS, seq // STREAMS, d), F32),
                   jax.ShapeDtypeStruct((bsz, STREAMS, seq // STREAMS, d), BF16)),
        grid=(t // tm,),
        in_specs=[pl.BlockSpec((tm, d), lambda i: (i, 0)),
                  pl.BlockSpec((1, d), lambda i: (0, 0)),
                  pl.BlockSpec((1, 1, d), lambda i: (i // per_b, 0, 0)),
                  pl.BlockSpec((1, 1, d), lambda i: (i // per_b, 0, 0))],
        out_specs=(out_spec, out_spec),
        scratch_shapes=[pltpu.VMEM((d // LANES, tm, LANES), F32)],
        compiler_params=_cparams(("arbitrary",)),
        name="norm_modulate",
    )(x2d, g.reshape(1, d), shift.reshape(bsz, 1, d), scale.reshape(bsz, 1, d))
    return xp.reshape(t, d), u.reshape(t, d)


def _cast_weight_tile(w_ref, wb_ref):
    rows = 256

    def body(r, carry):
        sl = pl.ds(pl.multiple_of(r * rows, rows), rows)
        wb_ref[sl, :] = w_ref[sl, :].astype(BF16)
        return carry

    lax.fori_loop(0, w_ref.shape[0] // rows, body, 0)


def _proj_kernel(a_ref, w_ref, o_ref, wb_ref):
    @pl.when(pl.program_id(1) == 0)
    def _():
        _cast_weight_tile(w_ref, wb_ref)

    o_ref[...] = jnp.dot(a_ref[...], wb_ref[...], preferred_element_type=F32).astype(o_ref.dtype)


def _rope_project(a_ref, wb_ref, c_ref, sa_ref, sb_ref, o_ref, t):
    c, sa, sb = c_ref[...], sa_ref[...], sb_ref[...]
    pair = 2 * HEAD_DIM
    acc = jnp.dot(a_ref[...], wb_ref[:, t * pair:(t + 1) * pair], preferred_element_type=F32)
    for h in range(2):
        x = acc[:, h * HEAD_DIM:(h + 1) * HEAD_DIM]
        up = pltpu.roll(x, HEAD_DIM - ROPE_HALF, 1)
        dn = pltpu.roll(x, ROPE_HALF, 1)
        col = t * pair + h * HEAD_DIM
        o_ref[:, col:col + HEAD_DIM] = (x * c + up * sa + dn * sb).astype(o_ref.dtype)


def _plain_project(a_ref, wb_ref, o_ref, t):
    cols = slice(t * 2 * LANES, (t + 1) * 2 * LANES)
    o_ref[:, cols] = jnp.dot(a_ref[...], wb_ref[:, cols], preferred_element_type=F32).astype(o_ref.dtype)


def _proj_rope_kernel(a_ref, w_ref, c_ref, sa_ref, sb_ref, o_ref, wb_ref):
    @pl.when(pl.program_id(1) == 0)
    def _():
        _cast_weight_tile(w_ref, wb_ref)

    for t in range(o_ref.shape[1] // (2 * HEAD_DIM)):
        _rope_project(a_ref, wb_ref, c_ref, sa_ref, sb_ref, o_ref, t)


def _project(u, w, col0, ncols, rope=None):
    t, d = u.shape
    tm, tn = 1024, 1024
    j0 = col0 // tn
    in_specs = [pl.BlockSpec((tm, d), lambda j, i: (i, 0)),
                pl.BlockSpec((d, tn), lambda j, i: (0, j0 + j))]
    args = [u, w]
    if rope is not None:
        in_specs += [pl.BlockSpec((tm, LANES), lambda j, i: (i, 0)) for _ in range(3)]
        args += list(rope)
    return pl.pallas_call(
        _proj_kernel if rope is None else _proj_rope_kernel,
        out_shape=jax.ShapeDtypeStruct((t, ncols), BF16),
        grid=(ncols // tn, t // tm),
        in_specs=in_specs,
        out_specs=pl.BlockSpec((tm, tn), lambda j, i: (i, j)),
        scratch_shapes=[pltpu.VMEM((d, tn), BF16)],
        compiler_params=_cparams(("arbitrary", "arbitrary")),
        name="in_proj" if rope is None else "in_proj_rope",
    )(*args)


def _key_pieces(qb, sub):
    L = ATTN_BLOCK
    pieces = [(0, 0, L * (qb + 1))]
    for r in range(1, STREAMS):
        back = 0 if qb == 0 else (L // 2 if r == 1 else L // 4)
        pieces.append((r, L * qb - back, L + back))
    return pieces


def _attention_bias(sub):
    L = ATTN_BLOCK
    nqb = sub // L
    kmax = max(sum(p[2] for p in _key_pieces(qb, sub)) for qb in range(nqb))
    qi = jnp.arange(L, dtype=jnp.int32)[:, None]
    out = []
    for j in range(STREAMS):
        row = []
        for qb in range(nqb):
            cols = []
            for r, start, size in _key_pieces(qb, sub):
                jp = (j + r) % STREAMS
                kn = start + jnp.arange(size, dtype=jnp.int32)[None, :]
                dt = STREAMS * (L * qb + qi - kn) + (j - jp)
                cnt = jnp.zeros(dt.shape, F32)
                for window, dil in DILATIONS:
                    cnt += ((dt >= 0) & (dt <= window) & (dt % dil == 0)).astype(F32)
                cols.append(jnp.log2(cnt))
            tile = jnp.concatenate(cols, axis=1)
            row.append(jnp.pad(tile, ((0, 0), (0, kmax - tile.shape[1])), constant_values=-jnp.inf))
        out.append(jnp.stack(row))
    return jnp.stack(out)


def _attn_pipeline(q_ref, k_ref, v_ref, bias_ref, o_ref, bufs, j, interleave):
    L = ATTN_BLOCK
    sub = o_ref.shape[1] // STREAMS
    scale = HEAD_DIM ** -0.5 * 1.4426950408889634
    s_buf, m_buf, p_buf = bufs[0:2], bufs[2:4], bufs[4:6]
    units = [(g, qb) for g in range(HEADS_PER_STEP) for qb in range(sub // L)]
    own = pl.multiple_of(j * sub, sub)

    def key_rows(qb):
        rows = []
        for r, start, size in _key_pieces(qb, sub):
            base = own if r == 0 else pl.multiple_of(((j + r) % STREAMS) * sub, sub)
            rows.append(pl.ds(base + start, size))
        return rows, sum(p[2] for p in _key_pieces(qb, sub))

    def score_stage(u):
        g, qb = units[u]
        lanes = slice(g * HEAD_DIM, (g + 1) * HEAD_DIM)
        rows, ktot = key_rows(qb)
        q = q_ref[0, pl.ds(own + L * qb, L), lanes]
        k_all = jnp.concatenate([k_ref[0, rw, lanes] for rw in rows], axis=0)
        mx = None
        for c in range(0, ktot, 2 * LANES):
            w = min(2 * LANES, ktot - c)
            s = lax.dot_general(q, k_all[c:c + w], (((1,), (1,)), ((), ())), preferred_element_type=F32)
            s = s * scale + bias_ref[j, qb, :, c:c + w]
            s_buf[u % 2][:, c:c + w] = s
            for cc in range(0, w, LANES):
                part = s[:, cc:cc + LANES]
                mx = part if mx is None else jnp.maximum(mx, part)
        m_buf[u % 2][...] = jnp.broadcast_to(jnp.max(mx, axis=-1, keepdims=True), mx.shape)

    def prob_stage(u):
        _, qb = units[u]
        _, ktot = key_rows(qb)
        m = m_buf[u % 2][...]
        for c in range(0, ktot, LANES):
            p_buf[u % 2][:, c:c + LANES] = jnp.exp2(s_buf[u % 2][:, c:c + LANES] - m).astype(BF16)

    def out_stage(u):
        g, qb = units[u]
        lanes = slice(g * HEAD_DIM, (g + 1) * HEAD_DIM)
        rows, ktot = key_rows(qb)
        v_all = jnp.concatenate([v_ref[0, rw, lanes] for rw in rows], axis=0)
        v_ext = jnp.concatenate([v_all, jnp.ones_like(v_all)], axis=1)
        oe = jnp.dot(p_buf[u % 2][:, :ktot], v_ext, preferred_element_type=F32)
        y = oe[:, :HEAD_DIM] / oe[:, HEAD_DIM:]
        o_ref[0, pl.ds(own + L * qb, L), lanes] = y.astype(o_ref.dtype)

    for t in range(len(units) + 2):
        if t in interleave:
            interleave[t]()
        if t >= 2:
            out_stage(t - 2)
        if 1 <= t <= len(units):
            prob_stage(t - 1)
        if t < len(units):
            score_stage(t)


def _spread(n_steps, pieces, fn):
    every = max(n_steps // pieces, 1)
    return {p * every: functools.partial(fn, p) for p in range(pieces)}


def _proj_attn_kernel(a_ref, w_ref, q_ref, k_ref, v_ref, bias_ref, o_ref, y_ref, wb_ref, *bufs):
    i = pl.program_id(1)
    steps_per_group = STREAMS // ATTN_STREAMS_PER_STEP
    first = ((pl.program_id(0) * pl.num_programs(1) + i) % steps_per_group) * ATTN_STREAMS_PER_STEP

    @pl.when(i == 0)
    def _():
        _cast_weight_tile(w_ref, wb_ref)

    n_units = HEADS_PER_STEP * (y_ref.shape[1] // STREAMS // ATTN_BLOCK)
    pieces = o_ref.shape[1] // (2 * LANES) // ATTN_STREAMS_PER_STEP
    for l in range(ATTN_STREAMS_PER_STEP):
        interleave = _spread(n_units + 2, pieces,
                             lambda p, l=l: _plain_project(a_ref, wb_ref, o_ref, l * pieces + p))
        _attn_pipeline(q_ref, k_ref, v_ref, bias_ref, y_ref, bufs, first + l, interleave)


def _project_and_attention(u, w, col0, ncols, q, k, v, width, bsz, seq):
    t, d = u.shape
    gw = HEADS_PER_STEP * HEAD_DIM
    sub = seq // STREAMS
    tm, tn = 1024, 1024
    n_j, n_i = ncols // tn, t // tm
    n_hg = width // gw
    steps_per_group = STREAMS // ATTN_STREAMS_PER_STEP
    assert n_j * n_i == bsz * n_hg * steps_per_group, "ATTN_STREAMS_PER_STEP streams per projection tile"
    j0 = col0 // tn
    bias = _attention_bias(sub)

    def attn_pos(j, i):
        s = (j * n_i + i) // steps_per_group
        return s // n_hg, s % n_hg

    def attn_cols(src):
        a, c0 = src
        off = c0 // gw
        return a.reshape(bsz, seq, a.shape[1]), pl.BlockSpec(
            (1, seq, gw), lambda j, i: (attn_pos(j, i)[0], 0, off + attn_pos(j, i)[1]))

    (qa, qs), (ka, ks), (va, vs) = (attn_cols(s) for s in (q, k, v))
    proj, y = pl.pallas_call(
        _proj_attn_kernel,
        out_shape=(jax.ShapeDtypeStruct((t, ncols), BF16), jax.ShapeDtypeStruct((bsz, seq, width), BF16)),
        grid=(n_j, n_i),
        in_specs=[pl.BlockSpec((tm, d), lambda j, i: (i, 0)),
                  pl.BlockSpec((d, tn), lambda j, i: (0, j0 + j)),
                  qs, ks, vs,
                  pl.BlockSpec(bias.shape, lambda j, i: (0, 0, 0, 0), pipeline_mode=pl.Buffered(1))],
        out_specs=(pl.BlockSpec((tm, tn), lambda j, i: (i, j)),
                   pl.BlockSpec((1, seq, gw), lambda j, i: (attn_pos(j, i)[0], 0, attn_pos(j, i)[1]))),
        scratch_shapes=[pltpu.VMEM((d, tn), BF16)]
        + [pltpu.VMEM((ATTN_BLOCK, bias.shape[-1]), F32)] * 2
        + [pltpu.VMEM((ATTN_BLOCK, LANES), F32)] * 2
        + [pltpu.VMEM((ATTN_BLOCK, bias.shape[-1]), BF16)] * 2,
        compiler_params=_cparams(("arbitrary", "arbitrary")),
        name="proj_attention",
    )(u, w, qa, ka, va, bias)
    return proj, y.reshape(t, width)


def _rec_scratch():
    C, K = REC_CHUNK, HEAD_DIM
    return [pltpu.VMEM((K, K), F32), pltpu.VMEM((C, K), F32), pltpu.VMEM((C, K), F32),
            pltpu.VMEM((C, C), BF16), pltpu.VMEM((C, K), BF16), pltpu.VMEM((K, K), F32),
            pltpu.VMEM((1, K), F32)]


def _rec_pipeline(q_ref, f_ref, i_ref, g_ref, lb_ref, ng_ref, o_ref, scratch, c0, n, interleave):
    C = REC_CHUNK
    seq = q_ref.shape[1]
    sub = seq // STREAMS
    piece = C // STREAMS
    pi = lax.broadcasted_iota(jnp.int32, (C, C), 0)
    si = lax.broadcasted_iota(jnp.int32, (C, C), 1)
    time_of = lambda p: STREAMS * (p % piece) + p // piece
    causal = time_of(si) <= time_of(pi)
    tri = jnp.where(causal, 1.0, 0.0).astype(BF16)
    row_of = lambda tau: (tau % STREAMS) * piece + tau // STREAMS
    last = row_of(C - 1)
    anchor = row_of(C // 2 - 1)
    per_head = len(scratch) // REC_HEADS_PER_STEP
    tri2 = jnp.concatenate([tri, tri], axis=1)

    def head(g):
        return slice(g * HEAD_DIM, (g + 1) * HEAD_DIM), scratch[g * per_head:(g + 1) * per_head]

    def rows(c, j):
        start = j * sub + c * piece
        return pl.ds(start if isinstance(start, int) else pl.multiple_of(start, piece), piece)

    def load(ref, c, lanes):
        return jnp.concatenate([ref[0, rows(c, j), lanes] for j in range(STREAMS)], axis=0)

    def gates(c, g):
        lanes, (_, b_s, kk_s, _, _, _, _) = head(g)
        lb = lb_ref[:, lanes]
        x = load(f_ref, c, lanes).astype(F32)
        z = jnp.exp(-jnp.abs(x))
        r = 1.0 / (1.0 + z)
        pos = x >= 0
        sig_p = jnp.where(pos, r, z * r)
        sig_n = jnp.where(pos, z * r, r)
        logf = jnp.log(lb + (1.0 - lb) * sig_p)
        hi = logf.astype(BF16)
        lo = (logf - hi.astype(F32)).astype(BF16)
        b_s[...] = jnp.dot(tri2, jnp.concatenate([hi, lo], axis=0), preferred_element_type=F32)
        kk_s[...] = (1.0 - lb) * sig_n

    def scores(c, g):
        lanes, (_, b_s, kk_s, sc_s, qin_s, upd_s, dec_s) = head(g)
        b, kk = b_s[...], kk_s[...]
        b_last = b[last:last + 1, :]
        b_mid = b[anchor:anchor + 1, :]
        qs = _silu(load(q_ref, c, lanes).astype(F32))
        v = load(i_ref, c, lanes)
        q_a = (qs * jnp.exp(b - b_mid)).astype(BF16)
        k_a = (kk * jnp.exp(b_mid - b)).astype(BF16)
        k_e = (kk * jnp.exp(b_last - b)).astype(BF16)
        sc = lax.dot_general(q_a, k_a, (((1,), (1,)), ((), ())), preferred_element_type=F32)
        sc_s[...] = jnp.where(causal, sc, 0.0).astype(BF16)
        qin_s[...] = (qs * jnp.exp(b)).astype(BF16)
        upd_s[...] = lax.dot_general(v, k_e, (((0,), (0,)), ((), ())), preferred_element_type=F32)
        dec_s[...] = jnp.exp(b_last)

    def output(c, g):
        lanes, (st_ref, _, _, sc_s, qin_s, upd_s, dec_s) = head(g)
        v = load(i_ref, c, lanes)
        st = st_ref[...]
        o = (jnp.dot(sc_s[...], v, preferred_element_type=F32)
             + lax.dot_general(qin_s[...], st.astype(BF16), (((1,), (1,)), ((), ())),
                               preferred_element_type=F32))
        st_ref[...] = dec_s[...] * st + upd_s[...]
        y = o * lax.rsqrt(jnp.mean(o * o, axis=-1, keepdims=True) + NORM_EPS) * ng_ref[...]
        y = (y * _silu(load(g_ref, c, lanes).astype(F32))).astype(o_ref.dtype)
        for j in range(STREAMS):
            o_ref[0, rows(c, j), lanes] = y[j * piece:(j + 1) * piece]

    for t in range(n + 2):
        if t in interleave:
            interleave[t]()
        for g in range(REC_HEADS_PER_STEP):
            if t >= 2:
                output(c0 + (t - 2), g)
            if 1 <= t <= n:
                scores(c0 + (t - 1), g)
            if t < n:
                gates(c0 + t, g)


def _proj_rec_kernel(a_ref, w_ref, q_ref, f_ref, i_ref, g_ref, lb_ref, ng_ref, o_ref, y_ref, wb_ref, *scratch):
    n_i = pl.num_programs(1)
    i = pl.program_id(1)
    steps_per_seq = q_ref.shape[1] // (REC_CHUNK * REC_CHUNKS_PER_STEP)
    group = (pl.program_id(0) * n_i + i) % steps_per_seq

    @pl.when(i == 0)
    def _():
        _cast_weight_tile(w_ref, wb_ref)

    @pl.when(group == 0)
    def _():
        per_head = len(scratch) // REC_HEADS_PER_STEP
        for g in range(REC_HEADS_PER_STEP):
            scratch[g * per_head][...] = jnp.zeros_like(scratch[g * per_head])

    interleave = _spread(REC_CHUNKS_PER_STEP + 2, o_ref.shape[1] // (2 * LANES),
                         functools.partial(_plain_project, a_ref, wb_ref, o_ref))
    _rec_pipeline(q_ref, f_ref, i_ref, g_ref, lb_ref, ng_ref, y_ref, scratch,
                  group * REC_CHUNKS_PER_STEP, REC_CHUNKS_PER_STEP, interleave)


def _project_and_recurrence(u, w, col0, ncols, q_r, f_r, i_r, g_r, lower_bound, norm_g, bsz, seq):
    t, d = u.shape
    width = lower_bound.shape[0]
    gw = REC_HEADS_PER_STEP * HEAD_DIM
    tm, tn = 1024, 1024
    j0 = col0 // tn
    n_j, n_i = ncols // tn, t // tm
    n_hg = width // gw
    steps_per_seq = seq // (REC_CHUNK * REC_CHUNKS_PER_STEP)
    assert n_j * n_i == bsz * n_hg * steps_per_seq, "one recurrence slice per projection tile"

    def rec_pos(j, i):
        s = (j * n_i + i) // steps_per_seq
        return s // n_hg, s % n_hg

    def rec_cols(src):
        a, col0 = src
        off = col0 // gw
        return a.reshape(bsz, seq, a.shape[1]), pl.BlockSpec(
            (1, seq, gw), lambda j, i: (rec_pos(j, i)[0], 0, off + rec_pos(j, i)[1]))

    (qa, qs), (fa, fs), (ia, isp), (ga, gs) = (rec_cols(s) for s in (q_r, f_r, i_r, g_r))
    proj, y = pl.pallas_call(
        _proj_rec_kernel,
        out_shape=(jax.ShapeDtypeStruct((t, ncols), BF16), jax.ShapeDtypeStruct((bsz, seq, width), BF16)),
        grid=(n_j, n_i),
        in_specs=[pl.BlockSpec((tm, d), lambda j, i: (i, 0)),
                  pl.BlockSpec((d, tn), lambda j, i: (0, j0 + j)),
                  qs, fs, isp, gs,
                  pl.BlockSpec((1, gw), lambda j, i: (0, rec_pos(j, i)[1])),
                  pl.BlockSpec((1, HEAD_DIM), lambda j, i: (0, 0))],
        out_specs=(pl.BlockSpec((tm, tn), lambda j, i: (i, j)),
                   pl.BlockSpec((1, seq, gw), lambda j, i: (rec_pos(j, i)[0], 0, rec_pos(j, i)[1]))),
        scratch_shapes=[pltpu.VMEM((d, tn), BF16)]
        + [s for _ in range(REC_HEADS_PER_STEP) for s in _rec_scratch()],
        compiler_params=_cparams(("arbitrary", "arbitrary")),
        name="proj_hgrn2",
    )(u, w, qa, fa, ia, ga, lower_bound.reshape(1, width), norm_g.reshape(1, HEAD_DIM))
    return proj, y.reshape(t, width)


def _merge_kernel(ya_ref, yr_ref, wa_ref, wr_ref, ga_ref, gr_ref, o_ref, wab_ref, wrb_ref):
    @pl.when(pl.program_id(1) == 0)
    def _():
        _cast_weight_tile(wa_ref, wab_ref)
        _cast_weight_tile(wr_ref, wrb_ref)

    a = jnp.dot(ya_ref[...], wab_ref[...], preferred_element_type=F32)
    r = jnp.dot(yr_ref[...], wrb_ref[...], preferred_element_type=F32)
    m = _sigmoid(ga_ref[...].astype(F32)) * a + _sigmoid(gr_ref[...].astype(F32)) * r
    o_ref[...] = m.astype(o_ref.dtype)


def _merge(ya, yr, wa, wr, ga, gr):
    t, d = ya.shape
    n = wa.shape[1]
    tm, tn = 1024, 512
    row = pl.BlockSpec((tm, d), lambda j, i: (i, 0))
    col = pl.BlockSpec((d, tn), lambda j, i: (0, j))
    ga_off, gr_off = ga[1] // tn, gr[1] // tn
    return pl.pallas_call(
        _merge_kernel,
        out_shape=jax.ShapeDtypeStruct((t, n), BF16),
        grid=(n // tn, t // tm),
        in_specs=[row, row, col, col,
                  pl.BlockSpec((tm, tn), lambda j, i: (i, ga_off + j)),
                  pl.BlockSpec((tm, tn), lambda j, i: (i, gr_off + j))],
        out_specs=pl.BlockSpec((tm, tn), lambda j, i: (i, j)),
        scratch_shapes=[pltpu.VMEM((d, tn), BF16)] * 2,
        compiler_params=_cparams(("arbitrary", "arbitrary")),
        name="branch_merge",
    )(ya, yr, wa, wr, ga[0], gr[0])


def _mixout_kernel(m_ref, w_ref, x_ref, gt_ref, g_ref, sh_ref, sc_ref, rw_ref, rb_ref,
                   h_ref, u_ref, route_ref, wb_ref):
    @pl.when(pl.program_id(0) == 0)
    def _():
        _cast_weight_tile(w_ref, wb_ref)

    mix = jnp.dot(m_ref[...], wb_ref[...], preferred_element_type=F32)
    h = x_ref[...] + gt_ref[0] * mix
    h_ref[...] = h
    u = h * lax.rsqrt(jnp.mean(h * h, axis=-1, keepdims=True) + NORM_EPS) * g_ref[...]
    u = u * (1.0 + sc_ref[0]) + sh_ref[0]
    wc = 2 * u_ref.shape[2]
    for c in range(u_ref.shape[0]):
        u_ref[c] = _pack_bf16_pairs(u[:, c * wc:(c + 1) * wc])
    u_hi = u.astype(BF16)
    u_lo = (u - u_hi.astype(F32)).astype(BF16)
    rw = rw_ref[...]
    w_hi = rw.astype(BF16)
    w_lo = (rw - w_hi.astype(F32)).astype(BF16)
    logits = (jnp.dot(u_hi, w_hi, preferred_element_type=F32)
              + jnp.dot(u_lo, w_hi, preferred_element_type=F32)
              + jnp.dot(u_hi, w_lo, preferred_element_type=F32)) + rb_ref[...]
    lane = lax.broadcasted_iota(jnp.int32, logits.shape, 1).astype(F32)
    big = float(LANES)
    neg = -jnp.inf
    lg = jnp.where(lane < N_GROUPS, logits, neg)
    mg = jnp.max(lg, axis=-1, keepdims=True)
    g_sel = jnp.min(jnp.where(lg == mg, lane, big), axis=-1, keepdims=True)
    p_group = 1.0 / jnp.sum(jnp.exp(lg - mg), axis=-1, keepdims=True)
    lo = N_GROUPS + EXPERTS_PER_GROUP * g_sel
    le = jnp.where((lane >= lo) & (lane < lo + EXPERTS_PER_GROUP), logits, neg)
    t1 = jnp.max(le, axis=-1, keepdims=True)
    i1 = jnp.min(jnp.where(le == t1, lane, big), axis=-1, keepdims=True)
    le2 = jnp.where(lane == i1, neg, le)
    t2 = jnp.max(le2, axis=-1, keepdims=True)
    i2 = jnp.min(jnp.where(le2 == t2, lane, big), axis=-1, keepdims=True)
    e21 = jnp.exp(t2 - t1)
    w1 = p_group / (1.0 + e21)
    w2 = p_group * e21 / (1.0 + e21)
    route = jnp.where(lane == 0, i1 - N_GROUPS,
                      jnp.where(lane == 1, i2 - N_GROUPS,
                                jnp.where(lane == 2, w1, jnp.where(lane == 3, w2, 0.0))))
    route_ref[...] = route


def _mix_out(merged, w_out, x2d, gate, g, shift, scale, rw, rb, seq):
    t, d = x2d.shape
    tm = 512
    per_b = seq // tm
    bsz = gate.shape[0]
    row = lambda dt: pl.BlockSpec((tm, d), lambda i: (i, 0))
    per_batch = pl.BlockSpec((1, 1, d), lambda i: (i // per_b, 0, 0))
    const = lambda shape: pl.BlockSpec(shape, lambda i: (0,) * len(shape))
    wc = d // GATHER_CHUNKS // 2
    outs = pl.pallas_call(
        _mixout_kernel,
        out_shape=(jax.ShapeDtypeStruct((t, d), F32),
                   jax.ShapeDtypeStruct((GATHER_CHUNKS, t, wc), jnp.uint32),
                   jax.ShapeDtypeStruct((t, LANES), F32)),
        grid=(t // tm,),
        in_specs=[row(BF16), pl.BlockSpec((d, d), lambda i: (0, 0), pipeline_mode=pl.Buffered(1)),
                  row(F32), per_batch, const((1, d)), per_batch, per_batch,
                  const((d, LANES)), const((1, LANES))],
        out_specs=(row(F32), pl.BlockSpec((GATHER_CHUNKS, tm, wc), lambda i: (0, i, 0)),
                   pl.BlockSpec((tm, LANES), lambda i: (i, 0))),
        scratch_shapes=[pltpu.VMEM((d, d), BF16)],
        compiler_params=_cparams(("arbitrary",)),
        name="mix_out_router",
    )(merged, w_out, x2d, gate.reshape(bsz, 1, d), g.reshape(1, d),
      shift.reshape(bsz, 1, d), scale.reshape(bsz, 1, d), rw, rb)
    return outs


def _expert_kernel(be_ref, nx_ref, nu_ref, *refs):
    x_ref = refs[0]
    w_hbm = refs[1:4]
    o_ref = refs[-8]
    stage = refs[-7:-4]
    wb = refs[-4:-1]
    sem = refs[-1]
    nch = x_ref.shape[0]
    wc = 2 * o_ref.shape[2]
    i = pl.program_id(0)
    e = be_ref[i]
    nxt = nx_ref[i]
    active = i < nu_ref[0]
    first = i == 0
    run_start = jnp.logical_or(first, e != be_ref[jnp.maximum(i - 1, 0)])

    def weight_copies(expert):
        return [pltpu.make_async_copy(w_hbm[k].at[expert], stage[k], sem.at[k]) for k in range(3)]

    @pl.when(jnp.logical_and(active, first))
    def _():
        for cp in weight_copies(e):
            cp.start()

    @pl.when(jnp.logical_and(active, run_start))
    def _():
        for cp in weight_copies(e):
            cp.wait()
        for k in range(3):
            _cast_weight_tile(stage[k], wb[k])

        @pl.when(nxt >= 0)
        def _():
            for cp in weight_copies(nxt):
                cp.start()

    @pl.when(active)
    def _():
        x = jnp.concatenate([_unpack_bf16_pairs(x_ref[c]) for c in range(nch)], axis=1).astype(BF16)
        hg = jnp.dot(x, wb[0][...], preferred_element_type=F32)
        hu = jnp.dot(x, wb[1][...], preferred_element_type=F32)
        hdn = (_silu(hg) * hu).astype(BF16)
        y = jnp.dot(hdn, wb[2][...], preferred_element_type=F32)
        for c in range(nch):
            o_ref[c] = _pack_bf16_pairs(y[:, c * wc:(c + 1) * wc])

    @pl.when(jnp.logical_not(active))
    def _():
        o_ref[...] = jnp.zeros_like(o_ref)


def _expert_ffn(xs, w_gate, w_up, w_down, block_expert, next_expert, n_used, part, prev=None):
    nch, part_slots, wc = xs.shape
    n_slots = part_slots * EXPERT_PARTS
    d = 2 * wc * nch
    hid = w_gate.shape[2]
    bm = MOE_ROWS
    first = part * (part_slots // bm)
    chunk = pl.BlockSpec((nch, bm, wc), lambda i, be, nx, nu: (0, i, 0))
    hbm = pl.BlockSpec(memory_space=pl.ANY)
    args = [block_expert, next_expert, n_used, xs, w_gate, w_up, w_down]
    in_specs = [chunk, hbm, hbm, hbm]
    aliases = {}
    if prev is not None:
        in_specs.append(hbm)
        args.append(prev)
        aliases = {len(args) - 1: 0}
    grid_spec = pltpu.PrefetchScalarGridSpec(
        num_scalar_prefetch=3,
        grid=(part_slots // bm,),
        in_specs=in_specs,
        out_specs=pl.BlockSpec((nch, bm, wc), lambda i, be, nx, nu: (0, first + i, 0)),
        scratch_shapes=[pltpu.VMEM((d, hid), F32), pltpu.VMEM((d, hid), F32), pltpu.VMEM((hid, d), F32),
                        pltpu.VMEM((d, hid), BF16), pltpu.VMEM((d, hid), BF16), pltpu.VMEM((hid, d), BF16),
                        pltpu.SemaphoreType.DMA((3,))],
    )
    return pl.pallas_call(
        _expert_kernel,
        out_shape=jax.ShapeDtypeStruct((nch, n_slots, wc), jnp.uint32),
        grid_spec=grid_spec,
        input_output_aliases=aliases,
        compiler_params=_cparams(("arbitrary",)),
        name="expert_ffn",
    )(*args)


def _gather_rows(chunks, idx):
    nch, n, d = chunks.shape
    table = chunks.reshape(nch * n, d)
    idx = (idx[None, :] + (jnp.arange(nch, dtype=jnp.int32) * n)[:, None]).reshape(-1)
    m = idx.shape[0]
    window = LANES
    mesh = plsc.VectorSubcoreMesh(core_axis_name="core", subcore_axis_name="subcore")

    @pl.kernel(out_type=jax.ShapeDtypeStruct((m, d), table.dtype), mesh=mesh, scratch_types=[])
    def gather(x_hbm, i_hbm, o_hbm):
        def body(i_vmem, o_vmem):
            pltpu.sync_copy(x_hbm.at[i_vmem.at[0]], o_vmem)

        pltpu.emit_pipeline(
            body,
            grid=(m // window,),
            in_specs=[pl.BlockSpec((1, window), lambda i: (0, i))],
            out_specs=[pl.BlockSpec((window, d), lambda i: (i, 0))],
            core_axis_name=("core", "subcore"),
            dimension_semantics=(pltpu.PARALLEL,),
        )(i_hbm, o_hbm)

    return gather(table, idx.reshape(1, m)).reshape(nch, m // nch, d)


def _final_kernel(h_ref, *rest):
    y_refs = rest[:TOP_K]
    rt_ref, gt_ref, g_ref = rest[TOP_K:TOP_K + 3]
    o_ref, slab_ref = rest[-2:]
    sub = h_ref.shape[2]
    nslab = h_ref.shape[3] // LANES
    for j in range(STREAMS):
        route = rt_ref[0, j]
        ffn = sum(route[:, TOP_K + k:TOP_K + k + 1]
                  * jnp.concatenate([_unpack_bf16_pairs(y_refs[k][c, 0, 0, j])
                                     for c in range(GATHER_CHUNKS)], axis=1) for k in range(TOP_K))
        h = h_ref[0, j] + gt_ref[0] * ffn
        y = h * lax.rsqrt(jnp.mean(h * h, axis=-1, keepdims=True) + NORM_EPS) * g_ref[...]
        for c in range(nslab):
            slab_ref[c, pl.ds(j, sub, stride=STREAMS), :] = y[:, c * LANES:(c + 1) * LANES]
    for c in range(nslab):
        o_ref[:, c * LANES:(c + 1) * LANES] = slab_ref[c]


def _final(h, y2, route, gate, g, seq, part, parts, prev=None):
    t, d = h.shape
    tm = 512
    per_b = seq // tm
    bsz = gate.shape[0]
    sub = tm // STREAMS
    steps = t // tm // parts
    first = part * steps
    spec = pl.BlockSpec((1, STREAMS, sub, d), lambda i: ((first + i) // per_b, 0, (first + i) % per_b, 0))
    wc = d // GATHER_CHUNKS // 2
    y6 = y2.reshape(GATHER_CHUNKS, TOP_K, bsz // parts, STREAMS, seq // STREAMS, wc)

    def yspec(k):
        return pl.BlockSpec((GATHER_CHUNKS, 1, 1, STREAMS, sub, wc),
                            lambda i: (0, k, i // per_b, 0, i % per_b, 0))

    in_specs = [spec, *[yspec(k) for k in range(TOP_K)],
                pl.BlockSpec((1, STREAMS, sub, LANES),
                             lambda i: ((first + i) // per_b, 0, (first + i) % per_b, 0)),
                pl.BlockSpec((1, 1, d), lambda i: ((first + i) // per_b, 0, 0)),
                pl.BlockSpec((1, d), lambda i: (0, 0))]
    args = [h.reshape(bsz, STREAMS, seq // STREAMS, d), *([y6] * TOP_K),
            route.reshape(bsz, STREAMS, seq // STREAMS, LANES), gate.reshape(bsz, 1, d), g.reshape(1, d)]
    aliases = {}
    if prev is not None:
        in_specs.append(pl.BlockSpec(memory_space=pl.ANY))
        args.append(prev)
        aliases = {len(args) - 1: 0}
    return pl.pallas_call(
        _final_kernel,
        out_shape=jax.ShapeDtypeStruct((t, d), F32),
        grid=(steps,),
        in_specs=in_specs,
        out_specs=pl.BlockSpec((tm, d), lambda i: (first + i, 0)),
        scratch_shapes=[pltpu.VMEM((d // LANES, tm, LANES), F32)],
        input_output_aliases=aliases,
        compiler_params=_cparams(("arbitrary",)),
        name="final_norm",
    )(*args)


def _dispatch_plan(expert_idx):
    n_assign = expert_idx.size
    n_blocks = n_assign // MOE_ROWS + N_EXPERTS
    n_slots = n_blocks * MOE_ROWS
    flat_e = expert_idx.reshape(-1)
    ids = jnp.arange(n_assign, dtype=jnp.int32)
    eids = jnp.arange(N_EXPERTS, dtype=jnp.int32)[None, :]
    _, order = lax.sort((flat_e, ids), num_keys=1, is_stable=True)
    _, rank_sorted = lax.sort((order, ids), num_keys=1)
    hot_a = (flat_e[:, None] == eids).astype(jnp.int32)
    counts = jnp.sum(hot_a, axis=0)
    padded = ((counts + MOE_ROWS - 1) // MOE_ROWS) * MOE_ROWS
    pad_end = jnp.cumsum(padded)
    pad_start = pad_end - padded
    start = jnp.cumsum(counts) - counts
    dest = (rank_sorted + jnp.sum(hot_a * (pad_start - start)[None, :], axis=1)).reshape(-1, TOP_K)
    blk0 = jnp.arange(n_blocks, dtype=jnp.int32) * MOE_ROWS
    block_expert = jnp.minimum(jnp.sum((pad_end[None, :] <= blk0[:, None]).astype(jnp.int32), axis=1),
                               N_EXPERTS - 1)
    hot_b = (block_expert[:, None] == eids).astype(jnp.int32)
    blk_shift = jnp.sum(hot_b * (start - pad_start)[None, :], axis=1)
    blk_count = jnp.sum(hot_b * (pad_start + counts)[None, :], axis=1)
    slot = jnp.arange(n_slots, dtype=jnp.int32).reshape(n_blocks, MOE_ROWS)
    valid = (slot < blk_count[:, None]).reshape(-1)
    src = jnp.clip(slot + blk_shift[:, None], 0, n_assign - 1).reshape(-1)
    assign = order[src]
    token_of_slot = jnp.where(valid, assign // TOP_K, slot.reshape(-1) % (n_assign // TOP_K))
    n_used = (pad_end[-1:] // MOE_ROWS).astype(jnp.int32)
    bi = jnp.arange(n_blocks, dtype=jnp.int32)
    part = bi // (n_blocks // EXPERT_PARTS)
    later = ((bi[None, :] > bi[:, None]) & (block_expert[None, :] != block_expert[:, None])
             & (bi[None, :] < n_used[0]) & (part[None, :] == part[:, None]))
    next_expert = jnp.where(jnp.any(later, axis=1), block_expert[jnp.argmax(later, axis=1)], -1)
    return token_of_slot, dest, block_expert, next_expert.astype(jnp.int32), n_used


def kernel(x, c, positions, ada_w, ada_b, mix_norm_g, w_in, w_attn_branch, w_rec_branch, w_mix_out,
           rec_norm_g, rec_lb_logits, ffn_norm_g, router_group_w, router_group_b, router_expert_w,
           router_expert_b, expert_w_gate, expert_w_up, expert_w_down, final_norm_g):
    bsz, seq, d = x.shape
    t = bsz * seq
    depth = ada_w.shape[0]
    assert depth == 1, "final norm is fused after the single layer"
    lower_bounds = jnp.cumsum(jax.nn.softmax(rec_lb_logits.astype(F32), axis=0), axis=0)
    pos_streams = positions.reshape(bsz, seq // STREAMS, STREAMS).transpose(0, 2, 1)
    rope = _rope_tables(pos_streams)
    h = x.reshape(t, d)
    for layer in range(depth):
        mod = _modulation(c, ada_w[layer], ada_b[layer])
        sh_m, sc_m, gt_m, sh_f, sc_f, gt_f = jnp.split(mod, 6, axis=-1)
        h, u = _norm_modulate(h, mix_norm_g[layer], sh_m, sc_m, seq)
        w = w_in[layer]
        qk = _project(u, w, 0, 2 * d, rope=rope)
        v_a = _project(u, w, 2 * d, d)
        rec_in, y_attn = _project_and_attention(u, w, 3 * d, 4 * d, (qk, 0), (qk, d), (v_a, 0),
                                                d, bsz, seq)
        gates, y_rec = _project_and_recurrence(u, w, 7 * d, 2 * d, (rec_in, 0), (rec_in, d), (rec_in, 2 * d),
                                               (rec_in, 3 * d), lower_bounds[layer], rec_norm_g[layer],
                                               bsz, seq)
        merged = _merge(y_attn, y_rec, w_attn_branch[layer], w_rec_branch[layer], (gates, 0), (gates, d))
        rw = jnp.concatenate([router_group_w[layer], router_expert_w[layer],
                              jnp.zeros((d, LANES - N_GROUPS - N_EXPERTS), F32)], axis=1)
        rb = jnp.concatenate([router_group_b[layer], router_expert_b[layer],
                              jnp.zeros((LANES - N_GROUPS - N_EXPERTS,), F32)]).reshape(1, LANES)
        h, u2, route = _mix_out(merged, w_mix_out[layer], h, gt_m, ffn_norm_g[layer],
                                sh_f, sc_f, rw, rb, seq)
        expert_idx = route[:, :TOP_K].astype(jnp.int32)
        tok, dest, block_expert, next_expert, n_used = _dispatch_plan(expert_idx)
        blocks = block_expert.shape[0] // EXPERT_PARTS
        ys = None
        for part in range(EXPERT_PARTS):
            sl = slice(part * blocks, (part + 1) * blocks)
            xs = _gather_rows(u2, tok[part * blocks * MOE_ROWS:(part + 1) * blocks * MOE_ROWS])
            ys = _expert_ffn(xs, expert_w_gate[layer], expert_w_up[layer], expert_w_down[layer],
                             block_expert[sl], next_expert[sl], jnp.clip(n_used - part * blocks, 0, blocks),
                             part, prev=ys)
        parts = 2
        out = None
        for part in range(parts):
            tok_part = dest[part * (t // parts):(part + 1) * (t // parts)]
            y2 = _gather_rows(ys, tok_part.T.reshape(-1))
            out = _final(h, y2, route, gt_f, final_norm_g, seq, part, parts, prev=out)
        h = out
    return h.reshape(bsz, seq, d)
```
